```python
import jax, jax.numpy as jnp
from jax import lax
import numpy as np

D_MODEL = 1024
BATCH = 2
SEQ = 8192
DEPTH = 1

N_META = 16
GLA_HEADS = 4
GLA_DK = D_MODEL // 2
GLA_DV = D_MODEL
GLA_HEAD_DK = GLA_DK // GLA_HEADS
GLA_HEAD_DV = GLA_DV // GLA_HEADS
DECAY_RANK = 16
GATE_TEMP = 16.0
CHUNK = 64
FNET_GROUPS = 4
FNET_WIDTH = D_MODEL // 2
FNET_GROUP_DIM = FNET_WIDTH // FNET_GROUPS
D_FF = 4 * D_MODEL
EPS = 1e-6
IN_SPLITS = (GLA_DK, GLA_DK, GLA_DV, GLA_DV, DECAY_RANK, DECAY_RANK, FNET_WIDTH, D_MODEL, D_MODEL)
D_IN = 2 * GLA_DK + 2 * GLA_DV + 2 * DECAY_RANK + FNET_WIDTH + 2 * D_MODEL

kernel_name = "bidir_gla_fnet_gated_hybrid_block"


def rms_norm(x, w):
    xf = x.astype(jnp.float32)
    y = xf * lax.rsqrt(jnp.mean(xf * xf, axis=-1, keepdims=True) + EPS)
    return y.astype(x.dtype) * w


def to_heads(t, n_heads):
    B, L, W = t.shape
    return t.reshape(B, L, n_heads, W // n_heads).transpose(0, 2, 1, 3)


def gla_chunked(q, k, v, log_a):
    B, H, T, dk = q.shape
    dv = v.shape[-1]
    n = T // CHUNK
    q, k, log_a = (t.reshape(B, H, n, CHUNK, dk) for t in (q, k, log_a))
    v = v.reshape(B, H, n, CHUNK, dv)
    b = jnp.cumsum(log_a, axis=3)
    b_ref = b[:, :, :, CHUNK // 2 - 1:CHUNK // 2]
    b_last = b[:, :, :, -1:]
    scores = jnp.einsum("bhncd,bhnsd->bhncs", q * jnp.exp(b - b_ref), k * jnp.exp(b_ref - b))
    lower = jnp.tril(jnp.ones((CHUNK, CHUNK), dtype=bool))
    scores = jnp.where(lower, scores, 0.0)
    o_intra = jnp.einsum("bhncs,bhnse->bhnce", scores, v)
    d_state = jnp.einsum("bhncd,bhnce->bhnde", k * jnp.exp(b_last - b), v)
    chunk_decay = jnp.exp(b_last[:, :, :, 0])

    def step(S, inp):
        ds_n, a_n = inp
        return a_n[..., None] * S + ds_n, S

    S0 = jnp.zeros((B, H, dk, dv), q.dtype)
    _, S_prev = lax.scan(step, S0, (jnp.moveaxis(d_state, 2, 0), jnp.moveaxis(chunk_decay, 2, 0)))
    S_prev = jnp.moveaxis(S_prev, 0, 2)
    o_inter = jnp.einsum("bhncd,bhnde->bhnce", q * jnp.exp(b), S_prev)
    return (o_intra + o_inter).reshape(B, H, T, dv)


def bidirectional_gla(q, k, v, log_a_fwd, log_a_bwd):
    pad_len = CHUNK - N_META
    pad = ((0, 0), (0, 0), (pad_len, 0), (0, 0))
    q, k, v, la_f, la_b = (jnp.pad(t.astype(jnp.float32), pad) for t in (q, k, v, log_a_fwd, log_a_bwd))
    o_fwd = gla_chunked(q, k, v, la_f)
    flip = lambda t: jnp.flip(t, axis=2)
    o_bwd = flip(gla_chunked(flip(q), flip(k), flip(v), flip(la_b)))
    return (o_fwd + o_bwd)[:, :, pad_len:]


def hybrid_mixer(h, norm_w, w_in, w_dec_f, b_dec_f, w_dec_b, b_dec_b, gla_norm, w_o_gla, w_fnet, w_out):
    B, L, _ = h.shape
    u = rms_norm(h, norm_w)
    proj = u @ w_in
    split_points = [int(s) for s in np.cumsum(IN_SPLITS)[:-1]]
    q, k, v, g, r_f, r_b, f_in, gate_a_logit, gate_f_logit = jnp.split(proj, split_points, axis=-1)

    log_a_f = jax.nn.log_sigmoid(r_f @ w_dec_f + b_dec_f) / GATE_TEMP
    log_a_b = jax.nn.log_sigmoid(r_b @ w_dec_b + b_dec_b) / GATE_TEMP
    o = bidirectional_gla(to_heads(q * GLA_HEAD_DK ** -0.5, GLA_HEADS), to_heads(k, GLA_HEADS),
                          to_heads(v, GLA_HEADS), to_heads(log_a_f, GLA_HEADS),
                          to_heads(log_a_b, GLA_HEADS))
    o = rms_norm(o.astype(h.dtype), gla_norm)
    o = o.transpose(0, 2, 1, 3).reshape(B, L, GLA_DV) * jax.nn.silu(g)
    y_gla = o @ w_o_gla

    f = f_in.reshape(B, L, FNET_GROUPS, FNET_GROUP_DIM).astype(jnp.float32)
    f = jnp.fft.fft2(f, axes=(1, 3), norm="ortho").real
    y_fnet = f.astype(h.dtype).reshape(B, L, FNET_WIDTH) @ w_fnet

    merged = jax.nn.sigmoid(gate_a_logit) * y_gla + jax.nn.sigmoid(gate_f_logit) * y_fnet
    return h + merged @ w_out


def squared_relu_mlp(u, w1, w2):
    return jnp.square(jax.nn.relu(u @ w1)) @ w2


def setup_inputs(seed: int = 0) -> dict:
    key = jax.random.key(seed)
    ks = jax.random.split(key, 18)
    nrm = lambda k, shape, fan_in: jax.random.normal(k, shape, jnp.float32) * fan_in ** -0.5
    gain = lambda k, shape: 1.0 + 0.02 * jax.random.normal(k, shape, jnp.float32)
    return {
        "x": jax.random.normal(ks[0], (BATCH, SEQ, D_MODEL), jnp.float32),
        "meta_tokens": jax.random.normal(ks[1], (N_META, D_MODEL), jnp.float32),
        "norm_mix": gain(ks[2], (DEPTH, D_MODEL)),
        "w_in": nrm(ks[3], (DEPTH, D_MODEL, D_IN), D_MODEL),
        "w_decay_fwd": nrm(ks[4], (DEPTH, DECAY_RANK, GLA_DK), DECAY_RANK),
        "b_decay_fwd": 0.1 * jax.random.normal(ks[5], (DEPTH, GLA_DK), jnp.float32),
        "w_decay_bwd": nrm(ks[6], (DEPTH, DECAY_RANK, GLA_DK), DECAY_RANK),
        "b_decay_bwd": 0.1 * jax.random.normal(ks[7], (DEPTH, GLA_DK), jnp.float32),
        "gla_norm": gain(ks[8], (DEPTH, GLA_HEAD_DV)),
        "w_o_gla": nrm(ks[9], (DEPTH, GLA_DV, D_MODEL), GLA_DV),
        "w_fnet": nrm(ks[10], (DEPTH, FNET_WIDTH, D_MODEL), FNET_WIDTH),
        "w_out": nrm(ks[11], (DEPTH, D_MODEL, D_MODEL), D_MODEL),
        "norm_ffn": gain(ks[12], (DEPTH, D_MODEL)),
        "w_ff1": nrm(ks[13], (DEPTH, D_MODEL, D_FF), D_MODEL),
        "w_ff2": nrm(ks[14], (DEPTH, D_FF, D_MODEL), D_FF),
        "norm_final": gain(ks[15], (D_MODEL,)),
    }


def reference(x, meta_tokens, norm_mix, w_in, w_decay_fwd, b_decay_fwd, w_decay_bwd, b_decay_bwd,
              gla_norm, w_o_gla, w_fnet, w_out, norm_ffn, w_ff1, w_ff2, norm_final):
    B = x.shape[0]
    meta = jnp.broadcast_to(meta_tokens.astype(x.dtype)[None], (B, N_META, D_MODEL))
    h = jnp.concatenate([meta, x], axis=1)
    for layer in range(DEPTH):
        h = hybrid_mixer(h, norm_mix[layer], w_in[layer], w_decay_fwd[layer], b_decay_fwd[layer],
                         w_decay_bwd[layer], b_decay_bwd[layer], gla_norm[layer], w_o_gla[layer],
                         w_fnet[layer], w_out[layer])
        h = h + squared_relu_mlp(rms_norm(h, norm_ffn[layer]), w_ff1[layer], w_ff2[layer])
    h = rms_norm(h, norm_final)
    return h[:, N_META:]
```

```python
import functools

import numpy as np
import jax
import jax.numpy as jnp
from jax import lax
from jax.experimental import pallas as pl
from jax.experimental.pallas import tpu as pltpu

F32 = jnp.float32
BF16 = jnp.bfloat16

D_MODEL = 1024
N_META = 16
GLA_HEADS = 4
GLA_DK = 512
GLA_DV = 1024
HEAD_DK = GLA_DK // GLA_HEADS
HEAD_DV = GLA_DV // GLA_HEADS
DECAY_RANK = 16
GATE_TEMP = 16.0
SUB = 64
FNET_GROUPS = 4
FNET_WIDTH = 512
FNET_GROUP_DIM = FNET_WIDTH // FNET_GROUPS
D_FF = 4 * D_MODEL
EPS = 1e-6

_OFF_Q, _OFF_K, _OFF_V, _OFF_G, _OFF_F, _OFF_GA, _OFF_GF, _OFF_END = (
    0, 512, 1024, 2048, 3072, 3584, 4608, 5632)
R_PAD = 128

GLA_NSUB = 4
GLA_C = SUB * GLA_NSUB

DFT_T = 512
LANE = 128
V7X_VMEM_LIMIT = 56 * 1024 * 1024


def _rms(x, w):
    ms = jnp.mean(x * x, axis=-1, keepdims=True)
    return x * lax.rsqrt(ms + EPS) * w


def _sigmoid(x):
    return 1.0 / (1.0 + jnp.exp(-x))


def _resident(shape):
    nd = len(shape)
    return pl.BlockSpec(shape, lambda *_: (0,) * nd, pipeline_mode=pl.Buffered(1))


def _proj_kernel(x_ref, nw_ref, w_ref, wr_ref, cs_ref,
                 q_ref, k_ref, v_ref, g_ref, r_ref, p_ref, qq_ref, ga_ref, gf_ref):
    xn = _rms(x_ref[...], nw_ref[...]).astype(BF16)

    def mm(lo, hi):
        return jnp.dot(xn, w_ref[:, lo:hi], preferred_element_type=F32)

    q_ref[...] = (mm(_OFF_Q, _OFF_K) * (HEAD_DK ** -0.5)).astype(BF16)
    k_ref[...] = mm(_OFF_K, _OFF_V).astype(BF16)
    v_ref[...] = mm(_OFF_V, _OFF_G).astype(BF16)
    g_ref[...] = mm(_OFF_G, _OFF_F).astype(BF16)
    ga_ref[...] = mm(_OFF_GA, _OFF_GF).astype(BF16)
    gf_ref[...] = mm(_OFF_GF, _OFF_END).astype(BF16)
    r_ref[...] = jnp.dot(xn, wr_ref[...], preferred_element_type=F32).astype(BF16)
    f = mm(_OFF_F, _OFF_GA).astype(BF16)
    cs = cs_ref[...].astype(BF16)
    for grp in range(FNET_GROUPS):
        sl = slice(grp * FNET_GROUP_DIM, (grp + 1) * FNET_GROUP_DIM)
        pq = jnp.dot(f[:, sl], cs, preferred_element_type=F32)
        p_ref[:, sl] = pq[:, :FNET_GROUP_DIM].astype(BF16)
        qq_ref[:, sl] = pq[:, FNET_GROUP_DIM:].astype(BF16)


def _proj_call(x2d, nw, w_main, w_r, cs, rows_per_batch, tm):
    m = x2d.shape[0]
    nt = rows_per_batch // tm
    nb = m // rows_per_batch
    row = lambda n: pl.BlockSpec((tm, n), lambda i: (i, 0))
    pq_spec = pl.BlockSpec((tm, FNET_WIDTH), lambda i: (i % nt, i // nt))
    out_shape = (
        jax.ShapeDtypeStruct((m, GLA_DK), BF16), jax.ShapeDtypeStruct((m, GLA_DK), BF16),
        jax.ShapeDtypeStruct((m, GLA_DV), BF16), jax.ShapeDtypeStruct((m, GLA_DV), BF16),
        jax.ShapeDtypeStruct((m, R_PAD), BF16),
        jax.ShapeDtypeStruct((rows_per_batch, nb * FNET_WIDTH), BF16),
        jax.ShapeDtypeStruct((rows_per_batch, nb * FNET_WIDTH), BF16),
        jax.ShapeDtypeStruct((m, D_MODEL), BF16), jax.ShapeDtypeStruct((m, D_MODEL), BF16),
    )
    return pl.pallas_call(
        _proj_kernel,
        grid=(m // tm,),
        in_specs=[row(D_MODEL), _resident((1, D_MODEL)), _resident(w_main.shape),
                  _resident(w_r.shape), _resident(cs.shape)],
        out_specs=(row(GLA_DK), row(GLA_DK), row(GLA_DV), row(GLA_DV), row(R_PAD),
                   pq_spec, pq_spec, row(D_MODEL), row(D_MODEL)),
        out_shape=out_shape,
        compiler_params=pltpu.CompilerParams(
            dimension_semantics=("arbitrary",), vmem_limit_bytes=V7X_VMEM_LIMIT),
        name="proj",
    )(x2d, nw, w_main, w_r, cs)


def _log_decay(r_bf, w_ref, b_ref):
    x = jnp.dot(r_bf, w_ref[...], preferred_element_type=F32) + b_ref[...]
    return (jnp.minimum(x, 0.0) - jnp.log1p(jnp.exp(-jnp.abs(x)))) * (1.0 / GATE_TEMP)


def _block_cumsum(tri_bf, la):
    hi = la.astype(BF16)
    lo = (la - hi.astype(F32)).astype(BF16)
    return (jnp.dot(tri_bf, hi, preferred_element_type=F32)
            + jnp.dot(tri_bf, lo, preferred_element_type=F32))


_NT = (((1,), (1,)), ((), ()))
_TN = (((0,), (0,)), ((), ()))


def _gla_chunk(q_ref, k_ref, v_ref, r_ref, w_ref, b_ref, tri_ref, mask_ref, st_ref, rows,
               backward):
    n, s = GLA_NSUB, SUB
    la = _log_decay(r_ref[rows, :], w_ref, b_ref)
    b = _block_cumsum(tri_ref[...], la).reshape(n, s, HEAD_DK)
    if backward:
        b_ref_row, b_end = b[:, s // 2:s // 2 + 1, :], b[:, 0:1, :]
    else:
        b_ref_row, b_end = b[:, s // 2 - 1:s // 2, :], b[:, s - 1:s, :]
    q3 = q_ref[rows, :].astype(F32).reshape(n, s, HEAD_DK)
    k3 = k_ref[rows, :].astype(F32).reshape(n, s, HEAD_DK)
    vc = v_ref[rows, :]
    qd = (q3 * jnp.exp(b - b_ref_row)).reshape(GLA_C, HEAD_DK).astype(BF16)
    kd = (k3 * jnp.exp(b_ref_row - b)).reshape(GLA_C, HEAD_DK).astype(BF16)
    ke = (k3 * jnp.exp(b_end - b)).astype(BF16)
    qe = (q3 * jnp.exp(b)).astype(BF16)
    decay = jnp.exp(b_end)
    sc = lax.dot_general(qd, kd, _NT, preferred_element_type=F32)
    sc = jnp.where(mask_ref[...] != 0.0, sc, 0.0).astype(BF16)
    o = jnp.dot(sc, vc, preferred_element_type=F32)
    st = st_ref[...]
    inter = [None] * n
    order = range(n - 1, -1, -1) if backward else range(n)
    for i in order:
        inter[i] = lax.dot_general(qe[i], st.astype(BF16), _NT, preferred_element_type=F32)
        d_st = lax.dot_general(vc[i * s:(i + 1) * s, :], ke[i], _TN,
                               preferred_element_type=F32)
        st = st * decay[i] + d_st
    st_ref[...] = st
    return o + jnp.concatenate(inter, axis=0)


def _gla_kernel(q_ref, k_ref, v_ref, g_ref, r_ref, wf_ref, wb_ref, bf_ref, bb_ref, gn_ref,
                km_ref, vm_ref, rm_ref, trif_ref, trib_ref, maskf_ref, maskb_ref,
                o_ref, ofwd_ref, st_ref):
    nc = q_ref.shape[0] // GLA_C

    la = _log_decay(rm_ref[...], wf_ref, bf_ref)
    b = _block_cumsum(trif_ref[0:SUB, 0:SUB], la)
    ke = (km_ref[...].astype(F32) * jnp.exp(b[SUB - 1:SUB, :] - b)).astype(BF16)
    st_ref[...] = lax.dot_general(vm_ref[...], ke, _TN, preferred_element_type=F32)

    def fwd_body(c, carry):
        rows = pl.ds(pl.multiple_of(c * GLA_C, GLA_C), GLA_C)
        ofwd_ref[rows, :] = _gla_chunk(q_ref, k_ref, v_ref, r_ref, wf_ref, bf_ref, trif_ref,
                                       maskf_ref, st_ref, rows, backward=False)
        return carry

    lax.fori_loop(0, nc, fwd_body, 0)

    st_ref[...] = jnp.zeros_like(st_ref)

    def bwd_body(t, carry):
        c = nc - 1 - t
        rows = pl.ds(pl.multiple_of(c * GLA_C, GLA_C), GLA_C)
        o = ofwd_ref[rows, :] + _gla_chunk(q_ref, k_ref, v_ref, r_ref, wb_ref, bb_ref,
                                           trib_ref, maskb_ref, st_ref, rows, backward=True)
        g = g_ref[rows, :].astype(F32)
        o_ref[rows, :] = (_rms(o, gn_ref[...]) * (g * _sigmoid(g))).astype(BF16)
        return carry

    lax.fori_loop(0, nc, bwd_body, 0)


def _gla_call(q, k, v, g, r, wdec, bdec, gn, km, vm, rm, trif, trib, maskf, maskb, seq, batch):
    h = GLA_HEADS
    blk = lambda n: pl.BlockSpec((seq, n), lambda b, hh: (b, hh))
    full = lambda a: pl.BlockSpec(a.shape, lambda b, hh: (0,) * a.ndim)
    in_specs = [
        blk(HEAD_DK), blk(HEAD_DK), blk(HEAD_DV), blk(HEAD_DV),
        pl.BlockSpec((seq, R_PAD), lambda b, hh: (b, 0)),
        pl.BlockSpec((R_PAD, HEAD_DK), lambda b, hh: (0, hh)),
        pl.BlockSpec((R_PAD, HEAD_DK), lambda b, hh: (0, h + hh)),
        pl.BlockSpec((1, HEAD_DK), lambda b, hh: (0, hh)),
        pl.BlockSpec((1, HEAD_DK), lambda b, hh: (0, h + hh)),
        full(gn),
        pl.BlockSpec((SUB, HEAD_DK), lambda b, hh: (0, hh)),
        pl.BlockSpec((SUB, HEAD_DV), lambda b, hh: (0, hh)),
        full(rm), full(trif), full(trib), full(maskf), full(maskb),
    ]
    return pl.pallas_call(
        _gla_kernel,
        grid=(batch, h),
        in_specs=in_specs,
        out_specs=blk(HEAD_DV),
        out_shape=jax.ShapeDtypeStruct((batch * seq, GLA_DV), BF16),
        scratch_shapes=[pltpu.VMEM((seq, HEAD_DV), F32), pltpu.VMEM((HEAD_DV, HEAD_DK), F32)],
        compiler_params=pltpu.CompilerParams(
            dimension_semantics=("arbitrary", "arbitrary"), vmem_limit_bytes=V7X_VMEM_LIMIT),
        name="gla",
    )(q, k, v, g, r, wdec, wdec, bdec, bdec, gn, km, vm, rm, trif, trib, maskf, maskb)


def _dft_kernel(p_ref, q_ref, ec_ref, es_ref, rowc_ref, rows_ref, colc_ref, cols_ref,
                mt_ref, pqm_ref, y_ref, acc_ref, tc_ref, ts_ref, *, scale):
    j = pl.program_id(1)

    @pl.when(j == 0)
    def _():
        acc_ref[...] = jnp.dot(mt_ref[...].astype(BF16), pqm_ref[...],
                               preferred_element_type=F32)

    cc, cs = colc_ref[0], cols_ref[0]
    for cb in range(DFT_T // LANE):
        sl = slice(cb * LANE, (cb + 1) * LANE)
        rc, rs = rowc_ref[0, 0:1, sl], rows_ref[0, 0:1, sl]
        wc = cc * rc - cs * rs
        ws = cs * rc + cc * rs
        ec, es = ec_ref[:, sl], es_ref[:, sl]
        tc_ref[:, sl] = (ec * wc - es * ws).astype(BF16)
        ts_ref[:, sl] = (-(es * wc + ec * ws)).astype(BF16)
    acc_ref[...] += (jnp.dot(tc_ref[...], p_ref[...], preferred_element_type=F32)
                     + jnp.dot(ts_ref[...], q_ref[...], preferred_element_type=F32))

    @pl.when(j == pl.num_programs(1) - 1)
    def _():
        y_ref[...] = (acc_ref[...] * scale).astype(BF16)


@functools.lru_cache(maxsize=None)
def _dft_tables(seq, n_meta):
    big_l = seq + n_meta
    t = DFT_T
    nt = seq // t
    ang = lambda m: 2.0 * np.pi * (m % big_l).astype(np.float64) / big_l
    d = np.arange(t, dtype=np.int64)
    e = ang(d[:, None] * d[None, :])
    k0 = n_meta + t * np.arange(nt, dtype=np.int64)
    row = ang(k0[:, None, None] * d[None, None, :] + (k0[:, None] * k0[None, :])[:, :, None])
    row = np.broadcast_to(row.reshape(nt * nt, 1, t), (nt * nt, 8, t))
    col = ang(d[None, :] * k0[:, None])
    col = np.broadcast_to(col[:, :, None], (nt, t, LANE))
    kk = n_meta + np.arange(seq, dtype=np.int64)
    mang = ang(kk[:, None] * np.arange(n_meta, dtype=np.int64)[None, :])
    mt = np.concatenate([np.cos(mang), -np.sin(mang)], axis=1)
    f = lambda a: np.ascontiguousarray(a, dtype=np.float32)
    return (f(np.cos(e)), f(np.sin(e)), f(np.cos(row)), f(np.sin(row)),
            f(np.cos(col)), f(np.sin(col)), f(mt))


def _dft_call(p, q, pqm, seq, n_meta):
    t = DFT_T
    nt = seq // t
    ncol = p.shape[1]
    ec, es, rowc, rows, colc, cols, mt = _dft_tables(seq, n_meta)
    scale = float((seq + n_meta) * FNET_GROUP_DIM) ** -0.5
    rhs = pl.BlockSpec((t, ncol), lambda i, j: (j, 0))
    rowspec = pl.BlockSpec((1, 8, t), lambda i, j: (i * nt + j, 0, 0))
    colspec = pl.BlockSpec((1, t, LANE), lambda i, j: (j, 0, 0))
    return pl.pallas_call(
        functools.partial(_dft_kernel, scale=scale),
        grid=(nt, nt),
        in_specs=[rhs, rhs, _resident((t, t)), _resident((t, t)), rowspec, rowspec,
                  colspec, colspec,
                  pl.BlockSpec((t, 2 * n_meta), lambda i, j: (i, 0)),
                  _resident(pqm.shape)],
        out_specs=pl.BlockSpec((t, ncol), lambda i, j: (i, 0)),
        out_shape=jax.ShapeDtypeStruct((seq, ncol), BF16),
        scratch_shapes=[pltpu.VMEM((t, ncol), F32), pltpu.VMEM((t, t), BF16),
                        pltpu.VMEM((t, t), BF16)],
        compiler_params=pltpu.CompilerParams(
            dimension_semantics=("arbitrary", "arbitrary"), vmem_limit_bytes=V7X_VMEM_LIMIT),
        name="seq_dft",
    )(p, q, jnp.asarray(ec), jnp.asarray(es), jnp.asarray(rowc), jnp.asarray(rows),
      jnp.asarray(colc), jnp.asarray(cols), jnp.asarray(mt), pqm)


def _merge_kernel(x_ref, og_ref, y_ref, ga_ref, gf_ref, wo_ref, wf_ref, wout_ref, h_ref):
    y_gla = jnp.dot(og_ref[...], wo_ref[...], preferred_element_type=F32)
    y_fnet = jnp.dot(y_ref[...], wf_ref[...], preferred_element_type=F32)
    merged = (_sigmoid(ga_ref[...].astype(F32)) * y_gla
              + _sigmoid(gf_ref[...].astype(F32)) * y_fnet)
    h_ref[...] = x_ref[...] + jnp.dot(merged.astype(BF16), wout_ref[...],
                                      preferred_element_type=F32)


def _merge_call(x2d, og, y, ga, gf, wo, wf, wout, seq, tm):
    m = x2d.shape[0]
    nt = seq // tm
    row = lambda n: pl.BlockSpec((tm, n), lambda i: (i, 0))
    return pl.pallas_call(
        _merge_kernel,
        grid=(m // tm,),
        in_specs=[row(D_MODEL), row(GLA_DV),
                  pl.BlockSpec((tm, FNET_WIDTH), lambda i: (i % nt, i // nt)),
                  row(D_MODEL), row(D_MODEL),
                  _resident(wo.shape), _resident(wf.shape), _resident(wout.shape)],
        out_specs=row(D_MODEL),
        out_shape=jax.ShapeDtypeStruct((m, D_MODEL), F32),
        compiler_params=pltpu.CompilerParams(
            dimension_semantics=("arbitrary",), vmem_limit_bytes=V7X_VMEM_LIMIT),
        name="merge",
    )(x2d, og, y, ga, gf, wo, wf, wout)


FFN_CHUNK = 1024


def _ffn_kernel(h_ref, nw_ref, w1_ref, w2_ref, nf_ref, o_ref):
    h = h_ref[...]
    u = _rms(h, nw_ref[...]).astype(BF16)
    acc = h
    for c in range(D_FF // FFN_CHUNK):
        sl = slice(c * FFN_CHUNK, (c + 1) * FFN_CHUNK)
        a = jnp.maximum(jnp.dot(u, w1_ref[:, sl], preferred_element_type=F32), 0.0)
        acc = acc + jnp.dot((a * a).astype(BF16), w2_ref[sl, :], preferred_element_type=F32)
    o_ref[...] = _rms(acc, nf_ref[...])


def _ffn_call(h, nw, w1, w2, nf, tm):
    m = h.shape[0]
    row = pl.BlockSpec((tm, D_MODEL), lambda i: (i, 0))
    return pl.pallas_call(
        _ffn_kernel,
        grid=(m // tm,),
        in_specs=[row, _resident((1, D_MODEL)), _resident(w1.shape), _resident(w2.shape),
                  _resident((1, D_MODEL))],
        out_specs=row,
        out_shape=jax.ShapeDtypeStruct((m, D_MODEL), F32),
        compiler_params=pltpu.CompilerParams(
            dimension_semantics=("arbitrary",), vmem_limit_bytes=V7X_VMEM_LIMIT),
        name="ffn",
    )(h, nw, w1, w2, nf)


@functools.lru_cache(maxsize=None)
def _gla_constants():
    idx = np.arange(GLA_C)
    same = (idx[:, None] // SUB) == (idx[None, :] // SUB)
    tri_f = (same & (idx[None, :] <= idx[:, None])).astype(np.float32)
    return tri_f, np.ascontiguousarray(tri_f.T)


@functools.lru_cache(maxsize=None)
def _channel_dft():
    d = np.arange(FNET_GROUP_DIM, dtype=np.int64)
    ang = 2.0 * np.pi * ((d[:, None] * d[None, :]) % FNET_GROUP_DIM) / FNET_GROUP_DIM
    return np.concatenate([np.cos(ang), np.sin(ang)], axis=1).astype(np.float32)


def kernel(x, meta_tokens, norm_mix, w_in, w_decay_fwd, b_decay_fwd, w_decay_bwd, b_decay_bwd,
           gla_norm, w_o_gla, w_fnet, w_out, norm_ffn, w_ff1, w_ff2, norm_final):
    batch, seq, d = x.shape
    assert d == D_MODEL and meta_tokens.shape == (N_META, D_MODEL)
    assert w_in.shape[0] == 1 and seq % GLA_C == 0 and seq % DFT_T == 0
    tm = 512

    wi = w_in[0]
    splits = np.cumsum([GLA_DK, GLA_DK, GLA_DV, GLA_DV, DECAY_RANK, DECAY_RANK, FNET_WIDTH,
                        D_MODEL])
    wq, wk, wv, wg, wrf, wrb, wf_in, wga, wgf = jnp.split(wi, [int(s) for s in splits], axis=1)
    w_main = jnp.concatenate([wq, wk, wv, wg, wf_in, wga, wgf], axis=1).astype(BF16)
    w_r = jnp.concatenate(
        [wrf, wrb, jnp.zeros((D_MODEL, R_PAD - 2 * DECAY_RANK), F32)], axis=1).astype(BF16)
    wdec = jnp.zeros((R_PAD, 2 * GLA_DK), F32)
    wdec = wdec.at[:DECAY_RANK, :GLA_DK].set(w_decay_fwd[0])
    wdec = wdec.at[DECAY_RANK:2 * DECAY_RANK, GLA_DK:].set(w_decay_bwd[0]).astype(BF16)
    bdec = jnp.concatenate([b_decay_fwd[0], b_decay_bwd[0]])[None, :]
    cs = jnp.asarray(_channel_dft())
    nw_mix = norm_mix[0][None, :]

    x2d = x.reshape(batch * seq, D_MODEL)
    q, k, v, g, r, p, pq_q, ga, gf = _proj_call(x2d, nw_mix, w_main, w_r, cs, seq, tm)
    _, km, vm, _, rm, pm, qm, _, _ = _proj_call(meta_tokens, nw_mix, w_main, w_r, cs,
                                                N_META, N_META)

    lead = ((SUB - N_META, 0), (0, 0))
    km, vm, rm = jnp.pad(km, lead), jnp.pad(vm, lead), jnp.pad(rm, lead)
    tri_f, tri_b = _gla_constants()
    og = _gla_call(q, k, v, g, r, wdec, bdec, gla_norm[0][None, :], km, vm, rm,
                   jnp.asarray(tri_f, dtype=BF16), jnp.asarray(tri_b, dtype=BF16),
                   jnp.asarray(tri_f), jnp.asarray(tri_b), seq, batch)

    pqm = jnp.concatenate([jnp.tile(pm, (1, batch)), jnp.tile(qm, (1, batch))], axis=0)
    y = _dft_call(p, pq_q, pqm, seq, N_META)

    h1 = _merge_call(x2d, og, y, ga, gf, w_o_gla[0].astype(BF16), w_fnet[0].astype(BF16),
                     w_out[0].astype(BF16), seq, tm)
    out = _ffn_call(h1, norm_ffn[0][None, :], w_ff1[0].astype(BF16), w_ff2[0].astype(BF16),
                    norm_final[None, :], tm)
    return out.reshape(batch, seq, D_MODEL)
```

```python
import functools

import numpy as np
import jax
import jax.numpy as jnp
from jax import lax
from jax.experimental import pallas as pl
from jax.experimental.pallas import tpu as pltpu

F32 = jnp.float32
BF16 = jnp.bfloat16

D_MODEL = 1024
N_META = 16
GLA_HEADS = 4
GLA_DK = 512
GLA_DV = 1024
HEAD_DK = GLA_DK // GLA_HEADS
HEAD_DV = GLA_DV // GLA_HEADS
DECAY_RANK = 16
GATE_TEMP = 16.0
SUB = 64
FNET_GROUPS = 4
FNET_WIDTH = 512
FNET_GROUP_DIM = FNET_WIDTH // FNET_GROUPS
D_FF = 4 * D_MODEL
EPS = 1e-6
LOG2E = 1.4426950408889634

_OFF_Q, _OFF_K, _OFF_V, _OFF_G, _OFF_F, _OFF_GA, _OFF_GF, _OFF_END = (
    0, 512, 1024, 2048, 3072, 3584, 4608, 5632)
R_PAD = 128

GLA_C = 4 * SUB
GLA_TM = 512

DFT_T = 512
LANE = 128
V7X_VMEM_LIMIT = 56 * 1024 * 1024


def _rms(x, w):
    ms = jnp.mean(x * x, axis=-1, keepdims=True)
    return x * lax.rsqrt(ms + EPS) * w


def _sigmoid(x):
    return 1.0 / (1.0 + jnp.exp(-x))


def _resident(shape):
    nd = len(shape)
    return pl.BlockSpec(shape, lambda *_: (0,) * nd, pipeline_mode=pl.Buffered(1))


def _proj_kernel(x_ref, nw_ref, w_ref, wr_ref, cs_ref,
                 q_ref, k_ref, v_ref, g_ref, r_ref, p_ref, qq_ref, ga_ref, gf_ref):
    xn = _rms(x_ref[...], nw_ref[...]).astype(BF16)

    def mm(lo, hi):
        return jnp.dot(xn, w_ref[:, lo:hi], preferred_element_type=F32)

    q_ref[...] = (mm(_OFF_Q, _OFF_K) * (HEAD_DK ** -0.5)).astype(BF16)
    k_ref[...] = mm(_OFF_K, _OFF_V).astype(BF16)
    v_ref[...] = mm(_OFF_V, _OFF_G).astype(BF16)
    g_ref[...] = mm(_OFF_G, _OFF_F).astype(BF16)
    ga_ref[...] = mm(_OFF_GA, _OFF_GF).astype(BF16)
    gf_ref[...] = mm(_OFF_GF, _OFF_END).astype(BF16)
    r_ref[...] = jnp.dot(xn, wr_ref[...], preferred_element_type=F32).astype(BF16)
    f = mm(_OFF_F, _OFF_GA).astype(BF16)
    cs = cs_ref[...].astype(BF16)
    for grp in range(FNET_GROUPS):
        sl = slice(grp * FNET_GROUP_DIM, (grp + 1) * FNET_GROUP_DIM)
        pq = jnp.dot(f[:, sl], cs, preferred_element_type=F32)
        p_ref[:, sl] = pq[:, :FNET_GROUP_DIM].astype(BF16)
        qq_ref[:, sl] = pq[:, FNET_GROUP_DIM:].astype(BF16)


def _proj_call(x2d, nw, w_main, w_r, cs, rows_per_batch, tm):
    m = x2d.shape[0]
    nt = rows_per_batch // tm
    nb = m // rows_per_batch
    row = lambda n: pl.BlockSpec((tm, n), lambda i: (i, 0))
    pq_spec = pl.BlockSpec((tm, FNET_WIDTH), lambda i: (i % nt, i // nt))
    out_shape = (
        jax.ShapeDtypeStruct((m, GLA_DK), BF16), jax.ShapeDtypeStruct((m, GLA_DK), BF16),
        jax.ShapeDtypeStruct((m, GLA_DV), BF16), jax.ShapeDtypeStruct((m, GLA_DV), BF16),
        jax.ShapeDtypeStruct((m, R_PAD), BF16),
        jax.ShapeDtypeStruct((rows_per_batch, nb * FNET_WIDTH), BF16),
        jax.ShapeDtypeStruct((rows_per_batch, nb * FNET_WIDTH), BF16),
        jax.ShapeDtypeStruct((m, D_MODEL), BF16), jax.ShapeDtypeStruct((m, D_MODEL), BF16),
    )
    return pl.pallas_call(
        _proj_kernel,
        grid=(m // tm,),
        in_specs=[row(D_MODEL), _resident((1, D_MODEL)), _resident(w_main.shape),
                  _resident(w_r.shape), _resident(cs.shape)],
        out_specs=(row(GLA_DK), row(GLA_DK), row(GLA_DV), row(GLA_DV), row(R_PAD),
                   pq_spec, pq_spec, row(D_MODEL), row(D_MODEL)),
        out_shape=out_shape,
        compiler_params=pltpu.CompilerParams(
            dimension_semantics=("arbitrary",), vmem_limit_bytes=V7X_VMEM_LIMIT),
        name="proj",
    )(x2d, nw, w_main, w_r, cs)


_NT = (((1,), (1,)), ((), ()))
_TN = (((0,), (0,)), ((), ()))


def _gla_local_kernel(q_ref, k_ref, v_ref, r_ref, wdec_ref, bdec_ref, trif_ref, trib_ref,
                      maskf_ref, maskb_ref, ol_ref, qe_ref, ke_ref, bend_ref, qd_ref, kd_ref,
                      *, chunk):
    tm = q_ref.shape[0]
    n, s = chunk // SUB, SUB
    x = jnp.dot(r_ref[...], wdec_ref[...], preferred_element_type=F32) + bdec_ref[...]
    la = (jnp.minimum(x, 0.0) - jnp.log(1.0 + jnp.exp(-jnp.abs(x)))) * (LOG2E / GATE_TEMP)
    hi = la.astype(BF16)
    lo = (la - hi.astype(F32)).astype(BF16)
    tris = (trif_ref[...], trib_ref[...])
    for c in range(tm // chunk):
        rows = slice(c * chunk, (c + 1) * chunk)
        q3 = q_ref[rows, :].astype(F32).reshape(n, s, GLA_DK)
        k3 = k_ref[rows, :].astype(F32).reshape(n, s, GLA_DK)
        for d in range(2):
            cols = slice(d * GLA_DK, (d + 1) * GLA_DK)
            b = (jnp.dot(tris[d], hi[rows, cols], preferred_element_type=F32)
                 + jnp.dot(tris[d], lo[rows, cols], preferred_element_type=F32))
            b = b.reshape(n, s, GLA_DK)
            if d == 0:
                b_mid, b_end = b[:, s // 2 - 1:s // 2, :], b[:, s - 1:s, :]
            else:
                b_mid, b_end = b[:, s // 2:s // 2 + 1, :], b[:, 0:1, :]
            qd = q3 * jnp.exp2(b - b_mid)
            kd = k3 * jnp.exp2(b_mid - b)
            qe = qd * jnp.exp2(b_mid)
            ke = kd * jnp.exp2(b_end - b_mid)
            qd_ref[d, rows, :] = qd.reshape(chunk, GLA_DK).astype(BF16)
            kd_ref[d, rows, :] = kd.reshape(chunk, GLA_DK).astype(BF16)
            qe_ref[rows, cols] = qe.reshape(chunk, GLA_DK).astype(BF16)
            ke_ref[rows, cols] = ke.reshape(chunk, GLA_DK).astype(BF16)
            bend_ref[c * n:(c + 1) * n, cols] = b_end.reshape(n, GLA_DK)
    masks = (maskf_ref[...] != 0.0, maskb_ref[...] != 0.0)
    for c in range(tm // chunk):
        rows = slice(c * chunk, (c + 1) * chunk)
        for h in range(GLA_HEADS):
            hk = slice(h * HEAD_DK, (h + 1) * HEAD_DK)
            hv = slice(h * HEAD_DV, (h + 1) * HEAD_DV)
            sc = [lax.dot_general(qd_ref[d, rows, hk], kd_ref[d, rows, hk], _NT,
                                  preferred_element_type=F32) for d in range(2)]
            sc = (jnp.where(masks[0], sc[0], 0.0) + jnp.where(masks[1], sc[1], 0.0)).astype(BF16)
            ol_ref[rows, hv] = jnp.dot(sc, v_ref[rows, hv], preferred_element_type=F32)


def _gla_local_call(q, k, v, r, wdec, bdec, tri_f, tri_b, tm, chunk):
    m = q.shape[0]
    row = lambda n: pl.BlockSpec((tm, n), lambda i: (i, 0))
    tri_bf = (jnp.asarray(tri_f).astype(BF16), jnp.asarray(tri_b).astype(BF16))
    masks = (jnp.asarray(tri_f), jnp.asarray(tri_b))
    sq = _resident((chunk, chunk))
    return pl.pallas_call(
        functools.partial(_gla_local_kernel, chunk=chunk),
        grid=(m // tm,),
        in_specs=[row(GLA_DK), row(GLA_DK), row(GLA_DV), row(R_PAD), _resident(wdec.shape),
                  _resident(bdec.shape), sq, sq, sq, sq],
        out_specs=(row(GLA_DV), row(2 * GLA_DK), row(2 * GLA_DK),
                   pl.BlockSpec((tm // SUB, 2 * GLA_DK), lambda i: (i, 0))),
        out_shape=(jax.ShapeDtypeStruct((m, GLA_DV), F32),
                   jax.ShapeDtypeStruct((m, 2 * GLA_DK), BF16),
                   jax.ShapeDtypeStruct((m, 2 * GLA_DK), BF16),
                   jax.ShapeDtypeStruct((m // SUB, 2 * GLA_DK), F32)),
        scratch_shapes=[pltpu.VMEM((2, tm, GLA_DK), BF16), pltpu.VMEM((2, tm, GLA_DK), BF16)],
        compiler_params=pltpu.CompilerParams(
            dimension_semantics=("arbitrary",), vmem_limit_bytes=V7X_VMEM_LIMIT),
        name="gla_local",
    )(q, k, v, r, wdec, bdec, tri_bf[0], tri_bf[1], masks[0], masks[1])


SCAN_G = 512
SCAN_RB = 1024


def _scan_groups(dirs):
    s, n = SUB, SCAN_G // SUB
    inc, dec, st, inter = {}, [], [], [{}, {}]
    for d, (qe_ref, ke_ref, v_ref, bend_ref, st_ref, rows, crows, _) in enumerate(dirs):
        dec.append(jnp.exp2(bend_ref[crows, :]))
        st.append(st_ref[...])
        for i in range(n):
            sub = pl.ds(pl.multiple_of(rows.start + i * s, s), s)
            inc[d, i] = lax.dot_general(v_ref[sub, :], ke_ref[sub, :], _TN,
                                        preferred_element_type=F32)
    for step in range(n):
        for d, (qe_ref, _, _, _, _, rows, _, backward) in enumerate(dirs):
            i = n - 1 - step if backward else step
            sub = pl.ds(pl.multiple_of(rows.start + i * s, s), s)
            inter[d][i] = lax.dot_general(qe_ref[sub, :], st[d].astype(BF16), _NT,
                                          preferred_element_type=F32)
            st[d] = st[d] * dec[d][i:i + 1, :] + inc[d, i]
    for d, spec in enumerate(dirs):
        spec[4][...] = st[d]
    return [jnp.concatenate([inter[d][i] for i in range(n)], axis=0) for d in range(2)]


def _gla_scan_kernel(qef_ref, kef_ref, vf_ref, gf_ref, olf_ref, bendf_ref,
                     qeb_ref, keb_ref, vb_ref, gb_ref, bendb_ref,
                     gn_ref, kem_ref, vm_ref, o_ref, oacc_ref, stf_ref, stb_ref):
    t, nt = pl.program_id(2), pl.num_programs(2)
    rb = qef_ref.shape[0]
    ng = rb // SCAN_G

    @pl.when(t == 0)
    def _():
        stf_ref[...] = lax.dot_general(vm_ref[...], kem_ref[...], _TN,
                                       preferred_element_type=F32)
        stb_ref[...] = jnp.zeros_like(stb_ref)

    def finish(o, g_ref, rows, grows):
        g = g_ref[rows, :].astype(F32)
        o_ref[grows, :] = (_rms(o, gn_ref[...]) * (g * _sigmoid(g))).astype(BF16)

    def run(second):
        def body(j, carry):
            jb = ng - 1 - j
            rows_f = pl.ds(pl.multiple_of(j * SCAN_G, SCAN_G), SCAN_G)
            rows_b = pl.ds(pl.multiple_of(jb * SCAN_G, SCAN_G), SCAN_G)
            cr_f = pl.ds(pl.multiple_of(j * (SCAN_G // SUB), 8), SCAN_G // SUB)
            cr_b = pl.ds(pl.multiple_of(jb * (SCAN_G // SUB), 8), SCAN_G // SUB)
            in_f, in_b = _scan_groups([
                (qef_ref, kef_ref, vf_ref, bendf_ref, stf_ref, rows_f, cr_f, False),
                (qeb_ref, keb_ref, vb_ref, bendb_ref, stb_ref, rows_b, cr_b, True)])
            grows_f = pl.ds(pl.multiple_of(t * rb + j * SCAN_G, SCAN_G), SCAN_G)
            grows_b = pl.ds(pl.multiple_of((nt - 1 - t) * rb + jb * SCAN_G, SCAN_G), SCAN_G)
            tot_f = olf_ref[rows_f, :] + in_f
            tot_b = in_b
            if second:
                finish(oacc_ref[grows_f, :] + tot_f, gf_ref, rows_f, grows_f)
                finish(oacc_ref[grows_b, :] + tot_b, gb_ref, rows_b, grows_b)
            else:
                oacc_ref[grows_f, :] = tot_f
                oacc_ref[grows_b, :] = tot_b
            return carry

        lax.fori_loop(0, ng, body, 0)

    pl.when(t < nt // 2)(lambda: run(False))
    pl.when(t >= nt // 2)(lambda: run(True))


def _gla_scan_call(qe, ke, v, g, ol, bend, gn, kem, vm, seq, batch):
    h, rb = GLA_HEADS, SCAN_RB
    nt = seq // rb
    fwd = lambda b, hh, t: b * nt + t
    bwd = lambda b, hh, t: b * nt + nt - 1 - t
    cb = rb // SUB

    def specs(rowfn, dircol, with_ol):
        dk_blk = lambda rows: pl.BlockSpec(
            (rows, HEAD_DK), lambda b, hh, t: (rowfn(b, hh, t), dircol + hh))
        dv_blk = pl.BlockSpec((rb, HEAD_DV), lambda b, hh, t: (rowfn(b, hh, t), hh))
        return [dk_blk(rb), dk_blk(rb), dv_blk, dv_blk] + [dv_blk] * with_ol + [dk_blk(cb)]

    in_specs = specs(fwd, 0, True) + specs(bwd, h, False) + [
        pl.BlockSpec(gn.shape, lambda b, hh, t: (0, 0)),
        pl.BlockSpec((SUB, HEAD_DK), lambda b, hh, t: (0, hh)),
        pl.BlockSpec((SUB, HEAD_DV), lambda b, hh, t: (0, hh)),
    ]
    return pl.pallas_call(
        _gla_scan_kernel,
        grid=(batch, h, nt),
        in_specs=in_specs,
        out_specs=pl.BlockSpec((seq, HEAD_DV), lambda b, hh, t: (b, hh)),
        out_shape=jax.ShapeDtypeStruct((batch * seq, GLA_DV), BF16),
        scratch_shapes=[pltpu.VMEM((seq, HEAD_DV), F32), pltpu.VMEM((HEAD_DV, HEAD_DK), F32),
                        pltpu.VMEM((HEAD_DV, HEAD_DK), F32)],
        compiler_params=pltpu.CompilerParams(
            dimension_semantics=("arbitrary", "arbitrary", "arbitrary"),
            vmem_limit_bytes=V7X_VMEM_LIMIT),
        name="gla_scan",
    )(qe, ke, v, g, ol, bend, qe, ke, v, g, bend, gn, kem, vm)


def _dft_kernel(p_ref, q_ref, ec_ref, es_ref, rowc_ref, rows_ref, colc_ref, cols_ref,
                mt_ref, pqm_ref, y_ref, acc_ref, tc_ref, ts_ref, *, scale):
    j = pl.program_id(1)

    @pl.when(j == 0)
    def _():
        acc_ref[...] = jnp.dot(mt_ref[...].astype(BF16), pqm_ref[...],
                               preferred_element_type=F32)

    cc, cs = colc_ref[0], cols_ref[0]
    for cb in range(DFT_T // LANE):
        sl = slice(cb * LANE, (cb + 1) * LANE)
        rc, rs = rowc_ref[0, 0:1, sl], rows_ref[0, 0:1, sl]
        wc = cc * rc - cs * rs
        ws = cs * rc + cc * rs
        ec, es = ec_ref[:, sl], es_ref[:, sl]
        tc_ref[:, sl] = (ec * wc - es * ws).astype(BF16)
        ts_ref[:, sl] = (-(es * wc + ec * ws)).astype(BF16)
    acc_ref[...] += (jnp.dot(tc_ref[...], p_ref[...], preferred_element_type=F32)
                     + jnp.dot(ts_ref[...], q_ref[...], preferred_element_type=F32))

    @pl.when(j == pl.num_programs(1) - 1)
    def _():
        y_ref[...] = (acc_ref[...] * scale).astype(BF16)


@functools.lru_cache(maxsize=None)
def _dft_tables(seq, n_meta):
    big_l = seq + n_meta
    t = DFT_T
    nt = seq // t
    ang = lambda m: 2.0 * np.pi * (m % big_l).astype(np.float64) / big_l
    d = np.arange(t, dtype=np.int64)
    e = ang(d[:, None] * d[None, :])
    k0 = n_meta + t * np.arange(nt, dtype=np.int64)
    row = ang(k0[:, None, None] * d[None, None, :] + (k0[:, None] * k0[None, :])[:, :, None])
    row = np.broadcast_to(row.reshape(nt * nt, 1, t), (nt * nt, 8, t))
    col = ang(d[None, :] * k0[:, None])
    col = np.broadcast_to(col[:, :, None], (nt, t, LANE))
    kk = n_meta + np.arange(seq, dtype=np.int64)
    mang = ang(kk[:, None] * np.arange(n_meta, dtype=np.int64)[None, :])
    mt = np.concatenate([np.cos(mang), -np.sin(mang)], axis=1)
    f = lambda a: np.ascontiguousarray(a, dtype=np.float32)
    return (f(np.cos(e)), f(np.sin(e)), f(np.cos(row)), f(np.sin(row)),
            f(np.cos(col)), f(np.sin(col)), f(mt))


def _dft_call(p, q, pqm, seq, n_meta):
    t = DFT_T
    nt = seq // t
    ncol = p.shape[1]
    ec, es, rowc, rows, colc, cols, mt = _dft_tables(seq, n_meta)
    scale = float((seq + n_meta) * FNET_GROUP_DIM) ** -0.5
    rhs = pl.BlockSpec((t, ncol), lambda i, j: (j, 0))
    rowspec = pl.BlockSpec((1, 8, t), lambda i, j: (i * nt + j, 0, 0))
    colspec = pl.BlockSpec((1, t, LANE), lambda i, j: (j, 0, 0))
    return pl.pallas_call(
        functools.partial(_dft_kernel, scale=scale),
        grid=(nt, nt),
        in_specs=[rhs, rhs, _resident((t, t)), _resident((t, t)), rowspec, rowspec,
                  colspec, colspec,
                  pl.BlockSpec((t, 2 * n_meta), lambda i, j: (i, 0)),
                  _resident(pqm.shape)],
        out_specs=pl.BlockSpec((t, ncol), lambda i, j: (i, 0)),
        out_shape=jax.ShapeDtypeStruct((seq, ncol), BF16),
        scratch_shapes=[pltpu.VMEM((t, ncol), F32), pltpu.VMEM((t, t), BF16),
                        pltpu.VMEM((t, t), BF16)],
        compiler_params=pltpu.CompilerParams(
            dimension_semantics=("arbitrary", "arbitrary"), vmem_limit_bytes=V7X_VMEM_LIMIT),
        name="seq_dft",
    )(p, q, jnp.asarray(ec), jnp.asarray(es), jnp.asarray(rowc), jnp.asarray(rows),
      jnp.asarray(colc), jnp.asarray(cols), jnp.asarray(mt), pqm)


def _merge_kernel(x_ref, og_ref, y_ref, ga_ref, gf_ref, wo_ref, wf_ref, wout_ref, h_ref):
    y_gla = jnp.dot(og_ref[...], wo_ref[...], preferred_element_type=F32)
    y_fnet = jnp.dot(y_ref[...], wf_ref[...], preferred_element_type=F32)
    merged = (_sigmoid(ga_ref[...].astype(F32)) * y_gla
              + _sigmoid(gf_ref[...].astype(F32)) * y_fnet)
    h_ref[...] = x_ref[...] + jnp.dot(merged.astype(BF16), wout_ref[...],
                                      preferred_element_type=F32)


def _merge_call(x2d, og, y, ga, gf, wo, wf, wout, seq, tm):
    m = x2d.shape[0]
    nt = seq // tm
    row = lambda n: pl.BlockSpec((tm, n), lambda i: (i, 0))
    return pl.pallas_call(
        _merge_kernel,
        grid=(m // tm,),
        in_specs=[row(D_MODEL), row(GLA_DV),
                  pl.BlockSpec((tm, FNET_WIDTH), lambda i: (i % nt, i // nt)),
                  row(D_MODEL), row(D_MODEL),
                  _resident(wo.shape), _resident(wf.shape), _resident(wout.shape)],
        out_specs=row(D_MODEL),
        out_shape=jax.ShapeDtypeStruct((m, D_MODEL), F32),
        compiler_params=pltpu.CompilerParams(
            dimension_semantics=("arbitrary",), vmem_limit_bytes=V7X_VMEM_LIMIT),
        name="merge",
    )(x2d, og, y, ga, gf, wo, wf, wout)


FFN_CHUNK = 1024


def _ffn_kernel(h_ref, nw_ref, w1_ref, w2_ref, nf_ref, o_ref):
    h = h_ref[...]
    u = _rms(h, nw_ref[...]).astype(BF16)
    acc = h
    for c in range(D_FF // FFN_CHUNK):
        sl = slice(c * FFN_CHUNK, (c + 1) * FFN_CHUNK)
        a = jnp.maximum(jnp.dot(u, w1_ref[:, sl], preferred_element_type=F32), 0.0)
        acc = acc + jnp.dot((a * a).astype(BF16), w2_ref[sl, :], preferred_element_type=F32)
    o_ref[...] = _rms(acc, nf_ref[...])


def _ffn_call(h, nw, w1, w2, nf, tm):
    m = h.shape[0]
    row = pl.BlockSpec((tm, D_MODEL), lambda i: (i, 0))
    return pl.pallas_call(
        _ffn_kernel,
        grid=(m // tm,),
        in_specs=[row, _resident((1, D_MODEL)), _resident(w1.shape), _resident(w2.shape),
                  _resident((1, D_MODEL))],
        out_specs=row,
        out_shape=jax.ShapeDtypeStruct((m, D_MODEL), F32),
        compiler_params=pltpu.CompilerParams(
            dimension_semantics=("arbitrary",), vmem_limit_bytes=V7X_VMEM_LIMIT),
        name="ffn",
    )(h, nw, w1, w2, nf)


@functools.lru_cache(maxsize=None)
def _gla_constants(rows):
    idx = np.arange(rows)
    same = (idx[:, None] // SUB) == (idx[None, :] // SUB)
    tri_f = (same & (idx[None, :] <= idx[:, None])).astype(np.float32)
    return tri_f, np.ascontiguousarray(tri_f.T)


@functools.lru_cache(maxsize=None)
def _channel_dft():
    d = np.arange(FNET_GROUP_DIM, dtype=np.int64)
    ang = 2.0 * np.pi * ((d[:, None] * d[None, :]) % FNET_GROUP_DIM) / FNET_GROUP_DIM
    return np.concatenate([np.cos(ang), np.sin(ang)], axis=1).astype(np.float32)


def kernel(x, meta_tokens, norm_mix, w_in, w_decay_fwd, b_decay_fwd, w_decay_bwd, b_decay_bwd,
           gla_norm, w_o_gla, w_fnet, w_out, norm_ffn, w_ff1, w_ff2, norm_final):
    batch, seq, d = x.shape
    assert d == D_MODEL and meta_tokens.shape == (N_META, D_MODEL)
    assert w_in.shape[0] == 1 and seq % (2 * SCAN_RB) == 0 and seq % DFT_T == 0
    tm = 512

    wi = w_in[0]
    splits = np.cumsum([GLA_DK, GLA_DK, GLA_DV, GLA_DV, DECAY_RANK, DECAY_RANK, FNET_WIDTH,
                        D_MODEL])
    wq, wk, wv, wg, wrf, wrb, wf_in, wga, wgf = jnp.split(wi, [int(s) for s in splits], axis=1)
    w_main = jnp.concatenate([wq, wk, wv, wg, wf_in, wga, wgf], axis=1).astype(BF16)
    w_r = jnp.concatenate(
        [wrf, wrb, jnp.zeros((D_MODEL, R_PAD - 2 * DECAY_RANK), F32)], axis=1).astype(BF16)
    wdec = jnp.zeros((R_PAD, 2 * GLA_DK), F32)
    wdec = wdec.at[:DECAY_RANK, :GLA_DK].set(w_decay_fwd[0])
    wdec = wdec.at[DECAY_RANK:2 * DECAY_RANK, GLA_DK:].set(w_decay_bwd[0]).astype(BF16)
    bdec = jnp.concatenate([b_decay_fwd[0], b_decay_bwd[0]])[None, :]
    cs = jnp.asarray(_channel_dft())
    nw_mix = norm_mix[0][None, :]

    x2d = x.reshape(batch * seq, D_MODEL)
    q, k, v, g, r, p, pq_q, ga, gf = _proj_call(x2d, nw_mix, w_main, w_r, cs, seq, tm)
    _, km, vm, _, rm, pm, qm, _, _ = _proj_call(meta_tokens, nw_mix, w_main, w_r, cs,
                                                N_META, N_META)

    lead = ((SUB - N_META, 0), (0, 0))
    km, vm, rm = jnp.pad(km, lead), jnp.pad(vm, lead), jnp.pad(rm, lead)
    ol, qe, ke, bend = _gla_local_call(q, k, v, r, wdec, bdec, *_gla_constants(GLA_C),
                                       GLA_TM, GLA_C)
    _, _, kem, _ = _gla_local_call(km, km, vm, rm, wdec, bdec, *_gla_constants(SUB), SUB, SUB)
    og = _gla_scan_call(qe, ke, v, g, ol, bend, gla_norm[0][None, :], kem, vm, seq, batch)

    pqm = jnp.concatenate([jnp.tile(pm, (1, batch)), jnp.tile(qm, (1, batch))], axis=0)
    y = _dft_call(p, pq_q, pqm, seq, N_META)

    h1 = _merge_call(x2d, og, y, ga, gf, w_o_gla[0].astype(BF16), w_fnet[0].astype(BF16),
                     w_out[0].astype(BF16), seq, tm)
    out = _ffn_call(h1, norm_ffn[0][None, :], w_ff1[0].astype(BF16), w_ff2[0].astype(BF16),
                    norm_final[None, :], tm)
    return out.reshape(batch, seq, D_MODEL)
```

```python
import functools

import numpy as np
import jax
import jax.numpy as jnp
from jax import lax
from jax.experimental import pallas as pl
from jax.experimental.pallas import tpu as pltpu

F32 = jnp.float32
BF16 = jnp.bfloat16

D_MODEL = 1024
N_META = 16
GLA_HEADS = 4
GLA_DK = 512
GLA_DV = 1024
HEAD_DK = GLA_DK // GLA_HEADS
HEAD_DV = GLA_DV // GLA_HEADS
DECAY_RANK = 16
GATE_TEMP = 16.0
SUB = 64
FNET_GROUPS = 4
FNET_WIDTH = 512
FNET_GROUP_DIM = FNET_WIDTH // FNET_GROUPS
D_FF = 4 * D_MODEL
EPS = 1e-6
LOG2E = 1.4426950408889634

_OFF_Q, _OFF_K, _OFF_V, _OFF_G, _OFF_F, _OFF_GA, _OFF_GF, _OFF_END = (
    0, 512, 1024, 2048, 3072, 3584, 4608, 5632)
R_PAD = 128

GLA_C = 4 * SUB
GLA_TM = 512

LANE = 128
V7X_VMEM_LIMIT = 56 * 1024 * 1024


def _rms(x, w):
    ms = jnp.mean(x * x, axis=-1, keepdims=True)
    return x * lax.rsqrt(ms + EPS) * w


def _sigmoid(x):
    return 1.0 / (1.0 + jnp.exp(-x))


def _resident(shape):
    nd = len(shape)
    return pl.BlockSpec(shape, lambda *_: (0,) * nd, pipeline_mode=pl.Buffered(1))


def _proj_kernel(x_ref, nw_ref, w_ref, wr_ref, cs_ref,
                 q_ref, k_ref, v_ref, g_ref, r_ref, p_ref, qq_ref, ga_ref, gf_ref):
    xn = _rms(x_ref[...], nw_ref[...]).astype(BF16)

    def mm(lo, hi):
        return jnp.dot(xn, w_ref[:, lo:hi], preferred_element_type=F32)

    q_ref[...] = (mm(_OFF_Q, _OFF_K) * (HEAD_DK ** -0.5)).astype(BF16)
    k_ref[...] = mm(_OFF_K, _OFF_V).astype(BF16)
    v_ref[...] = mm(_OFF_V, _OFF_G).astype(BF16)
    g_ref[...] = mm(_OFF_G, _OFF_F).astype(BF16)
    ga_ref[...] = mm(_OFF_GA, _OFF_GF).astype(BF16)
    gf_ref[...] = mm(_OFF_GF, _OFF_END).astype(BF16)
    r_ref[...] = jnp.dot(xn, wr_ref[...], preferred_element_type=F32).astype(BF16)
    f = mm(_OFF_F, _OFF_GA).astype(BF16)
    cs = cs_ref[...].astype(BF16)
    for grp in range(FNET_GROUPS):
        sl = slice(grp * FNET_GROUP_DIM, (grp + 1) * FNET_GROUP_DIM)
        pq = jnp.dot(f[:, sl], cs, preferred_element_type=F32)
        p_ref[:, sl] = pq[:, :FNET_GROUP_DIM].astype(BF16)
        qq_ref[:, sl] = pq[:, FNET_GROUP_DIM:].astype(BF16)


def _proj_call(x2d, nw, w_main, w_r, cs, rows_per_batch, tm):
    m = x2d.shape[0]
    nt = rows_per_batch // tm
    nb = m // rows_per_batch
    row = lambda n: pl.BlockSpec((tm, n), lambda i: (i, 0))
    pq_spec = pl.BlockSpec((tm, FNET_WIDTH), lambda i: (i % nt, i // nt))
    out_shape = (
        jax.ShapeDtypeStruct((m, GLA_DK), BF16), jax.ShapeDtypeStruct((m, GLA_DK), BF16),
        jax.ShapeDtypeStruct((m, GLA_DV), BF16), jax.ShapeDtypeStruct((m, GLA_DV), BF16),
        jax.ShapeDtypeStruct((m, R_PAD), BF16),
        jax.ShapeDtypeStruct((rows_per_batch, nb * FNET_WIDTH), BF16),
        jax.ShapeDtypeStruct((rows_per_batch, nb * FNET_WIDTH), BF16),
        jax.ShapeDtypeStruct((m, D_MODEL), BF16), jax.ShapeDtypeStruct((m, D_MODEL), BF16),
    )
    return pl.pallas_call(
        _proj_kernel,
        grid=(m // tm,),
        in_specs=[row(D_MODEL), _resident((1, D_MODEL)), _resident(w_main.shape),
                  _resident(w_r.shape), _resident(cs.shape)],
        out_specs=(row(GLA_DK), row(GLA_DK), row(GLA_DV), row(GLA_DV), row(R_PAD),
                   pq_spec, pq_spec, row(D_MODEL), row(D_MODEL)),
        out_shape=out_shape,
        compiler_params=pltpu.CompilerParams(
            dimension_semantics=("arbitrary",), vmem_limit_bytes=V7X_VMEM_LIMIT),
        name="proj",
    )(x2d, nw, w_main, w_r, cs)


_NT = (((1,), (1,)), ((), ()))
_TN = (((0,), (0,)), ((), ()))


def _gla_local_kernel(q_ref, k_ref, v_ref, r_ref, wdec_ref, bdec_ref, trif_ref, trib_ref,
                      maskf_ref, maskb_ref, ol_ref, qe_ref, ke_ref, bend_ref, qd_ref, kd_ref,
                      *, chunk):
    tm = q_ref.shape[0]
    n, s = chunk // SUB, SUB
    x = jnp.dot(r_ref[...], wdec_ref[...], preferred_element_type=F32) + bdec_ref[...]
    la = (jnp.minimum(x, 0.0) - jnp.log(1.0 + jnp.exp(-jnp.abs(x)))) * (LOG2E / GATE_TEMP)
    hi = la.astype(BF16)
    lo = (la - hi.astype(F32)).astype(BF16)
    tris = (trif_ref[...], trib_ref[...])
    for c in range(tm // chunk):
        rows = slice(c * chunk, (c + 1) * chunk)
        q3 = q_ref[rows, :].astype(F32).reshape(n, s, GLA_DK)
        k3 = k_ref[rows, :].astype(F32).reshape(n, s, GLA_DK)
        for d in range(2):
            cols = slice(d * GLA_DK, (d + 1) * GLA_DK)
            b = (jnp.dot(tris[d], hi[rows, cols], preferred_element_type=F32)
                 + jnp.dot(tris[d], lo[rows, cols], preferred_element_type=F32))
            b = b.reshape(n, s, GLA_DK)
            if d == 0:
                b_mid, b_end = b[:, s // 2 - 1:s // 2, :], b[:, s - 1:s, :]
            else:
                b_mid, b_end = b[:, s // 2:s // 2 + 1, :], b[:, 0:1, :]
            qd = q3 * jnp.exp2(b - b_mid)
            kd = k3 * jnp.exp2(b_mid - b)
            qe = qd * jnp.exp2(b_mid)
            ke = kd * jnp.exp2(b_end - b_mid)
            qd_ref[d, rows, :] = qd.reshape(chunk, GLA_DK).astype(BF16)
            kd_ref[d, rows, :] = kd.reshape(chunk, GLA_DK).astype(BF16)
            qe_ref[rows, cols] = qe.reshape(chunk, GLA_DK).astype(BF16)
            ke_ref[rows, cols] = ke.reshape(chunk, GLA_DK).astype(BF16)
            bend_ref[c * n:(c + 1) * n, cols] = b_end.reshape(n, GLA_DK)
    masks = (maskf_ref[...] != 0.0, maskb_ref[...] != 0.0)
    for c in range(tm // chunk):
        rows = slice(c * chunk, (c + 1) * chunk)
        for h in range(GLA_HEADS):
            hk = slice(h * HEAD_DK, (h + 1) * HEAD_DK)
            hv = slice(h * HEAD_DV, (h + 1) * HEAD_DV)
            sc = [lax.dot_general(qd_ref[d, rows, hk], kd_ref[d, rows, hk], _NT,
                                  preferred_element_type=F32) for d in range(2)]
            sc = (jnp.where(masks[0], sc[0], 0.0) + jnp.where(masks[1], sc[1], 0.0)).astype(BF16)
            ol_ref[rows, hv] = jnp.dot(sc, v_ref[rows, hv], preferred_element_type=F32)


def _gla_local_call(q, k, v, r, wdec, bdec, tri_f, tri_b, tm, chunk):
    m = q.shape[0]
    row = lambda n: pl.BlockSpec((tm, n), lambda i: (i, 0))
    tri_bf = (jnp.asarray(tri_f).astype(BF16), jnp.asarray(tri_b).astype(BF16))
    masks = (jnp.asarray(tri_f), jnp.asarray(tri_b))
    sq = _resident((chunk, chunk))
    return pl.pallas_call(
        functools.partial(_gla_local_kernel, chunk=chunk),
        grid=(m // tm,),
        in_specs=[row(GLA_DK), row(GLA_DK), row(GLA_DV), row(R_PAD), _resident(wdec.shape),
                  _resident(bdec.shape), sq, sq, sq, sq],
        out_specs=(row(GLA_DV), row(2 * GLA_DK), row(2 * GLA_DK),
                   pl.BlockSpec((tm // SUB, 2 * GLA_DK), lambda i: (i, 0))),
        out_shape=(jax.ShapeDtypeStruct((m, GLA_DV), F32),
                   jax.ShapeDtypeStruct((m, 2 * GLA_DK), BF16),
                   jax.ShapeDtypeStruct((m, 2 * GLA_DK), BF16),
                   jax.ShapeDtypeStruct((m // SUB, 2 * GLA_DK), F32)),
        scratch_shapes=[pltpu.VMEM((2, tm, GLA_DK), BF16), pltpu.VMEM((2, tm, GLA_DK), BF16)],
        compiler_params=pltpu.CompilerParams(
            dimension_semantics=("arbitrary",), vmem_limit_bytes=V7X_VMEM_LIMIT),
        name="gla_local",
    )(q, k, v, r, wdec, bdec, tri_bf[0], tri_bf[1], masks[0], masks[1])


SCAN_G = 512
SCAN_RB = 1024


def _scan_groups(dirs):
    s, n = SUB, SCAN_G // SUB
    inc, dec, st, inter = {}, [], [], [{}, {}]
    for d, (qe_ref, ke_ref, v_ref, bend_ref, st_ref, rows, crows, _) in enumerate(dirs):
        dec.append(jnp.exp2(bend_ref[crows, :]))
        st.append(st_ref[...])
        for i in range(n):
            sub = pl.ds(pl.multiple_of(rows.start + i * s, s), s)
            inc[d, i] = lax.dot_general(v_ref[sub, :], ke_ref[sub, :], _TN,
                                        preferred_element_type=F32)
    for step in range(n):
        for d, (qe_ref, _, _, _, _, rows, _, backward) in enumerate(dirs):
            i = n - 1 - step if backward else step
            sub = pl.ds(pl.multiple_of(rows.start + i * s, s), s)
            inter[d][i] = lax.dot_general(qe_ref[sub, :], st[d].astype(BF16), _NT,
                                          preferred_element_type=F32)
            st[d] = st[d] * dec[d][i:i + 1, :] + inc[d, i]
    for d, spec in enumerate(dirs):
        spec[4][...] = st[d]
    return [jnp.concatenate([inter[d][i] for i in range(n)], axis=0) for d in range(2)]


def _gla_scan_kernel(qef_ref, kef_ref, vf_ref, gf_ref, olf_ref, bendf_ref,
                     qeb_ref, keb_ref, vb_ref, gb_ref, bendb_ref,
                     gn_ref, kem_ref, vm_ref, o_ref, oacc_ref, stf_ref, stb_ref):
    t, nt = pl.program_id(2), pl.num_programs(2)
    rb = qef_ref.shape[0]
    ng = rb // SCAN_G

    @pl.when(t == 0)
    def _():
        stf_ref[...] = lax.dot_general(vm_ref[...], kem_ref[...], _TN,
                                       preferred_element_type=F32)
        stb_ref[...] = jnp.zeros_like(stb_ref)

    def finish(o, g_ref, rows, grows):
        g = g_ref[rows, :].astype(F32)
        o_ref[grows, :] = (_rms(o, gn_ref[...]) * (g * _sigmoid(g))).astype(BF16)

    def run(second):
        def body(j, carry):
            jb = ng - 1 - j
            rows_f = pl.ds(pl.multiple_of(j * SCAN_G, SCAN_G), SCAN_G)
            rows_b = pl.ds(pl.multiple_of(jb * SCAN_G, SCAN_G), SCAN_G)
            cr_f = pl.ds(pl.multiple_of(j * (SCAN_G // SUB), 8), SCAN_G // SUB)
            cr_b = pl.ds(pl.multiple_of(jb * (SCAN_G // SUB), 8), SCAN_G // SUB)
            in_f, in_b = _scan_groups([
                (qef_ref, kef_ref, vf_ref, bendf_ref, stf_ref, rows_f, cr_f, False),
                (qeb_ref, keb_ref, vb_ref, bendb_ref, stb_ref, rows_b, cr_b, True)])
            grows_f = pl.ds(pl.multiple_of(t * rb + j * SCAN_G, SCAN_G), SCAN_G)
            grows_b = pl.ds(pl.multiple_of((nt - 1 - t) * rb + jb * SCAN_G, SCAN_G), SCAN_G)
            tot_f = olf_ref[rows_f, :] + in_f
            tot_b = in_b
            if second:
                finish(oacc_ref[grows_f, :] + tot_f, gf_ref, rows_f, grows_f)
                finish(oacc_ref[grows_b, :] + tot_b, gb_ref, rows_b, grows_b)
            else:
                oacc_ref[grows_f, :] = tot_f
                oacc_ref[grows_b, :] = tot_b
            return carry

        lax.fori_loop(0, ng, body, 0)

    pl.when(t < nt // 2)(lambda: run(False))
    pl.when(t >= nt // 2)(lambda: run(True))


def _gla_scan_call(qe, ke, v, g, ol, bend, gn, kem, vm, seq, batch):
    h, rb = GLA_HEADS, SCAN_RB
    nt = seq // rb
    fwd = lambda b, hh, t: b * nt + t
    bwd = lambda b, hh, t: b * nt + nt - 1 - t
    cb = rb // SUB

    def specs(rowfn, dircol, with_ol):
        dk_blk = lambda rows: pl.BlockSpec(
            (rows, HEAD_DK), lambda b, hh, t: (rowfn(b, hh, t), dircol + hh))
        dv_blk = pl.BlockSpec((rb, HEAD_DV), lambda b, hh, t: (rowfn(b, hh, t), hh))
        return [dk_blk(rb), dk_blk(rb), dv_blk, dv_blk] + [dv_blk] * with_ol + [dk_blk(cb)]

    in_specs = specs(fwd, 0, True) + specs(bwd, h, False) + [
        pl.BlockSpec(gn.shape, lambda b, hh, t: (0, 0)),
        pl.BlockSpec((SUB, HEAD_DK), lambda b, hh, t: (0, hh)),
        pl.BlockSpec((SUB, HEAD_DV), lambda b, hh, t: (0, hh)),
    ]
    return pl.pallas_call(
        _gla_scan_kernel,
        grid=(batch, h, nt),
        in_specs=in_specs,
        out_specs=pl.BlockSpec((seq, HEAD_DV), lambda b, hh, t: (b, hh)),
        out_shape=jax.ShapeDtypeStruct((batch * seq, GLA_DV), BF16),
        scratch_shapes=[pltpu.VMEM((seq, HEAD_DV), F32), pltpu.VMEM((HEAD_DV, HEAD_DK), F32),
                        pltpu.VMEM((HEAD_DV, HEAD_DK), F32)],
        compiler_params=pltpu.CompilerParams(
            dimension_semantics=("arbitrary", "arbitrary", "arbitrary"),
            vmem_limit_bytes=V7X_VMEM_LIMIT),
        name="gla_scan",
    )(qe, ke, v, g, ol, bend, qe, ke, v, g, bend, gn, kem, vm)


DFT_TILE = 256


def _mirror(j_ref, a_ref, b_ref):
    t = DFT_TILE
    return (jnp.dot(j_ref[0, :, :t], a_ref[...], preferred_element_type=F32)
            + jnp.dot(j_ref[0, :, t:], b_ref[...], preferred_element_type=F32))


def _fold_kernel(pd_ref, pa_ref, pb_ref, qd_ref, qa_ref, qb_ref, pz_ref, qz_ref, j_ref,
                 pe_ref, qo_ref, *, m_last):
    i = pl.program_id(0)
    first = i == 0
    p_dir = jnp.where(first, pz_ref[...], pd_ref[...]).astype(F32)
    q_dir = jnp.where(first, qz_ref[...], qd_ref[...]).astype(F32)
    m = i * DFT_TILE + lax.broadcasted_iota(jnp.int32, (DFT_TILE, 1), 0)
    keep = m <= m_last
    pe_ref[...] = jnp.where(keep, p_dir + _mirror(j_ref, pa_ref, pb_ref), 0.0).astype(BF16)
    qo_ref[...] = jnp.where(keep, q_dir - _mirror(j_ref, qa_ref, qb_ref), 0.0).astype(BF16)


def _dft_kernel(pe_ref, qo_ref, dc_ref, ds_ref, rowc_ref, rows_ref, u_ref, w_ref, *, scale):
    t = DFT_TILE
    acc_a = acc_b = None
    for c in range(pe_ref.shape[0] // t):
        sl = slice(c * t, (c + 1) * t)
        rc, rs = rowc_ref[0, 0:1, sl], rows_ref[0, 0:1, sl]
        dc, ds = dc_ref[:, sl], ds_ref[:, sl]
        tc = (dc * rc - ds * rs).astype(BF16)
        ts = (ds * rc + dc * rs).astype(BF16)
        a = jnp.dot(tc, pe_ref[sl, :], preferred_element_type=F32)
        b = jnp.dot(ts, qo_ref[sl, :], preferred_element_type=F32)
        acc_a = a if acc_a is None else acc_a + a
        acc_b = b if acc_b is None else acc_b + b
    u_ref[...] = ((acc_a - acc_b) * scale).astype(BF16)
    w_ref[...] = ((acc_a + acc_b) * scale).astype(BF16)


def _unfold_kernel(ud_ref, wa_ref, wb_ref, j_ref, y_ref, *, r_last, i_mixed):
    i = pl.program_id(0)

    @pl.when(i < i_mixed)
    def _():
        y_ref[...] = ud_ref[...]

    @pl.when(i >= i_mixed)
    def _():
        r = i * DFT_TILE + lax.broadcasted_iota(jnp.int32, (DFT_TILE, 1), 0)
        y_ref[...] = jnp.where(r <= r_last, ud_ref[...],
                               _mirror(j_ref, wa_ref, wb_ref).astype(BF16))


@functools.lru_cache(maxsize=None)
def _dft_constants(seq, n_meta):
    t = DFT_TILE
    off = t - n_meta
    big_l = seq + n_meta
    half = big_l // 2
    n_real = seq // t
    nf = -(-(half + 1 + off) // t)
    hp = nf * t
    jf = np.zeros((nf, t, 2 * t), np.float32)
    for i in range(nf):
        a_idx, b_idx = min(n_real - i, n_real - 1), n_real - 1 - i
        for j in range(t):
            pos = i * t + j - off
            if 1 <= pos <= half - 1:
                tau, rho = divmod(big_l - pos - n_meta, t)
                assert tau in (a_idx, b_idx)
                jf[i, j, (t if tau == b_idx else 0) + rho] = 1.0
    r_last = half - n_meta
    i_mixed = r_last // t
    ju = np.zeros((3, t, 2 * t), np.float32)
    for i in range(i_mixed, n_real):
        a_idx, b_idx = n_real - i, n_real - 1 - i
        blk = np.zeros((t, 2 * t), np.float32)
        for j in range(t):
            r = i * t + j
            if r > r_last:
                tau, rho = divmod(big_l - (r + n_meta) + off, t)
                assert tau in (a_idx, b_idx)
                blk[j, (t if tau == b_idx else 0) + rho] = 1.0
        var = 1 if i == i_mixed else 2
        assert var == 1 or not ju[2].any() or (ju[2] == blk).all()
        ju[var] = blk
    ang = lambda prod: 2.0 * np.pi * (prod % big_l).astype(np.float64) / big_l
    pos = np.arange(hp, dtype=np.int64) - off
    d = ang(np.arange(t, dtype=np.int64)[:, None] * pos[None, :])
    k0 = t * np.arange(nf, dtype=np.int64) - off
    row = np.broadcast_to(ang(k0[:, None] * pos[None, :])[:, None, :], (nf, 8, hp))
    f = lambda a: np.ascontiguousarray(a, dtype=np.float32)
    return dict(jf=jf, ju=ju, dc=f(np.cos(d)), ds=f(np.sin(d)), rowc=f(np.cos(row)),
                rows=f(np.sin(row)), nf=nf, hp=hp, off=off, n_real=n_real,
                m_last=half + off, r_last=r_last, i_mixed=i_mixed)


def _seq_dft(p, q, pm, qm, seq, n_meta):
    c = _dft_constants(seq, n_meta)
    t, nf, hp, n_real, ncol = DFT_TILE, c["nf"], c["hp"], c["n_real"], p.shape[1]
    params = pltpu.CompilerParams(dimension_semantics=("arbitrary",),
                                  vmem_limit_bytes=V7X_VMEM_LIMIT)
    tile = lambda fn: pl.BlockSpec((t, ncol), lambda i: (fn(i), 0))
    lead = ((c["off"], 0), (0, 0))
    src = [tile(lambda i: jnp.maximum(i - 1, 0)),
           tile(lambda i: jnp.minimum(n_real - i, n_real - 1)),
           tile(lambda i: n_real - 1 - i)]
    pe, qo = pl.pallas_call(
        functools.partial(_fold_kernel, m_last=c["m_last"]),
        grid=(nf,),
        in_specs=src + src + [_resident((t, ncol)), _resident((t, ncol)),
                              pl.BlockSpec((1, t, 2 * t), lambda i: (i, 0, 0))],
        out_specs=(tile(lambda i: i), tile(lambda i: i)),
        out_shape=(jax.ShapeDtypeStruct((hp, ncol), BF16),) * 2,
        compiler_params=params,
        name="dft_fold",
    )(p, p, p, q, q, q, jnp.pad(pm, lead), jnp.pad(qm, lead), jnp.asarray(c["jf"]).astype(BF16))

    scale = float((seq + n_meta) * FNET_GROUP_DIM) ** -0.5
    rowspec = pl.BlockSpec((1, 8, hp), lambda i: (i, 0, 0))
    u, w = pl.pallas_call(
        functools.partial(_dft_kernel, scale=scale),
        grid=(nf,),
        in_specs=[_resident((hp, ncol)), _resident((hp, ncol)), _resident((t, hp)),
                  _resident((t, hp)), rowspec, rowspec],
        out_specs=(tile(lambda i: i), tile(lambda i: i)),
        out_shape=(jax.ShapeDtypeStruct((hp, ncol), BF16),) * 2,
        compiler_params=params,
        name="seq_dft",
    )(pe, qo, jnp.asarray(c["dc"]), jnp.asarray(c["ds"]), jnp.asarray(c["rowc"]),
      jnp.asarray(c["rows"]))

    i_mixed = c["i_mixed"]
    clip = lambda v: jnp.clip(v, 0, nf - 1)
    return pl.pallas_call(
        functools.partial(_unfold_kernel, r_last=c["r_last"], i_mixed=i_mixed),
        grid=(n_real,),
        in_specs=[tile(lambda i: clip(i + 1)), tile(lambda i: clip(n_real - i)),
                  tile(lambda i: clip(n_real - 1 - i)),
                  pl.BlockSpec((1, t, 2 * t),
                               lambda i: (jnp.where(i < i_mixed, 0,
                                                    jnp.where(i == i_mixed, 1, 2)), 0, 0))],
        out_specs=tile(lambda i: i),
        out_shape=jax.ShapeDtypeStruct((seq, ncol), BF16),
        compiler_params=params,
        name="dft_unfold",
    )(u, w, w, jnp.asarray(c["ju"]).astype(BF16))


def _merge_kernel(x_ref, og_ref, y_ref, ga_ref, gf_ref, wo_ref, wf_ref, wout_ref, h_ref):
    y_gla = jnp.dot(og_ref[...], wo_ref[...], preferred_element_type=F32)
    y_fnet = jnp.dot(y_ref[...], wf_ref[...], preferred_element_type=F32)
    merged = (_sigmoid(ga_ref[...].astype(F32)) * y_gla
              + _sigmoid(gf_ref[...].astype(F32)) * y_fnet)
    h_ref[...] = x_ref[...] + jnp.dot(merged.astype(BF16), wout_ref[...],
                                      preferred_element_type=F32)


def _merge_call(x2d, og, y, ga, gf, wo, wf, wout, seq, tm):
    m = x2d.shape[0]
    nt = seq // tm
    row = lambda n: pl.BlockSpec((tm, n), lambda i: (i, 0))
    return pl.pallas_call(
        _merge_kernel,
        grid=(m // tm,),
        in_specs=[row(D_MODEL), row(GLA_DV),
                  pl.BlockSpec((tm, FNET_WIDTH), lambda i: (i % nt, i // nt)),
                  row(D_MODEL), row(D_MODEL),
                  _resident(wo.shape), _resident(wf.shape), _resident(wout.shape)],
        out_specs=row(D_MODEL),
        out_shape=jax.ShapeDtypeStruct((m, D_MODEL), F32),
        compiler_params=pltpu.CompilerParams(
            dimension_semantics=("arbitrary",), vmem_limit_bytes=V7X_VMEM_LIMIT),
        name="merge",
    )(x2d, og, y, ga, gf, wo, wf, wout)


FFN_CHUNK = 1024


def _ffn_kernel(h_ref, nw_ref, w1_ref, w2_ref, nf_ref, o_ref):
    h = h_ref[...]
    u = _rms(h, nw_ref[...]).astype(BF16)
    acc = h
    for c in range(D_FF // FFN_CHUNK):
        sl = slice(c * FFN_CHUNK, (c + 1) * FFN_CHUNK)
        a = jnp.maximum(jnp.dot(u, w1_ref[:, sl], preferred_element_type=F32), 0.0)
        acc = acc + jnp.dot((a * a).astype(BF16), w2_ref[sl, :], preferred_element_type=F32)
    o_ref[...] = _rms(acc, nf_ref[...])


def _ffn_call(h, nw, w1, w2, nf, tm):
    m = h.shape[0]
    row = pl.BlockSpec((tm, D_MODEL), lambda i: (i, 0))
    return pl.pallas_call(
        _ffn_kernel,
        grid=(m // tm,),
        in_specs=[row, _resident((1, D_MODEL)), _resident(w1.shape), _resident(w2.shape),
                  _resident((1, D_MODEL))],
        out_specs=row,
        out_shape=jax.ShapeDtypeStruct((m, D_MODEL), F32),
        compiler_params=pltpu.CompilerParams(
            dimension_semantics=("arbitrary",), vmem_limit_bytes=V7X_VMEM_LIMIT),
        name="ffn",
    )(h, nw, w1, w2, nf)


@functools.lru_cache(maxsize=None)
def _gla_constants(rows):
    idx = np.arange(rows)
    same = (idx[:, None] // SUB) == (idx[None, :] // SUB)
    tri_f = (same & (idx[None, :] <= idx[:, None])).astype(np.float32)
    return tri_f, np.ascontiguousarray(tri_f.T)


@functools.lru_cache(maxsize=None)
def _channel_dft():
    d = np.arange(FNET_GROUP_DIM, dtype=np.int64)
    ang = 2.0 * np.pi * ((d[:, None] * d[None, :]) % FNET_GROUP_DIM) / FNET_GROUP_DIM
    return np.concatenate([np.cos(ang), np.sin(ang)], axis=1).astype(np.float32)


def kernel(x, meta_tokens, norm_mix, w_in, w_decay_fwd, b_decay_fwd, w_decay_bwd, b_decay_bwd,
           gla_norm, w_o_gla, w_fnet, w_out, norm_ffn, w_ff1, w_ff2, norm_final):
    batch, seq, d = x.shape
    assert d == D_MODEL and meta_tokens.shape == (N_META, D_MODEL)
    assert w_in.shape[0] == 1 and seq % (2 * SCAN_RB) == 0 and seq % DFT_TILE == 0
    tm = 512

    wi = w_in[0]
    splits = np.cumsum([GLA_DK, GLA_DK, GLA_DV, GLA_DV, DECAY_RANK, DECAY_RANK, FNET_WIDTH,
                        D_MODEL])
    wq, wk, wv, wg, wrf, wrb, wf_in, wga, wgf = jnp.split(wi, [int(s) for s in splits], axis=1)
    w_main = jnp.concatenate([wq, wk, wv, wg, wf_in, wga, wgf], axis=1).astype(BF16)
    w_r = jnp.concatenate(
        [wrf, wrb, jnp.zeros((D_MODEL, R_PAD - 2 * DECAY_RANK), F32)], axis=1).astype(BF16)
    wdec = jnp.zeros((R_PAD, 2 * GLA_DK), F32)
    wdec = wdec.at[:DECAY_RANK, :GLA_DK].set(w_decay_fwd[0])
    wdec = wdec.at[DECAY_RANK:2 * DECAY_RANK, GLA_DK:].set(w_decay_bwd[0]).astype(BF16)
    bdec = jnp.concatenate([b_decay_fwd[0], b_decay_bwd[0]])[None, :]
    cs = jnp.asarray(_channel_dft())
    nw_mix = norm_mix[0][None, :]

    x2d = x.reshape(batch * seq, D_MODEL)
    q, k, v, g, r, p, pq_q, ga, gf = _proj_call(x2d, nw_mix, w_main, w_r, cs, seq, tm)
    _, km, vm, _, rm, pm, qm, _, _ = _proj_call(meta_tokens, nw_mix, w_main, w_r, cs,
                                                N_META, N_META)

    lead = ((SUB - N_META, 0), (0, 0))
    km, vm, rm = jnp.pad(km, lead), jnp.pad(vm, lead), jnp.pad(rm, lead)
    ol, qe, ke, bend = _gla_local_call(q, k, v, r, wdec, bdec, *_gla_constants(GLA_C),
                                       GLA_TM, GLA_C)
    _, _, kem, _ = _gla_local_call(km, km, vm, rm, wdec, bdec, *_gla_constants(SUB), SUB, SUB)
    og = _gla_scan_call(qe, ke, v, g, ol, bend, gla_norm[0][None, :], kem, vm, seq, batch)

    y = _seq_dft(p, pq_q, jnp.tile(pm, (1, batch)), jnp.tile(qm, (1, batch)), seq, N_META)

    h1 = _merge_call(x2d, og, y, ga, gf, w_o_gla[0].astype(BF16), w_fnet[0].astype(BF16),
                     w_out[0].astype(BF16), seq, tm)
    out = _ffn_call(h1, norm_ffn[0][None, :], w_ff1[0].astype(BF16), w_ff2[0].astype(BF16),
                    norm_final[None, :], tm)
    return out.reshape(batch, seq, D_MODEL)
```

```python
import functools

import numpy as np
import jax
import jax.numpy as jnp
from jax import lax
from jax.experimental import pallas as pl
from jax.experimental.pallas import tpu as pltpu

F32 = jnp.float32
BF16 = jnp.bfloat16

D_MODEL = 1024
N_META = 16
GLA_HEADS = 4
GLA_DK = 512
GLA_DV = 1024
HEAD_DK = GLA_DK // GLA_HEADS
HEAD_DV = GLA_DV // GLA_HEADS
DECAY_RANK = 16
GATE_TEMP = 16.0
SUB = 64
FNET_GROUPS = 4
FNET_WIDTH = 512
FNET_GROUP_DIM = FNET_WIDTH // FNET_GROUPS
D_FF = 4 * D_MODEL
EPS = 1e-6
LOG2E = 1.4426950408889634

_OFF_Q, _OFF_K, _OFF_V, _OFF_G, _OFF_F, _OFF_GA, _OFF_GF, _OFF_END = (
    0, 512, 1024, 2048, 3072, 3584, 4608, 5632)
R_PAD = 128

GLA_C = 4 * SUB
GLA_TM = 512

LANE = 128
V7X_VMEM_LIMIT = 56 * 1024 * 1024


def _rms(x, w):
    ms = jnp.mean(x * x, axis=-1, keepdims=True)
    return x * lax.rsqrt(ms + EPS) * w


def _sigmoid(x):
    return 1.0 / (1.0 + jnp.exp(-x))


def _resident(shape):
    nd = len(shape)
    return pl.BlockSpec(shape, lambda *_: (0,) * nd, pipeline_mode=pl.Buffered(1))


def _proj_kernel(x_ref, nw_ref, w_ref, wr_ref, cs_ref,
                 q_ref, k_ref, v_ref, g_ref, r_ref, p_ref, qq_ref, ga_ref, gf_ref):
    xn = _rms(x_ref[...], nw_ref[...]).astype(BF16)

    def mm(lo, hi):
        return jnp.dot(xn, w_ref[:, lo:hi], preferred_element_type=F32)

    q_ref[...] = (mm(_OFF_Q, _OFF_K) * (HEAD_DK ** -0.5)).astype(BF16)
    k_ref[...] = mm(_OFF_K, _OFF_V).astype(BF16)
    v_ref[...] = mm(_OFF_V, _OFF_G).astype(BF16)
    g_ref[...] = mm(_OFF_G, _OFF_F).astype(BF16)
    ga_ref[...] = mm(_OFF_GA, _OFF_GF).astype(BF16)
    gf_ref[...] = mm(_OFF_GF, _OFF_END).astype(BF16)
    r_ref[...] = jnp.dot(xn, wr_ref[...], preferred_element_type=F32).astype(BF16)
    f = mm(_OFF_F, _OFF_GA).astype(BF16)
    cs = cs_ref[...].astype(BF16)
    for grp in range(FNET_GROUPS):
        sl = slice(grp * FNET_GROUP_DIM, (grp + 1) * FNET_GROUP_DIM)
        pq = jnp.dot(f[:, sl], cs, preferred_element_type=F32)
        p_ref[:, sl] = pq[:, :FNET_GROUP_DIM].astype(BF16)
        qq_ref[:, sl] = pq[:, FNET_GROUP_DIM:].astype(BF16)


def _proj_call(x2d, nw, w_main, w_r, cs, rows_per_batch, tm):
    m = x2d.shape[0]
    nt = rows_per_batch // tm
    nb = m // rows_per_batch
    row = lambda n: pl.BlockSpec((tm, n), lambda i: (i, 0))
    pq_spec = pl.BlockSpec((tm, FNET_WIDTH), lambda i: (i % nt, i // nt))
    out_shape = (
        jax.ShapeDtypeStruct((m, GLA_DK), BF16), jax.ShapeDtypeStruct((m, GLA_DK), BF16),
        jax.ShapeDtypeStruct((m, GLA_DV), BF16), jax.ShapeDtypeStruct((m, GLA_DV), BF16),
        jax.ShapeDtypeStruct((m, R_PAD), BF16),
        jax.ShapeDtypeStruct((rows_per_batch, nb * FNET_WIDTH), BF16),
        jax.ShapeDtypeStruct((rows_per_batch, nb * FNET_WIDTH), BF16),
        jax.ShapeDtypeStruct((m, D_MODEL), BF16), jax.ShapeDtypeStruct((m, D_MODEL), BF16),
    )
    return pl.pallas_call(
        _proj_kernel,
        grid=(m // tm,),
        in_specs=[row(D_MODEL), _resident((1, D_MODEL)), _resident(w_main.shape),
                  _resident(w_r.shape), _resident(cs.shape)],
        out_specs=(row(GLA_DK), row(GLA_DK), row(GLA_DV), row(GLA_DV), row(R_PAD),
                   pq_spec, pq_spec, row(D_MODEL), row(D_MODEL)),
        out_shape=out_shape,
        compiler_params=pltpu.CompilerParams(
            dimension_semantics=("arbitrary",), vmem_limit_bytes=V7X_VMEM_LIMIT),
        name="proj",
    )(x2d, nw, w_main, w_r, cs)


_NT = (((1,), (1,)), ((), ()))
_TN = (((0,), (0,)), ((), ()))


def _gla_local_kernel(q_ref, k_ref, v_ref, r_ref, wdec_ref, bdec_ref, trif_ref, trib_ref,
                      maskf_ref, maskb_ref, ol_ref, qe_ref, ke_ref, bend_ref, qd_ref, kd_ref,
                      *, chunk):
    tm = q_ref.shape[0]
    n, s = chunk // SUB, SUB
    x = jnp.dot(r_ref[...], wdec_ref[...], preferred_element_type=F32) + bdec_ref[...]
    la = (jnp.minimum(x, 0.0) - jnp.log(1.0 + jnp.exp(-jnp.abs(x)))) * (LOG2E / GATE_TEMP)
    hi = la.astype(BF16)
    lo = (la - hi.astype(F32)).astype(BF16)
    tris = (trif_ref[...], trib_ref[...])
    for c in range(tm // chunk):
        rows = slice(c * chunk, (c + 1) * chunk)
        q3 = q_ref[rows, :].astype(F32).reshape(n, s, GLA_DK)
        k3 = k_ref[rows, :].astype(F32).reshape(n, s, GLA_DK)
        for d in range(2):
            cols = slice(d * GLA_DK, (d + 1) * GLA_DK)
            b = (jnp.dot(tris[d], hi[rows, cols], preferred_element_type=F32)
                 + jnp.dot(tris[d], lo[rows, cols], preferred_element_type=F32))
            b = b.reshape(n, s, GLA_DK)
            if d == 0:
                b_mid, b_end = b[:, s // 2 - 1:s // 2, :], b[:, s - 1:s, :]
            else:
                b_mid, b_end = b[:, s // 2:s // 2 + 1, :], b[:, 0:1, :]
            qd = q3 * jnp.exp2(b - b_mid)
            kd = k3 * jnp.exp2(b_mid - b)
            qe = qd * jnp.exp2(b_mid)
            ke = kd * jnp.exp2(b_end - b_mid)
            qd_ref[d, rows, :] = qd.reshape(chunk, GLA_DK).astype(BF16)
            kd_ref[d, rows, :] = kd.reshape(chunk, GLA_DK).astype(BF16)
            qe_ref[rows, cols] = qe.reshape(chunk, GLA_DK).astype(BF16)
            ke_ref[rows, cols] = ke.reshape(chunk, GLA_DK).astype(BF16)
            bend_ref[c * n:(c + 1) * n, cols] = b_end.reshape(n, GLA_DK)
    masks = (maskf_ref[...] != 0.0, maskb_ref[...] != 0.0)
    for c in range(tm // chunk):
        rows = slice(c * chunk, (c + 1) * chunk)
        for h in range(GLA_HEADS):
            hk = slice(h * HEAD_DK, (h + 1) * HEAD_DK)
            hv = slice(h * HEAD_DV, (h + 1) * HEAD_DV)
            sc = [lax.dot_general(qd_ref[d, rows, hk], kd_ref[d, rows, hk], _NT,
                                  preferred_element_type=F32) for d in range(2)]
            sc = (jnp.where(masks[0], sc[0], 0.0) + jnp.where(masks[1], sc[1], 0.0)).astype(BF16)
            ol_ref[rows, hv] = jnp.dot(sc, v_ref[rows, hv], preferred_element_type=F32)


def _gla_local_call(q, k, v, r, wdec, bdec, tri_f, tri_b, tm, chunk):
    m = q.shape[0]
    row = lambda n: pl.BlockSpec((tm, n), lambda i: (i, 0))
    tri_bf = (jnp.asarray(tri_f).astype(BF16), jnp.asarray(tri_b).astype(BF16))
    masks = (jnp.asarray(tri_f), jnp.asarray(tri_b))
    sq = _resident((chunk, chunk))
    return pl.pallas_call(
        functools.partial(_gla_local_kernel, chunk=chunk),
        grid=(m // tm,),
        in_specs=[row(GLA_DK), row(GLA_DK), row(GLA_DV), row(R_PAD), _resident(wdec.shape),
                  _resident(bdec.shape), sq, sq, sq, sq],
        out_specs=(row(GLA_DV), row(2 * GLA_DK), row(2 * GLA_DK),
                   pl.BlockSpec((tm // SUB, 2 * GLA_DK), lambda i: (i, 0))),
        out_shape=(jax.ShapeDtypeStruct((m, GLA_DV), F32),
                   jax.ShapeDtypeStruct((m, 2 * GLA_DK), BF16),
                   jax.ShapeDtypeStruct((m, 2 * GLA_DK), BF16),
                   jax.ShapeDtypeStruct((m // SUB, 2 * GLA_DK), F32)),
        scratch_shapes=[pltpu.VMEM((2, tm, GLA_DK), BF16), pltpu.VMEM((2, tm, GLA_DK), BF16)],
        compiler_params=pltpu.CompilerParams(
            dimension_semantics=("arbitrary",), vmem_limit_bytes=V7X_VMEM_LIMIT),
        name="gla_local",
    )(q, k, v, r, wdec, bdec, tri_bf[0], tri_bf[1], masks[0], masks[1])


SCAN_G = 512
SCAN_RB = 1024


def _scan_group(dirs):
    s, n = SUB, SCAN_G // SUB
    ahead = 2
    dec, st, inter = [], [], [{}, {}]
    for qe_ref, ke_ref, v_ref, bend_ref, st_ref, row0, chunk0, backward in dirs:
        dec_rows = jnp.tile(jnp.exp2(bend_ref[chunk0:chunk0 + n, :]), (HEAD_DK // n, 1))
        dec.append(dec_rows.T)
        st.append(st_ref[...])

    def order(step, backward):
        return n - 1 - step if backward else step

    def increment(d, step):
        _, ke_ref, v_ref, _, _, row0, _, backward = dirs[d]
        i = order(step, backward)
        sub = slice(row0 + i * s, row0 + (i + 1) * s)
        return lax.dot_general(ke_ref[sub, :], v_ref[sub, :], _TN,
                               preferred_element_type=F32)

    inc = {(d, step): increment(d, step) for step in range(ahead) for d in range(2)}
    for step in range(n):
        for d, (qe_ref, _, _, _, _, row0, _, backward) in enumerate(dirs):
            i = order(step, backward)
            sub = slice(row0 + i * s, row0 + (i + 1) * s)
            inter[d][i] = jnp.dot(qe_ref[sub, :], st[d].astype(BF16),
                                  preferred_element_type=F32)
            if step + ahead < n:
                inc[d, step + ahead] = increment(d, step + ahead)
            decay = jnp.broadcast_to(dec[d][:, i:i + 1], (HEAD_DK, HEAD_DV))
            st[d] = st[d] * decay + inc.pop((d, step))
    for d, spec in enumerate(dirs):
        spec[4][...] = st[d]
    return [jnp.concatenate([inter[d][i] for i in range(n)], axis=0) for d in range(2)]


def _gla_scan_kernel(qef_ref, kef_ref, vf_ref, olf_ref, bendf_ref,
                     qeb_ref, keb_ref, vb_ref, bendb_ref, kem_ref, vm_ref,
                     of_ref, ob_ref, stf_ref, stb_ref):
    rb = qef_ref.shape[0]
    ng = rb // SCAN_G

    @pl.when(pl.program_id(2) == 0)
    def _():
        stf_ref[...] = lax.dot_general(kem_ref[...], vm_ref[...], _TN,
                                       preferred_element_type=F32)
        stb_ref[...] = jnp.zeros_like(stb_ref)

    for j in range(ng):
        jb = ng - 1 - j
        cpg = SCAN_G // SUB
        in_f, in_b = _scan_group([
            (qef_ref, kef_ref, vf_ref, bendf_ref, stf_ref, j * SCAN_G, j * cpg, False),
            (qeb_ref, keb_ref, vb_ref, bendb_ref, stb_ref, jb * SCAN_G, jb * cpg, True)])
        rows_f = slice(j * SCAN_G, (j + 1) * SCAN_G)
        of_ref[rows_f, :] = (olf_ref[rows_f, :] + in_f).astype(BF16)
        ob_ref[jb * SCAN_G:(jb + 1) * SCAN_G, :] = in_b.astype(BF16)


def _gla_scan_call(qe, ke, v, ol, bend, kem, vm, seq, batch):
    h, rb = GLA_HEADS, SCAN_RB
    nt = seq // rb
    fwd = lambda b, hh, t: b * nt + t
    bwd = lambda b, hh, t: b * nt + nt - 1 - t
    cb = rb // SUB

    def dk_blk(rowfn, dircol, rows):
        return pl.BlockSpec((rows, HEAD_DK), lambda b, hh, t: (rowfn(b, hh, t), dircol + hh))

    def dv_blk(rowfn):
        return pl.BlockSpec((rb, HEAD_DV), lambda b, hh, t: (rowfn(b, hh, t), hh))

    in_specs = [
        dk_blk(fwd, 0, rb), dk_blk(fwd, 0, rb), dv_blk(fwd), dv_blk(fwd), dk_blk(fwd, 0, cb),
        dk_blk(bwd, h, rb), dk_blk(bwd, h, rb), dv_blk(bwd), dk_blk(bwd, h, cb),
        pl.BlockSpec((SUB, HEAD_DK), lambda b, hh, t: (0, hh)),
        pl.BlockSpec((SUB, HEAD_DV), lambda b, hh, t: (0, hh)),
    ]
    out = jax.ShapeDtypeStruct((batch * seq, GLA_DV), BF16)
    return pl.pallas_call(
        _gla_scan_kernel,
        grid=(batch, h, nt),
        in_specs=in_specs,
        out_specs=(dv_blk(fwd), dv_blk(bwd)),
        out_shape=(out, out),
        scratch_shapes=[pltpu.VMEM((HEAD_DK, HEAD_DV), F32), pltpu.VMEM((HEAD_DK, HEAD_DV), F32)],
        compiler_params=pltpu.CompilerParams(
            dimension_semantics=("arbitrary", "arbitrary", "arbitrary"),
            vmem_limit_bytes=V7X_VMEM_LIMIT),
        name="gla_scan",
    )(qe, ke, v, ol, bend, qe, ke, v, bend, kem, vm)


DFT_TILE = 256


def _mirror(j_ref, a_ref, b_ref):
    t = DFT_TILE
    return (jnp.dot(j_ref[0, :, :t], a_ref[...], preferred_element_type=F32)
            + jnp.dot(j_ref[0, :, t:], b_ref[...], preferred_element_type=F32))


def _fold_kernel(pd_ref, pa_ref, pb_ref, qd_ref, qa_ref, qb_ref, pz_ref, qz_ref, j_ref,
                 pe_ref, qo_ref, *, m_last):
    i = pl.program_id(0)
    first = i == 0
    p_dir = jnp.where(first, pz_ref[...], pd_ref[...]).astype(F32)
    q_dir = jnp.where(first, qz_ref[...], qd_ref[...]).astype(F32)
    m = i * DFT_TILE + lax.broadcasted_iota(jnp.int32, (DFT_TILE, 1), 0)
    keep = m <= m_last
    pe_ref[...] = jnp.where(keep, p_dir + _mirror(j_ref, pa_ref, pb_ref), 0.0).astype(BF16)
    qo_ref[...] = jnp.where(keep, q_dir - _mirror(j_ref, qa_ref, qb_ref), 0.0).astype(BF16)


def _dft_kernel(pe_ref, qo_ref, dc_ref, ds_ref, rowc_ref, rows_ref, u_ref, w_ref, *, scale):
    t = DFT_TILE
    acc_a = acc_b = None
    for c in range(pe_ref.shape[0] // t):
        sl = slice(c * t, (c + 1) * t)
        rc, rs = rowc_ref[0, 0:1, sl], rows_ref[0, 0:1, sl]
        dc, ds = dc_ref[:, sl], ds_ref[:, sl]
        tc = (dc * rc - ds * rs).astype(BF16)
        ts = (ds * rc + dc * rs).astype(BF16)
        a = jnp.dot(tc, pe_ref[sl, :], preferred_element_type=F32)
        b = jnp.dot(ts, qo_ref[sl, :], preferred_element_type=F32)
        acc_a = a if acc_a is None else acc_a + a
        acc_b = b if acc_b is None else acc_b + b
    u_ref[...] = ((acc_a - acc_b) * scale).astype(BF16)
    w_ref[...] = ((acc_a + acc_b) * scale).astype(BF16)


def _unfold_kernel(ud_ref, wa_ref, wb_ref, j_ref, y_ref, *, r_last, i_mixed):
    i = pl.program_id(0)

    @pl.when(i < i_mixed)
    def _():
        y_ref[...] = ud_ref[...]

    @pl.when(i >= i_mixed)
    def _():
        r = i * DFT_TILE + lax.broadcasted_iota(jnp.int32, (DFT_TILE, 1), 0)
        y_ref[...] = jnp.where(r <= r_last, ud_ref[...],
                               _mirror(j_ref, wa_ref, wb_ref).astype(BF16))


@functools.lru_cache(maxsize=None)
def _dft_constants(seq, n_meta):
    t = DFT_TILE
    off = t - n_meta
    big_l = seq + n_meta
    half = big_l // 2
    n_real = seq // t
    nf = -(-(half + 1 + off) // t)
    hp = nf * t
    jf = np.zeros((nf, t, 2 * t), np.float32)
    for i in range(nf):
        a_idx, b_idx = min(n_real - i, n_real - 1), n_real - 1 - i
        for j in range(t):
            pos = i * t + j - off
            if 1 <= pos <= half - 1:
                tau, rho = divmod(big_l - pos - n_meta, t)
                assert tau in (a_idx, b_idx)
                jf[i, j, (t if tau == b_idx else 0) + rho] = 1.0
    r_last = half - n_meta
    i_mixed = r_last // t
    ju = np.zeros((3, t, 2 * t), np.float32)
    for i in range(i_mixed, n_real):
        a_idx, b_idx = n_real - i, n_real - 1 - i
        blk = np.zeros((t, 2 * t), np.float32)
        for j in range(t):
            r = i * t + j
            if r > r_last:
                tau, rho = divmod(big_l - (r + n_meta) + off, t)
                assert tau in (a_idx, b_idx)
                blk[j, (t if tau == b_idx else 0) + rho] = 1.0
        var = 1 if i == i_mixed else 2
        assert var == 1 or not ju[2].any() or (ju[2] == blk).all()
        ju[var] = blk
    ang = lambda prod: 2.0 * np.pi * (prod % big_l).astype(np.float64) / big_l
    pos = np.arange(hp, dtype=np.int64) - off
    d = ang(np.arange(t, dtype=np.int64)[:, None] * pos[None, :])
    k0 = t * np.arange(nf, dtype=np.int64) - off
    row = np.broadcast_to(ang(k0[:, None] * pos[None, :])[:, None, :], (nf, 8, hp))
    f = lambda a: np.ascontiguousarray(a, dtype=np.float32)
    return dict(jf=jf, ju=ju, dc=f(np.cos(d)), ds=f(np.sin(d)), rowc=f(np.cos(row)),
                rows=f(np.sin(row)), nf=nf, hp=hp, off=off, n_real=n_real,
                m_last=half + off, r_last=r_last, i_mixed=i_mixed)


def _seq_dft(p, q, pm, qm, seq, n_meta):
    c = _dft_constants(seq, n_meta)
    t, nf, hp, n_real, ncol = DFT_TILE, c["nf"], c["hp"], c["n_real"], p.shape[1]
    params = pltpu.CompilerParams(dimension_semantics=("arbitrary",),
                                  vmem_limit_bytes=V7X_VMEM_LIMIT)
    tile = lambda fn: pl.BlockSpec((t, ncol), lambda i: (fn(i), 0))
    lead = ((c["off"], 0), (0, 0))
    src = [tile(lambda i: jnp.maximum(i - 1, 0)),
           tile(lambda i: jnp.minimum(n_real - i, n_real - 1)),
           tile(lambda i: n_real - 1 - i)]
    pe, qo = pl.pallas_call(
        functools.partial(_fold_kernel, m_last=c["m_last"]),
        grid=(nf,),
        in_specs=src + src + [_resident((t, ncol)), _resident((t, ncol)),
                              pl.BlockSpec((1, t, 2 * t), lambda i: (i, 0, 0))],
        out_specs=(tile(lambda i: i), tile(lambda i: i)),
        out_shape=(jax.ShapeDtypeStruct((hp, ncol), BF16),) * 2,
        compiler_params=params,
        name="dft_fold",
    )(p, p, p, q, q, q, jnp.pad(pm, lead), jnp.pad(qm, lead), jnp.asarray(c["jf"]).astype(BF16))

    scale = float((seq + n_meta) * FNET_GROUP_DIM) ** -0.5
    rowspec = pl.BlockSpec((1, 8, hp), lambda i: (i, 0, 0))
    u, w = pl.pallas_call(
        functools.partial(_dft_kernel, scale=scale),
        grid=(nf,),
        in_specs=[_resident((hp, ncol)), _resident((hp, ncol)), _resident((t, hp)),
                  _resident((t, hp)), rowspec, rowspec],
        out_specs=(tile(lambda i: i), tile(lambda i: i)),
        out_shape=(jax.ShapeDtypeStruct((hp, ncol), BF16),) * 2,
        compiler_params=params,
        name="seq_dft",
    )(pe, qo, jnp.asarray(c["dc"]), jnp.asarray(c["ds"]), jnp.asarray(c["rowc"]),
      jnp.asarray(c["rows"]))

    i_mixed = c["i_mixed"]
    clip = lambda v: jnp.clip(v, 0, nf - 1)
    return pl.pallas_call(
        functools.partial(_unfold_kernel, r_last=c["r_last"], i_mixed=i_mixed),
        grid=(n_real,),
        in_specs=[tile(lambda i: clip(i + 1)), tile(lambda i: clip(n_real - i)),
                  tile(lambda i: clip(n_real - 1 - i)),
                  pl.BlockSpec((1, t, 2 * t),
                               lambda i: (jnp.where(i < i_mixed, 0,
                                                    jnp.where(i == i_mixed, 1, 2)), 0, 0))],
        out_specs=tile(lambda i: i),
        out_shape=jax.ShapeDtypeStruct((seq, ncol), BF16),
        compiler_params=params,
        name="dft_unfold",
    )(u, w, w, jnp.asarray(c["ju"]).astype(BF16))


def _merge_kernel(x_ref, of_ref, ob_ref, g_ref, gn_ref, y_ref, ga_ref, gf_ref,
                  wo_ref, wf_ref, wout_ref, h_ref):
    y_gla = None
    for h in range(GLA_HEADS):
        hv = slice(h * HEAD_DV, (h + 1) * HEAD_DV)
        o = of_ref[:, hv].astype(F32) + ob_ref[:, hv].astype(F32)
        g = g_ref[:, hv].astype(F32)
        og = (_rms(o, gn_ref[...]) * (g * _sigmoid(g))).astype(BF16)
        part = jnp.dot(og, wo_ref[hv, :], preferred_element_type=F32)
        y_gla = part if y_gla is None else y_gla + part
    y_fnet = jnp.dot(y_ref[...], wf_ref[...], preferred_element_type=F32)
    merged = (_sigmoid(ga_ref[...].astype(F32)) * y_gla
              + _sigmoid(gf_ref[...].astype(F32)) * y_fnet)
    h_ref[...] = x_ref[...] + jnp.dot(merged.astype(BF16), wout_ref[...],
                                      preferred_element_type=F32)


def _merge_call(x2d, o_f, o_b, g, gn, y, ga, gf, wo, wf, wout, seq, tm):
    m = x2d.shape[0]
    nt = seq // tm
    row = lambda n: pl.BlockSpec((tm, n), lambda i: (i, 0))
    return pl.pallas_call(
        _merge_kernel,
        grid=(m // tm,),
        in_specs=[row(D_MODEL), row(GLA_DV), row(GLA_DV), row(GLA_DV), _resident(gn.shape),
                  pl.BlockSpec((tm, FNET_WIDTH), lambda i: (i % nt, i // nt)),
                  row(D_MODEL), row(D_MODEL),
                  _resident(wo.shape), _resident(wf.shape), _resident(wout.shape)],
        out_specs=row(D_MODEL),
        out_shape=jax.ShapeDtypeStruct((m, D_MODEL), F32),
        compiler_params=pltpu.CompilerParams(
            dimension_semantics=("arbitrary",), vmem_limit_bytes=V7X_VMEM_LIMIT),
        name="merge",
    )(x2d, o_f, o_b, g, gn, y, ga, gf, wo, wf, wout)


FFN_CHUNK = 1024


def _ffn_kernel(h_ref, nw_ref, w1_ref, w2_ref, nf_ref, o_ref):
    h = h_ref[...]
    u = _rms(h, nw_ref[...]).astype(BF16)
    acc = h
    for c in range(D_FF // FFN_CHUNK):
        sl = slice(c * FFN_CHUNK, (c + 1) * FFN_CHUNK)
        a = jnp.maximum(jnp.dot(u, w1_ref[:, sl], preferred_element_type=F32), 0.0)
        acc = acc + jnp.dot((a * a).astype(BF16), w2_ref[sl, :], preferred_element_type=F32)
    o_ref[...] = _rms(acc, nf_ref[...])


def _ffn_call(h, nw, w1, w2, nf, tm):
    m = h.shape[0]
    row = pl.BlockSpec((tm, D_MODEL), lambda i: (i, 0))
    return pl.pallas_call(
        _ffn_kernel,
        grid=(m // tm,),
        in_specs=[row, _resident((1, D_MODEL)), _resident(w1.shape), _resident(w2.shape),
                  _resident((1, D_MODEL))],
        out_specs=row,
        out_shape=jax.ShapeDtypeStruct((m, D_MODEL), F32),
        compiler_params=pltpu.CompilerParams(
            dimension_semantics=("arbitrary",), vmem_limit_bytes=V7X_VMEM_LIMIT),
        name="ffn",
    )(h, nw, w1, w2, nf)


@functools.lru_cache(maxsize=None)
def _gla_constants(rows):
    idx = np.arange(rows)
    same = (idx[:, None] // SUB) == (idx[None, :] // SUB)
    tri_f = (same & (idx[None, :] <= idx[:, None])).astype(np.float32)
    return tri_f, np.ascontiguousarray(tri_f.T)


@functools.lru_cache(maxsize=None)
def _channel_dft():
    d = np.arange(FNET_GROUP_DIM, dtype=np.int64)
    ang = 2.0 * np.pi * ((d[:, None] * d[None, :]) % FNET_GROUP_DIM) / FNET_GROUP_DIM
    return np.concatenate([np.cos(ang), np.sin(ang)], axis=1).astype(np.float32)


def kernel(x, meta_tokens, norm_mix, w_in, w_decay_fwd, b_decay_fwd, w_decay_bwd, b_decay_bwd,
           gla_norm, w_o_gla, w_fnet, w_out, norm_ffn, w_ff1, w_ff2, norm_final):
    batch, seq, d = x.shape
    assert d == D_MODEL and meta_tokens.shape == (N_META, D_MODEL)
    assert w_in.shape[0] == 1 and seq % (2 * SCAN_RB) == 0 and seq % DFT_TILE == 0
    tm = 512

    wi = w_in[0]
    splits = np.cumsum([GLA_DK, GLA_DK, GLA_DV, GLA_DV, DECAY_RANK, DECAY_RANK, FNET_WIDTH,
                        D_MODEL])
    wq, wk, wv, wg, wrf, wrb, wf_in, wga, wgf = jnp.split(wi, [int(s) for s in splits], axis=1)
    w_main = jnp.concatenate([wq, wk, wv, wg, wf_in, wga, wgf], axis=1).astype(BF16)
    w_r = jnp.concatenate(
        [wrf, wrb, jnp.zeros((D_MODEL, R_PAD - 2 * DECAY_RANK), F32)], axis=1).astype(BF16)
    wdec = jnp.zeros((R_PAD, 2 * GLA_DK), F32)
    wdec = wdec.at[:DECAY_RANK, :GLA_DK].set(w_decay_fwd[0])
    wdec = wdec.at[DECAY_RANK:2 * DECAY_RANK, GLA_DK:].set(w_decay_bwd[0]).astype(BF16)
    bdec = jnp.concatenate([b_decay_fwd[0], b_decay_bwd[0]])[None, :]
    cs = jnp.asarray(_channel_dft())
    nw_mix = norm_mix[0][None, :]

    x2d = x.reshape(batch * seq, D_MODEL)
    q, k, v, g, r, p, pq_q, ga, gf = _proj_call(x2d, nw_mix, w_main, w_r, cs, seq, tm)
    _, km, vm, _, rm, pm, qm, _, _ = _proj_call(meta_tokens, nw_mix, w_main, w_r, cs,
                                                N_META, N_META)

    lead = ((SUB - N_META, 0), (0, 0))
    km, vm, rm = jnp.pad(km, lead), jnp.pad(vm, lead), jnp.pad(rm, lead)
    ol, qe, ke, bend = _gla_local_call(q, k, v, r, wdec, bdec, *_gla_constants(GLA_C),
                                       GLA_TM, GLA_C)
    _, _, kem, _ = _gla_local_call(km, km, vm, rm, wdec, bdec, *_gla_constants(SUB), SUB, SUB)
    o_f, o_b = _gla_scan_call(qe, ke, v, ol, bend, kem, vm, seq, batch)

    y = _seq_dft(p, pq_q, jnp.tile(pm, (1, batch)), jnp.tile(qm, (1, batch)), seq, N_META)

    h1 = _merge_call(x2d, o_f, o_b, g, gla_norm[0][None, :], y, ga, gf,
                     w_o_gla[0].astype(BF16), w_fnet[0].astype(BF16),
                     w_out[0].astype(BF16), seq, tm)
    out = _ffn_call(h1, norm_ffn[0][None, :], w_ff1[0].astype(BF16), w_ff2[0].astype(BF16),
                    norm_final[None, :], tm)
    return out.reshape(batch, seq, D_MODEL)
```

```python
import functools

import numpy as np
import jax
import jax.numpy as jnp
from jax import lax
from jax.experimental import pallas as pl
from jax.experimental.pallas import tpu as pltpu

F32 = jnp.float32
BF16 = jnp.bfloat16

D_MODEL = 1024
N_META = 16
GLA_HEADS = 4
GLA_DK = 512
GLA_DV = 1024
HEAD_DK = GLA_DK // GLA_HEADS
HEAD_DV = GLA_DV // GLA_HEADS
DECAY_RANK = 16
GATE_TEMP = 16.0
SUB = 64
FNET_GROUPS = 4
FNET_WIDTH = 512
FNET_GROUP_DIM = FNET_WIDTH // FNET_GROUPS
D_FF = 4 * D_MODEL
EPS = 1e-6
LOG2E = 1.4426950408889634

_OFF_Q, _OFF_K, _OFF_V, _OFF_F, _OFF_END = 0, 512, 1024, 2048, 2560
_OFF_G, _OFF_GA, _OFF_GF, _OFF_GEND = 0, 1024, 2048, 3072
R_PAD = 128

GLA_C = 4 * SUB
GLA_TM = 512

LANE = 128
V7X_VMEM_LIMIT = 56 * 1024 * 1024


def _rms(x, w):
    ms = jnp.mean(x * x, axis=-1, keepdims=True)
    return x * lax.rsqrt(ms + EPS) * w


def _sigmoid(x):
    return 1.0 / (1.0 + jnp.exp(-x))


def _resident(shape):
    nd = len(shape)
    return pl.BlockSpec(shape, lambda *_: (0,) * nd, pipeline_mode=pl.Buffered(1))


def _proj_kernel(x_ref, nw_ref, w_ref, wr_ref, cs_ref,
                 q_ref, k_ref, v_ref, r_ref, p_ref, qq_ref):
    xn = _rms(x_ref[...], nw_ref[...]).astype(BF16)

    def mm(lo, hi):
        return jnp.dot(xn, w_ref[:, lo:hi], preferred_element_type=F32)

    q_ref[...] = (mm(_OFF_Q, _OFF_K) * (HEAD_DK ** -0.5)).astype(BF16)
    k_ref[...] = mm(_OFF_K, _OFF_V).astype(BF16)
    v_ref[...] = mm(_OFF_V, _OFF_F).astype(BF16)
    r_ref[...] = jnp.dot(xn, wr_ref[...], preferred_element_type=F32).astype(BF16)
    f = mm(_OFF_F, _OFF_END).astype(BF16)
    cs = cs_ref[...].astype(BF16)
    for grp in range(FNET_GROUPS):
        sl = slice(grp * FNET_GROUP_DIM, (grp + 1) * FNET_GROUP_DIM)
        pq = jnp.dot(f[:, sl], cs, preferred_element_type=F32)
        p_ref[:, sl] = pq[:, :FNET_GROUP_DIM].astype(BF16)
        qq_ref[:, sl] = pq[:, FNET_GROUP_DIM:].astype(BF16)


def _proj_call(x2d, nw, w_main, w_r, cs, rows_per_batch, tm):
    m = x2d.shape[0]
    nt = rows_per_batch // tm
    nb = m // rows_per_batch
    row = lambda n: pl.BlockSpec((tm, n), lambda i: (i, 0))
    pq_spec = pl.BlockSpec((tm, FNET_WIDTH), lambda i: (i % nt, i // nt))
    out_shape = (
        jax.ShapeDtypeStruct((m, GLA_DK), BF16), jax.ShapeDtypeStruct((m, GLA_DK), BF16),
        jax.ShapeDtypeStruct((m, GLA_DV), BF16), jax.ShapeDtypeStruct((m, R_PAD), BF16),
        jax.ShapeDtypeStruct((rows_per_batch, nb * FNET_WIDTH), BF16),
        jax.ShapeDtypeStruct((rows_per_batch, nb * FNET_WIDTH), BF16),
    )
    return pl.pallas_call(
        _proj_kernel,
        grid=(m // tm,),
        in_specs=[row(D_MODEL), _resident((1, D_MODEL)), _resident(w_main.shape),
                  _resident(w_r.shape), _resident(cs.shape)],
        out_specs=(row(GLA_DK), row(GLA_DK), row(GLA_DV), row(R_PAD), pq_spec, pq_spec),
        out_shape=out_shape,
        compiler_params=pltpu.CompilerParams(
            dimension_semantics=("arbitrary",), vmem_limit_bytes=V7X_VMEM_LIMIT),
        name="proj",
    )(x2d, nw, w_main, w_r, cs)


_NT = (((1,), (1,)), ((), ()))
_TN = (((0,), (0,)), ((), ()))


def _gla_local_kernel(q_ref, k_ref, v_ref, r_ref, wdec_ref, bdec_ref, trif_ref, trib_ref,
                      maskf_ref, maskb_ref, ol_ref, qe_ref, ke_ref, bend_ref, qd_ref, kd_ref,
                      *, chunk):
    tm = q_ref.shape[0]
    n, s = chunk // SUB, SUB
    x = jnp.dot(r_ref[...], wdec_ref[...], preferred_element_type=F32) + bdec_ref[...]
    la = (jnp.minimum(x, 0.0) - jnp.log(1.0 + jnp.exp(-jnp.abs(x)))) * (LOG2E / GATE_TEMP)
    hi = la.astype(BF16)
    lo = (la - hi.astype(F32)).astype(BF16)
    tris = (trif_ref[...], trib_ref[...])
    for c in range(tm // chunk):
        rows = slice(c * chunk, (c + 1) * chunk)
        q3 = q_ref[rows, :].astype(F32).reshape(n, s, GLA_DK)
        k3 = k_ref[rows, :].astype(F32).reshape(n, s, GLA_DK)
        for d in range(2):
            cols = slice(d * GLA_DK, (d + 1) * GLA_DK)
            b = (jnp.dot(tris[d], hi[rows, cols], preferred_element_type=F32)
                 + jnp.dot(tris[d], lo[rows, cols], preferred_element_type=F32))
            b = b.reshape(n, s, GLA_DK)
            if d == 0:
                b_mid, b_end = b[:, s // 2 - 1:s // 2, :], b[:, s - 1:s, :]
            else:
                b_mid, b_end = b[:, s // 2:s // 2 + 1, :], b[:, 0:1, :]
            qd = q3 * jnp.exp2(b - b_mid)
            kd = k3 * jnp.exp2(b_mid - b)
            qe = qd * jnp.exp2(b_mid)
            ke = kd * jnp.exp2(b_end - b_mid)
            qd_ref[d, rows, :] = qd.reshape(chunk, GLA_DK).astype(BF16)
            kd_ref[d, rows, :] = kd.reshape(chunk, GLA_DK).astype(BF16)
            qe_ref[rows, cols] = qe.reshape(chunk, GLA_DK).astype(BF16)
            ke_ref[rows, cols] = ke.reshape(chunk, GLA_DK).astype(BF16)
            bend_ref[c * n:(c + 1) * n, cols] = b_end.reshape(n, GLA_DK)
    masks = (maskf_ref[...] != 0.0, maskb_ref[...] != 0.0)
    for c in range(tm // chunk):
        rows = slice(c * chunk, (c + 1) * chunk)
        for h in range(GLA_HEADS):
            hk = slice(h * HEAD_DK, (h + 1) * HEAD_DK)
            hv = slice(h * HEAD_DV, (h + 1) * HEAD_DV)
            sc = [lax.dot_general(qd_ref[d, rows, hk], kd_ref[d, rows, hk], _NT,
                                  preferred_element_type=F32) for d in range(2)]
            sc = (jnp.where(masks[0], sc[0], 0.0) + jnp.where(masks[1], sc[1], 0.0)).astype(BF16)
            ol_ref[rows, hv] = jnp.dot(sc, v_ref[rows, hv], preferred_element_type=F32)


def _gla_local_call(q, k, v, r, wdec, bdec, tri_f, tri_b, tm, chunk):
    m = q.shape[0]
    row = lambda n: pl.BlockSpec((tm, n), lambda i: (i, 0))
    tri_bf = (jnp.asarray(tri_f).astype(BF16), jnp.asarray(tri_b).astype(BF16))
    masks = (jnp.asarray(tri_f), jnp.asarray(tri_b))
    sq = _resident((chunk, chunk))
    return pl.pallas_call(
        functools.partial(_gla_local_kernel, chunk=chunk),
        grid=(m // tm,),
        in_specs=[row(GLA_DK), row(GLA_DK), row(GLA_DV), row(R_PAD), _resident(wdec.shape),
                  _resident(bdec.shape), sq, sq, sq, sq],
        out_specs=(row(GLA_DV), row(2 * GLA_DK), row(2 * GLA_DK),
                   pl.BlockSpec((tm // SUB, 2 * GLA_DK), lambda i: (i, 0))),
        out_shape=(jax.ShapeDtypeStruct((m, GLA_DV), F32),
                   jax.ShapeDtypeStruct((m, 2 * GLA_DK), BF16),
                   jax.ShapeDtypeStruct((m, 2 * GLA_DK), BF16),
                   jax.ShapeDtypeStruct((m // SUB, 2 * GLA_DK), F32)),
        scratch_shapes=[pltpu.VMEM((2, tm, GLA_DK), BF16), pltpu.VMEM((2, tm, GLA_DK), BF16)],
        compiler_params=pltpu.CompilerParams(
            dimension_semantics=("arbitrary",), vmem_limit_bytes=V7X_VMEM_LIMIT),
        name="gla_local",
    )(q, k, v, r, wdec, bdec, tri_bf[0], tri_bf[1], masks[0], masks[1])


SCAN_G = 512
SCAN_RB = 1024


def _scan_group(dirs):
    s, n = SUB, SCAN_G // SUB
    ahead = 2
    dec, st, inter = [], [], [{}, {}]
    for qe_ref, ke_ref, v_ref, bend_ref, st_ref, row0, chunk0, backward in dirs:
        dec_rows = jnp.tile(jnp.exp2(bend_ref[chunk0:chunk0 + n, :]), (HEAD_DK // n, 1))
        dec.append(dec_rows.T)
        st.append(st_ref[...])

    def order(step, backward):
        return n - 1 - step if backward else step

    def increment(d, step):
        _, ke_ref, v_ref, _, _, row0, _, backward = dirs[d]
        i = order(step, backward)
        sub = slice(row0 + i * s, row0 + (i + 1) * s)
        return lax.dot_general(ke_ref[sub, :], v_ref[sub, :], _TN,
                               preferred_element_type=F32)

    inc = {(d, step): increment(d, step) for step in range(ahead) for d in range(2)}
    for step in range(n):
        for d, (qe_ref, _, _, _, _, row0, _, backward) in enumerate(dirs):
            i = order(step, backward)
            sub = slice(row0 + i * s, row0 + (i + 1) * s)
            inter[d][i] = jnp.dot(qe_ref[sub, :], st[d].astype(BF16),
                                  preferred_element_type=F32)
            if step + ahead < n:
                inc[d, step + ahead] = increment(d, step + ahead)
            decay = jnp.broadcast_to(dec[d][:, i:i + 1], (HEAD_DK, HEAD_DV))
            st[d] = st[d] * decay + inc.pop((d, step))
    for d, spec in enumerate(dirs):
        spec[4][...] = st[d]
    return [jnp.concatenate([inter[d][i] for i in range(n)], axis=0) for d in range(2)]


def _gla_scan_kernel(qef_ref, kef_ref, vf_ref, olf_ref, bendf_ref,
                     qeb_ref, keb_ref, vb_ref, bendb_ref, kem_ref, vm_ref,
                     of_ref, ob_ref, stf_ref, stb_ref):
    rb = qef_ref.shape[0]
    ng = rb // SCAN_G

    @pl.when(pl.program_id(2) == 0)
    def _():
        stf_ref[...] = lax.dot_general(kem_ref[...], vm_ref[...], _TN,
                                       preferred_element_type=F32)
        stb_ref[...] = jnp.zeros_like(stb_ref)

    for j in range(ng):
        jb = ng - 1 - j
        cpg = SCAN_G // SUB
        in_f, in_b = _scan_group([
            (qef_ref, kef_ref, vf_ref, bendf_ref, stf_ref, j * SCAN_G, j * cpg, False),
            (qeb_ref, keb_ref, vb_ref, bendb_ref, stb_ref, jb * SCAN_G, jb * cpg, True)])
        rows_f = slice(j * SCAN_G, (j + 1) * SCAN_G)
        of_ref[rows_f, :] = (olf_ref[rows_f, :] + in_f).astype(BF16)
        ob_ref[jb * SCAN_G:(jb + 1) * SCAN_G, :] = in_b.astype(BF16)


def _gla_scan_call(qe, ke, v, ol, bend, kem, vm, seq, batch):
    h, rb = GLA_HEADS, SCAN_RB
    nt = seq // rb
    fwd = lambda b, hh, t: b * nt + t
    bwd = lambda b, hh, t: b * nt + nt - 1 - t
    cb = rb // SUB

    def dk_blk(rowfn, dircol, rows):
        return pl.BlockSpec((rows, HEAD_DK), lambda b, hh, t: (rowfn(b, hh, t), dircol + hh))

    def dv_blk(rowfn):
        return pl.BlockSpec((rb, HEAD_DV), lambda b, hh, t: (rowfn(b, hh, t), hh))

    in_specs = [
        dk_blk(fwd, 0, rb), dk_blk(fwd, 0, rb), dv_blk(fwd), dv_blk(fwd), dk_blk(fwd, 0, cb),
        dk_blk(bwd, h, rb), dk_blk(bwd, h, rb), dv_blk(bwd), dk_blk(bwd, h, cb),
        pl.BlockSpec((SUB, HEAD_DK), lambda b, hh, t: (0, hh)),
        pl.BlockSpec((SUB, HEAD_DV), lambda b, hh, t: (0, hh)),
    ]
    out = jax.ShapeDtypeStruct((batch * seq, GLA_DV), BF16)
    return pl.pallas_call(
        _gla_scan_kernel,
        grid=(batch, h, nt),
        in_specs=in_specs,
        out_specs=(dv_blk(fwd), dv_blk(bwd)),
        out_shape=(out, out),
        scratch_shapes=[pltpu.VMEM((HEAD_DK, HEAD_DV), F32), pltpu.VMEM((HEAD_DK, HEAD_DV), F32)],
        compiler_params=pltpu.CompilerParams(
            dimension_semantics=("arbitrary", "arbitrary", "arbitrary"),
            vmem_limit_bytes=V7X_VMEM_LIMIT),
        name="gla_scan",
    )(qe, ke, v, ol, bend, qe, ke, v, bend, kem, vm)


DFT_TILE = 256


def _mirror(j_ref, a_ref, b_ref):
    t = DFT_TILE
    return (jnp.dot(j_ref[0, :, :t], a_ref[...], preferred_element_type=F32)
            + jnp.dot(j_ref[0, :, t:], b_ref[...], preferred_element_type=F32))


def _fold_kernel(pd_ref, pa_ref, pb_ref, qd_ref, qa_ref, qb_ref, pz_ref, qz_ref, j_ref,
                 pe_ref, qo_ref, *, m_last):
    i = pl.program_id(0)
    first = i == 0
    p_dir = jnp.where(first, pz_ref[...], pd_ref[...]).astype(F32)
    q_dir = jnp.where(first, qz_ref[...], qd_ref[...]).astype(F32)
    m = i * DFT_TILE + lax.broadcasted_iota(jnp.int32, (DFT_TILE, 1), 0)
    keep = m <= m_last
    pe_ref[...] = jnp.where(keep, p_dir + _mirror(j_ref, pa_ref, pb_ref), 0.0).astype(BF16)
    qo_ref[...] = jnp.where(keep, q_dir - _mirror(j_ref, qa_ref, qb_ref), 0.0).astype(BF16)


def _dft_kernel(pe_ref, qo_ref, dc_ref, ds_ref, rowc_ref, rows_ref, u_ref, w_ref, *, scale):
    t = DFT_TILE
    acc_a = acc_b = None
    for c in range(pe_ref.shape[0] // t):
        sl = slice(c * t, (c + 1) * t)
        rc, rs = rowc_ref[0, 0:1, sl], rows_ref[0, 0:1, sl]
        dc, ds = dc_ref[:, sl], ds_ref[:, sl]
        tc = (dc * rc - ds * rs).astype(BF16)
        ts = (ds * rc + dc * rs).astype(BF16)
        a = jnp.dot(tc, pe_ref[sl, :], preferred_element_type=F32)
        b = jnp.dot(ts, qo_ref[sl, :], preferred_element_type=F32)
        acc_a = a if acc_a is None else acc_a + a
        acc_b = b if acc_b is None else acc_b + b
    u_ref[...] = ((acc_a - acc_b) * scale).astype(BF16)
    w_ref[...] = ((acc_a + acc_b) * scale).astype(BF16)


def _unfold_kernel(ud_ref, wa_ref, wb_ref, j_ref, y_ref, *, r_last, i_mixed):
    i = pl.program_id(0)

    @pl.when(i < i_mixed)
    def _():
        y_ref[...] = ud_ref[...]

    @pl.when(i >= i_mixed)
    def _():
        r = i * DFT_TILE + lax.broadcasted_iota(jnp.int32, (DFT_TILE, 1), 0)
        y_ref[...] = jnp.where(r <= r_last, ud_ref[...],
                               _mirror(j_ref, wa_ref, wb_ref).astype(BF16))


@functools.lru_cache(maxsize=None)
def _dft_constants(seq, n_meta):
    t = DFT_TILE
    off = t - n_meta
    big_l = seq + n_meta
    half = big_l // 2
    n_real = seq // t
    nf = -(-(half + 1 + off) // t)
    hp = nf * t
    jf = np.zeros((nf, t, 2 * t), np.float32)
    for i in range(nf):
        a_idx, b_idx = min(n_real - i, n_real - 1), n_real - 1 - i
        for j in range(t):
            pos = i * t + j - off
            if 1 <= pos <= half - 1:
                tau, rho = divmod(big_l - pos - n_meta, t)
                assert tau in (a_idx, b_idx)
                jf[i, j, (t if tau == b_idx else 0) + rho] = 1.0
    r_last = half - n_meta
    i_mixed = r_last // t
    ju = np.zeros((3, t, 2 * t), np.float32)
    for i in range(i_mixed, n_real):
        a_idx, b_idx = n_real - i, n_real - 1 - i
        blk = np.zeros((t, 2 * t), np.float32)
        for j in range(t):
            r = i * t + j
            if r > r_last:
                tau, rho = divmod(big_l - (r + n_meta) + off, t)
                assert tau in (a_idx, b_idx)
                blk[j, (t if tau == b_idx else 0) + rho] = 1.0
        var = 1 if i == i_mixed else 2
        assert var == 1 or not ju[2].any() or (ju[2] == blk).all()
        ju[var] = blk
    ang = lambda prod: 2.0 * np.pi * (prod % big_l).astype(np.float64) / big_l
    pos = np.arange(hp, dtype=np.int64) - off
    d = ang(np.arange(t, dtype=np.int64)[:, None] * pos[None, :])
    k0 = t * np.arange(nf, dtype=np.int64) - off
    row = np.broadcast_to(ang(k0[:, None] * pos[None, :])[:, None, :], (nf, 8, hp))
    f = lambda a: np.ascontiguousarray(a, dtype=np.float32)
    return dict(jf=jf, ju=ju, dc=f(np.cos(d)), ds=f(np.sin(d)), rowc=f(np.cos(row)),
                rows=f(np.sin(row)), nf=nf, hp=hp, off=off, n_real=n_real,
                m_last=half + off, r_last=r_last, i_mixed=i_mixed)


def _seq_dft(p, q, pm, qm, seq, n_meta):
    c = _dft_constants(seq, n_meta)
    t, nf, hp, n_real, ncol = DFT_TILE, c["nf"], c["hp"], c["n_real"], p.shape[1]
    params = pltpu.CompilerParams(dimension_semantics=("arbitrary",),
                                  vmem_limit_bytes=V7X_VMEM_LIMIT)
    tile = lambda fn: pl.BlockSpec((t, ncol), lambda i: (fn(i), 0))
    lead = ((c["off"], 0), (0, 0))
    src = [tile(lambda i: jnp.maximum(i - 1, 0)),
           tile(lambda i: jnp.minimum(n_real - i, n_real - 1)),
           tile(lambda i: n_real - 1 - i)]
    pe, qo = pl.pallas_call(
        functools.partial(_fold_kernel, m_last=c["m_last"]),
        grid=(nf,),
        in_specs=src + src + [_resident((t, ncol)), _resident((t, ncol)),
                              pl.BlockSpec((1, t, 2 * t), lambda i: (i, 0, 0))],
        out_specs=(tile(lambda i: i), tile(lambda i: i)),
        out_shape=(jax.ShapeDtypeStruct((hp, ncol), BF16),) * 2,
        compiler_params=params,
        name="dft_fold",
    )(p, p, p, q, q, q, jnp.pad(pm, lead), jnp.pad(qm, lead), jnp.asarray(c["jf"]).astype(BF16))

    scale = float((seq + n_meta) * FNET_GROUP_DIM) ** -0.5
    rowspec = pl.BlockSpec((1, 8, hp), lambda i: (i, 0, 0))
    u, w = pl.pallas_call(
        functools.partial(_dft_kernel, scale=scale),
        grid=(nf,),
        in_specs=[_resident((hp, ncol)), _resident((hp, ncol)), _resident((t, hp)),
                  _resident((t, hp)), rowspec, rowspec],
        out_specs=(tile(lambda i: i), tile(lambda i: i)),
        out_shape=(jax.ShapeDtypeStruct((hp, ncol), BF16),) * 2,
        compiler_params=params,
        name="seq_dft",
    )(pe, qo, jnp.asarray(c["dc"]), jnp.asarray(c["ds"]), jnp.asarray(c["rowc"]),
      jnp.asarray(c["rows"]))

    i_mixed = c["i_mixed"]
    clip = lambda v: jnp.clip(v, 0, nf - 1)
    return pl.pallas_call(
        functools.partial(_unfold_kernel, r_last=c["r_last"], i_mixed=i_mixed),
        grid=(n_real,),
        in_specs=[tile(lambda i: clip(i + 1)), tile(lambda i: clip(n_real - i)),
                  tile(lambda i: clip(n_real - 1 - i)),
                  pl.BlockSpec((1, t, 2 * t),
                               lambda i: (jnp.where(i < i_mixed, 0,
                                                    jnp.where(i == i_mixed, 1, 2)), 0, 0))],
        out_specs=tile(lambda i: i),
        out_shape=jax.ShapeDtypeStruct((seq, ncol), BF16),
        compiler_params=params,
        name="dft_unfold",
    )(u, w, w, jnp.asarray(c["ju"]).astype(BF16))


FFN_CHUNK = 1024


def _tail_kernel(x_ref, of_ref, ob_ref, y_ref, nmix_ref, gn_ref, nffn_ref, nfin_ref,
                 wg_ref, wo_ref, wf_ref, wout_ref, w1_ref, w2_ref, o_ref):
    x = x_ref[...]
    xn = _rms(x, nmix_ref[...]).astype(BF16)

    def gate(lo, hi):
        return jnp.dot(xn, wg_ref[:, lo:hi], preferred_element_type=F32)

    y_gla = None
    for h in range(GLA_HEADS):
        hv = slice(h * HEAD_DV, (h + 1) * HEAD_DV)
        o = of_ref[:, hv].astype(F32) + ob_ref[:, hv].astype(F32)
        g = gate(_OFF_G + h * HEAD_DV, _OFF_G + (h + 1) * HEAD_DV)
        og = (_rms(o, gn_ref[...]) * (g * _sigmoid(g))).astype(BF16)
        part = jnp.dot(og, wo_ref[hv, :], preferred_element_type=F32)
        y_gla = part if y_gla is None else y_gla + part
    y_fnet = jnp.dot(y_ref[...], wf_ref[...], preferred_element_type=F32)
    merged = (_sigmoid(gate(_OFF_GA, _OFF_GF)) * y_gla
              + _sigmoid(gate(_OFF_GF, _OFF_GEND)) * y_fnet)
    h1 = x + jnp.dot(merged.astype(BF16), wout_ref[...], preferred_element_type=F32)
    u = _rms(h1, nffn_ref[...]).astype(BF16)
    acc = h1
    for c in range(D_FF // FFN_CHUNK):
        sl = slice(c * FFN_CHUNK, (c + 1) * FFN_CHUNK)
        a = jnp.maximum(jnp.dot(u, w1_ref[:, sl], preferred_element_type=F32), 0.0)
        acc = acc + jnp.dot((a * a).astype(BF16), w2_ref[sl, :], preferred_element_type=F32)
    o_ref[...] = _rms(acc, nfin_ref[...])


def _tail_call(x2d, o_f, o_b, y, nmix, gn, nffn, nfin, wg, wo, wf, wout, w1, w2, seq, tm):
    m = x2d.shape[0]
    nt = seq // tm
    row = lambda n: pl.BlockSpec((tm, n), lambda i: (i, 0))
    consts = [nmix, gn, nffn, nfin, wg, wo, wf, wout, w1, w2]
    return pl.pallas_call(
        _tail_kernel,
        grid=(m // tm,),
        in_specs=[row(D_MODEL), row(GLA_DV), row(GLA_DV),
                  pl.BlockSpec((tm, FNET_WIDTH), lambda i: (i % nt, i // nt))]
                 + [_resident(c.shape) for c in consts],
        out_specs=row(D_MODEL),
        out_shape=jax.ShapeDtypeStruct((m, D_MODEL), F32),
        compiler_params=pltpu.CompilerParams(
            dimension_semantics=("arbitrary",), vmem_limit_bytes=V7X_VMEM_LIMIT),
        name="tail",
    )(x2d, o_f, o_b, y, *consts)


@functools.lru_cache(maxsize=None)
def _gla_constants(rows):
    idx = np.arange(rows)
    same = (idx[:, None] // SUB) == (idx[None, :] // SUB)
    tri_f = (same & (idx[None, :] <= idx[:, None])).astype(np.float32)
    return tri_f, np.ascontiguousarray(tri_f.T)


@functools.lru_cache(maxsize=None)
def _channel_dft():
    d = np.arange(FNET_GROUP_DIM, dtype=np.int64)
    ang = 2.0 * np.pi * ((d[:, None] * d[None, :]) % FNET_GROUP_DIM) / FNET_GROUP_DIM
    return np.concatenate([np.cos(ang), np.sin(ang)], axis=1).astype(np.float32)


def kernel(x, meta_tokens, norm_mix, w_in, w_decay_fwd, b_decay_fwd, w_decay_bwd, b_decay_bwd,
           gla_norm, w_o_gla, w_fnet, w_out, norm_ffn, w_ff1, w_ff2, norm_final):
    batch, seq, d = x.shape
    assert d == D_MODEL and meta_tokens.shape == (N_META, D_MODEL)
    assert w_in.shape[0] == 1 and seq % (2 * SCAN_RB) == 0 and seq % DFT_TILE == 0
    tm = 512

    wi = w_in[0]
    splits = np.cumsum([GLA_DK, GLA_DK, GLA_DV, GLA_DV, DECAY_RANK, DECAY_RANK, FNET_WIDTH,
                        D_MODEL])
    wq, wk, wv, wg, wrf, wrb, wf_in, wga, wgf = jnp.split(wi, [int(s) for s in splits], axis=1)
    w_main = jnp.concatenate([wq, wk, wv, wf_in], axis=1).astype(BF16)
    w_gate = jnp.concatenate([wg, wga, wgf], axis=1).astype(BF16)
    w_r = jnp.concatenate(
        [wrf, wrb, jnp.zeros((D_MODEL, R_PAD - 2 * DECAY_RANK), F32)], axis=1).astype(BF16)
    wdec = jnp.zeros((R_PAD, 2 * GLA_DK), F32)
    wdec = wdec.at[:DECAY_RANK, :GLA_DK].set(w_decay_fwd[0])
    wdec = wdec.at[DECAY_RANK:2 * DECAY_RANK, GLA_DK:].set(w_decay_bwd[0]).astype(BF16)
    bdec = jnp.concatenate([b_decay_fwd[0], b_decay_bwd[0]])[None, :]
    cs = jnp.asarray(_channel_dft())
    nw_mix = norm_mix[0][None, :]

    x2d = x.reshape(batch * seq, D_MODEL)
    q, k, v, r, p, pq_q = _proj_call(x2d, nw_mix, w_main, w_r, cs, seq, tm)
    _, km, vm, rm, pm, qm = _proj_call(meta_tokens, nw_mix, w_main, w_r, cs, N_META, N_META)

    lead = ((SUB - N_META, 0), (0, 0))
    km, vm, rm = jnp.pad(km, lead), jnp.pad(vm, lead), jnp.pad(rm, lead)
    ol, qe, ke, bend = _gla_local_call(q, k, v, r, wdec, bdec, *_gla_constants(GLA_C),
                                       GLA_TM, GLA_C)
    _, _, kem, _ = _gla_local_call(km, km, vm, rm, wdec, bdec, *_gla_constants(SUB), SUB, SUB)
    o_f, o_b = _gla_scan_call(qe, ke, v, ol, bend, kem, vm, seq, batch)

    y = _seq_dft(p, pq_q, jnp.tile(pm, (1, batch)), jnp.tile(qm, (1, batch)), seq, N_META)

    out = _tail_call(x2d, o_f, o_b, y, nw_mix, gla_norm[0][None, :], norm_ffn[0][None, :],
                     norm_final[None, :], w_gate, w_o_gla[0].astype(BF16),
                     w_fnet[0].astype(BF16), w_out[0].astype(BF16), w_ff1[0].astype(BF16),
                     w_ff2[0].astype(BF16), seq, tm)
    return out.reshape(batch, seq, D_MODEL)
```

```python
import functools

import numpy as np
import jax
import jax.numpy as jnp
from jax import lax
from jax.experimental import pallas as pl
from jax.experimental.pallas import tpu as pltpu

F32 = jnp.float32
BF16 = jnp.bfloat16

D_MODEL = 1024
N_META = 16
GLA_HEADS = 4
GLA_DK = 512
GLA_DV = 1024
HEAD_DK = GLA_DK // GLA_HEADS
HEAD_DV = GLA_DV // GLA_HEADS
DECAY_RANK = 16
GATE_TEMP = 16.0
SUB = 64
FNET_GROUPS = 4
FNET_WIDTH = 512
FNET_GROUP_DIM = FNET_WIDTH // FNET_GROUPS
D_FF = 4 * D_MODEL
EPS = 1e-6
LOG2E = 1.4426950408889634

_OFF_Q, _OFF_K, _OFF_V, _OFF_F, _OFF_END = 0, 512, 1024, 2048, 2560
_OFF_G, _OFF_GA, _OFF_GF, _OFF_GEND = 0, 1024, 2048, 3072
R_PAD = 128
PROJ_SPLIT = 2

GLA_C = 4 * SUB
GLA_TM = 512

LANE = 128
V7X_VMEM_LIMIT = 56 * 1024 * 1024


def _rms(x, w):
    ms = jnp.mean(x * x, axis=-1, keepdims=True)
    return x * lax.rsqrt(ms + EPS) * w


def _sigmoid(x):
    return 1.0 / (1.0 + jnp.exp(-x))


def _resident(shape):
    nd = len(shape)
    return pl.BlockSpec(shape, lambda *_: (0,) * nd, pipeline_mode=pl.Buffered(1))


def _proj_kernel(x_ref, nw_ref, w_ref, wr_ref, cs_ref,
                 q_ref, k_ref, v_ref, r_ref, p_ref, qq_ref):
    tm = x_ref.shape[0]
    nsplit = PROJ_SPLIT if tm % (16 * PROJ_SPLIT) == 0 else 1
    sub = tm // nsplit
    tiles = [(dict(), slice(i * sub, (i + 1) * sub)) for i in range(nsplit)]
    cs = cs_ref[...].astype(BF16)

    def norm_in(s, r):
        s["xn"] = _rms(x_ref[r, :], nw_ref[...]).astype(BF16)

    def project(s, r):
        mm = lambda lo, hi: jnp.dot(s["xn"], w_ref[:, lo:hi], preferred_element_type=F32)
        q_ref[r, :] = (mm(_OFF_Q, _OFF_K) * (HEAD_DK ** -0.5)).astype(BF16)
        k_ref[r, :] = mm(_OFF_K, _OFF_V).astype(BF16)
        s["f"] = mm(_OFF_F, _OFF_END).astype(BF16)
        v_ref[r, :] = mm(_OFF_V, _OFF_F).astype(BF16)
        r_ref[r, :] = jnp.dot(s["xn"], wr_ref[...], preferred_element_type=F32).astype(BF16)

    def channel_dft(s, r):
        for grp in range(FNET_GROUPS):
            sl = slice(grp * FNET_GROUP_DIM, (grp + 1) * FNET_GROUP_DIM)
            pq = jnp.dot(s["f"][:, sl], cs, preferred_element_type=F32)
            p_ref[r, sl] = pq[:, :FNET_GROUP_DIM].astype(BF16)
            qq_ref[r, sl] = pq[:, FNET_GROUP_DIM:].astype(BF16)

    for stage in (norm_in, project, channel_dft):
        for s, r in tiles:
            stage(s, r)


def _proj_call(x2d, nw, w_main, w_r, cs, rows_per_batch, tm):
    m = x2d.shape[0]
    nt = rows_per_batch // tm
    nb = m // rows_per_batch
    row = lambda n: pl.BlockSpec((tm, n), lambda i: (i, 0))
    pq_spec = pl.BlockSpec((tm, FNET_WIDTH), lambda i: (i % nt, i // nt))
    out_shape = (
        jax.ShapeDtypeStruct((m, GLA_DK), BF16), jax.ShapeDtypeStruct((m, GLA_DK), BF16),
        jax.ShapeDtypeStruct((m, GLA_DV), BF16), jax.ShapeDtypeStruct((m, R_PAD), BF16),
        jax.ShapeDtypeStruct((rows_per_batch, nb * FNET_WIDTH), BF16),
        jax.ShapeDtypeStruct((rows_per_batch, nb * FNET_WIDTH), BF16),
    )
    return pl.pallas_call(
        _proj_kernel,
        grid=(m // tm,),
        in_specs=[row(D_MODEL), _resident((1, D_MODEL)), _resident(w_main.shape),
                  _resident(w_r.shape), _resident(cs.shape)],
        out_specs=(row(GLA_DK), row(GLA_DK), row(GLA_DV), row(R_PAD), pq_spec, pq_spec),
        out_shape=out_shape,
        compiler_params=pltpu.CompilerParams(
            dimension_semantics=("arbitrary",), vmem_limit_bytes=V7X_VMEM_LIMIT),
        name="proj",
    )(x2d, nw, w_main, w_r, cs)


_NT = (((1,), (1,)), ((), ()))
_TN = (((0,), (0,)), ((), ()))


def _gla_local_kernel(q_ref, k_ref, v_ref, r_ref, wdec_ref, bdec_ref, trif_ref, trib_ref,
                      maskf_ref, maskb_ref, ol_ref, qe_ref, ke_ref, bend_ref, qd_ref, kd_ref,
                      *, chunk):
    tm = q_ref.shape[0]
    n, s = chunk // SUB, SUB
    x = jnp.dot(r_ref[...], wdec_ref[...], preferred_element_type=F32) + bdec_ref[...]
    la = (jnp.minimum(x, 0.0) - jnp.log(1.0 + jnp.exp(-jnp.abs(x)))) * (LOG2E / GATE_TEMP)
    hi = la.astype(BF16)
    lo = (la - hi.astype(F32)).astype(BF16)
    tris = (trif_ref[...], trib_ref[...])
    for c in range(tm // chunk):
        rows = slice(c * chunk, (c + 1) * chunk)
        q3 = q_ref[rows, :].astype(F32).reshape(n, s, GLA_DK)
        k3 = k_ref[rows, :].astype(F32).reshape(n, s, GLA_DK)
        for d in range(2):
            cols = slice(d * GLA_DK, (d + 1) * GLA_DK)
            b = (jnp.dot(tris[d], hi[rows, cols], preferred_element_type=F32)
                 + jnp.dot(tris[d], lo[rows, cols], preferred_element_type=F32))
            b = b.reshape(n, s, GLA_DK)
            if d == 0:
                b_mid, b_end = b[:, s // 2 - 1:s // 2, :], b[:, s - 1:s, :]
            else:
                b_mid, b_end = b[:, s // 2:s // 2 + 1, :], b[:, 0:1, :]
            qd = q3 * jnp.exp2(b - b_mid)
            kd = k3 * jnp.exp2(b_mid - b)
            qe = qd * jnp.exp2(b_mid)
            ke = kd * jnp.exp2(b_end - b_mid)
            qd_ref[d, rows, :] = qd.reshape(chunk, GLA_DK).astype(BF16)
            kd_ref[d, rows, :] = kd.reshape(chunk, GLA_DK).astype(BF16)
            qe_ref[rows, cols] = qe.reshape(chunk, GLA_DK).astype(BF16)
            ke_ref[rows, cols] = ke.reshape(chunk, GLA_DK).astype(BF16)
            bend_ref[c * n:(c + 1) * n, cols] = b_end.reshape(n, GLA_DK)
    masks = (maskf_ref[...] != 0.0, maskb_ref[...] != 0.0)
    for c in range(tm // chunk):
        rows = slice(c * chunk, (c + 1) * chunk)
        for h in range(GLA_HEADS):
            hk = slice(h * HEAD_DK, (h + 1) * HEAD_DK)
            hv = slice(h * HEAD_DV, (h + 1) * HEAD_DV)
            sc = [lax.dot_general(qd_ref[d, rows, hk], kd_ref[d, rows, hk], _NT,
                                  preferred_element_type=F32) for d in range(2)]
            sc = (jnp.where(masks[0], sc[0], 0.0) + jnp.where(masks[1], sc[1], 0.0)).astype(BF16)
            ol_ref[rows, hv] = jnp.dot(sc, v_ref[rows, hv], preferred_element_type=F32)


def _gla_local_call(q, k, v, r, wdec, bdec, tri_f, tri_b, tm, chunk):
    m = q.shape[0]
    row = lambda n: pl.BlockSpec((tm, n), lambda i: (i, 0))
    tri_bf = (jnp.asarray(tri_f).astype(BF16), jnp.asarray(tri_b).astype(BF16))
    masks = (jnp.asarray(tri_f), jnp.asarray(tri_b))
    sq = _resident((chunk, chunk))
    return pl.pallas_call(
        functools.partial(_gla_local_kernel, chunk=chunk),
        grid=(m // tm,),
        in_specs=[row(GLA_DK), row(GLA_DK), row(GLA_DV), row(R_PAD), _resident(wdec.shape),
                  _resident(bdec.shape), sq, sq, sq, sq],
        out_specs=(row(GLA_DV), row(2 * GLA_DK), row(2 * GLA_DK),
                   pl.BlockSpec((tm // SUB, 2 * GLA_DK), lambda i: (i, 0))),
        out_shape=(jax.ShapeDtypeStruct((m, GLA_DV), F32),
                   jax.ShapeDtypeStruct((m, 2 * GLA_DK), BF16),
                   jax.ShapeDtypeStruct((m, 2 * GLA_DK), BF16),
                   jax.ShapeDtypeStruct((m // SUB, 2 * GLA_DK), F32)),
        scratch_shapes=[pltpu.VMEM((2, tm, GLA_DK), BF16), pltpu.VMEM((2, tm, GLA_DK), BF16)],
        compiler_params=pltpu.CompilerParams(
            dimension_semantics=("arbitrary",), vmem_limit_bytes=V7X_VMEM_LIMIT),
        name="gla_local",
    )(q, k, v, r, wdec, bdec, tri_bf[0], tri_bf[1], masks[0], masks[1])


SCAN_G = 512
SCAN_RB = 1024


def _scan_group(dirs):
    s, n = SUB, SCAN_G // SUB
    ahead = 2
    dec, st, inter = [], [], [{}, {}]
    for qe_ref, ke_ref, v_ref, bend_ref, st_ref, row0, chunk0, backward in dirs:
        dec_rows = jnp.tile(jnp.exp2(bend_ref[chunk0:chunk0 + n, :]), (HEAD_DK // n, 1))
        dec.append(dec_rows.T)
        st.append(st_ref[...])

    def order(step, backward):
        return n - 1 - step if backward else step

    def increment(d, step):
        _, ke_ref, v_ref, _, _, row0, _, backward = dirs[d]
        i = order(step, backward)
        sub = slice(row0 + i * s, row0 + (i + 1) * s)
        return lax.dot_general(ke_ref[sub, :], v_ref[sub, :], _TN,
                               preferred_element_type=F32)

    inc = {(d, step): increment(d, step) for step in range(ahead) for d in range(2)}
    for step in range(n):
        for d, (qe_ref, _, _, _, _, row0, _, backward) in enumerate(dirs):
            i = order(step, backward)
            sub = slice(row0 + i * s, row0 + (i + 1) * s)
            inter[d][i] = jnp.dot(qe_ref[sub, :], st[d].astype(BF16),
                                  preferred_element_type=F32)
            if step + ahead < n:
                inc[d, step + ahead] = increment(d, step + ahead)
            decay = jnp.broadcast_to(dec[d][:, i:i + 1], (HEAD_DK, HEAD_DV))
            st[d] = st[d] * decay + inc.pop((d, step))
    for d, spec in enumerate(dirs):
        spec[4][...] = st[d]
    return [jnp.concatenate([inter[d][i] for i in range(n)], axis=0) for d in range(2)]


def _gla_scan_kernel(qef_ref, kef_ref, vf_ref, bendf_ref,
                     qeb_ref, keb_ref, vb_ref, bendb_ref, kem_ref, vm_ref,
                     of_ref, ob_ref, stf_ref, stb_ref):
    rb = qef_ref.shape[0]
    ng = rb // SCAN_G

    @pl.when(pl.program_id(2) == 0)
    def _():
        stf_ref[...] = lax.dot_general(kem_ref[...], vm_ref[...], _TN,
                                       preferred_element_type=F32)
        stb_ref[...] = jnp.zeros_like(stb_ref)

    for j in range(ng):
        jb = ng - 1 - j
        cpg = SCAN_G // SUB
        in_f, in_b = _scan_group([
            (qef_ref, kef_ref, vf_ref, bendf_ref, stf_ref, j * SCAN_G, j * cpg, False),
            (qeb_ref, keb_ref, vb_ref, bendb_ref, stb_ref, jb * SCAN_G, jb * cpg, True)])
        of_ref[j * SCAN_G:(j + 1) * SCAN_G, :] = in_f.astype(BF16)
        ob_ref[jb * SCAN_G:(jb + 1) * SCAN_G, :] = in_b.astype(BF16)


def _gla_scan_call(qe, ke, v, bend, kem, vm, seq, batch):
    h, rb = GLA_HEADS, SCAN_RB
    nt = seq // rb
    fwd = lambda b, hh, t: b * nt + t
    bwd = lambda b, hh, t: b * nt + nt - 1 - t
    cb = rb // SUB

    def dk_blk(rowfn, dircol, rows):
        return pl.BlockSpec((rows, HEAD_DK), lambda b, hh, t: (rowfn(b, hh, t), dircol + hh))

    def dv_blk(rowfn):
        return pl.BlockSpec((rb, HEAD_DV), lambda b, hh, t: (rowfn(b, hh, t), hh))

    in_specs = [
        dk_blk(fwd, 0, rb), dk_blk(fwd, 0, rb), dv_blk(fwd), dk_blk(fwd, 0, cb),
        dk_blk(bwd, h, rb), dk_blk(bwd, h, rb), dv_blk(bwd), dk_blk(bwd, h, cb),
        pl.BlockSpec((SUB, HEAD_DK), lambda b, hh, t: (0, hh)),
        pl.BlockSpec((SUB, HEAD_DV), lambda b, hh, t: (0, hh)),
    ]
    out = jax.ShapeDtypeStruct((batch * seq, GLA_DV), BF16)
    return pl.pallas_call(
        _gla_scan_kernel,
        grid=(batch, h, nt),
        in_specs=in_specs,
        out_specs=(dv_blk(fwd), dv_blk(bwd)),
        out_shape=(out, out),
        scratch_shapes=[pltpu.VMEM((HEAD_DK, HEAD_DV), F32), pltpu.VMEM((HEAD_DK, HEAD_DV), F32)],
        compiler_params=pltpu.CompilerParams(
            dimension_semantics=("arbitrary", "arbitrary", "arbitrary"),
            vmem_limit_bytes=V7X_VMEM_LIMIT),
        name="gla_scan",
    )(qe, ke, v, bend, qe, ke, v, bend, kem, vm)


DFT_TILE = 256


def _mirror(j_ref, a_ref, b_ref):
    t = DFT_TILE
    return (jnp.dot(j_ref[0, :, :t], a_ref[...], preferred_element_type=F32)
            + jnp.dot(j_ref[0, :, t:], b_ref[...], preferred_element_type=F32))


def _fold_kernel(pd_ref, pa_ref, pb_ref, qd_ref, qa_ref, qb_ref, pz_ref, qz_ref, j_ref,
                 pe_ref, qo_ref, *, m_last):
    i = pl.program_id(0)
    first = i == 0
    p_dir = jnp.where(first, pz_ref[...], pd_ref[...]).astype(F32)
    q_dir = jnp.where(first, qz_ref[...], qd_ref[...]).astype(F32)
    m = i * DFT_TILE + lax.broadcasted_iota(jnp.int32, (DFT_TILE, 1), 0)
    keep = m <= m_last
    pe_ref[...] = jnp.where(keep, p_dir + _mirror(j_ref, pa_ref, pb_ref), 0.0).astype(BF16)
    qo_ref[...] = jnp.where(keep, q_dir - _mirror(j_ref, qa_ref, qb_ref), 0.0).astype(BF16)


def _dft_kernel(pe_ref, qo_ref, dc_ref, ds_ref, rowc_ref, rows_ref, u_ref, w_ref, *, scale):
    t = DFT_TILE
    acc_a = acc_b = None
    for c in range(pe_ref.shape[0] // t):
        sl = slice(c * t, (c + 1) * t)
        rc, rs = rowc_ref[0, 0:1, sl], rows_ref[0, 0:1, sl]
        dc, ds = dc_ref[:, sl], ds_ref[:, sl]
        tc = (dc * rc - ds * rs).astype(BF16)
        ts = (ds * rc + dc * rs).astype(BF16)
        a = jnp.dot(tc, pe_ref[sl, :], preferred_element_type=F32)
        b = jnp.dot(ts, qo_ref[sl, :], preferred_element_type=F32)
        acc_a = a if acc_a is None else acc_a + a
        acc_b = b if acc_b is None else acc_b + b
    u_ref[...] = ((acc_a - acc_b) * scale).astype(BF16)
    w_ref[...] = ((acc_a + acc_b) * scale).astype(BF16)


def _unfold_kernel(ud_ref, wa_ref, wb_ref, j_ref, y_ref, *, r_last, i_mixed):
    i = pl.program_id(0)

    @pl.when(i < i_mixed)
    def _():
        y_ref[...] = ud_ref[...]

    @pl.when(i >= i_mixed)
    def _():
        r = i * DFT_TILE + lax.broadcasted_iota(jnp.int32, (DFT_TILE, 1), 0)
        y_ref[...] = jnp.where(r <= r_last, ud_ref[...],
                               _mirror(j_ref, wa_ref, wb_ref).astype(BF16))


@functools.lru_cache(maxsize=None)
def _dft_constants(seq, n_meta):
    t = DFT_TILE
    off = t - n_meta
    big_l = seq + n_meta
    half = big_l // 2
    n_real = seq // t
    nf = -(-(half + 1 + off) // t)
    hp = nf * t
    jf = np.zeros((nf, t, 2 * t), np.float32)
    for i in range(nf):
        a_idx, b_idx = min(n_real - i, n_real - 1), n_real - 1 - i
        for j in range(t):
            pos = i * t + j - off
            if 1 <= pos <= half - 1:
                tau, rho = divmod(big_l - pos - n_meta, t)
                assert tau in (a_idx, b_idx)
                jf[i, j, (t if tau == b_idx else 0) + rho] = 1.0
    r_last = half - n_meta
    i_mixed = r_last // t
    ju = np.zeros((3, t, 2 * t), np.float32)
    for i in range(i_mixed, n_real):
        a_idx, b_idx = n_real - i, n_real - 1 - i
        blk = np.zeros((t, 2 * t), np.float32)
        for j in range(t):
            r = i * t + j
            if r > r_last:
                tau, rho = divmod(big_l - (r + n_meta) + off, t)
                assert tau in (a_idx, b_idx)
                blk[j, (t if tau == b_idx else 0) + rho] = 1.0
        var = 1 if i == i_mixed else 2
        assert var == 1 or not ju[2].any() or (ju[2] == blk).all()
        ju[var] = blk
    ang = lambda prod: 2.0 * np.pi * (prod % big_l).astype(np.float64) / big_l
    pos = np.arange(hp, dtype=np.int64) - off
    d = ang(np.arange(t, dtype=np.int64)[:, None] * pos[None, :])
    k0 = t * np.arange(nf, dtype=np.int64) - off
    row = np.broadcast_to(ang(k0[:, None] * pos[None, :])[:, None, :], (nf, 8, hp))
    f = lambda a: np.ascontiguousarray(a, dtype=np.float32)
    return dict(jf=jf, ju=ju, dc=f(np.cos(d)), ds=f(np.sin(d)), rowc=f(np.cos(row)),
                rows=f(np.sin(row)), nf=nf, hp=hp, off=off, n_real=n_real,
                m_last=half + off, r_last=r_last, i_mixed=i_mixed)


def _seq_dft(p, q, pm, qm, seq, n_meta):
    c = _dft_constants(seq, n_meta)
    t, nf, hp, n_real, ncol = DFT_TILE, c["nf"], c["hp"], c["n_real"], p.shape[1]
    params = pltpu.CompilerParams(dimension_semantics=("arbitrary",),
                                  vmem_limit_bytes=V7X_VMEM_LIMIT)
    tile = lambda fn: pl.BlockSpec((t, ncol), lambda i: (fn(i), 0))
    lead = ((c["off"], 0), (0, 0))
    src = [tile(lambda i: jnp.maximum(i - 1, 0)),
           tile(lambda i: jnp.minimum(n_real - i, n_real - 1)),
           tile(lambda i: n_real - 1 - i)]
    pe, qo = pl.pallas_call(
        functools.partial(_fold_kernel, m_last=c["m_last"]),
        grid=(nf,),
        in_specs=src + src + [_resident((t, ncol)), _resident((t, ncol)),
                              pl.BlockSpec((1, t, 2 * t), lambda i: (i, 0, 0))],
        out_specs=(tile(lambda i: i), tile(lambda i: i)),
        out_shape=(jax.ShapeDtypeStruct((hp, ncol), BF16),) * 2,
        compiler_params=params,
        name="dft_fold",
    )(p, p, p, q, q, q, jnp.pad(pm, lead), jnp.pad(qm, lead), jnp.asarray(c["jf"]).astype(BF16))

    scale = float((seq + n_meta) * FNET_GROUP_DIM) ** -0.5
    rowspec = pl.BlockSpec((1, 8, hp), lambda i: (i, 0, 0))
    u, w = pl.pallas_call(
        functools.partial(_dft_kernel, scale=scale),
        grid=(nf,),
        in_specs=[_resident((hp, ncol)), _resident((hp, ncol)), _resident((t, hp)),
                  _resident((t, hp)), rowspec, rowspec],
        out_specs=(tile(lambda i: i), tile(lambda i: i)),
        out_shape=(jax.ShapeDtypeStruct((hp, ncol), BF16),) * 2,
        compiler_params=params,
        name="seq_dft",
    )(pe, qo, jnp.asarray(c["dc"]), jnp.asarray(c["ds"]), jnp.asarray(c["rowc"]),
      jnp.asarray(c["rows"]))

    i_mixed = c["i_mixed"]
    clip = lambda v: jnp.clip(v, 0, nf - 1)
    return pl.pallas_call(
        functools.partial(_unfold_kernel, r_last=c["r_last"], i_mixed=i_mixed),
        grid=(n_real,),
        in_specs=[tile(lambda i: clip(i + 1)), tile(lambda i: clip(n_real - i)),
                  tile(lambda i: clip(n_real - 1 - i)),
                  pl.BlockSpec((1, t, 2 * t),
                               lambda i: (jnp.where(i < i_mixed, 0,
                                                    jnp.where(i == i_mixed, 1, 2)), 0, 0))],
        out_specs=tile(lambda i: i),
        out_shape=jax.ShapeDtypeStruct((seq, ncol), BF16),
        compiler_params=params,
        name="dft_unfold",
    )(u, w, w, jnp.asarray(c["ju"]).astype(BF16))


FFN_CHUNK = 1024
TAIL_SPLIT = 2


def _tail_kernel(x_ref, ol_ref, of_ref, ob_ref, y_ref, nmix_ref, gn_ref, nffn_ref, nfin_ref,
                 wg_ref, wo_ref, wf_ref, wout_ref, w1_ref, w2_ref, o_ref):
    tm = x_ref.shape[0]
    sub = tm // TAIL_SPLIT
    tiles = [(dict(), slice(i * sub, (i + 1) * sub)) for i in range(TAIL_SPLIT)]
    dot = functools.partial(jnp.dot, preferred_element_type=F32)
    nchunk = D_FF // FFN_CHUNK

    def norm_in(s, r):
        s["x"] = x_ref[r, :]
        s["xn"] = _rms(s["x"], nmix_ref[...]).astype(BF16)

    def gates(s, r):
        s["g"] = [dot(s["xn"], wg_ref[:, _OFF_G + h * HEAD_DV:_OFF_G + (h + 1) * HEAD_DV])
                  for h in range(GLA_HEADS)]
        s["ga"] = dot(s["xn"], wg_ref[:, _OFF_GA:_OFF_GF])
        s["gf"] = dot(s["xn"], wg_ref[:, _OFF_GF:_OFF_GEND])
        s["y_fnet"] = dot(y_ref[r, :], wf_ref[...])

    def gla_out(s, r):
        y_gla = None
        for h in range(GLA_HEADS):
            hv = slice(h * HEAD_DV, (h + 1) * HEAD_DV)
            o = ol_ref[r, hv] + of_ref[r, hv].astype(F32) + ob_ref[r, hv].astype(F32)
            g = s["g"][h]
            og = (_rms(o, gn_ref[...]) * (g * _sigmoid(g))).astype(BF16)
            part = dot(og, wo_ref[hv, :])
            y_gla = part if y_gla is None else y_gla + part
        s["y_gla"] = y_gla

    def merge(s, r):
        merged = _sigmoid(s["ga"]) * s["y_gla"] + _sigmoid(s["gf"]) * s["y_fnet"]
        s["h1"] = s["x"] + dot(merged.astype(BF16), wout_ref[...])
        s["u"] = _rms(s["h1"], nffn_ref[...]).astype(BF16)
        s["acc"] = s["h1"]

    def ffn_up(c):
        def stage(s, r):
            a = jnp.maximum(dot(s["u"], w1_ref[:, c * FFN_CHUNK:(c + 1) * FFN_CHUNK]), 0.0)
            s["a", c] = (a * a).astype(BF16)
        return stage

    def ffn_down(c):
        def stage(s, r):
            s["acc"] = s["acc"] + dot(s.pop(("a", c)),
                                      w2_ref[c * FFN_CHUNK:(c + 1) * FFN_CHUNK, :])
        return stage

    def norm_out(s, r):
        o_ref[r, :] = _rms(s["acc"], nfin_ref[...])

    stages = [norm_in, gates, gla_out, merge, ffn_up(0)]
    for c in range(nchunk):
        if c + 1 < nchunk:
            stages.append(ffn_up(c + 1))
        stages.append(ffn_down(c))
    stages.append(norm_out)
    for stage in stages:
        for s, r in tiles:
            stage(s, r)


def _tail_call(x2d, ol, o_f, o_b, y, nmix, gn, nffn, nfin, wg, wo, wf, wout, w1, w2, seq, tm):
    m = x2d.shape[0]
    nt = seq // tm
    row = lambda n: pl.BlockSpec((tm, n), lambda i: (i, 0))
    consts = [nmix, gn, nffn, nfin, wg, wo, wf, wout, w1, w2]
    return pl.pallas_call(
        _tail_kernel,
        grid=(m // tm,),
        in_specs=[row(D_MODEL), row(GLA_DV), row(GLA_DV), row(GLA_DV),
                  pl.BlockSpec((tm, FNET_WIDTH), lambda i: (i % nt, i // nt))]
                 + [_resident(c.shape) for c in consts],
        out_specs=row(D_MODEL),
        out_shape=jax.ShapeDtypeStruct((m, D_MODEL), F32),
        compiler_params=pltpu.CompilerParams(
            dimension_semantics=("arbitrary",), vmem_limit_bytes=V7X_VMEM_LIMIT),
        name="tail",
    )(x2d, ol, o_f, o_b, y, *consts)


@functools.lru_cache(maxsize=None)
def _gla_constants(rows):
    idx = np.arange(rows)
    same = (idx[:, None] // SUB) == (idx[None, :] // SUB)
    tri_f = (same & (idx[None, :] <= idx[:, None])).astype(np.float32)
    return tri_f, np.ascontiguousarray(tri_f.T)


@functools.lru_cache(maxsize=None)
def _channel_dft():
    d = np.arange(FNET_GROUP_DIM, dtype=np.int64)
    ang = 2.0 * np.pi * ((d[:, None] * d[None, :]) % FNET_GROUP_DIM) / FNET_GROUP_DIM
    return np.concatenate([np.cos(ang), np.sin(ang)], axis=1).astype(np.float32)


def kernel(x, meta_tokens, norm_mix, w_in, w_decay_fwd, b_decay_fwd, w_decay_bwd, b_decay_bwd,
           gla_norm, w_o_gla, w_fnet, w_out, norm_ffn, w_ff1, w_ff2, norm_final):
    batch, seq, d = x.shape
    assert d == D_MODEL and meta_tokens.shape == (N_META, D_MODEL)
    assert w_in.shape[0] == 1 and seq % (2 * SCAN_RB) == 0 and seq % DFT_TILE == 0
    tm = 512

    wi = w_in[0]
    splits = np.cumsum([GLA_DK, GLA_DK, GLA_DV, GLA_DV, DECAY_RANK, DECAY_RANK, FNET_WIDTH,
                        D_MODEL])
    wq, wk, wv, wg, wrf, wrb, wf_in, wga, wgf = jnp.split(wi, [int(s) for s in splits], axis=1)
    w_main = jnp.concatenate([wq, wk, wv, wf_in], axis=1).astype(BF16)
    w_gate = jnp.concatenate([wg, wga, wgf], axis=1).astype(BF16)
    w_r = jnp.concatenate(
        [wrf, wrb, jnp.zeros((D_MODEL, R_PAD - 2 * DECAY_RANK), F32)], axis=1).astype(BF16)
    wdec = jnp.zeros((R_PAD, 2 * GLA_DK), F32)
    wdec = wdec.at[:DECAY_RANK, :GLA_DK].set(w_decay_fwd[0])
    wdec = wdec.at[DECAY_RANK:2 * DECAY_RANK, GLA_DK:].set(w_decay_bwd[0]).astype(BF16)
    bdec = jnp.concatenate([b_decay_fwd[0], b_decay_bwd[0]])[None, :]
    cs = jnp.asarray(_channel_dft())
    nw_mix = norm_mix[0][None, :]

    x2d = x.reshape(batch * seq, D_MODEL)
    q, k, v, r, p, pq_q = _proj_call(x2d, nw_mix, w_main, w_r, cs, seq, tm)
    _, km, vm, rm, pm, qm = _proj_call(meta_tokens, nw_mix, w_main, w_r, cs, N_META, N_META)

    lead = ((SUB - N_META, 0), (0, 0))
    km, vm, rm = jnp.pad(km, lead), jnp.pad(vm, lead), jnp.pad(rm, lead)
    ol, qe, ke, bend = _gla_local_call(q, k, v, r, wdec, bdec, *_gla_constants(GLA_C),
                                       GLA_TM, GLA_C)
    _, _, kem, _ = _gla_local_call(km, km, vm, rm, wdec, bdec, *_gla_constants(SUB), SUB, SUB)
    o_f, o_b = _gla_scan_call(qe, ke, v, bend, kem, vm, seq, batch)

    y = _seq_dft(p, pq_q, jnp.tile(pm, (1, batch)), jnp.tile(qm, (1, batch)), seq, N_META)

    out = _tail_call(x2d, ol, o_f, o_b, y, nw_mix, gla_norm[0][None, :], norm_ffn[0][None, :],
                     norm_final[None, :], w_gate, w_o_gla[0].astype(BF16),
                     w_fnet[0].astype(BF16), w_out[0].astype(BF16), w_ff1[0].astype(BF16),
                     w_ff2[0].astype(BF16), seq, tm)
    return out.reshape(batch, seq, D_MODEL)
```

```python
import functools

import numpy as np
import jax
import jax.numpy as jnp
from jax import lax
from jax.experimental import pallas as pl
from jax.experimental.pallas import tpu as pltpu

F32 = jnp.float32
BF16 = jnp.bfloat16

D_MODEL = 1024
N_META = 16
GLA_HEADS = 4
GLA_DK = 512
GLA_DV = 1024
HEAD_DK = GLA_DK // GLA_HEADS
HEAD_DV = GLA_DV // GLA_HEADS
DECAY_RANK = 16
GATE_TEMP = 16.0
SUB = 64
FNET_GROUPS = 4
FNET_WIDTH = 512
FNET_GROUP_DIM = FNET_WIDTH // FNET_GROUPS
D_FF = 4 * D_MODEL
EPS = 1e-6
LOG2E = 1.4426950408889634

_OFF_Q, _OFF_K, _OFF_V, _OFF_F, _OFF_END = 0, 512, 1024, 2048, 2560
_OFF_G, _OFF_GA, _OFF_GF, _OFF_GEND = 0, 1024, 2048, 3072
R_PAD = 128
PROJ_SPLIT = 2

GLA_C = 4 * SUB
GLA_TM = 512

LANE = 128
V7X_VMEM_LIMIT = 56 * 1024 * 1024


def _rms(x, w):
    ms = jnp.mean(x * x, axis=-1, keepdims=True)
    return x * lax.rsqrt(ms + EPS) * w


def _sigmoid(x):
    return 1.0 / (1.0 + jnp.exp(-x))


def _run_interleaved(stages, tiles):
    for stage in stages:
        for state, rows in tiles:
            stage(state, rows)


def _resident(shape):
    nd = len(shape)
    return pl.BlockSpec(shape, lambda *_: (0,) * nd, pipeline_mode=pl.Buffered(1))


def _proj_kernel(x_ref, nw_ref, w_ref, wr_ref, wdec_ref, bdec_ref, cs_ref,
                 q_ref, k_ref, v_ref, lah_ref, lal_ref, p_ref, qq_ref):
    tm = x_ref.shape[0]
    nsplit = PROJ_SPLIT if tm % (16 * PROJ_SPLIT) == 0 else 1
    sub = tm // nsplit
    tiles = [(dict(), slice(i * sub, (i + 1) * sub)) for i in range(nsplit)]
    cs = cs_ref[...].astype(BF16)

    def norm_in(s, r):
        s["xn"] = _rms(x_ref[r, :], nw_ref[...]).astype(BF16)

    def decay_rank(s, r):
        s["r"] = jnp.dot(s["xn"], wr_ref[...], preferred_element_type=F32).astype(BF16)

    def log_decay(s, r, cols):
        x = (jnp.dot(s["r"], wdec_ref[:, cols], preferred_element_type=F32) + bdec_ref[:, cols])
        la = (jnp.minimum(x, 0.0) - jnp.log(1.0 + jnp.exp(-jnp.abs(x)))) * (LOG2E / GATE_TEMP)
        hi = la.astype(BF16)
        lah_ref[r, cols] = hi
        lal_ref[r, cols] = (la - hi.astype(F32)).astype(BF16)

    def project(s, r):
        mm = lambda lo, hi: jnp.dot(s["xn"], w_ref[:, lo:hi], preferred_element_type=F32)
        quarter = lambda j: slice(j * (GLA_DK // 2), (j + 1) * (GLA_DK // 2))
        log_decay(s, r, quarter(0))
        q_ref[r, :] = (mm(_OFF_Q, _OFF_K) * (HEAD_DK ** -0.5)).astype(BF16)
        log_decay(s, r, quarter(1))
        k_ref[r, :] = mm(_OFF_K, _OFF_V).astype(BF16)
        log_decay(s, r, quarter(2))
        s["f"] = mm(_OFF_F, _OFF_END).astype(BF16)
        log_decay(s, r, quarter(3))
        v_ref[r, :] = mm(_OFF_V, _OFF_F).astype(BF16)

    def channel_dft(s, r):
        for grp in range(FNET_GROUPS):
            sl = slice(grp * FNET_GROUP_DIM, (grp + 1) * FNET_GROUP_DIM)
            pq = jnp.dot(s["f"][:, sl], cs, preferred_element_type=F32)
            p_ref[r, sl] = pq[:, :FNET_GROUP_DIM].astype(BF16)
            qq_ref[r, sl] = pq[:, FNET_GROUP_DIM:].astype(BF16)

    _run_interleaved((norm_in, decay_rank, project, channel_dft), tiles)


def _proj_call(x2d, nw, w_main, w_r, wdec, bdec, cs, rows_per_batch, tm):
    m = x2d.shape[0]
    nt = rows_per_batch // tm
    nb = m // rows_per_batch
    row = lambda n: pl.BlockSpec((tm, n), lambda i: (i, 0))
    pq_spec = pl.BlockSpec((tm, FNET_WIDTH), lambda i: (i % nt, i // nt))
    out_shape = (
        jax.ShapeDtypeStruct((m, GLA_DK), BF16), jax.ShapeDtypeStruct((m, GLA_DK), BF16),
        jax.ShapeDtypeStruct((m, GLA_DV), BF16),
        jax.ShapeDtypeStruct((m, 2 * GLA_DK), BF16), jax.ShapeDtypeStruct((m, 2 * GLA_DK), BF16),
        jax.ShapeDtypeStruct((rows_per_batch, nb * FNET_WIDTH), BF16),
        jax.ShapeDtypeStruct((rows_per_batch, nb * FNET_WIDTH), BF16),
    )
    return pl.pallas_call(
        _proj_kernel,
        grid=(m // tm,),
        in_specs=[row(D_MODEL), _resident((1, D_MODEL)), _resident(w_main.shape),
                  _resident(w_r.shape), _resident(wdec.shape), _resident(bdec.shape),
                  _resident(cs.shape)],
        out_specs=(row(GLA_DK), row(GLA_DK), row(GLA_DV), row(2 * GLA_DK), row(2 * GLA_DK),
                   pq_spec, pq_spec),
        out_shape=out_shape,
        compiler_params=pltpu.CompilerParams(
            dimension_semantics=("arbitrary",), vmem_limit_bytes=V7X_VMEM_LIMIT),
        name="proj",
    )(x2d, nw, w_main, w_r, wdec, bdec, cs)


_NT = (((1,), (1,)), ((), ()))
_TN = (((0,), (0,)), ((), ()))


def _gla_local_kernel(q_ref, k_ref, v_ref, lah_ref, lal_ref, trif_ref, trib_ref,
                      maskf_ref, maskb_ref, ol_ref, qe_ref, ke_ref, bend_ref, qd_ref, kd_ref,
                      *, chunk):
    tm = q_ref.shape[0]
    n, s = chunk // SUB, SUB
    tris = (trif_ref[...], trib_ref[...])
    for c in range(tm // chunk):
        rows = slice(c * chunk, (c + 1) * chunk)
        q3 = q_ref[rows, :].astype(F32).reshape(n, s, GLA_DK)
        k3 = k_ref[rows, :].astype(F32).reshape(n, s, GLA_DK)
        for d in range(2):
            cols = slice(d * GLA_DK, (d + 1) * GLA_DK)
            b = (jnp.dot(tris[d], lah_ref[rows, cols], preferred_element_type=F32)
                 + jnp.dot(tris[d], lal_ref[rows, cols], preferred_element_type=F32))
            b = b.reshape(n, s, GLA_DK)
            if d == 0:
                b_mid, b_end = b[:, s // 2 - 1:s // 2, :], b[:, s - 1:s, :]
            else:
                b_mid, b_end = b[:, s // 2:s // 2 + 1, :], b[:, 0:1, :]
            qd = q3 * jnp.exp2(b - b_mid)
            kd = k3 * jnp.exp2(b_mid - b)
            qe = qd * jnp.exp2(b_mid)
            ke = kd * jnp.exp2(b_end - b_mid)
            qd_ref[d, rows, :] = qd.reshape(chunk, GLA_DK).astype(BF16)
            kd_ref[d, rows, :] = kd.reshape(chunk, GLA_DK).astype(BF16)
            qe_ref[rows, cols] = qe.reshape(chunk, GLA_DK).astype(BF16)
            ke_ref[rows, cols] = ke.reshape(chunk, GLA_DK).astype(BF16)
            bend_ref[c * n:(c + 1) * n, cols] = b_end.reshape(n, GLA_DK)
    masks = (maskf_ref[0:s, 0:s] != 0.0, maskb_ref[0:s, 0:s] != 0.0)
    chunks = [slice(c * s, (c + 1) * s) for c in range(tm // s)]
    for h in range(GLA_HEADS):
        hk = slice(h * HEAD_DK, (h + 1) * HEAD_DK)
        hv = slice(h * HEAD_DV, (h + 1) * HEAD_DV)
        sc = [[lax.dot_general(qd_ref[d, rows, hk], kd_ref[d, rows, hk], _NT,
                               preferred_element_type=F32) for d in range(2)] for rows in chunks]
        for rows, (sf, sb) in zip(chunks, sc):
            both = (jnp.where(masks[0], sf, 0.0) + jnp.where(masks[1], sb, 0.0)).astype(BF16)
            ol_ref[rows, hv] = jnp.dot(both, v_ref[rows, hv],
                                       preferred_element_type=F32).astype(BF16)


def _gla_local_call(q, k, v, la_hi, la_lo, tri_f, tri_b, tm, chunk):
    m = q.shape[0]
    row = lambda n: pl.BlockSpec((tm, n), lambda i: (i, 0))
    tri_bf = (jnp.asarray(tri_f).astype(BF16), jnp.asarray(tri_b).astype(BF16))
    masks = (jnp.asarray(tri_f), jnp.asarray(tri_b))
    sq = _resident((chunk, chunk))
    return pl.pallas_call(
        functools.partial(_gla_local_kernel, chunk=chunk),
        grid=(m // tm,),
        in_specs=[row(GLA_DK), row(GLA_DK), row(GLA_DV), row(2 * GLA_DK), row(2 * GLA_DK),
                  sq, sq, sq, sq],
        out_specs=(row(GLA_DV), row(2 * GLA_DK), row(2 * GLA_DK),
                   pl.BlockSpec((tm // SUB, 2 * GLA_DK), lambda i: (i, 0))),
        out_shape=(jax.ShapeDtypeStruct((m, GLA_DV), BF16),
                   jax.ShapeDtypeStruct((m, 2 * GLA_DK), BF16),
                   jax.ShapeDtypeStruct((m, 2 * GLA_DK), BF16),
                   jax.ShapeDtypeStruct((m // SUB, 2 * GLA_DK), F32)),
        scratch_shapes=[pltpu.VMEM((2, tm, GLA_DK), BF16), pltpu.VMEM((2, tm, GLA_DK), BF16)],
        compiler_params=pltpu.CompilerParams(
            dimension_semantics=("arbitrary",), vmem_limit_bytes=V7X_VMEM_LIMIT),
        name="gla_local",
    )(q, k, v, la_hi, la_lo, tri_bf[0], tri_bf[1], masks[0], masks[1])


SCAN_G = 512
SCAN_RB = 1024


def _scan_group(dirs):
    s, n = SUB, SCAN_G // SUB
    ahead = 2
    dec, st, inter = [], [], [{}, {}]
    for qe_ref, ke_ref, v_ref, bend_ref, st_ref, row0, chunk0, backward in dirs:
        dec_rows = jnp.tile(jnp.exp2(bend_ref[chunk0:chunk0 + n, :]), (HEAD_DK // n, 1))
        dec.append(dec_rows.T)
        st.append(st_ref[...])

    def order(step, backward):
        return n - 1 - step if backward else step

    def increment(d, step):
        _, ke_ref, v_ref, _, _, row0, _, backward = dirs[d]
        i = order(step, backward)
        sub = slice(row0 + i * s, row0 + (i + 1) * s)
        return lax.dot_general(ke_ref[sub, :], v_ref[sub, :], _TN,
                               preferred_element_type=F32)

    inc = {(d, step): increment(d, step) for step in range(ahead) for d in range(2)}
    for step in range(n):
        for d, (qe_ref, _, _, _, _, row0, _, backward) in enumerate(dirs):
            i = order(step, backward)
            sub = slice(row0 + i * s, row0 + (i + 1) * s)
            inter[d][i] = jnp.dot(qe_ref[sub, :], st[d].astype(BF16),
                                  preferred_element_type=F32)
            if step + ahead < n:
                inc[d, step + ahead] = increment(d, step + ahead)
            decay = jnp.broadcast_to(dec[d][:, i:i + 1], (HEAD_DK, HEAD_DV))
            st[d] = st[d] * decay + inc.pop((d, step))
    for d, spec in enumerate(dirs):
        spec[4][...] = st[d]
    return [jnp.concatenate([inter[d][i] for i in range(n)], axis=0) for d in range(2)]


def _gla_scan_kernel(qef_ref, kef_ref, vf_ref, bendf_ref,
                     qeb_ref, keb_ref, vb_ref, bendb_ref, kem_ref, vm_ref,
                     of_ref, ob_ref, stf_ref, stb_ref):
    rb = qef_ref.shape[0]
    ng = rb // SCAN_G

    @pl.when(pl.program_id(2) == 0)
    def _():
        stf_ref[...] = lax.dot_general(kem_ref[...], vm_ref[...], _TN,
                                       preferred_element_type=F32)
        stb_ref[...] = jnp.zeros_like(stb_ref)

    for j in range(ng):
        jb = ng - 1 - j
        cpg = SCAN_G // SUB
        in_f, in_b = _scan_group([
            (qef_ref, kef_ref, vf_ref, bendf_ref, stf_ref, j * SCAN_G, j * cpg, False),
            (qeb_ref, keb_ref, vb_ref, bendb_ref, stb_ref, jb * SCAN_G, jb * cpg, True)])
        of_ref[j * SCAN_G:(j + 1) * SCAN_G, :] = in_f.astype(BF16)
        ob_ref[jb * SCAN_G:(jb + 1) * SCAN_G, :] = in_b.astype(BF16)


def _gla_scan_call(qe, ke, v, bend, kem, vm, seq, batch):
    h, rb = GLA_HEADS, SCAN_RB
    nt = seq // rb
    fwd = lambda b, hh, t: b * nt + t
    bwd = lambda b, hh, t: b * nt + nt - 1 - t
    cb = rb // SUB

    def dk_blk(rowfn, dircol, rows):
        return pl.BlockSpec((rows, HEAD_DK), lambda b, hh, t: (rowfn(b, hh, t), dircol + hh))

    def dv_blk(rowfn):
        return pl.BlockSpec((rb, HEAD_DV), lambda b, hh, t: (rowfn(b, hh, t), hh))

    in_specs = [
        dk_blk(fwd, 0, rb), dk_blk(fwd, 0, rb), dv_blk(fwd), dk_blk(fwd, 0, cb),
        dk_blk(bwd, h, rb), dk_blk(bwd, h, rb), dv_blk(bwd), dk_blk(bwd, h, cb),
        pl.BlockSpec((SUB, HEAD_DK), lambda b, hh, t: (0, hh)),
        pl.BlockSpec((SUB, HEAD_DV), lambda b, hh, t: (0, hh)),
    ]
    out = jax.ShapeDtypeStruct((batch * seq, GLA_DV), BF16)
    return pl.pallas_call(
        _gla_scan_kernel,
        grid=(batch, h, nt),
        in_specs=in_specs,
        out_specs=(dv_blk(fwd), dv_blk(bwd)),
        out_shape=(out, out),
        scratch_shapes=[pltpu.VMEM((HEAD_DK, HEAD_DV), F32), pltpu.VMEM((HEAD_DK, HEAD_DV), F32)],
        compiler_params=pltpu.CompilerParams(
            dimension_semantics=("arbitrary", "arbitrary", "arbitrary"),
            vmem_limit_bytes=V7X_VMEM_LIMIT),
        name="gla_scan",
    )(qe, ke, v, bend, qe, ke, v, bend, kem, vm)


DFT_TILE = 256


def _mirror(j_ref, a_ref, b_ref):
    t = DFT_TILE
    return (jnp.dot(j_ref[0, :, :t], a_ref[...], preferred_element_type=F32)
            + jnp.dot(j_ref[0, :, t:], b_ref[...], preferred_element_type=F32))


def _fold_kernel(pd_ref, pa_ref, pb_ref, qd_ref, qa_ref, qb_ref, pz_ref, qz_ref, j_ref,
                 pe_ref, qo_ref, *, m_last):
    i = pl.program_id(0)
    first = i == 0
    p_dir = jnp.where(first, pz_ref[...], pd_ref[...]).astype(F32)
    q_dir = jnp.where(first, qz_ref[...], qd_ref[...]).astype(F32)
    m = i * DFT_TILE + lax.broadcasted_iota(jnp.int32, (DFT_TILE, 1), 0)
    keep = m <= m_last
    pe_ref[...] = jnp.where(keep, p_dir + _mirror(j_ref, pa_ref, pb_ref), 0.0).astype(BF16)
    qo_ref[...] = jnp.where(keep, q_dir - _mirror(j_ref, qa_ref, qb_ref), 0.0).astype(BF16)


def _dft_kernel(pe_ref, qo_ref, dc_ref, ds_ref, rowc_ref, rows_ref, u_ref, w_ref, *, scale):
    t = DFT_TILE
    acc_a = acc_b = None
    for c in range(pe_ref.shape[0] // t):
        sl = slice(c * t, (c + 1) * t)
        rc, rs = rowc_ref[0, 0:1, sl], rows_ref[0, 0:1, sl]
        dc, ds = dc_ref[:, sl], ds_ref[:, sl]
        tc = (dc * rc - ds * rs).astype(BF16)
        ts = (ds * rc + dc * rs).astype(BF16)
        a = jnp.dot(tc, pe_ref[sl, :], preferred_element_type=F32)
        b = jnp.dot(ts, qo_ref[sl, :], preferred_element_type=F32)
        acc_a = a if acc_a is None else acc_a + a
        acc_b = b if acc_b is None else acc_b + b
    u_ref[...] = ((acc_a - acc_b) * scale).astype(BF16)
    w_ref[...] = ((acc_a + acc_b) * scale).astype(BF16)


def _unfold_kernel(ud_ref, wa_ref, wb_ref, j_ref, y_ref, *, r_last, i_mixed):
    i = pl.program_id(0)

    @pl.when(i < i_mixed)
    def _():
        y_ref[...] = ud_ref[...]

    @pl.when(i >= i_mixed)
    def _():
        r = i * DFT_TILE + lax.broadcasted_iota(jnp.int32, (DFT_TILE, 1), 0)
        y_ref[...] = jnp.where(r <= r_last, ud_ref[...],
                               _mirror(j_ref, wa_ref, wb_ref).astype(BF16))


@functools.lru_cache(maxsize=None)
def _dft_constants(seq, n_meta):
    t = DFT_TILE
    off = t - n_meta
    big_l = seq + n_meta
    half = big_l // 2
    n_real = seq // t
    nf = -(-(half + 1 + off) // t)
    hp = nf * t
    jf = np.zeros((nf, t, 2 * t), np.float32)
    for i in range(nf):
        a_idx, b_idx = min(n_real - i, n_real - 1), n_real - 1 - i
        for j in range(t):
            pos = i * t + j - off
            if 1 <= pos <= half - 1:
                tau, rho = divmod(big_l - pos - n_meta, t)
                assert tau in (a_idx, b_idx)
                jf[i, j, (t if tau == b_idx else 0) + rho] = 1.0
    r_last = half - n_meta
    i_mixed = r_last // t
    ju = np.zeros((3, t, 2 * t), np.float32)
    for i in range(i_mixed, n_real):
        a_idx, b_idx = n_real - i, n_real - 1 - i
        blk = np.zeros((t, 2 * t), np.float32)
        for j in range(t):
            r = i * t + j
            if r > r_last:
                tau, rho = divmod(big_l - (r + n_meta) + off, t)
                assert tau in (a_idx, b_idx)
                blk[j, (t if tau == b_idx else 0) + rho] = 1.0
        var = 1 if i == i_mixed else 2
        assert var == 1 or not ju[2].any() or (ju[2] == blk).all()
        ju[var] = blk
    ang = lambda prod: 2.0 * np.pi * (prod % big_l).astype(np.float64) / big_l
    pos = np.arange(hp, dtype=np.int64) - off
    d = ang(np.arange(t, dtype=np.int64)[:, None] * pos[None, :])
    k0 = t * np.arange(nf, dtype=np.int64) - off
    row = np.broadcast_to(ang(k0[:, None] * pos[None, :])[:, None, :], (nf, 8, hp))
    f = lambda a: np.ascontiguousarray(a, dtype=np.float32)
    return dict(jf=jf, ju=ju, dc=f(np.cos(d)), ds=f(np.sin(d)), rowc=f(np.cos(row)),
                rows=f(np.sin(row)), nf=nf, hp=hp, off=off, n_real=n_real,
                m_last=half + off, r_last=r_last, i_mixed=i_mixed)


def _seq_dft(p, q, pm, qm, seq, n_meta):
    c = _dft_constants(seq, n_meta)
    t, nf, hp, n_real, ncol = DFT_TILE, c["nf"], c["hp"], c["n_real"], p.shape[1]
    params = pltpu.CompilerParams(dimension_semantics=("arbitrary",),
                                  vmem_limit_bytes=V7X_VMEM_LIMIT)
    tile = lambda fn: pl.BlockSpec((t, ncol), lambda i: (fn(i), 0))
    lead = ((c["off"], 0), (0, 0))
    src = [tile(lambda i: jnp.maximum(i - 1, 0)),
           tile(lambda i: jnp.minimum(n_real - i, n_real - 1)),
           tile(lambda i: n_real - 1 - i)]
    pe, qo = pl.pallas_call(
        functools.partial(_fold_kernel, m_last=c["m_last"]),
        grid=(nf,),
        in_specs=src + src + [_resident((t, ncol)), _resident((t, ncol)),
                              pl.BlockSpec((1, t, 2 * t), lambda i: (i, 0, 0))],
        out_specs=(tile(lambda i: i), tile(lambda i: i)),
        out_shape=(jax.ShapeDtypeStruct((hp, ncol), BF16),) * 2,
        compiler_params=params,
        name="dft_fold",
    )(p, p, p, q, q, q, jnp.pad(pm, lead), jnp.pad(qm, lead), jnp.asarray(c["jf"]).astype(BF16))

    scale = float((seq + n_meta) * FNET_GROUP_DIM) ** -0.5
    rowspec = pl.BlockSpec((1, 8, hp), lambda i: (i, 0, 0))
    u, w = pl.pallas_call(
        functools.partial(_dft_kernel, scale=scale),
        grid=(nf,),
        in_specs=[_resident((hp, ncol)), _resident((hp, ncol)), _resident((t, hp)),
                  _resident((t, hp)), rowspec, rowspec],
        out_specs=(tile(lambda i: i), tile(lambda i: i)),
        out_shape=(jax.ShapeDtypeStruct((hp, ncol), BF16),) * 2,
        compiler_params=params,
        name="seq_dft",
    )(pe, qo, jnp.asarray(c["dc"]), jnp.asarray(c["ds"]), jnp.asarray(c["rowc"]),
      jnp.asarray(c["rows"]))

    i_mixed = c["i_mixed"]
    clip = lambda v: jnp.clip(v, 0, nf - 1)
    return pl.pallas_call(
        functools.partial(_unfold_kernel, r_last=c["r_last"], i_mixed=i_mixed),
        grid=(n_real,),
        in_specs=[tile(lambda i: clip(i + 1)), tile(lambda i: clip(n_real - i)),
                  tile(lambda i: clip(n_real - 1 - i)),
                  pl.BlockSpec((1, t, 2 * t),
                               lambda i: (jnp.where(i < i_mixed, 0,
                                                    jnp.where(i == i_mixed, 1, 2)), 0, 0))],
        out_specs=tile(lambda i: i),
        out_shape=jax.ShapeDtypeStruct((seq, ncol), BF16),
        compiler_params=params,
        name="dft_unfold",
    )(u, w, w, jnp.asarray(c["ju"]).astype(BF16))


FFN_CHUNK = 1024
TAIL_SPLIT = 2


def _tail_kernel(x_ref, ol_ref, of_ref, ob_ref, y_ref, nmix_ref, gn_ref, nffn_ref, nfin_ref,
                 wg_ref, wo_ref, wf_ref, wout_ref, w1_ref, w2_ref, o_ref):
    tm = x_ref.shape[0]
    sub = tm // TAIL_SPLIT
    tiles = [(dict(), slice(i * sub, (i + 1) * sub)) for i in range(TAIL_SPLIT)]
    dot = functools.partial(jnp.dot, preferred_element_type=F32)
    nchunk = D_FF // FFN_CHUNK

    def norm_in(s, r):
        s["x"] = x_ref[r, :]
        s["xn"] = _rms(s["x"], nmix_ref[...]).astype(BF16)

    def gates(s, r):
        s["g"] = [dot(s["xn"], wg_ref[:, _OFF_G + h * HEAD_DV:_OFF_G + (h + 1) * HEAD_DV])
                  for h in range(GLA_HEADS)]
        s["ga"] = dot(s["xn"], wg_ref[:, _OFF_GA:_OFF_GF])
        s["gf"] = dot(s["xn"], wg_ref[:, _OFF_GF:_OFF_GEND])
        s["y_fnet"] = dot(y_ref[r, :], wf_ref[...])

    def gla_out(s, r):
        y_gla = None
        for h in range(GLA_HEADS):
            hv = slice(h * HEAD_DV, (h + 1) * HEAD_DV)
            o = (ol_ref[r, hv].astype(F32) + of_ref[r, hv].astype(F32)
                 + ob_ref[r, hv].astype(F32))
            g = s["g"][h]
            og = (_rms(o, gn_ref[...]) * (g * _sigmoid(g))).astype(BF16)
            part = dot(og, wo_ref[hv, :])
            y_gla = part if y_gla is None else y_gla + part
        s["y_gla"] = y_gla

    def merge(s, r):
        merged = _sigmoid(s["ga"]) * s["y_gla"] + _sigmoid(s["gf"]) * s["y_fnet"]
        s["h1"] = s["x"] + dot(merged.astype(BF16), wout_ref[...])
        s["u"] = _rms(s["h1"], nffn_ref[...]).astype(BF16)
        s["acc"] = s["h1"]

    def ffn_up(c):
        def stage(s, r):
            a = jnp.maximum(dot(s["u"], w1_ref[:, c * FFN_CHUNK:(c + 1) * FFN_CHUNK]), 0.0)
            s["a", c] = (a * a).astype(BF16)
        return stage

    def ffn_down(c):
        def stage(s, r):
            s["acc"] = s["acc"] + dot(s.pop(("a", c)),
                                      w2_ref[c * FFN_CHUNK:(c + 1) * FFN_CHUNK, :])
        return stage

    def norm_out(s, r):
        o_ref[r, :] = _rms(s["acc"], nfin_ref[...])

    stages = [norm_in, gates, gla_out, merge, ffn_up(0)]
    for c in range(nchunk):
        if c + 1 < nchunk:
            stages.append(ffn_up(c + 1))
        stages.append(ffn_down(c))
    stages.append(norm_out)
    _run_interleaved(stages, tiles)


def _tail_call(x2d, ol, o_f, o_b, y, nmix, gn, nffn, nfin, wg, wo, wf, wout, w1, w2, seq, tm):
    m = x2d.shape[0]
    nt = seq // tm
    row = lambda n: pl.BlockSpec((tm, n), lambda i: (i, 0))
    consts = [nmix, gn, nffn, nfin, wg, wo, wf, wout, w1, w2]
    return pl.pallas_call(
        _tail_kernel,
        grid=(m // tm,),
        in_specs=[row(D_MODEL), row(GLA_DV), row(GLA_DV), row(GLA_DV),
                  pl.BlockSpec((tm, FNET_WIDTH), lambda i: (i % nt, i // nt))]
                 + [_resident(c.shape) for c in consts],
        out_specs=row(D_MODEL),
        out_shape=jax.ShapeDtypeStruct((m, D_MODEL), F32),
        compiler_params=pltpu.CompilerParams(
            dimension_semantics=("arbitrary",), vmem_limit_bytes=V7X_VMEM_LIMIT),
        name="tail",
    )(x2d, ol, o_f, o_b, y, *consts)


@functools.lru_cache(maxsize=None)
def _gla_constants(rows):
    idx = np.arange(rows)
    same = (idx[:, None] // SUB) == (idx[None, :] // SUB)
    tri_f = (same & (idx[None, :] <= idx[:, None])).astype(np.float32)
    return tri_f, np.ascontiguousarray(tri_f.T)


@functools.lru_cache(maxsize=None)
def _channel_dft():
    d = np.arange(FNET_GROUP_DIM, dtype=np.int64)
    ang = 2.0 * np.pi * ((d[:, None] * d[None, :]) % FNET_GROUP_DIM) / FNET_GROUP_DIM
    return np.concatenate([np.cos(ang), np.sin(ang)], axis=1).astype(np.float32)


def kernel(x, meta_tokens, norm_mix, w_in, w_decay_fwd, b_decay_fwd, w_decay_bwd, b_decay_bwd,
           gla_norm, w_o_gla, w_fnet, w_out, norm_ffn, w_ff1, w_ff2, norm_final):
    batch, seq, d = x.shape
    assert d == D_MODEL and meta_tokens.shape == (N_META, D_MODEL)
    assert w_in.shape[0] == 1 and seq % (2 * SCAN_RB) == 0 and seq % DFT_TILE == 0
    tm = 512

    wi = w_in[0]
    splits = np.cumsum([GLA_DK, GLA_DK, GLA_DV, GLA_DV, DECAY_RANK, DECAY_RANK, FNET_WIDTH,
                        D_MODEL])
    wq, wk, wv, wg, wrf, wrb, wf_in, wga, wgf = jnp.split(wi, [int(s) for s in splits], axis=1)
    w_main = jnp.concatenate([wq, wk, wv, wf_in], axis=1).astype(BF16)
    w_gate = jnp.concatenate([wg, wga, wgf], axis=1).astype(BF16)
    w_r = jnp.concatenate(
        [wrf, wrb, jnp.zeros((D_MODEL, R_PAD - 2 * DECAY_RANK), F32)], axis=1).astype(BF16)
    wdec = jnp.zeros((R_PAD, 2 * GLA_DK), F32)
    wdec = wdec.at[:DECAY_RANK, :GLA_DK].set(w_decay_fwd[0])
    wdec = wdec.at[DECAY_RANK:2 * DECAY_RANK, GLA_DK:].set(w_decay_bwd[0]).astype(BF16)
    bdec = jnp.concatenate([b_decay_fwd[0], b_decay_bwd[0]])[None, :]
    cs = jnp.asarray(_channel_dft())
    nw_mix = norm_mix[0][None, :]

    x2d = x.reshape(batch * seq, D_MODEL)
    q, k, v, la_hi, la_lo, p, pq_q = _proj_call(x2d, nw_mix, w_main, w_r, wdec, bdec, cs,
                                                seq, tm)
    _, km, vm, lahm, lalm, pm, qm = _proj_call(meta_tokens, nw_mix, w_main, w_r, wdec, bdec, cs,
                                               N_META, N_META)

    lead = ((SUB - N_META, 0), (0, 0))
    km, vm, lahm, lalm = (jnp.pad(a, lead) for a in (km, vm, lahm, lalm))
    ol, qe, ke, bend = _gla_local_call(q, k, v, la_hi, la_lo, *_gla_constants(GLA_C),
                                       GLA_TM, GLA_C)
    _, _, kem, _ = _gla_local_call(km, km, vm, lahm, lalm, *_gla_constants(SUB), SUB, SUB)
    o_f, o_b = _gla_scan_call(qe, ke, v, bend, kem, vm, seq, batch)

    y = _seq_dft(p, pq_q, jnp.tile(pm, (1, batch)), jnp.tile(qm, (1, batch)), seq, N_META)

    out = _tail_call(x2d, ol, o_f, o_b, y, nw_mix, gla_norm[0][None, :], norm_ffn[0][None, :],
                     norm_final[None, :], w_gate, w_o_gla[0].astype(BF16),
                     w_fnet[0].astype(BF16), w_out[0].astype(BF16), w_ff1[0].astype(BF16),
                     w_ff2[0].astype(BF16), seq, tm)
    return out.reshape(batch, seq, D_MODEL)
```

```python
import functools

import numpy as np
import jax
import jax.numpy as jnp
from jax import lax
from jax.experimental import pallas as pl
from jax.experimental.pallas import tpu as pltpu

F32 = jnp.float32
BF16 = jnp.bfloat16

D_MODEL = 1024
N_META = 16
GLA_HEADS = 4
GLA_DK = 512
GLA_DV = 1024
HEAD_DK = GLA_DK // GLA_HEADS
HEAD_DV = GLA_DV // GLA_HEADS
DECAY_RANK = 16
GATE_TEMP = 16.0
SUB = 64
FNET_GROUPS = 4
FNET_WIDTH = 512
FNET_GROUP_DIM = FNET_WIDTH // FNET_GROUPS
D_FF = 4 * D_MODEL
EPS = 1e-6
LOG2E = 1.4426950408889634

_OFF_Q, _OFF_K, _OFF_V, _OFF_F, _OFF_END = 0, 512, 1024, 2048, 2560
_OFF_G, _OFF_GA, _OFF_GF, _OFF_GEND = 0, 1024, 2048, 3072
R_PAD = 128

GLA_C = 4 * SUB
MIX_TM = 512

LANE = 128
V7X_VMEM_LIMIT = 56 * 1024 * 1024


def _rms(x, w):
    ms = jnp.mean(x * x, axis=-1, keepdims=True)
    return x * lax.rsqrt(ms + EPS) * w


def _sigmoid(x):
    return 1.0 / (1.0 + jnp.exp(-x))


def _run_interleaved(stages, tiles):
    for stage in stages:
        for state, rows in tiles:
            stage(state, rows)


def _resident(shape):
    nd = len(shape)
    return pl.BlockSpec(shape, lambda *_: (0,) * nd, pipeline_mode=pl.Buffered(1))


_NT = (((1,), (1,)), ((), ()))
_TN = (((0,), (0,)), ((), ()))


def _mixin_kernel(x_ref, nw_ref, w_ref, wr_ref, wdec_ref, bdec_ref, cs_ref,
                  trif_ref, trib_ref, maskf_ref, maskb_ref,
                  v_ref, ol_ref, qe_ref, ke_ref, bend_ref, p_ref, qq_ref, qd_ref, kd_ref,
                  *, chunk):
    tm = x_ref.shape[0]
    n, s = chunk // SUB, SUB
    tiles = [(dict(), slice(i * chunk, (i + 1) * chunk)) for i in range(tm // chunk)]
    dot = functools.partial(jnp.dot, preferred_element_type=F32)
    cs = cs_ref[...].astype(BF16)
    tris = (trif_ref[...], trib_ref[...])
    masks = (maskf_ref[0:s, 0:s] != 0.0, maskb_ref[0:s, 0:s] != 0.0)
    half = GLA_DK // 2

    def norm_in(st, r):
        st["xn"] = _rms(x_ref[r, :], nw_ref[...]).astype(BF16)

    def decay_rank(st, r):
        st["r"] = dot(st["xn"], wr_ref[...]).astype(BF16)

    def log_decay(st, j):
        cols = slice(j * half, (j + 1) * half)
        x = dot(st["r"], wdec_ref[:, cols]) + bdec_ref[:, cols]
        la = (jnp.minimum(x, 0.0) - jnp.log(1.0 + jnp.exp(-jnp.abs(x)))) * (LOG2E / GATE_TEMP)
        hi = la.astype(BF16)
        st["la", j] = (hi, (la - hi.astype(F32)).astype(BF16))

    def project(st, r):
        mm = lambda lo, hi: dot(st["xn"], w_ref[:, lo:hi])
        log_decay(st, 0)
        st["q"] = mm(_OFF_Q, _OFF_K) * (HEAD_DK ** -0.5)
        log_decay(st, 1)
        st["k"] = mm(_OFF_K, _OFF_V)
        log_decay(st, 2)
        st["f"] = mm(_OFF_F, _OFF_END).astype(BF16)
        log_decay(st, 3)
        v_ref[r, :] = mm(_OFF_V, _OFF_F).astype(BF16)

    def decay_factors(st, r):
        c0 = r.start // s
        for j in range(4):
            d, qcols = j // 2, slice((j % 2) * half, (j % 2 + 1) * half)
            cols = slice(j * half, (j + 1) * half)
            hi, lo = st.pop(("la", j))
            b = (dot(tris[d], hi) + dot(tris[d], lo)).reshape(n, s, half)
            if d == 0:
                b_mid, b_end = b[:, s // 2 - 1:s // 2, :], b[:, s - 1:s, :]
            else:
                b_mid, b_end = b[:, s // 2:s // 2 + 1, :], b[:, 0:1, :]
            qd = st["q"][:, qcols].reshape(n, s, half) * jnp.exp2(b - b_mid)
            kd = st["k"][:, qcols].reshape(n, s, half) * jnp.exp2(b_mid - b)
            qe = qd * jnp.exp2(b_mid)
            ke = kd * jnp.exp2(b_end - b_mid)
            qd_ref[d, r, qcols] = qd.reshape(chunk, half).astype(BF16)
            kd_ref[d, r, qcols] = kd.reshape(chunk, half).astype(BF16)
            qe_ref[r, cols] = qe.reshape(chunk, half).astype(BF16)
            ke_ref[r, cols] = ke.reshape(chunk, half).astype(BF16)
            bend_ref[c0:c0 + n, cols] = b_end.reshape(n, half)

    def intra_chunk(st, r):
        chunks = [slice(r.start + c * s, r.start + (c + 1) * s) for c in range(n)]
        for h in range(GLA_HEADS):
            hk = slice(h * HEAD_DK, (h + 1) * HEAD_DK)
            hv = slice(h * HEAD_DV, (h + 1) * HEAD_DV)
            sc = [[lax.dot_general(qd_ref[d, rows, hk], kd_ref[d, rows, hk], _NT,
                                   preferred_element_type=F32) for d in range(2)]
                  for rows in chunks]
            for rows, (sf, sb) in zip(chunks, sc):
                both = (jnp.where(masks[0], sf, 0.0) + jnp.where(masks[1], sb, 0.0)).astype(BF16)
                ol_ref[rows, hv] = dot(both, v_ref[rows, hv]).astype(BF16)

    def channel_dft(st, r):
        for grp in range(FNET_GROUPS):
            sl = slice(grp * FNET_GROUP_DIM, (grp + 1) * FNET_GROUP_DIM)
            pq = dot(st["f"][:, sl], cs)
            p_ref[r, sl] = pq[:, :FNET_GROUP_DIM].astype(BF16)
            qq_ref[r, sl] = pq[:, FNET_GROUP_DIM:].astype(BF16)

    _run_interleaved((norm_in, decay_rank, project, decay_factors, intra_chunk, channel_dft),
                     tiles)


def _mixin_call(x2d, nw, w_main, w_r, wdec, bdec, cs, tri_f, tri_b, rows_per_batch, tm, chunk):
    m = x2d.shape[0]
    nt = rows_per_batch // tm
    nb = m // rows_per_batch
    row = lambda n: pl.BlockSpec((tm, n), lambda i: (i, 0))
    pq_spec = pl.BlockSpec((tm, FNET_WIDTH), lambda i: (i % nt, i // nt))
    wide = jax.ShapeDtypeStruct((m, 2 * GLA_DK), BF16)
    pq_shape = jax.ShapeDtypeStruct((rows_per_batch, nb * FNET_WIDTH), BF16)
    consts = [nw, w_main, w_r, wdec, bdec, cs,
              jnp.asarray(tri_f).astype(BF16), jnp.asarray(tri_b).astype(BF16),
              jnp.asarray(tri_f), jnp.asarray(tri_b)]
    return pl.pallas_call(
        functools.partial(_mixin_kernel, chunk=chunk),
        grid=(m // tm,),
        in_specs=[row(D_MODEL)] + [_resident(c.shape) for c in consts],
        out_specs=(row(GLA_DV), row(GLA_DV), row(2 * GLA_DK), row(2 * GLA_DK),
                   pl.BlockSpec((tm // SUB, 2 * GLA_DK), lambda i: (i, 0)), pq_spec, pq_spec),
        out_shape=(jax.ShapeDtypeStruct((m, GLA_DV), BF16), jax.ShapeDtypeStruct((m, GLA_DV), BF16),
                   wide, wide, jax.ShapeDtypeStruct((m // SUB, 2 * GLA_DK), F32),
                   pq_shape, pq_shape),
        scratch_shapes=[pltpu.VMEM((2, tm, GLA_DK), BF16), pltpu.VMEM((2, tm, GLA_DK), BF16)],
        compiler_params=pltpu.CompilerParams(
            dimension_semantics=("arbitrary",), vmem_limit_bytes=V7X_VMEM_LIMIT),
        name="mixin",
    )(x2d, *consts)


SCAN_G = 512
SCAN_RB = 1024


def _scan_group(dirs):
    s, n = SUB, SCAN_G // SUB
    ahead = 2
    dec, st, inter = [], [], [{}, {}]
    for qe_ref, ke_ref, v_ref, bend_ref, st_ref, row0, chunk0, backward in dirs:
        dec_rows = jnp.tile(jnp.exp2(bend_ref[chunk0:chunk0 + n, :]), (HEAD_DK // n, 1))
        dec.append(dec_rows.T)
        st.append(st_ref[...])

    def order(step, backward):
        return n - 1 - step if backward else step

    def increment(d, step):
        _, ke_ref, v_ref, _, _, row0, _, backward = dirs[d]
        i = order(step, backward)
        sub = slice(row0 + i * s, row0 + (i + 1) * s)
        return lax.dot_general(ke_ref[sub, :], v_ref[sub, :], _TN,
                               preferred_element_type=F32)

    inc = {(d, step): increment(d, step) for step in range(ahead) for d in range(2)}
    for step in range(n):
        for d, (qe_ref, _, _, _, _, row0, _, backward) in enumerate(dirs):
            i = order(step, backward)
            sub = slice(row0 + i * s, row0 + (i + 1) * s)
            inter[d][i] = jnp.dot(qe_ref[sub, :], st[d].astype(BF16),
                                  preferred_element_type=F32)
            if step + ahead < n:
                inc[d, step + ahead] = increment(d, step + ahead)
            decay = jnp.broadcast_to(dec[d][:, i:i + 1], (HEAD_DK, HEAD_DV))
            st[d] = st[d] * decay + inc.pop((d, step))
    for d, spec in enumerate(dirs):
        spec[4][...] = st[d]
    return [jnp.concatenate([inter[d][i] for i in range(n)], axis=0) for d in range(2)]


def _gla_scan_kernel(qef_ref, kef_ref, vf_ref, bendf_ref,
                     qeb_ref, keb_ref, vb_ref, bendb_ref, kem_ref, vm_ref,
                     of_ref, ob_ref, stf_ref, stb_ref):
    rb = qef_ref.shape[0]
    ng = rb // SCAN_G

    @pl.when(pl.program_id(2) == 0)
    def _():
        stf_ref[...] = lax.dot_general(kem_ref[...], vm_ref[...], _TN,
                                       preferred_element_type=F32)
        stb_ref[...] = jnp.zeros_like(stb_ref)

    for j in range(ng):
        jb = ng - 1 - j
        cpg = SCAN_G // SUB
        in_f, in_b = _scan_group([
            (qef_ref, kef_ref, vf_ref, bendf_ref, stf_ref, j * SCAN_G, j * cpg, False),
            (qeb_ref, keb_ref, vb_ref, bendb_ref, stb_ref, jb * SCAN_G, jb * cpg, True)])
        of_ref[j * SCAN_G:(j + 1) * SCAN_G, :] = in_f.astype(BF16)
        ob_ref[jb * SCAN_G:(jb + 1) * SCAN_G, :] = in_b.astype(BF16)


def _gla_scan_call(qe, ke, v, bend, kem, vm, seq, batch):
    h, rb = GLA_HEADS, SCAN_RB
    nt = seq // rb
    fwd = lambda b, hh, t: b * nt + t
    bwd = lambda b, hh, t: b * nt + nt - 1 - t
    cb = rb // SUB

    def dk_blk(rowfn, dircol, rows):
        return pl.BlockSpec((rows, HEAD_DK), lambda b, hh, t: (rowfn(b, hh, t), dircol + hh))

    def dv_blk(rowfn):
        return pl.BlockSpec((rb, HEAD_DV), lambda b, hh, t: (rowfn(b, hh, t), hh))

    in_specs = [
        dk_blk(fwd, 0, rb), dk_blk(fwd, 0, rb), dv_blk(fwd), dk_blk(fwd, 0, cb),
        dk_blk(bwd, h, rb), dk_blk(bwd, h, rb), dv_blk(bwd), dk_blk(bwd, h, cb),
        pl.BlockSpec((SUB, HEAD_DK), lambda b, hh, t: (0, hh)),
        pl.BlockSpec((SUB, HEAD_DV), lambda b, hh, t: (0, hh)),
    ]
    out = jax.ShapeDtypeStruct((batch * seq, GLA_DV), BF16)
    return pl.pallas_call(
        _gla_scan_kernel,
        grid=(batch, h, nt),
        in_specs=in_specs,
        out_specs=(dv_blk(fwd), dv_blk(bwd)),
        out_shape=(out, out),
        scratch_shapes=[pltpu.VMEM((HEAD_DK, HEAD_DV), F32), pltpu.VMEM((HEAD_DK, HEAD_DV), F32)],
        compiler_params=pltpu.CompilerParams(
            dimension_semantics=("arbitrary", "arbitrary", "arbitrary"),
            vmem_limit_bytes=V7X_VMEM_LIMIT),
        name="gla_scan",
    )(qe, ke, v, bend, qe, ke, v, bend, kem, vm)


DFT_TILE = 256


def _mirror(j_ref, a_ref, b_ref):
    t = DFT_TILE
    return (jnp.dot(j_ref[0, :, :t], a_ref[...], preferred_element_type=F32)
            + jnp.dot(j_ref[0, :, t:], b_ref[...], preferred_element_type=F32))


def _fold_kernel(pd_ref, pa_ref, pb_ref, qd_ref, qa_ref, qb_ref, pz_ref, qz_ref, j_ref,
                 pe_ref, qo_ref, *, m_last):
    i = pl.program_id(0)
    first = i == 0
    p_dir = jnp.where(first, pz_ref[...], pd_ref[...]).astype(F32)
    q_dir = jnp.where(first, qz_ref[...], qd_ref[...]).astype(F32)
    m = i * DFT_TILE + lax.broadcasted_iota(jnp.int32, (DFT_TILE, 1), 0)
    keep = m <= m_last
    pe_ref[...] = jnp.where(keep, p_dir + _mirror(j_ref, pa_ref, pb_ref), 0.0).astype(BF16)
    qo_ref[...] = jnp.where(keep, q_dir - _mirror(j_ref, qa_ref, qb_ref), 0.0).astype(BF16)


def _dft_kernel(pe_ref, qo_ref, dc_ref, ds_ref, rowc_ref, rows_ref, u_ref, w_ref, *, scale):
    t = DFT_TILE
    acc_a = acc_b = None
    for c in range(pe_ref.shape[0] // t):
        sl = slice(c * t, (c + 1) * t)
        rc, rs = rowc_ref[0, 0:1, sl], rows_ref[0, 0:1, sl]
        dc, ds = dc_ref[:, sl], ds_ref[:, sl]
        tc = (dc * rc - ds * rs).astype(BF16)
        ts = (ds * rc + dc * rs).astype(BF16)
        a = jnp.dot(tc, pe_ref[sl, :], preferred_element_type=F32)
        b = jnp.dot(ts, qo_ref[sl, :], preferred_element_type=F32)
        acc_a = a if acc_a is None else acc_a + a
        acc_b = b if acc_b is None else acc_b + b
    u_ref[...] = ((acc_a - acc_b) * scale).astype(BF16)
    w_ref[...] = ((acc_a + acc_b) * scale).astype(BF16)


def _unfold_kernel(ud_ref, wa_ref, wb_ref, j_ref, y_ref, *, r_last, i_mixed):
    i = pl.program_id(0)

    @pl.when(i < i_mixed)
    def _():
        y_ref[...] = ud_ref[...]

    @pl.when(i >= i_mixed)
    def _():
        r = i * DFT_TILE + lax.broadcasted_iota(jnp.int32, (DFT_TILE, 1), 0)
        y_ref[...] = jnp.where(r <= r_last, ud_ref[...],
                               _mirror(j_ref, wa_ref, wb_ref).astype(BF16))


@functools.lru_cache(maxsize=None)
def _dft_constants(seq, n_meta):
    t = DFT_TILE
    off = t - n_meta
    big_l = seq + n_meta
    half = big_l // 2
    n_real = seq // t
    nf = -(-(half + 1 + off) // t)
    hp = nf * t
    jf = np.zeros((nf, t, 2 * t), np.float32)
    for i in range(nf):
        a_idx, b_idx = min(n_real - i, n_real - 1), n_real - 1 - i
        for j in range(t):
            pos = i * t + j - off
            if 1 <= pos <= half - 1:
                tau, rho = divmod(big_l - pos - n_meta, t)
                assert tau in (a_idx, b_idx)
                jf[i, j, (t if tau == b_idx else 0) + rho] = 1.0
    r_last = half - n_meta
    i_mixed = r_last // t
    ju = np.zeros((3, t, 2 * t), np.float32)
    for i in range(i_mixed, n_real):
        a_idx, b_idx = n_real - i, n_real - 1 - i
        blk = np.zeros((t, 2 * t), np.float32)
        for j in range(t):
            r = i * t + j
            if r > r_last:
                tau, rho = divmod(big_l - (r + n_meta) + off, t)
                assert tau in (a_idx, b_idx)
                blk[j, (t if tau == b_idx else 0) + rho] = 1.0
        var = 1 if i == i_mixed else 2
        assert var == 1 or not ju[2].any() or (ju[2] == blk).all()
        ju[var] = blk
    ang = lambda prod: 2.0 * np.pi * (prod % big_l).astype(np.float64) / big_l
    pos = np.arange(hp, dtype=np.int64) - off
    d = ang(np.arange(t, dtype=np.int64)[:, None] * pos[None, :])
    k0 = t * np.arange(nf, dtype=np.int64) - off
    row = np.broadcast_to(ang(k0[:, None] * pos[None, :])[:, None, :], (nf, 8, hp))
    f = lambda a: np.ascontiguousarray(a, dtype=np.float32)
    return dict(jf=jf, ju=ju, dc=f(np.cos(d)), ds=f(np.sin(d)), rowc=f(np.cos(row)),
                rows=f(np.sin(row)), nf=nf, hp=hp, off=off, n_real=n_real,
                m_last=half + off, r_last=r_last, i_mixed=i_mixed)


def _seq_dft(p, q, pm, qm, seq, n_meta):
    c = _dft_constants(seq, n_meta)
    t, nf, hp, n_real, ncol = DFT_TILE, c["nf"], c["hp"], c["n_real"], p.shape[1]
    params = pltpu.CompilerParams(dimension_semantics=("arbitrary",),
                                  vmem_limit_bytes=V7X_VMEM_LIMIT)
    tile = lambda fn: pl.BlockSpec((t, ncol), lambda i: (fn(i), 0))
    lead = ((c["off"], 0), (0, 0))
    src = [tile(lambda i: jnp.maximum(i - 1, 0)),
           tile(lambda i: jnp.minimum(n_real - i, n_real - 1)),
           tile(lambda i: n_real - 1 - i)]
    pe, qo = pl.pallas_call(
        functools.partial(_fold_kernel, m_last=c["m_last"]),
        grid=(nf,),
        in_specs=src + src + [_resident((t, ncol)), _resident((t, ncol)),
                              pl.BlockSpec((1, t, 2 * t), lambda i: (i, 0, 0))],
        out_specs=(tile(lambda i: i), tile(lambda i: i)),
        out_shape=(jax.ShapeDtypeStruct((hp, ncol), BF16),) * 2,
        compiler_params=params,
        name="dft_fold",
    )(p, p, p, q, q, q, jnp.pad(pm, lead), jnp.pad(qm, lead), jnp.asarray(c["jf"]).astype(BF16))

    scale = float((seq + n_meta) * FNET_GROUP_DIM) ** -0.5
    rowspec = pl.BlockSpec((1, 8, hp), lambda i: (i, 0, 0))
    u, w = pl.pallas_call(
        functools.partial(_dft_kernel, scale=scale),
        grid=(nf,),
        in_specs=[_resident((hp, ncol)), _resident((hp, ncol)), _resident((t, hp)),
                  _resident((t, hp)), rowspec, rowspec],
        out_specs=(tile(lambda i: i), tile(lambda i: i)),
        out_shape=(jax.ShapeDtypeStruct((hp, ncol), BF16),) * 2,
        compiler_params=params,
        name="seq_dft",
    )(pe, qo, jnp.asarray(c["dc"]), jnp.asarray(c["ds"]), jnp.asarray(c["rowc"]),
      jnp.asarray(c["rows"]))

    i_mixed = c["i_mixed"]
    clip = lambda v: jnp.clip(v, 0, nf - 1)
    return pl.pallas_call(
        functools.partial(_unfold_kernel, r_last=c["r_last"], i_mixed=i_mixed),
        grid=(n_real,),
        in_specs=[tile(lambda i: clip(i + 1)), tile(lambda i: clip(n_real - i)),
                  tile(lambda i: clip(n_real - 1 - i)),
                  pl.BlockSpec((1, t, 2 * t),
                               lambda i: (jnp.where(i < i_mixed, 0,
                                                    jnp.where(i == i_mixed, 1, 2)), 0, 0))],
        out_specs=tile(lambda i: i),
        out_shape=jax.ShapeDtypeStruct((seq, ncol), BF16),
        compiler_params=params,
        name="dft_unfold",
    )(u, w, w, jnp.asarray(c["ju"]).astype(BF16))


FFN_CHUNK = 1024
TAIL_SPLIT = 2
TAIL_TM = 512


def _tail_kernel(x_ref, ol_ref, of_ref, ob_ref, y_ref, nmix_ref, gn_ref, nffn_ref, nfin_ref,
                 wg_ref, wo_ref, wf_ref, wout_ref, w1_ref, w2_ref, o_ref):
    tm = x_ref.shape[0]
    sub = tm // TAIL_SPLIT
    tiles = [(dict(), slice(i * sub, (i + 1) * sub)) for i in range(TAIL_SPLIT)]
    dot = functools.partial(jnp.dot, preferred_element_type=F32)
    nchunk = D_FF // FFN_CHUNK

    def norm_in(s, r):
        s["x"] = x_ref[r, :]
        s["xn"] = _rms(s["x"], nmix_ref[...]).astype(BF16)

    def gates(s, r):
        s["g"] = [dot(s["xn"], wg_ref[:, _OFF_G + h * HEAD_DV:_OFF_G + (h + 1) * HEAD_DV])
                  for h in range(GLA_HEADS)]
        s["ga"] = dot(s["xn"], wg_ref[:, _OFF_GA:_OFF_GF])
        s["gf"] = dot(s["xn"], wg_ref[:, _OFF_GF:_OFF_GEND])
        s["y_fnet"] = dot(y_ref[r, :], wf_ref[...])

    def gla_out(s, r):
        y_gla = None
        for h in range(GLA_HEADS):
            hv = slice(h * HEAD_DV, (h + 1) * HEAD_DV)
            o = (ol_ref[r, hv].astype(F32) + of_ref[r, hv].astype(F32)
                 + ob_ref[r, hv].astype(F32))
            g = s["g"][h]
            og = (_rms(o, gn_ref[...]) * (g * _sigmoid(g))).astype(BF16)
            part = dot(og, wo_ref[hv, :])
            y_gla = part if y_gla is None else y_gla + part
        s["y_gla"] = y_gla

    def merge(s, r):
        merged = _sigmoid(s["ga"]) * s["y_gla"] + _sigmoid(s["gf"]) * s["y_fnet"]
        s["h1"] = s["x"] + dot(merged.astype(BF16), wout_ref[...])
        s["u"] = _rms(s["h1"], nffn_ref[...]).astype(BF16)
        s["acc"] = s["h1"]

    def ffn_up(c):
        def stage(s, r):
            a = jnp.maximum(dot(s["u"], w1_ref[:, c * FFN_CHUNK:(c + 1) * FFN_CHUNK]), 0.0)
            s["a", c] = (a * a).astype(BF16)
        return stage

    def ffn_down(c):
        def stage(s, r):
            s["acc"] = s["acc"] + dot(s.pop(("a", c)),
                                      w2_ref[c * FFN_CHUNK:(c + 1) * FFN_CHUNK, :])
        return stage

    def norm_out(s, r):
        o_ref[r, :] = _rms(s["acc"], nfin_ref[...])

    stages = [norm_in, gates, gla_out, merge, ffn_up(0)]
    for c in range(nchunk):
        if c + 1 < nchunk:
            stages.append(ffn_up(c + 1))
        stages.append(ffn_down(c))
    stages.append(norm_out)
    _run_interleaved(stages, tiles)


def _tail_call(x2d, ol, o_f, o_b, y, nmix, gn, nffn, nfin, wg, wo, wf, wout, w1, w2, seq, tm):
    m = x2d.shape[0]
    nt = seq // tm
    row = lambda n: pl.BlockSpec((tm, n), lambda i: (i, 0))
    consts = [nmix, gn, nffn, nfin, wg, wo, wf, wout, w1, w2]
    return pl.pallas_call(
        _tail_kernel,
        grid=(m // tm,),
        in_specs=[row(D_MODEL), row(GLA_DV), row(GLA_DV), row(GLA_DV),
                  pl.BlockSpec((tm, FNET_WIDTH), lambda i: (i % nt, i // nt))]
                 + [_resident(c.shape) for c in consts],
        out_specs=row(D_MODEL),
        out_shape=jax.ShapeDtypeStruct((m, D_MODEL), F32),
        compiler_params=pltpu.CompilerParams(
            dimension_semantics=("arbitrary",), vmem_limit_bytes=V7X_VMEM_LIMIT),
        name="tail",
    )(x2d, ol, o_f, o_b, y, *consts)


@functools.lru_cache(maxsize=None)
def _gla_constants(rows):
    idx = np.arange(rows)
    same = (idx[:, None] // SUB) == (idx[None, :] // SUB)
    tri_f = (same & (idx[None, :] <= idx[:, None])).astype(np.float32)
    return tri_f, np.ascontiguousarray(tri_f.T)


@functools.lru_cache(maxsize=None)
def _channel_dft():
    d = np.arange(FNET_GROUP_DIM, dtype=np.int64)
    ang = 2.0 * np.pi * ((d[:, None] * d[None, :]) % FNET_GROUP_DIM) / FNET_GROUP_DIM
    return np.concatenate([np.cos(ang), np.sin(ang)], axis=1).astype(np.float32)


def kernel(x, meta_tokens, norm_mix, w_in, w_decay_fwd, b_decay_fwd, w_decay_bwd, b_decay_bwd,
           gla_norm, w_o_gla, w_fnet, w_out, norm_ffn, w_ff1, w_ff2, norm_final):
    batch, seq, d = x.shape
    assert d == D_MODEL and meta_tokens.shape == (N_META, D_MODEL)
    assert w_in.shape[0] == 1 and seq % (2 * SCAN_RB) == 0 and seq % DFT_TILE == 0
    wi = w_in[0]
    splits = np.cumsum([GLA_DK, GLA_DK, GLA_DV, GLA_DV, DECAY_RANK, DECAY_RANK, FNET_WIDTH,
                        D_MODEL])
    wq, wk, wv, wg, wrf, wrb, wf_in, wga, wgf = jnp.split(wi, [int(s) for s in splits], axis=1)
    w_main = jnp.concatenate([wq, wk, wv, wf_in], axis=1).astype(BF16)
    w_gate = jnp.concatenate([wg, wga, wgf], axis=1).astype(BF16)
    w_r = jnp.concatenate(
        [wrf, wrb, jnp.zeros((D_MODEL, R_PAD - 2 * DECAY_RANK), F32)], axis=1).astype(BF16)
    wdec = jnp.zeros((R_PAD, 2 * GLA_DK), F32)
    wdec = wdec.at[:DECAY_RANK, :GLA_DK].set(w_decay_fwd[0])
    wdec = wdec.at[DECAY_RANK:2 * DECAY_RANK, GLA_DK:].set(w_decay_bwd[0]).astype(BF16)
    bdec = jnp.concatenate([b_decay_fwd[0], b_decay_bwd[0]])[None, :]
    cs = jnp.asarray(_channel_dft())
    nw_mix = norm_mix[0][None, :]

    x2d = x.reshape(batch * seq, D_MODEL)
    v, ol, qe, ke, bend, p, pq_q = _mixin_call(x2d, nw_mix, w_main, w_r, wdec, bdec, cs,
                                               *_gla_constants(GLA_C), seq, MIX_TM, GLA_C)
    meta_chunk = jnp.pad(meta_tokens, ((SUB - N_META, 0), (0, 0)))
    vm, _, _, kem, _, pm, qm = _mixin_call(meta_chunk, nw_mix, w_main, w_r, wdec, bdec, cs,
                                           *_gla_constants(SUB), SUB, SUB, SUB)
    pm, qm = pm[SUB - N_META:], qm[SUB - N_META:]
    o_f, o_b = _gla_scan_call(qe, ke, v, bend, kem, vm, seq, batch)

    y = _seq_dft(p, pq_q, jnp.tile(pm, (1, batch)), jnp.tile(qm, (1, batch)), seq, N_META)

    out = _tail_call(x2d, ol, o_f, o_b, y, nw_mix, gla_norm[0][None, :], norm_ffn[0][None, :],
                     norm_final[None, :], w_gate, w_o_gla[0].astype(BF16),
                     w_fnet[0].astype(BF16), w_out[0].astype(BF16), w_ff1[0].astype(BF16),
                     w_ff2[0].astype(BF16), seq, TAIL_TM)
    return out.reshape(batch, seq, D_MODEL)
```

```python
import functools

import numpy as np
import jax
import jax.numpy as jnp
from jax import lax
from jax.experimental import pallas as pl
from jax.experimental.pallas import tpu as pltpu

F32 = jnp.float32
BF16 = jnp.bfloat16

D_MODEL = 1024
N_META = 16
GLA_HEADS = 4
GLA_DK = 512
GLA_DV = 1024
HEAD_DK = GLA_DK // GLA_HEADS
HEAD_DV = GLA_DV // GLA_HEADS
DECAY_RANK = 16
GATE_TEMP = 16.0
SUB = 64
FNET_GROUPS = 4
FNET_WIDTH = 512
FNET_GROUP_DIM = FNET_WIDTH // FNET_GROUPS
D_FF = 4 * D_MODEL
EPS = 1e-6
LOG2E = 1.4426950408889634
LANE = 128

_OFF_Q, _OFF_K, _OFF_V, _OFF_QKV_END = 0, 512, 1024, 2048
W_WINDOW = 1024
R_PAD = 128
F_LEAD = 2 * DECAY_RANK
F_EXT = F_LEAD + FNET_WIDTH + (LANE - F_LEAD)

GLA_C = 4 * SUB
MIX_TM = 512

V7X_VMEM_LIMIT = 56 * 1024 * 1024


def _rms(x, w):
    ms = jnp.mean(x * x, axis=-1, keepdims=True)
    return x * lax.rsqrt(ms + EPS) * w


def _sigmoid(x):
    return 1.0 / (1.0 + jnp.exp(-x))


def _run_interleaved(stages, tiles):
    for stage in stages:
        for state, rows in tiles:
            stage(state, rows)


def _window(block, index):
    return pl.BlockSpec(block, lambda *_: index, pipeline_mode=pl.Buffered(1))


def _resident(shape):
    return _window(shape, (0,) * len(shape))


_NT = (((1,), (1,)), ((), ()))
_TN = (((0,), (0,)), ((), ()))


def _mixin_kernel(x_ref, nw_ref, wqkv_ref, wrf_ref, wdec_ref, bdec_ref, cs_ref,
                  trif_ref, trib_ref, maskf_ref, maskb_ref,
                  v_ref, ol_ref, qe_ref, ke_ref, bend_ref, p_ref, qq_ref, qd_ref, kd_ref,
                  *, chunk):
    tm = x_ref.shape[0]
    n, s = chunk // SUB, SUB
    tiles = [(dict(), slice(i * chunk, (i + 1) * chunk)) for i in range(tm // chunk)]
    dot = functools.partial(jnp.dot, preferred_element_type=F32)
    cs = cs_ref[...].astype(BF16)
    tris = (trif_ref[...], trib_ref[...])
    masks = (maskf_ref[0:s, 0:s] != 0.0, maskb_ref[0:s, 0:s] != 0.0)
    half = GLA_DK // 2

    def norm_in(st, r):
        st["xn"] = _rms(x_ref[r, :], nw_ref[...]).astype(BF16)

    def decay_rank(st, r):
        st["f"] = dot(st["xn"], wrf_ref[:, 0:F_EXT]).astype(BF16)
        st["r"] = st["f"][:, 0:R_PAD]

    def log_decay(st, j):
        cols = slice(j * half, (j + 1) * half)
        x = dot(st["r"], wdec_ref[:, cols]) + bdec_ref[:, cols]
        la = (jnp.minimum(x, 0.0) - jnp.log(1.0 + jnp.exp(-jnp.abs(x)))) * (LOG2E / GATE_TEMP)
        hi = la.astype(BF16)
        st["la", j] = (hi, (la - hi.astype(F32)).astype(BF16))

    def project(st, r):
        mm = lambda lo, hi: dot(st["xn"], wqkv_ref[:, lo:hi])
        vmid = (_OFF_V + _OFF_QKV_END) // 2
        log_decay(st, 0)
        st["q"] = mm(_OFF_Q, _OFF_K) * (HEAD_DK ** -0.5)
        log_decay(st, 1)
        st["k"] = mm(_OFF_K, _OFF_V)
        log_decay(st, 2)
        v_ref[r, 0:GLA_DV // 2] = mm(_OFF_V, vmid).astype(BF16)
        log_decay(st, 3)
        v_ref[r, GLA_DV // 2:GLA_DV] = mm(vmid, _OFF_QKV_END).astype(BF16)

    def decay_factors(st, r):
        c0 = r.start // s
        for j in range(4):
            d, qcols = j // 2, slice((j % 2) * half, (j % 2 + 1) * half)
            cols = slice(j * half, (j + 1) * half)
            hi, lo = st.pop(("la", j))
            b = (dot(tris[d], hi) + dot(tris[d], lo)).reshape(n, s, half)
            if d == 0:
                b_mid, b_end = b[:, s // 2 - 1:s // 2, :], b[:, s - 1:s, :]
            else:
                b_mid, b_end = b[:, s // 2:s // 2 + 1, :], b[:, 0:1, :]
            qd = st["q"][:, qcols].reshape(n, s, half) * jnp.exp2(b - b_mid)
            kd = st["k"][:, qcols].reshape(n, s, half) * jnp.exp2(b_mid - b)
            qe = qd * jnp.exp2(b_mid)
            ke = kd * jnp.exp2(b_end - b_mid)
            qd_ref[d, r, qcols] = qd.reshape(chunk, half).astype(BF16)
            kd_ref[d, r, qcols] = kd.reshape(chunk, half).astype(BF16)
            qe_ref[r, cols] = qe.reshape(chunk, half).astype(BF16)
            ke_ref[r, cols] = ke.reshape(chunk, half).astype(BF16)
            bend_ref[c0:c0 + n, cols] = b_end.reshape(n, half)

    def intra_chunk(st, r):
        chunks = [slice(r.start + c * s, r.start + (c + 1) * s) for c in range(n)]
        for h in range(GLA_HEADS):
            hk = slice(h * HEAD_DK, (h + 1) * HEAD_DK)
            hv = slice(h * HEAD_DV, (h + 1) * HEAD_DV)
            sc = [[lax.dot_general(qd_ref[d, rows, hk], kd_ref[d, rows, hk], _NT,
                                   preferred_element_type=F32) for d in range(2)]
                  for rows in chunks]
            for rows, (sf, sb) in zip(chunks, sc):
                both = (jnp.where(masks[0], sf, 0.0) + jnp.where(masks[1], sb, 0.0)).astype(BF16)
                ol_ref[rows, hv] = dot(both, v_ref[rows, hv]).astype(BF16)

    def channel_dft(st, r):
        for grp in range(FNET_GROUPS):
            sl = slice(grp * FNET_GROUP_DIM, (grp + 1) * FNET_GROUP_DIM)
            pq = dot(st["f"][:, grp * FNET_GROUP_DIM:(grp + 2) * FNET_GROUP_DIM], cs)
            p_ref[r, sl] = pq[:, :FNET_GROUP_DIM].astype(BF16)
            qq_ref[r, sl] = pq[:, FNET_GROUP_DIM:].astype(BF16)

    _run_interleaved((norm_in, decay_rank, project, decay_factors, intra_chunk, channel_dft),
                     tiles)


def _mixin_call(x2d, nw, w_in_bf, wdec, bdec, cs, tri_f, tri_b, rows_per_batch, tm, chunk):
    m = x2d.shape[0]
    d_model = w_in_bf.shape[0]
    nt = rows_per_batch // tm
    nb = m // rows_per_batch
    row = lambda n: pl.BlockSpec((tm, n), lambda i: (i, 0))
    pq_spec = pl.BlockSpec((tm, FNET_WIDTH), lambda i: (i % nt, i // nt))
    wide = jax.ShapeDtypeStruct((m, 2 * GLA_DK), BF16)
    pq_shape = jax.ShapeDtypeStruct((rows_per_batch, nb * FNET_WIDTH), BF16)
    consts = [wdec, bdec, cs,
              jnp.asarray(tri_f).astype(BF16), jnp.asarray(tri_b).astype(BF16),
              jnp.asarray(tri_f), jnp.asarray(tri_b)]
    return pl.pallas_call(
        functools.partial(_mixin_kernel, chunk=chunk),
        grid=(m // tm,),
        in_specs=[row(D_MODEL), _resident(nw.shape),
                  _window((d_model, _OFF_QKV_END), (0, 0)),
                  _window((d_model, W_WINDOW), (0, 3))]
                 + [_resident(c.shape) for c in consts],
        out_specs=(row(GLA_DV), row(GLA_DV), row(2 * GLA_DK), row(2 * GLA_DK),
                   pl.BlockSpec((tm // SUB, 2 * GLA_DK), lambda i: (i, 0)), pq_spec, pq_spec),
        out_shape=(jax.ShapeDtypeStruct((m, GLA_DV), BF16), jax.ShapeDtypeStruct((m, GLA_DV), BF16),
                   wide, wide, jax.ShapeDtypeStruct((m // SUB, 2 * GLA_DK), F32),
                   pq_shape, pq_shape),
        scratch_shapes=[pltpu.VMEM((2, tm, GLA_DK), BF16), pltpu.VMEM((2, tm, GLA_DK), BF16)],
        compiler_params=pltpu.CompilerParams(
            dimension_semantics=("arbitrary",), vmem_limit_bytes=V7X_VMEM_LIMIT),
        name="mixin",
    )(x2d, nw, w_in_bf, w_in_bf, *consts)


SCAN_G = 512
SCAN_RB = 4096


def _scan_group(dirs):
    s, n = SUB, SCAN_G // SUB
    ahead = 2
    dec, st, inter = [], [], [{}, {}]
    for qe_ref, ke_ref, v_ref, bend_ref, st_ref, row0, chunk0, backward in dirs:
        dec_rows = jnp.tile(jnp.exp2(bend_ref[chunk0:chunk0 + n, :]), (HEAD_DK // n, 1))
        dec.append(dec_rows.T)
        st.append(st_ref[...])

    def order(step, backward):
        return n - 1 - step if backward else step

    def increment(d, step):
        _, ke_ref, v_ref, _, _, row0, _, backward = dirs[d]
        i = order(step, backward)
        sub = slice(row0 + i * s, row0 + (i + 1) * s)
        return lax.dot_general(ke_ref[sub, :], v_ref[sub, :], _TN,
                               preferred_element_type=F32)

    inc = {(d, step): increment(d, step) for step in range(ahead) for d in range(2)}
    for step in range(n):
        for d, (qe_ref, _, _, _, _, row0, _, backward) in enumerate(dirs):
            i = order(step, backward)
            sub = slice(row0 + i * s, row0 + (i + 1) * s)
            inter[d][i] = jnp.dot(qe_ref[sub, :], st[d].astype(BF16),
                                  preferred_element_type=F32)
            if step + ahead < n:
                inc[d, step + ahead] = increment(d, step + ahead)
            decay = jnp.broadcast_to(dec[d][:, i:i + 1], (HEAD_DK, HEAD_DV))
            st[d] = st[d] * decay + inc.pop((d, step))
    for d, spec in enumerate(dirs):
        spec[4][...] = st[d]
    return [jnp.concatenate([inter[d][i] for i in range(n)], axis=0) for d in range(2)]


def _gla_scan_kernel(qef_ref, kef_ref, vf_ref, bendf_ref,
                     qeb_ref, keb_ref, vb_ref, bendb_ref, kem_ref, vm_ref,
                     of_ref, ob_ref, stf_ref, stb_ref):
    rb = qef_ref.shape[0]
    ng = rb // SCAN_G

    @pl.when(pl.program_id(2) == 0)
    def _():
        stf_ref[...] = lax.dot_general(kem_ref[...], vm_ref[...], _TN,
                                       preferred_element_type=F32)
        stb_ref[...] = jnp.zeros_like(stb_ref)

    for j in range(ng):
        jb = ng - 1 - j
        cpg = SCAN_G // SUB
        in_f, in_b = _scan_group([
            (qef_ref, kef_ref, vf_ref, bendf_ref, stf_ref, j * SCAN_G, j * cpg, False),
            (qeb_ref, keb_ref, vb_ref, bendb_ref, stb_ref, jb * SCAN_G, jb * cpg, True)])
        of_ref[j * SCAN_G:(j + 1) * SCAN_G, :] = in_f.astype(BF16)
        ob_ref[jb * SCAN_G:(jb + 1) * SCAN_G, :] = in_b.astype(BF16)


def _gla_scan_call(qe, ke, v, bend, kem, vm, seq, batch):
    h, rb = GLA_HEADS, SCAN_RB
    nt = seq // rb
    fwd = lambda b, hh, t: b * nt + t
    bwd = lambda b, hh, t: b * nt + nt - 1 - t
    cb = rb // SUB

    def dk_blk(rowfn, dircol, rows):
        return pl.BlockSpec((rows, HEAD_DK), lambda b, hh, t: (rowfn(b, hh, t), dircol + hh))

    def dv_blk(rowfn):
        return pl.BlockSpec((rb, HEAD_DV), lambda b, hh, t: (rowfn(b, hh, t), hh))

    in_specs = [
        dk_blk(fwd, 0, rb), dk_blk(fwd, 0, rb), dv_blk(fwd), dk_blk(fwd, 0, cb),
        dk_blk(bwd, h, rb), dk_blk(bwd, h, rb), dv_blk(bwd), dk_blk(bwd, h, cb),
        pl.BlockSpec((SUB, HEAD_DK), lambda b, hh, t: (0, hh)),
        pl.BlockSpec((SUB, HEAD_DV), lambda b, hh, t: (0, hh)),
    ]
    out = jax.ShapeDtypeStruct((batch * seq, GLA_DV), BF16)
    return pl.pallas_call(
        _gla_scan_kernel,
        grid=(batch, h, nt),
        in_specs=in_specs,
        out_specs=(dv_blk(fwd), dv_blk(bwd)),
        out_shape=(out, out),
        scratch_shapes=[pltpu.VMEM((HEAD_DK, HEAD_DV), F32), pltpu.VMEM((HEAD_DK, HEAD_DV), F32)],
        compiler_params=pltpu.CompilerParams(
            dimension_semantics=("arbitrary", "arbitrary", "arbitrary"),
            vmem_limit_bytes=V7X_VMEM_LIMIT),
        name="gla_scan",
    )(qe, ke, v, bend, qe, ke, v, bend, kem, vm)


DFT_TILE = 256


def _mirror(j_ref, a_ref, b_ref):
    t = DFT_TILE
    return (jnp.dot(j_ref[0, :, :t], a_ref[...], preferred_element_type=F32)
            + jnp.dot(j_ref[0, :, t:], b_ref[...], preferred_element_type=F32))


def _fold_kernel(pd_ref, pa_ref, pb_ref, qd_ref, qa_ref, qb_ref, pz_ref, qz_ref, j_ref,
                 pe_ref, qo_ref, *, m_last):
    i = pl.program_id(0)
    first = i == 0
    p_dir = jnp.where(first, pz_ref[...], pd_ref[...]).astype(F32)
    q_dir = jnp.where(first, qz_ref[...], qd_ref[...]).astype(F32)
    m = i * DFT_TILE + lax.broadcasted_iota(jnp.int32, (DFT_TILE, 1), 0)
    keep = m <= m_last
    pe_ref[...] = jnp.where(keep, p_dir + _mirror(j_ref, pa_ref, pb_ref), 0.0).astype(BF16)
    qo_ref[...] = jnp.where(keep, q_dir - _mirror(j_ref, qa_ref, qb_ref), 0.0).astype(BF16)


def _dft_kernel(pe_ref, qo_ref, dc_ref, ds_ref, rowc_ref, rows_ref, u_ref, w_ref, *, scale):
    t = DFT_TILE
    acc_a = acc_b = None
    for c in range(pe_ref.shape[0] // t):
        sl = slice(c * t, (c + 1) * t)
        rc, rs = rowc_ref[0, 0:1, sl], rows_ref[0, 0:1, sl]
        dc, ds = dc_ref[:, sl], ds_ref[:, sl]
        tc = (dc * rc - ds * rs).astype(BF16)
        ts = (ds * rc + dc * rs).astype(BF16)
        a = jnp.dot(tc, pe_ref[sl, :], preferred_element_type=F32)
        b = jnp.dot(ts, qo_ref[sl, :], preferred_element_type=F32)
        acc_a = a if acc_a is None else acc_a + a
        acc_b = b if acc_b is None else acc_b + b
    u_ref[...] = ((acc_a - acc_b) * scale).astype(BF16)
    w_ref[...] = ((acc_a + acc_b) * scale).astype(BF16)


def _unfold_kernel(ud_ref, wa_ref, wb_ref, j_ref, y_ref, *, r_last, i_mixed):
    i = pl.program_id(0)

    @pl.when(i < i_mixed)
    def _():
        y_ref[...] = ud_ref[...]

    @pl.when(i >= i_mixed)
    def _():
        r = i * DFT_TILE + lax.broadcasted_iota(jnp.int32, (DFT_TILE, 1), 0)
        y_ref[...] = jnp.where(r <= r_last, ud_ref[...],
                               _mirror(j_ref, wa_ref, wb_ref).astype(BF16))


@functools.lru_cache(maxsize=None)
def _dft_constants(seq, n_meta):
    t = DFT_TILE
    off = t - n_meta
    big_l = seq + n_meta
    half = big_l // 2
    n_real = seq // t
    nf = -(-(half + 1 + off) // t)
    hp = nf * t
    jf = np.zeros((nf, t, 2 * t), np.float32)
    for i in range(nf):
        a_idx, b_idx = min(n_real - i, n_real - 1), n_real - 1 - i
        for j in range(t):
            pos = i * t + j - off
            if 1 <= pos <= half - 1:
                tau, rho = divmod(big_l - pos - n_meta, t)
                assert tau in (a_idx, b_idx)
                jf[i, j, (t if tau == b_idx else 0) + rho] = 1.0
    r_last = half - n_meta
    i_mixed = r_last // t
    ju = np.zeros((3, t, 2 * t), np.float32)
    for i in range(i_mixed, n_real):
        a_idx, b_idx = n_real - i, n_real - 1 - i
        blk = np.zeros((t, 2 * t), np.float32)
        for j in range(t):
            r = i * t + j
            if r > r_last:
                tau, rho = divmod(big_l - (r + n_meta) + off, t)
                assert tau in (a_idx, b_idx)
                blk[j, (t if tau == b_idx else 0) + rho] = 1.0
        var = 1 if i == i_mixed else 2
        assert var == 1 or not ju[2].any() or (ju[2] == blk).all()
        ju[var] = blk
    ang = lambda prod: 2.0 * np.pi * (prod % big_l).astype(np.float64) / big_l
    pos = np.arange(hp, dtype=np.int64) - off
    d = ang(np.arange(t, dtype=np.int64)[:, None] * pos[None, :])
    k0 = t * np.arange(nf, dtype=np.int64) - off
    row = np.broadcast_to(ang(k0[:, None] * pos[None, :])[:, None, :], (nf, 8, hp))
    f = lambda a: np.ascontiguousarray(a, dtype=np.float32)
    return dict(jf=jf, ju=ju, dc=f(np.cos(d)), ds=f(np.sin(d)), rowc=f(np.cos(row)),
                rows=f(np.sin(row)), nf=nf, hp=hp, off=off, n_real=n_real,
                m_last=half + off, r_last=r_last, i_mixed=i_mixed)


def _seq_dft(p, q, pm, qm, seq, n_meta):
    c = _dft_constants(seq, n_meta)
    t, nf, hp, n_real, ncol = DFT_TILE, c["nf"], c["hp"], c["n_real"], p.shape[1]
    params = pltpu.CompilerParams(dimension_semantics=("arbitrary",),
                                  vmem_limit_bytes=V7X_VMEM_LIMIT)
    tile = lambda fn: pl.BlockSpec((t, ncol), lambda i: (fn(i), 0))
    lead = ((c["off"], 0), (0, 0))
    src = [tile(lambda i: jnp.maximum(i - 1, 0)),
           tile(lambda i: jnp.minimum(n_real - i, n_real - 1)),
           tile(lambda i: n_real - 1 - i)]
    pe, qo = pl.pallas_call(
        functools.partial(_fold_kernel, m_last=c["m_last"]),
        grid=(nf,),
        in_specs=src + src + [_resident((t, ncol)), _resident((t, ncol)),
                              pl.BlockSpec((1, t, 2 * t), lambda i: (i, 0, 0))],
        out_specs=(tile(lambda i: i), tile(lambda i: i)),
        out_shape=(jax.ShapeDtypeStruct((hp, ncol), BF16),) * 2,
        compiler_params=params,
        name="dft_fold",
    )(p, p, p, q, q, q, jnp.pad(pm, lead), jnp.pad(qm, lead), jnp.asarray(c["jf"]).astype(BF16))

    scale = float((seq + n_meta) * FNET_GROUP_DIM) ** -0.5
    rowspec = pl.BlockSpec((1, 8, hp), lambda i: (i, 0, 0))
    u, w = pl.pallas_call(
        functools.partial(_dft_kernel, scale=scale),
        grid=(nf,),
        in_specs=[_resident((hp, ncol)), _resident((hp, ncol)), _resident((t, hp)),
                  _resident((t, hp)), rowspec, rowspec],
        out_specs=(tile(lambda i: i), tile(lambda i: i)),
        out_shape=(jax.ShapeDtypeStruct((hp, ncol), BF16),) * 2,
        compiler_params=params,
        name="seq_dft",
    )(pe, qo, jnp.asarray(c["dc"]), jnp.asarray(c["ds"]), jnp.asarray(c["rowc"]),
      jnp.asarray(c["rows"]))

    i_mixed = c["i_mixed"]
    clip = lambda v: jnp.clip(v, 0, nf - 1)
    return pl.pallas_call(
        functools.partial(_unfold_kernel, r_last=c["r_last"], i_mixed=i_mixed),
        grid=(n_real,),
        in_specs=[tile(lambda i: clip(i + 1)), tile(lambda i: clip(n_real - i)),
                  tile(lambda i: clip(n_real - 1 - i)),
                  pl.BlockSpec((1, t, 2 * t),
                               lambda i: (jnp.where(i < i_mixed, 0,
                                                    jnp.where(i == i_mixed, 1, 2)), 0, 0))],
        out_specs=tile(lambda i: i),
        out_shape=jax.ShapeDtypeStruct((seq, ncol), BF16),
        compiler_params=params,
        name="dft_unfold",
    )(u, w, w, jnp.asarray(c["ju"]).astype(BF16))


FFN_CHUNK = 1024
TAIL_SPLIT = 2
TAIL_TM = 512


def _tail_kernel(x_ref, ol_ref, of_ref, ob_ref, y_ref, nmix_ref, gn_ref, nffn_ref, nfin_ref,
                 wg_ref, wm_ref, wo_ref, wf_ref, wout_ref, w1_ref, w2_ref, o_ref):
    tm = x_ref.shape[0]
    sub = tm // TAIL_SPLIT
    tiles = [(dict(), slice(i * sub, (i + 1) * sub)) for i in range(TAIL_SPLIT)]
    dot = functools.partial(jnp.dot, preferred_element_type=F32)
    nchunk = D_FF // FFN_CHUNK

    def norm_in(s, r):
        s["x"] = x_ref[r, :]
        s["xn"] = _rms(s["x"], nmix_ref[...]).astype(BF16)

    def gates(s, r):
        s["g"] = [dot(s["xn"], wg_ref[:, h * HEAD_DV:(h + 1) * HEAD_DV])
                  for h in range(GLA_HEADS)]
        s["ga"] = dot(s["xn"], wm_ref[:, 0:D_MODEL])
        s["gf"] = dot(s["xn"], wm_ref[:, D_MODEL:2 * D_MODEL])
        s["y_fnet"] = dot(y_ref[r, :], wf_ref[...])

    def gla_out(s, r):
        y_gla = None
        for h in range(GLA_HEADS):
            hv = slice(h * HEAD_DV, (h + 1) * HEAD_DV)
            o = (ol_ref[r, hv].astype(F32) + of_ref[r, hv].astype(F32)
                 + ob_ref[r, hv].astype(F32))
            g = s["g"][h]
            og = (_rms(o, gn_ref[...]) * (g * _sigmoid(g))).astype(BF16)
            part = dot(og, wo_ref[hv, :])
            y_gla = part if y_gla is None else y_gla + part
        s["y_gla"] = y_gla

    def merge(s, r):
        merged = _sigmoid(s["ga"]) * s["y_gla"] + _sigmoid(s["gf"]) * s["y_fnet"]
        s["h1"] = s["x"] + dot(merged.astype(BF16), wout_ref[...])
        s["u"] = _rms(s["h1"], nffn_ref[...]).astype(BF16)
        s["acc"] = s["h1"]

    def ffn_up(c):
        def stage(s, r):
            a = jnp.maximum(dot(s["u"], w1_ref[:, c * FFN_CHUNK:(c + 1) * FFN_CHUNK]), 0.0)
            s["a", c] = (a * a).astype(BF16)
        return stage

    def ffn_down(c):
        def stage(s, r):
            s["acc"] = s["acc"] + dot(s.pop(("a", c)),
                                      w2_ref[c * FFN_CHUNK:(c + 1) * FFN_CHUNK, :])
        return stage

    def norm_out(s, r):
        o_ref[r, :] = _rms(s["acc"], nfin_ref[...])

    stages = [norm_in, gates, gla_out, merge, ffn_up(0)]
    for c in range(nchunk):
        if c + 1 < nchunk:
            stages.append(ffn_up(c + 1))
        stages.append(ffn_down(c))
    stages.append(norm_out)
    _run_interleaved(stages, tiles)


def _tail_call(x2d, ol, o_f, o_b, y, nmix, gn, nffn, nfin, w_in_bf, w_merge, wo, wf, wout, w1,
               w2, seq, tm):
    m = x2d.shape[0]
    nt = seq // tm
    row = lambda n: pl.BlockSpec((tm, n), lambda i: (i, 0))
    norms = [nmix, gn, nffn, nfin]
    consts = [w_merge, wo, wf, wout, w1, w2]
    return pl.pallas_call(
        _tail_kernel,
        grid=(m // tm,),
        in_specs=[row(D_MODEL), row(GLA_DV), row(GLA_DV), row(GLA_DV),
                  pl.BlockSpec((tm, FNET_WIDTH), lambda i: (i % nt, i // nt))]
                 + [_resident(c.shape) for c in norms]
                 + [_window((w_in_bf.shape[0], W_WINDOW), (0, 2))]
                 + [_resident(c.shape) for c in consts],
        out_specs=row(D_MODEL),
        out_shape=jax.ShapeDtypeStruct((m, D_MODEL), F32),
        compiler_params=pltpu.CompilerParams(
            dimension_semantics=("arbitrary",), vmem_limit_bytes=V7X_VMEM_LIMIT),
        name="tail",
    )(x2d, ol, o_f, o_b, y, *norms, w_in_bf, *consts)


@functools.lru_cache(maxsize=None)
def _gla_constants(rows):
    idx = np.arange(rows)
    same = (idx[:, None] // SUB) == (idx[None, :] // SUB)
    tri_f = (same & (idx[None, :] <= idx[:, None])).astype(np.float32)
    return tri_f, np.ascontiguousarray(tri_f.T)


@functools.lru_cache(maxsize=None)
def _channel_dft():
    d = np.arange(FNET_GROUP_DIM, dtype=np.int64)
    ang = 2.0 * np.pi * ((d[:, None] * d[None, :]) % FNET_GROUP_DIM) / FNET_GROUP_DIM
    cs = np.zeros((2 * FNET_GROUP_DIM, 2 * FNET_GROUP_DIM), np.float32)
    cs[F_LEAD:F_LEAD + FNET_GROUP_DIM] = np.concatenate([np.cos(ang), np.sin(ang)], axis=1)
    return cs


def kernel(x, meta_tokens, norm_mix, w_in, w_decay_fwd, b_decay_fwd, w_decay_bwd, b_decay_bwd,
           gla_norm, w_o_gla, w_fnet, w_out, norm_ffn, w_ff1, w_ff2, norm_final):
    batch, seq, d = x.shape
    assert d == D_MODEL and meta_tokens.shape == (N_META, D_MODEL)
    assert w_in.shape[0] == 1 and seq % (2 * SCAN_RB) == 0 and seq % DFT_TILE == 0
    wi = w_in[0].astype(BF16)
    gate0 = 2 * GLA_DK + 2 * GLA_DV + 2 * DECAY_RANK + FNET_WIDTH
    assert wi.shape[1] == gate0 + 2 * D_MODEL
    w_merge = wi[:, gate0:]
    wdec = jnp.zeros((R_PAD, 2 * GLA_DK), F32)
    wdec = wdec.at[:DECAY_RANK, :GLA_DK].set(w_decay_fwd[0])
    wdec = wdec.at[DECAY_RANK:2 * DECAY_RANK, GLA_DK:].set(w_decay_bwd[0]).astype(BF16)
    bdec = jnp.concatenate([b_decay_fwd[0], b_decay_bwd[0]])[None, :]
    cs = jnp.asarray(_channel_dft())
    nw_mix = norm_mix[0][None, :]

    x2d = x.reshape(batch * seq, D_MODEL)
    v, ol, qe, ke, bend, p, pq_q = _mixin_call(x2d, nw_mix, wi, wdec, bdec, cs,
                                               *_gla_constants(GLA_C), seq, MIX_TM, GLA_C)
    meta_chunk = jnp.pad(meta_tokens, ((SUB - N_META, 0), (0, 0)))
    vm, _, _, kem, _, pm, qm = _mixin_call(meta_chunk, nw_mix, wi, wdec, bdec, cs,
                                           *_gla_constants(SUB), SUB, SUB, SUB)
    pm, qm = pm[SUB - N_META:], qm[SUB - N_META:]
    o_f, o_b = _gla_scan_call(qe, ke, v, bend, kem, vm, seq, batch)

    y = _seq_dft(p, pq_q, jnp.tile(pm, (1, batch)), jnp.tile(qm, (1, batch)), seq, N_META)

    out = _tail_call(x2d, ol, o_f, o_b, y, nw_mix, gla_norm[0][None, :], norm_ffn[0][None, :],
                     norm_final[None, :], wi, w_merge, w_o_gla[0].astype(BF16),
                     w_fnet[0].astype(BF16), w_out[0].astype(BF16), w_ff1[0].astype(BF16),
                     w_ff2[0].astype(BF16), seq, TAIL_TM)
    return out.reshape(batch, seq, D_MODEL)
```

```python
import functools

import numpy as np
import jax
import jax.numpy as jnp
from jax import lax
from jax.experimental import pallas as pl
from jax.experimental.pallas import tpu as pltpu

F32 = jnp.float32
BF16 = jnp.bfloat16

D_MODEL = 1024
N_META = 16
GLA_HEADS = 4
GLA_DK = 512
GLA_DV = 1024
HEAD_DK = GLA_DK // GLA_HEADS
HEAD_DV = GLA_DV // GLA_HEADS
DECAY_RANK = 16
GATE_TEMP = 16.0
SUB = 64
FNET_GROUPS = 4
FNET_WIDTH = 512
FNET_GROUP_DIM = FNET_WIDTH // FNET_GROUPS
D_FF = 4 * D_MODEL
EPS = 1e-6
LOG2E = 1.4426950408889634
LANE = 128

_OFF_Q, _OFF_K, _OFF_V, _OFF_QKV_END = 0, 512, 1024, 2048
W_WINDOW = 1024
R_PAD = 128
F_LEAD = 2 * DECAY_RANK
F_EXT = F_LEAD + FNET_WIDTH + (LANE - F_LEAD)

GLA_C = 4 * SUB
MIX_TM = 1024

V7X_VMEM_LIMIT = 56 * 1024 * 1024


def _rms(x, w):
    ms = jnp.mean(x * x, axis=-1, keepdims=True)
    return x * lax.rsqrt(ms + EPS) * w


def _sigmoid(x):
    return 1.0 / (1.0 + jnp.exp(-x))


def _run_interleaved(stages, tiles):
    for stage in stages:
        for state, rows in tiles:
            stage(state, rows)


def _window(block, index):
    return pl.BlockSpec(block, lambda *_: index, pipeline_mode=pl.Buffered(1))


def _resident(shape):
    return _window(shape, (0,) * len(shape))


_NT = (((1,), (1,)), ((), ()))
_TN = (((0,), (0,)), ((), ()))


def _mixin_kernel(x_ref, nw_ref, wqkv_ref, wrf_ref, wdec_ref, bdec_ref, cs_ref,
                  trif_ref, trib_ref, maskf_ref, maskb_ref,
                  v_ref, ol_ref, qe_ref, ke_ref, bend_ref, p_ref, qq_ref, *, chunk):
    tm = x_ref.shape[0]
    n, s = chunk // SUB, SUB
    tiles = [(dict(), slice(i * chunk, (i + 1) * chunk)) for i in range(tm // chunk)]
    dot = functools.partial(jnp.dot, preferred_element_type=F32)
    cs = cs_ref[...].astype(BF16)
    tris = (trif_ref[...], trib_ref[...])
    masks = (maskf_ref[0:s, 0:s] != 0.0, maskb_ref[0:s, 0:s] != 0.0)
    half = GLA_DK // 2

    def norm_in(st, r):
        st["xn"] = _rms(x_ref[r, :], nw_ref[...]).astype(BF16)

    def decay_rank(st, r):
        st["f"] = dot(st["xn"], wrf_ref[:, 0:F_EXT]).astype(BF16)
        st["r"] = st["f"][:, 0:R_PAD]

    def log_decay(st, j):
        cols = slice(j * half, (j + 1) * half)
        x = dot(st["r"], wdec_ref[:, cols]) + bdec_ref[:, cols]
        la = (jnp.minimum(x, 0.0) - jnp.log(1.0 + jnp.exp(-jnp.abs(x)))) * (LOG2E / GATE_TEMP)
        hi = la.astype(BF16)
        st["la", j] = (hi, (la - hi.astype(F32)).astype(BF16))

    def project(st, r):
        mm = lambda lo, hi: dot(st["xn"], wqkv_ref[:, lo:hi])
        vmid = (_OFF_V + _OFF_QKV_END) // 2
        log_decay(st, 0)
        st["q"] = mm(_OFF_Q, _OFF_K) * (HEAD_DK ** -0.5)
        log_decay(st, 1)
        st["k"] = mm(_OFF_K, _OFF_V)
        log_decay(st, 2)
        v_ref[r, 0:GLA_DV // 2] = mm(_OFF_V, vmid).astype(BF16)
        log_decay(st, 3)
        v_ref[r, GLA_DV // 2:GLA_DV] = mm(vmid, _OFF_QKV_END).astype(BF16)

    def decay_factors(st, r):
        c0 = r.start // s
        for j in range(4):
            d, qcols = j // 2, slice((j % 2) * half, (j % 2 + 1) * half)
            cols = slice(j * half, (j + 1) * half)
            hi, lo = st.pop(("la", j))
            b = (dot(tris[d], hi) + dot(tris[d], lo)).reshape(n, s, half)
            if d == 0:
                b_mid, b_end = b[:, s // 2 - 1:s // 2, :], b[:, s - 1:s, :]
            else:
                b_mid, b_end = b[:, s // 2:s // 2 + 1, :], b[:, 0:1, :]
            qd = st["q"][:, qcols].reshape(n, s, half) * jnp.exp2(b - b_mid)
            kd = st["k"][:, qcols].reshape(n, s, half) * jnp.exp2(b_mid - b)
            qe = qd * jnp.exp2(b_mid)
            ke = kd * jnp.exp2(b_end - b_mid)
            st["qd", d, j % 2] = qd.reshape(chunk, half).astype(BF16)
            st["kd", d, j % 2] = kd.reshape(chunk, half).astype(BF16)
            qe_ref[r, cols] = qe.reshape(chunk, half).astype(BF16)
            ke_ref[r, cols] = ke.reshape(chunk, half).astype(BF16)
            bend_ref[c0:c0 + n, cols] = b_end.reshape(n, half)

    def intra_chunk(st, r):
        for h in range(GLA_HEADS):
            hk = slice((h % 2) * HEAD_DK, (h % 2 + 1) * HEAD_DK)
            hv = slice(h * HEAD_DV, (h + 1) * HEAD_DV)
            sc = [[lax.dot_general(st["qd", d, h // 2][c * s:(c + 1) * s, hk],
                                   st["kd", d, h // 2][c * s:(c + 1) * s, hk], _NT,
                                   preferred_element_type=F32) for d in range(2)]
                  for c in range(n)]
            for c, (sf, sb) in enumerate(sc):
                rows = slice(r.start + c * s, r.start + (c + 1) * s)
                both = (jnp.where(masks[0], sf, 0.0) + jnp.where(masks[1], sb, 0.0)).astype(BF16)
                ol_ref[rows, hv] = dot(both, v_ref[rows, hv]).astype(BF16)

    def channel_dft(st, r):
        for grp in range(FNET_GROUPS):
            sl = slice(grp * FNET_GROUP_DIM, (grp + 1) * FNET_GROUP_DIM)
            pq = dot(st["f"][:, grp * FNET_GROUP_DIM:(grp + 2) * FNET_GROUP_DIM], cs)
            p_ref[r, sl] = pq[:, :FNET_GROUP_DIM].astype(BF16)
            qq_ref[r, sl] = pq[:, FNET_GROUP_DIM:].astype(BF16)

    _run_interleaved((norm_in, decay_rank, project, decay_factors, intra_chunk, channel_dft),
                     tiles)


def _mixin_call(x2d, nw, w_in_bf, wdec, bdec, cs, tri_f, tri_b, rows_per_batch, tm, chunk):
    m = x2d.shape[0]
    d_model = w_in_bf.shape[0]
    nt = rows_per_batch // tm
    nb = m // rows_per_batch
    row = lambda n: pl.BlockSpec((tm, n), lambda i: (i, 0))
    pq_spec = pl.BlockSpec((tm, FNET_WIDTH), lambda i: (i % nt, i // nt))
    wide = jax.ShapeDtypeStruct((m, 2 * GLA_DK), BF16)
    pq_shape = jax.ShapeDtypeStruct((rows_per_batch, nb * FNET_WIDTH), BF16)
    consts = [wdec, bdec, cs,
              jnp.asarray(tri_f).astype(BF16), jnp.asarray(tri_b).astype(BF16),
              jnp.asarray(tri_f), jnp.asarray(tri_b)]
    return pl.pallas_call(
        functools.partial(_mixin_kernel, chunk=chunk),
        grid=(m // tm,),
        in_specs=[row(D_MODEL), _resident(nw.shape),
                  _window((d_model, _OFF_QKV_END), (0, 0)),
                  _window((d_model, W_WINDOW), (0, 3))]
                 + [_resident(c.shape) for c in consts],
        out_specs=(row(GLA_DV), row(GLA_DV), row(2 * GLA_DK), row(2 * GLA_DK),
                   pl.BlockSpec((tm // SUB, 2 * GLA_DK), lambda i: (i, 0)), pq_spec, pq_spec),
        out_shape=(jax.ShapeDtypeStruct((m, GLA_DV), BF16), jax.ShapeDtypeStruct((m, GLA_DV), BF16),
                   wide, wide, jax.ShapeDtypeStruct((m // SUB, 2 * GLA_DK), F32),
                   pq_shape, pq_shape),
        compiler_params=pltpu.CompilerParams(
            dimension_semantics=("arbitrary",), vmem_limit_bytes=V7X_VMEM_LIMIT),
        name="mixin",
    )(x2d, nw, w_in_bf, w_in_bf, *consts)


SCAN_G = 512
SCAN_RB = 4096


def _scan_group(dirs):
    s, n = SUB, SCAN_G // SUB
    ahead = 2
    dec, st, inter = [], [], [{}, {}]
    for qe_ref, ke_ref, v_ref, bend_ref, st_ref, row0, chunk0, backward in dirs:
        dec_rows = jnp.tile(jnp.exp2(bend_ref[chunk0:chunk0 + n, :]), (HEAD_DK // n, 1))
        dec.append(dec_rows.T)
        st.append(st_ref[...])

    def order(step, backward):
        return n - 1 - step if backward else step

    def increment(d, step):
        _, ke_ref, v_ref, _, _, row0, _, backward = dirs[d]
        i = order(step, backward)
        sub = slice(row0 + i * s, row0 + (i + 1) * s)
        return lax.dot_general(ke_ref[sub, :], v_ref[sub, :], _TN,
                               preferred_element_type=F32)

    inc = {(d, step): increment(d, step) for step in range(ahead) for d in range(2)}
    for step in range(n):
        for d, (qe_ref, _, _, _, _, row0, _, backward) in enumerate(dirs):
            i = order(step, backward)
            sub = slice(row0 + i * s, row0 + (i + 1) * s)
            inter[d][i] = jnp.dot(qe_ref[sub, :], st[d].astype(BF16),
                                  preferred_element_type=F32)
            if step + ahead < n:
                inc[d, step + ahead] = increment(d, step + ahead)
            decay = jnp.broadcast_to(dec[d][:, i:i + 1], (HEAD_DK, HEAD_DV))
            st[d] = st[d] * decay + inc.pop((d, step))
    for d, spec in enumerate(dirs):
        spec[4][...] = st[d]
    return [jnp.concatenate([inter[d][i] for i in range(n)], axis=0) for d in range(2)]


def _gla_scan_kernel(qef_ref, kef_ref, vf_ref, bendf_ref,
                     qeb_ref, keb_ref, vb_ref, bendb_ref, kem_ref, vm_ref,
                     of_ref, ob_ref, stf_ref, stb_ref):
    rb = qef_ref.shape[0]
    ng = rb // SCAN_G

    @pl.when(pl.program_id(2) == 0)
    def _():
        stf_ref[...] = lax.dot_general(kem_ref[...], vm_ref[...], _TN,
                                       preferred_element_type=F32)
        stb_ref[...] = jnp.zeros_like(stb_ref)

    for j in range(ng):
        jb = ng - 1 - j
        cpg = SCAN_G // SUB
        in_f, in_b = _scan_group([
            (qef_ref, kef_ref, vf_ref, bendf_ref, stf_ref, j * SCAN_G, j * cpg, False),
            (qeb_ref, keb_ref, vb_ref, bendb_ref, stb_ref, jb * SCAN_G, jb * cpg, True)])
        of_ref[j * SCAN_G:(j + 1) * SCAN_G, :] = in_f.astype(BF16)
        ob_ref[jb * SCAN_G:(jb + 1) * SCAN_G, :] = in_b.astype(BF16)


def _gla_scan_call(qe, ke, v, bend, kem, vm, seq, batch):
    h, rb = GLA_HEADS, SCAN_RB
    nt = seq // rb
    fwd = lambda b, hh, t: b * nt + t
    bwd = lambda b, hh, t: b * nt + nt - 1 - t
    cb = rb // SUB

    def dk_blk(rowfn, dircol, rows):
        return pl.BlockSpec((rows, HEAD_DK), lambda b, hh, t: (rowfn(b, hh, t), dircol + hh))

    def dv_blk(rowfn):
        return pl.BlockSpec((rb, HEAD_DV), lambda b, hh, t: (rowfn(b, hh, t), hh))

    in_specs = [
        dk_blk(fwd, 0, rb), dk_blk(fwd, 0, rb), dv_blk(fwd), dk_blk(fwd, 0, cb),
        dk_blk(bwd, h, rb), dk_blk(bwd, h, rb), dv_blk(bwd), dk_blk(bwd, h, cb),
        pl.BlockSpec((SUB, HEAD_DK), lambda b, hh, t: (0, hh)),
        pl.BlockSpec((SUB, HEAD_DV), lambda b, hh, t: (0, hh)),
    ]
    out = jax.ShapeDtypeStruct((batch * seq, GLA_DV), BF16)
    return pl.pallas_call(
        _gla_scan_kernel,
        grid=(batch, h, nt),
        in_specs=in_specs,
        out_specs=(dv_blk(fwd), dv_blk(bwd)),
        out_shape=(out, out),
        scratch_shapes=[pltpu.VMEM((HEAD_DK, HEAD_DV), F32), pltpu.VMEM((HEAD_DK, HEAD_DV), F32)],
        compiler_params=pltpu.CompilerParams(
            dimension_semantics=("arbitrary", "arbitrary", "arbitrary"),
            vmem_limit_bytes=V7X_VMEM_LIMIT),
        name="gla_scan",
    )(qe, ke, v, bend, qe, ke, v, bend, kem, vm)


DFT_TILE = 256


def _mirror(j_ref, a_ref, b_ref):
    t = DFT_TILE
    return (jnp.dot(j_ref[0, :, :t], a_ref[...], preferred_element_type=F32)
            + jnp.dot(j_ref[0, :, t:], b_ref[...], preferred_element_type=F32))


def _fold_kernel(pd_ref, pa_ref, pb_ref, qd_ref, qa_ref, qb_ref, pz_ref, qz_ref, j_ref,
                 pe_ref, qo_ref, *, m_last):
    i = pl.program_id(0)
    first = i == 0
    p_dir = jnp.where(first, pz_ref[...], pd_ref[...]).astype(F32)
    q_dir = jnp.where(first, qz_ref[...], qd_ref[...]).astype(F32)
    m = i * DFT_TILE + lax.broadcasted_iota(jnp.int32, (DFT_TILE, 1), 0)
    keep = m <= m_last
    pe_ref[...] = jnp.where(keep, p_dir + _mirror(j_ref, pa_ref, pb_ref), 0.0).astype(BF16)
    qo_ref[...] = jnp.where(keep, q_dir - _mirror(j_ref, qa_ref, qb_ref), 0.0).astype(BF16)


def _dft_kernel(pe_ref, qo_ref, dc_ref, ds_ref, rowc_ref, rows_ref, u_ref, w_ref, *, scale):
    t = DFT_TILE
    acc_a = acc_b = None
    for c in range(pe_ref.shape[0] // t):
        sl = slice(c * t, (c + 1) * t)
        rc, rs = rowc_ref[0, 0:1, sl], rows_ref[0, 0:1, sl]
        dc, ds = dc_ref[:, sl], ds_ref[:, sl]
        tc = (dc * rc - ds * rs).astype(BF16)
        ts = (ds * rc + dc * rs).astype(BF16)
        a = jnp.dot(tc, pe_ref[sl, :], preferred_element_type=F32)
        b = jnp.dot(ts, qo_ref[sl, :], preferred_element_type=F32)
        acc_a = a if acc_a is None else acc_a + a
        acc_b = b if acc_b is None else acc_b + b
    u_ref[...] = ((acc_a - acc_b) * scale).astype(BF16)
    w_ref[...] = ((acc_a + acc_b) * scale).astype(BF16)


def _unfold_tile(tile, ud_ref, wa_ref, wb_ref, j_ref, r_last):
    r = tile * DFT_TILE + lax.broadcasted_iota(jnp.int32, (DFT_TILE, 1), 0)
    return jnp.where(r <= r_last, ud_ref[...], _mirror(j_ref, wa_ref, wb_ref).astype(BF16))


def _unfold_specs(c, col_of, tile_of):
    t, nf, n_real, i_mixed = DFT_TILE, c["nf"], c["n_real"], c["i_mixed"]
    clip = lambda v: jnp.clip(v, 0, nf - 1)
    blk = lambda fn: pl.BlockSpec((t, FNET_WIDTH), lambda i: (clip(fn(tile_of(i))), col_of(i)))
    variant = lambda tl: jnp.where(tl < i_mixed, 0, jnp.where(tl == i_mixed, 1, 2))
    return [blk(lambda tl: tl + 1), blk(lambda tl: n_real - tl), blk(lambda tl: n_real - 1 - tl),
            pl.BlockSpec((1, t, 2 * t), lambda i: (variant(tile_of(i)), 0, 0))]


@functools.lru_cache(maxsize=None)
def _dft_constants(seq, n_meta):
    t = DFT_TILE
    off = t - n_meta
    big_l = seq + n_meta
    half = big_l // 2
    n_real = seq // t
    nf = -(-(half + 1 + off) // t)
    hp = nf * t
    jf = np.zeros((nf, t, 2 * t), np.float32)
    for i in range(nf):
        a_idx, b_idx = min(n_real - i, n_real - 1), n_real - 1 - i
        for j in range(t):
            pos = i * t + j - off
            if 1 <= pos <= half - 1:
                tau, rho = divmod(big_l - pos - n_meta, t)
                assert tau in (a_idx, b_idx)
                jf[i, j, (t if tau == b_idx else 0) + rho] = 1.0
    r_last = half - n_meta
    i_mixed = r_last // t
    ju = np.zeros((3, t, 2 * t), np.float32)
    for i in range(i_mixed, n_real):
        a_idx, b_idx = n_real - i, n_real - 1 - i
        blk = np.zeros((t, 2 * t), np.float32)
        for j in range(t):
            r = i * t + j
            if r > r_last:
                tau, rho = divmod(big_l - (r + n_meta) + off, t)
                assert tau in (a_idx, b_idx)
                blk[j, (t if tau == b_idx else 0) + rho] = 1.0
        var = 1 if i == i_mixed else 2
        assert var == 1 or not ju[2].any() or (ju[2] == blk).all()
        ju[var] = blk
    ang = lambda prod: 2.0 * np.pi * (prod % big_l).astype(np.float64) / big_l
    pos = np.arange(hp, dtype=np.int64) - off
    d = ang(np.arange(t, dtype=np.int64)[:, None] * pos[None, :])
    k0 = t * np.arange(nf, dtype=np.int64) - off
    row = np.broadcast_to(ang(k0[:, None] * pos[None, :])[:, None, :], (nf, 8, hp))
    f = lambda a: np.ascontiguousarray(a, dtype=np.float32)
    return dict(jf=jf, ju=ju, dc=f(np.cos(d)), ds=f(np.sin(d)), rowc=f(np.cos(row)),
                rows=f(np.sin(row)), nf=nf, hp=hp, off=off, n_real=n_real,
                m_last=half + off, r_last=r_last, i_mixed=i_mixed)


def _seq_dft(p, q, pm, qm, seq, n_meta):
    c = _dft_constants(seq, n_meta)
    t, nf, hp, n_real, ncol = DFT_TILE, c["nf"], c["hp"], c["n_real"], p.shape[1]
    params = pltpu.CompilerParams(dimension_semantics=("arbitrary",),
                                  vmem_limit_bytes=V7X_VMEM_LIMIT)
    tile = lambda fn: pl.BlockSpec((t, ncol), lambda i: (fn(i), 0))
    lead = ((c["off"], 0), (0, 0))
    src = [tile(lambda i: jnp.maximum(i - 1, 0)),
           tile(lambda i: jnp.minimum(n_real - i, n_real - 1)),
           tile(lambda i: n_real - 1 - i)]
    pe, qo = pl.pallas_call(
        functools.partial(_fold_kernel, m_last=c["m_last"]),
        grid=(nf,),
        in_specs=src + src + [_resident((t, ncol)), _resident((t, ncol)),
                              pl.BlockSpec((1, t, 2 * t), lambda i: (i, 0, 0))],
        out_specs=(tile(lambda i: i), tile(lambda i: i)),
        out_shape=(jax.ShapeDtypeStruct((hp, ncol), BF16),) * 2,
        compiler_params=params,
        name="dft_fold",
    )(p, p, p, q, q, q, jnp.pad(pm, lead), jnp.pad(qm, lead), jnp.asarray(c["jf"]).astype(BF16))

    scale = float((seq + n_meta) * FNET_GROUP_DIM) ** -0.5
    rowspec = pl.BlockSpec((1, 8, hp), lambda i: (i, 0, 0))
    u, w = pl.pallas_call(
        functools.partial(_dft_kernel, scale=scale),
        grid=(nf,),
        in_specs=[_resident((hp, ncol)), _resident((hp, ncol)), _resident((t, hp)),
                  _resident((t, hp)), rowspec, rowspec],
        out_specs=(tile(lambda i: i), tile(lambda i: i)),
        out_shape=(jax.ShapeDtypeStruct((hp, ncol), BF16),) * 2,
        compiler_params=params,
        name="seq_dft",
    )(pe, qo, jnp.asarray(c["dc"]), jnp.asarray(c["ds"]), jnp.asarray(c["rowc"]),
      jnp.asarray(c["rows"]))

    return u, w, jnp.asarray(c["ju"]).astype(BF16)


FFN_CHUNK = 1024
TAIL_SPLIT = 2
TAIL_TM = 512


def _tail_kernel(x_ref, ol_ref, of_ref, ob_ref, *refs, tiles_per_seq, r_last):
    unfold_refs = [refs[4 * i:4 * i + 4] for i in range(TAIL_SPLIT)]
    (nmix_ref, gn_ref, nffn_ref, nfin_ref, wg_ref, wm_ref, wo_ref, wf_ref, wout_ref, w1_ref,
     w2_ref, o_ref) = refs[4 * TAIL_SPLIT:]
    tm = x_ref.shape[0]
    sub = tm // TAIL_SPLIT
    tiles = [(dict(idx=i), slice(i * sub, (i + 1) * sub)) for i in range(TAIL_SPLIT)]
    dot = functools.partial(jnp.dot, preferred_element_type=F32)
    nchunk = D_FF // FFN_CHUNK

    def norm_in(s, r):
        s["x"] = x_ref[r, :]
        s["xn"] = _rms(s["x"], nmix_ref[...]).astype(BF16)

    def gates(s, r):
        s["g"] = [dot(s["xn"], wg_ref[:, h * HEAD_DV:(h + 1) * HEAD_DV])
                  for h in range(GLA_HEADS)]
        s["ga"] = dot(s["xn"], wm_ref[:, 0:D_MODEL])
        s["gf"] = dot(s["xn"], wm_ref[:, D_MODEL:2 * D_MODEL])
        seq_tile = (pl.program_id(0) * TAIL_SPLIT + s["idx"]) % tiles_per_seq
        y = _unfold_tile(seq_tile, *unfold_refs[s["idx"]], r_last)
        s["y_fnet"] = dot(y, wf_ref[...])

    def gla_out(s, r):
        y_gla = None
        for h in range(GLA_HEADS):
            hv = slice(h * HEAD_DV, (h + 1) * HEAD_DV)
            o = (ol_ref[r, hv].astype(F32) + of_ref[r, hv].astype(F32)
                 + ob_ref[r, hv].astype(F32))
            g = s["g"][h]
            og = (_rms(o, gn_ref[...]) * (g * _sigmoid(g))).astype(BF16)
            part = dot(og, wo_ref[hv, :])
            y_gla = part if y_gla is None else y_gla + part
        s["y_gla"] = y_gla

    def merge(s, r):
        merged = _sigmoid(s["ga"]) * s["y_gla"] + _sigmoid(s["gf"]) * s["y_fnet"]
        s["h1"] = s["x"] + dot(merged.astype(BF16), wout_ref[...])
        s["u"] = _rms(s["h1"], nffn_ref[...]).astype(BF16)
        s["acc"] = s["h1"]

    def ffn_up(c):
        def stage(s, r):
            a = jnp.maximum(dot(s["u"], w1_ref[:, c * FFN_CHUNK:(c + 1) * FFN_CHUNK]), 0.0)
            s["a", c] = (a * a).astype(BF16)
        return stage

    def ffn_down(c):
        def stage(s, r):
            s["acc"] = s["acc"] + dot(s.pop(("a", c)),
                                      w2_ref[c * FFN_CHUNK:(c + 1) * FFN_CHUNK, :])
        return stage

    def norm_out(s, r):
        o_ref[r, :] = _rms(s["acc"], nfin_ref[...])

    stages = [norm_in, gates, gla_out, merge, ffn_up(0)]
    for c in range(nchunk):
        if c + 1 < nchunk:
            stages.append(ffn_up(c + 1))
        stages.append(ffn_down(c))
    stages.append(norm_out)
    _run_interleaved(stages, tiles)


def _tail_call(x2d, ol, o_f, o_b, dft_u, dft_w, dft_j, nmix, gn, nffn, nfin, w_in_bf, w_merge,
               wo, wf, wout, w1, w2, seq, n_meta, tm):
    m = x2d.shape[0]
    assert tm == TAIL_SPLIT * DFT_TILE
    row = lambda n: pl.BlockSpec((tm, n), lambda i: (i, 0))
    norms = [nmix, gn, nffn, nfin]
    consts = [w_merge, wo, wf, wout, w1, w2]
    dft = _dft_constants(seq, n_meta)
    tiles_per_seq = seq // DFT_TILE
    unfold_specs, unfold_args = [], []
    for sidx in range(TAIL_SPLIT):
        tile_of = lambda i, sidx=sidx: (i * TAIL_SPLIT + sidx) % tiles_per_seq
        col_of = lambda i, sidx=sidx: (i * TAIL_SPLIT + sidx) // tiles_per_seq
        unfold_specs += _unfold_specs(dft, col_of, tile_of)
        unfold_args += [dft_u, dft_w, dft_w, dft_j]
    return pl.pallas_call(
        functools.partial(_tail_kernel, tiles_per_seq=tiles_per_seq, r_last=dft["r_last"]),
        grid=(m // tm,),
        in_specs=[row(D_MODEL), row(GLA_DV), row(GLA_DV), row(GLA_DV)] + unfold_specs
                 + [_resident(c.shape) for c in norms]
                 + [_window((w_in_bf.shape[0], W_WINDOW), (0, 2))]
                 + [_resident(c.shape) for c in consts],
        out_specs=row(D_MODEL),
        out_shape=jax.ShapeDtypeStruct((m, D_MODEL), F32),
        compiler_params=pltpu.CompilerParams(
            dimension_semantics=("arbitrary",), vmem_limit_bytes=V7X_VMEM_LIMIT),
        name="tail",
    )(x2d, ol, o_f, o_b, *unfold_args, *norms, w_in_bf, *consts)


@functools.lru_cache(maxsize=None)
def _gla_constants(rows):
    idx = np.arange(rows)
    same = (idx[:, None] // SUB) == (idx[None, :] // SUB)
    tri_f = (same & (idx[None, :] <= idx[:, None])).astype(np.float32)
    return tri_f, np.ascontiguousarray(tri_f.T)


@functools.lru_cache(maxsize=None)
def _channel_dft():
    d = np.arange(FNET_GROUP_DIM, dtype=np.int64)
    ang = 2.0 * np.pi * ((d[:, None] * d[None, :]) % FNET_GROUP_DIM) / FNET_GROUP_DIM
    cs = np.zeros((2 * FNET_GROUP_DIM, 2 * FNET_GROUP_DIM), np.float32)
    cs[F_LEAD:F_LEAD + FNET_GROUP_DIM] = np.concatenate([np.cos(ang), np.sin(ang)], axis=1)
    return cs


def kernel(x, meta_tokens, norm_mix, w_in, w_decay_fwd, b_decay_fwd, w_decay_bwd, b_decay_bwd,
           gla_norm, w_o_gla, w_fnet, w_out, norm_ffn, w_ff1, w_ff2, norm_final):
    batch, seq, d = x.shape
    assert d == D_MODEL and meta_tokens.shape == (N_META, D_MODEL)
    assert w_in.shape[0] == 1 and seq % (2 * SCAN_RB) == 0 and seq % DFT_TILE == 0
    wi = w_in[0].astype(BF16)
    gate0 = 2 * GLA_DK + 2 * GLA_DV + 2 * DECAY_RANK + FNET_WIDTH
    assert wi.shape[1] == gate0 + 2 * D_MODEL
    w_merge = wi[:, gate0:]
    wdec = jnp.zeros((R_PAD, 2 * GLA_DK), F32)
    wdec = wdec.at[:DECAY_RANK, :GLA_DK].set(w_decay_fwd[0])
    wdec = wdec.at[DECAY_RANK:2 * DECAY_RANK, GLA_DK:].set(w_decay_bwd[0]).astype(BF16)
    bdec = jnp.concatenate([b_decay_fwd[0], b_decay_bwd[0]])[None, :]
    cs = jnp.asarray(_channel_dft())
    nw_mix = norm_mix[0][None, :]

    x2d = x.reshape(batch * seq, D_MODEL)
    v, ol, qe, ke, bend, p, pq_q = _mixin_call(x2d, nw_mix, wi, wdec, bdec, cs,
                                               *_gla_constants(GLA_C), seq, MIX_TM, GLA_C)
    meta_chunk = jnp.pad(meta_tokens, ((SUB - N_META, 0), (0, 0)))
    vm, _, _, kem, _, pm, qm = _mixin_call(meta_chunk, nw_mix, wi, wdec, bdec, cs,
                                           *_gla_constants(SUB), SUB, SUB, SUB)
    pm, qm = pm[SUB - N_META:], qm[SUB - N_META:]
    o_f, o_b = _gla_scan_call(qe, ke, v, bend, kem, vm, seq, batch)

    dft_u, dft_w, dft_j = _seq_dft(p, pq_q, jnp.tile(pm, (1, batch)), jnp.tile(qm, (1, batch)),
                                   seq, N_META)

    out = _tail_call(x2d, ol, o_f, o_b, dft_u, dft_w, dft_j, nw_mix, gla_norm[0][None, :],
                     norm_ffn[0][None, :], norm_final[None, :], wi, w_merge,
                     w_o_gla[0].astype(BF16), w_fnet[0].astype(BF16), w_out[0].astype(BF16),
                     w_ff1[0].astype(BF16), w_ff2[0].astype(BF16), seq, N_META, TAIL_TM)
    return out.reshape(batch, seq, D_MODEL)
```

```python
import functools

import numpy as np
import jax
import jax.numpy as jnp
from jax import lax
from jax.experimental import pallas as pl
from jax.experimental.pallas import tpu as pltpu

F32 = jnp.float32
BF16 = jnp.bfloat16

D_MODEL = 1024
N_META = 16
GLA_HEADS = 4
GLA_DK = 512
GLA_DV = 1024
HEAD_DK = GLA_DK // GLA_HEADS
HEAD_DV = GLA_DV // GLA_HEADS
DECAY_RANK = 16
GATE_TEMP = 16.0
SUB = 64
FNET_GROUPS = 4
FNET_WIDTH = 512
FNET_GROUP_DIM = FNET_WIDTH // FNET_GROUPS
D_FF = 4 * D_MODEL
EPS = 1e-6
LOG2E = 1.4426950408889634
LANE = 128

_OFF_Q, _OFF_K, _OFF_V, _OFF_QKV_END = 0, 512, 1024, 2048
W_WINDOW = 1024
R_PAD = 128
F_LEAD = 2 * DECAY_RANK
F_EXT = F_LEAD + FNET_WIDTH + (LANE - F_LEAD)

GLA_C = 4 * SUB
MIX_TM = 1024

V7X_VMEM_LIMIT = 58 * 1024 * 1024


def _rms(x, w):
    ms = jnp.mean(x * x, axis=-1, keepdims=True)
    return x * lax.rsqrt(ms + EPS) * w


def _sigmoid(x):
    return 1.0 / (1.0 + jnp.exp(-x))


def _run_interleaved(stages, tiles):
    for stage in stages:
        for state, rows in tiles:
            stage(state, rows)


def _window(block, index):
    return pl.BlockSpec(block, lambda *_: index, pipeline_mode=pl.Buffered(1))


def _resident(shape):
    return _window(shape, (0,) * len(shape))


_NT = (((1,), (1,)), ((), ()))
_TN = (((0,), (0,)), ((), ()))


def _mixin_kernel(x_ref, nw_ref, wqkv_ref, wrf_ref, wdec_ref, bdec_ref, cs_ref,
                  trif_ref, trib_ref, maskf_ref, maskb_ref,
                  v_ref, ol_ref, qe_ref, ke_ref, bend_ref, p_ref, qq_ref, *, chunk):
    tm = x_ref.shape[0]
    n, s = chunk // SUB, SUB
    tiles = [(dict(), slice(i * chunk, (i + 1) * chunk)) for i in range(tm // chunk)]
    dot = functools.partial(jnp.dot, preferred_element_type=F32)
    cs = cs_ref[...].astype(BF16)
    tris = (trif_ref[...], trib_ref[...])
    masks = (maskf_ref[0:s, 0:s] != 0.0, maskb_ref[0:s, 0:s] != 0.0)
    half = GLA_DK // 2

    def norm_in(st, r):
        st["xn"] = _rms(x_ref[r, :], nw_ref[...]).astype(BF16)

    def decay_rank(st, r):
        st["f"] = dot(st["xn"], wrf_ref[:, 0:F_EXT].astype(BF16)).astype(BF16)
        st["r"] = st["f"][:, 0:R_PAD]

    def log_decay(st, j):
        cols = slice(j * half, (j + 1) * half)
        x = dot(st["r"], wdec_ref[:, cols]) + bdec_ref[:, cols]
        la = (jnp.minimum(x, 0.0) - jnp.log(1.0 + jnp.exp(-jnp.abs(x)))) * (LOG2E / GATE_TEMP)
        hi = la.astype(BF16)
        st["la", j] = (hi, (la - hi.astype(F32)).astype(BF16))

    def project(st, r):
        mm = lambda lo, hi: dot(st["xn"], wqkv_ref[:, lo:hi].astype(BF16))
        vmid = (_OFF_V + _OFF_QKV_END) // 2
        log_decay(st, 0)
        st["q"] = mm(_OFF_Q, _OFF_K) * (HEAD_DK ** -0.5)
        log_decay(st, 1)
        st["k"] = mm(_OFF_K, _OFF_V)
        log_decay(st, 2)
        v_ref[r, 0:GLA_DV // 2] = mm(_OFF_V, vmid).astype(BF16)
        log_decay(st, 3)
        v_ref[r, GLA_DV // 2:GLA_DV] = mm(vmid, _OFF_QKV_END).astype(BF16)

    def decay_factors(st, r):
        c0 = r.start // s
        for j in range(4):
            d, qcols = j // 2, slice((j % 2) * half, (j % 2 + 1) * half)
            cols = slice(j * half, (j + 1) * half)
            hi, lo = st.pop(("la", j))
            b = (dot(tris[d], hi) + dot(tris[d], lo)).reshape(n, s, half)
            if d == 0:
                b_mid, b_end = b[:, s // 2 - 1:s // 2, :], b[:, s - 1:s, :]
            else:
                b_mid, b_end = b[:, s // 2:s // 2 + 1, :], b[:, 0:1, :]
            qd = st["q"][:, qcols].reshape(n, s, half) * jnp.exp2(b - b_mid)
            kd = st["k"][:, qcols].reshape(n, s, half) * jnp.exp2(b_mid - b)
            qe = qd * jnp.exp2(b_mid)
            ke = kd * jnp.exp2(b_end - b_mid)
            st["qd", d, j % 2] = qd.reshape(chunk, half).astype(BF16)
            st["kd", d, j % 2] = kd.reshape(chunk, half).astype(BF16)
            qe_ref[r, cols] = qe.reshape(chunk, half).astype(BF16)
            ke_ref[r, cols] = ke.reshape(chunk, half).astype(BF16)
            bend_ref[c0:c0 + n, cols] = b_end.reshape(n, half)

    def intra_chunk(st, r):
        for h in range(GLA_HEADS):
            hk = slice((h % 2) * HEAD_DK, (h % 2 + 1) * HEAD_DK)
            hv = slice(h * HEAD_DV, (h + 1) * HEAD_DV)
            sc = [[lax.dot_general(st["qd", d, h // 2][c * s:(c + 1) * s, hk],
                                   st["kd", d, h // 2][c * s:(c + 1) * s, hk], _NT,
                                   preferred_element_type=F32) for d in range(2)]
                  for c in range(n)]
            for c, (sf, sb) in enumerate(sc):
                rows = slice(r.start + c * s, r.start + (c + 1) * s)
                both = (jnp.where(masks[0], sf, 0.0) + jnp.where(masks[1], sb, 0.0)).astype(BF16)
                ol_ref[rows, hv] = dot(both, v_ref[rows, hv]).astype(BF16)

    def channel_dft(st, r):
        for grp in range(FNET_GROUPS):
            sl = slice(grp * FNET_GROUP_DIM, (grp + 1) * FNET_GROUP_DIM)
            pq = dot(st["f"][:, grp * FNET_GROUP_DIM:(grp + 2) * FNET_GROUP_DIM], cs)
            p_ref[r, sl] = pq[:, :FNET_GROUP_DIM].astype(BF16)
            qq_ref[r, sl] = pq[:, FNET_GROUP_DIM:].astype(BF16)

    _run_interleaved((norm_in, decay_rank, project, decay_factors, intra_chunk, channel_dft),
                     tiles)


def _mixin_call(x2d, nw, w_in_bf, wdec, bdec, cs, tri_f, tri_b, rows_per_batch, tm, chunk):
    m = x2d.shape[0]
    d_model = w_in_bf.shape[0]
    nt = rows_per_batch // tm
    nb = m // rows_per_batch
    row = lambda n: pl.BlockSpec((tm, n), lambda i: (i, 0))
    pq_spec = pl.BlockSpec((tm, FNET_WIDTH), lambda i: (i % nt, i // nt))
    wide = jax.ShapeDtypeStruct((m, 2 * GLA_DK), BF16)
    pq_shape = jax.ShapeDtypeStruct((rows_per_batch, nb * FNET_WIDTH), BF16)
    consts = [wdec, bdec, cs,
              jnp.asarray(tri_f).astype(BF16), jnp.asarray(tri_b).astype(BF16),
              jnp.asarray(tri_f), jnp.asarray(tri_b)]
    return pl.pallas_call(
        functools.partial(_mixin_kernel, chunk=chunk),
        grid=(m // tm,),
        in_specs=[row(D_MODEL), _resident(nw.shape),
                  _window((d_model, _OFF_QKV_END), (0, 0)),
                  _window((d_model, W_WINDOW), (0, 3))]
                 + [_resident(c.shape) for c in consts],
        out_specs=(row(GLA_DV), row(GLA_DV), row(2 * GLA_DK), row(2 * GLA_DK),
                   pl.BlockSpec((tm // SUB, 2 * GLA_DK), lambda i: (i, 0)), pq_spec, pq_spec),
        out_shape=(jax.ShapeDtypeStruct((m, GLA_DV), BF16), jax.ShapeDtypeStruct((m, GLA_DV), BF16),
                   wide, wide, jax.ShapeDtypeStruct((m // SUB, 2 * GLA_DK), F32),
                   pq_shape, pq_shape),
        compiler_params=pltpu.CompilerParams(
            dimension_semantics=("arbitrary",), vmem_limit_bytes=V7X_VMEM_LIMIT),
        name="mixin",
    )(x2d, nw, w_in_bf, w_in_bf, *consts)


SCAN_G = 512
SCAN_RB = 4096


def _scan_group(dirs):
    s, n = SUB, SCAN_G // SUB
    ahead = 2
    dec, st, inter = [], [], [{}, {}]
    for qe_ref, ke_ref, v_ref, bend_ref, st_ref, row0, chunk0, backward in dirs:
        dec_rows = jnp.tile(jnp.exp2(bend_ref[chunk0:chunk0 + n, :]), (HEAD_DK // n, 1))
        dec.append(dec_rows.T)
        st.append(st_ref[...])

    def order(step, backward):
        return n - 1 - step if backward else step

    def increment(d, step):
        _, ke_ref, v_ref, _, _, row0, _, backward = dirs[d]
        i = order(step, backward)
        sub = slice(row0 + i * s, row0 + (i + 1) * s)
        return lax.dot_general(ke_ref[sub, :], v_ref[sub, :], _TN,
                               preferred_element_type=F32)

    inc = {(d, step): increment(d, step) for step in range(ahead) for d in range(2)}
    for step in range(n):
        for d, (qe_ref, _, _, _, _, row0, _, backward) in enumerate(dirs):
            i = order(step, backward)
            sub = slice(row0 + i * s, row0 + (i + 1) * s)
            inter[d][i] = jnp.dot(qe_ref[sub, :], st[d].astype(BF16),
                                  preferred_element_type=F32)
            if step + ahead < n:
                inc[d, step + ahead] = increment(d, step + ahead)
            decay = jnp.broadcast_to(dec[d][:, i:i + 1], (HEAD_DK, HEAD_DV))
            st[d] = st[d] * decay + inc.pop((d, step))
    for d, spec in enumerate(dirs):
        spec[4][...] = st[d]
    return [jnp.concatenate([inter[d][i] for i in range(n)], axis=0) for d in range(2)]


def _gla_scan_kernel(qef_ref, kef_ref, vf_ref, bendf_ref,
                     qeb_ref, keb_ref, vb_ref, bendb_ref, kem_ref, vm_ref,
                     of_ref, ob_ref, stf_ref, stb_ref):
    rb = qef_ref.shape[0]
    ng = rb // SCAN_G

    @pl.when(pl.program_id(2) == 0)
    def _():
        stf_ref[...] = lax.dot_general(kem_ref[...], vm_ref[...], _TN,
                                       preferred_element_type=F32)
        stb_ref[...] = jnp.zeros_like(stb_ref)

    for j in range(ng):
        jb = ng - 1 - j
        cpg = SCAN_G // SUB
        in_f, in_b = _scan_group([
            (qef_ref, kef_ref, vf_ref, bendf_ref, stf_ref, j * SCAN_G, j * cpg, False),
            (qeb_ref, keb_ref, vb_ref, bendb_ref, stb_ref, jb * SCAN_G, jb * cpg, True)])
        of_ref[j * SCAN_G:(j + 1) * SCAN_G, :] = in_f.astype(BF16)
        ob_ref[jb * SCAN_G:(jb + 1) * SCAN_G, :] = in_b.astype(BF16)


def _gla_scan_call(qe, ke, v, bend, kem, vm, seq, batch):
    h, rb = GLA_HEADS, SCAN_RB
    nt = seq // rb
    fwd = lambda b, hh, t: b * nt + t
    bwd = lambda b, hh, t: b * nt + nt - 1 - t
    cb = rb // SUB

    def dk_blk(rowfn, dircol, rows):
        return pl.BlockSpec((rows, HEAD_DK), lambda b, hh, t: (rowfn(b, hh, t), dircol + hh))

    def dv_blk(rowfn):
        return pl.BlockSpec((rb, HEAD_DV), lambda b, hh, t: (rowfn(b, hh, t), hh))

    in_specs = [
        dk_blk(fwd, 0, rb), dk_blk(fwd, 0, rb), dv_blk(fwd), dk_blk(fwd, 0, cb),
        dk_blk(bwd, h, rb), dk_blk(bwd, h, rb), dv_blk(bwd), dk_blk(bwd, h, cb),
        pl.BlockSpec((SUB, HEAD_DK), lambda b, hh, t: (0, hh)),
        pl.BlockSpec((SUB, HEAD_DV), lambda b, hh, t: (0, hh)),
    ]
    out = jax.ShapeDtypeStruct((batch * seq, GLA_DV), BF16)
    return pl.pallas_call(
        _gla_scan_kernel,
        grid=(batch, h, nt),
        in_specs=in_specs,
        out_specs=(dv_blk(fwd), dv_blk(bwd)),
        out_shape=(out, out),
        scratch_shapes=[pltpu.VMEM((HEAD_DK, HEAD_DV), F32), pltpu.VMEM((HEAD_DK, HEAD_DV), F32)],
        compiler_params=pltpu.CompilerParams(
            dimension_semantics=("arbitrary", "arbitrary", "arbitrary"),
            vmem_limit_bytes=V7X_VMEM_LIMIT),
        name="gla_scan",
    )(qe, ke, v, bend, qe, ke, v, bend, kem, vm)


DFT_TILE = 256


def _mirror(j_ref, a_ref, b_ref):
    t = DFT_TILE
    return (jnp.dot(j_ref[0, :, :t], a_ref[...], preferred_element_type=F32)
            + jnp.dot(j_ref[0, :, t:], b_ref[...], preferred_element_type=F32))


def _fold_kernel(pd_ref, pa_ref, pb_ref, qd_ref, qa_ref, qb_ref, pz_ref, qz_ref, j_ref,
                 pe_ref, qo_ref, *, m_last):
    i = pl.program_id(0)
    first = i == 0
    p_dir = jnp.where(first, pz_ref[...], pd_ref[...]).astype(F32)
    q_dir = jnp.where(first, qz_ref[...], qd_ref[...]).astype(F32)
    m = i * DFT_TILE + lax.broadcasted_iota(jnp.int32, (DFT_TILE, 1), 0)
    keep = m <= m_last
    pe_ref[...] = jnp.where(keep, p_dir + _mirror(j_ref, pa_ref, pb_ref), 0.0).astype(BF16)
    qo_ref[...] = jnp.where(keep, q_dir - _mirror(j_ref, qa_ref, qb_ref), 0.0).astype(BF16)


def _dft_kernel(pe_ref, qo_ref, dc_ref, ds_ref, rowc_ref, rows_ref, u_ref, w_ref, *, scale):
    t = DFT_TILE
    acc_a = acc_b = None
    for c in range(pe_ref.shape[0] // t):
        sl = slice(c * t, (c + 1) * t)
        rc, rs = rowc_ref[0, 0:1, sl], rows_ref[0, 0:1, sl]
        dc, ds = dc_ref[:, sl], ds_ref[:, sl]
        tc = (dc * rc - ds * rs).astype(BF16)
        ts = (ds * rc + dc * rs).astype(BF16)
        a = jnp.dot(tc, pe_ref[sl, :], preferred_element_type=F32)
        b = jnp.dot(ts, qo_ref[sl, :], preferred_element_type=F32)
        acc_a = a if acc_a is None else acc_a + a
        acc_b = b if acc_b is None else acc_b + b
    u_ref[...] = ((acc_a - acc_b) * scale).astype(BF16)
    w_ref[...] = ((acc_a + acc_b) * scale).astype(BF16)


def _unfold_tile(tile, ud_ref, wa_ref, wb_ref, j_ref, r_last):
    r = tile * DFT_TILE + lax.broadcasted_iota(jnp.int32, (DFT_TILE, 1), 0)
    return jnp.where(r <= r_last, ud_ref[...], _mirror(j_ref, wa_ref, wb_ref).astype(BF16))


def _unfold_specs(c, col_of, tile_of):
    t, nf, n_real, i_mixed = DFT_TILE, c["nf"], c["n_real"], c["i_mixed"]
    clip = lambda v: jnp.clip(v, 0, nf - 1)
    blk = lambda fn: pl.BlockSpec((t, FNET_WIDTH), lambda i: (clip(fn(tile_of(i))), col_of(i)))
    variant = lambda tl: jnp.where(tl < i_mixed, 0, jnp.where(tl == i_mixed, 1, 2))
    return [blk(lambda tl: tl + 1), blk(lambda tl: n_real - tl), blk(lambda tl: n_real - 1 - tl),
            pl.BlockSpec((1, t, 2 * t), lambda i: (variant(tile_of(i)), 0, 0))]


@functools.lru_cache(maxsize=None)
def _dft_constants(seq, n_meta):
    t = DFT_TILE
    off = t - n_meta
    big_l = seq + n_meta
    half = big_l // 2
    n_real = seq // t
    nf = -(-(half + 1 + off) // t)
    hp = nf * t
    jf = np.zeros((nf, t, 2 * t), np.float32)
    for i in range(nf):
        a_idx, b_idx = min(n_real - i, n_real - 1), n_real - 1 - i
        for j in range(t):
            pos = i * t + j - off
            if 1 <= pos <= half - 1:
                tau, rho = divmod(big_l - pos - n_meta, t)
                assert tau in (a_idx, b_idx)
                jf[i, j, (t if tau == b_idx else 0) + rho] = 1.0
    r_last = half - n_meta
    i_mixed = r_last // t
    ju = np.zeros((3, t, 2 * t), np.float32)
    for i in range(i_mixed, n_real):
        a_idx, b_idx = n_real - i, n_real - 1 - i
        blk = np.zeros((t, 2 * t), np.float32)
        for j in range(t):
            r = i * t + j
            if r > r_last:
                tau, rho = divmod(big_l - (r + n_meta) + off, t)
                assert tau in (a_idx, b_idx)
                blk[j, (t if tau == b_idx else 0) + rho] = 1.0
        var = 1 if i == i_mixed else 2
        assert var == 1 or not ju[2].any() or (ju[2] == blk).all()
        ju[var] = blk
    ang = lambda prod: 2.0 * np.pi * (prod % big_l).astype(np.float64) / big_l
    pos = np.arange(hp, dtype=np.int64) - off
    d = ang(np.arange(t, dtype=np.int64)[:, None] * pos[None, :])
    k0 = t * np.arange(nf, dtype=np.int64) - off
    row = np.broadcast_to(ang(k0[:, None] * pos[None, :])[:, None, :], (nf, 8, hp))
    f = lambda a: np.ascontiguousarray(a, dtype=np.float32)
    return dict(jf=jf, ju=ju, dc=f(np.cos(d)), ds=f(np.sin(d)), rowc=f(np.cos(row)),
                rows=f(np.sin(row)), nf=nf, hp=hp, off=off, n_real=n_real,
                m_last=half + off, r_last=r_last, i_mixed=i_mixed)


def _seq_dft(p, q, pm, qm, seq, n_meta):
    c = _dft_constants(seq, n_meta)
    t, nf, hp, n_real, ncol = DFT_TILE, c["nf"], c["hp"], c["n_real"], p.shape[1]
    params = pltpu.CompilerParams(dimension_semantics=("arbitrary",),
                                  vmem_limit_bytes=V7X_VMEM_LIMIT)
    tile = lambda fn: pl.BlockSpec((t, ncol), lambda i: (fn(i), 0))
    lead = ((c["off"], 0), (0, 0))
    src = [tile(lambda i: jnp.maximum(i - 1, 0)),
           tile(lambda i: jnp.minimum(n_real - i, n_real - 1)),
           tile(lambda i: n_real - 1 - i)]
    pe, qo = pl.pallas_call(
        functools.partial(_fold_kernel, m_last=c["m_last"]),
        grid=(nf,),
        in_specs=src + src + [_resident((t, ncol)), _resident((t, ncol)),
                              pl.BlockSpec((1, t, 2 * t), lambda i: (i, 0, 0))],
        out_specs=(tile(lambda i: i), tile(lambda i: i)),
        out_shape=(jax.ShapeDtypeStruct((hp, ncol), BF16),) * 2,
        compiler_params=params,
        name="dft_fold",
    )(p, p, p, q, q, q, jnp.pad(pm, lead), jnp.pad(qm, lead), jnp.asarray(c["jf"]).astype(BF16))

    scale = float((seq + n_meta) * FNET_GROUP_DIM) ** -0.5
    rowspec = pl.BlockSpec((1, 8, hp), lambda i: (i, 0, 0))
    u, w = pl.pallas_call(
        functools.partial(_dft_kernel, scale=scale),
        grid=(nf,),
        in_specs=[_resident((hp, ncol)), _resident((hp, ncol)), _resident((t, hp)),
                  _resident((t, hp)), rowspec, rowspec],
        out_specs=(tile(lambda i: i), tile(lambda i: i)),
        out_shape=(jax.ShapeDtypeStruct((hp, ncol), BF16),) * 2,
        compiler_params=params,
        name="seq_dft",
    )(pe, qo, jnp.asarray(c["dc"]), jnp.asarray(c["ds"]), jnp.asarray(c["rowc"]),
      jnp.asarray(c["rows"]))

    return u, w, jnp.asarray(c["ju"]).astype(BF16)


FFN_CHUNK = 1024
TAIL_SPLIT = 2
TAIL_TM = 512


def _tail_kernel(x_ref, ol_ref, of_ref, ob_ref, *refs, tiles_per_seq, r_last):
    unfold_refs = [refs[4 * i:4 * i + 4] for i in range(TAIL_SPLIT)]
    (nmix_ref, gn_ref, nffn_ref, nfin_ref, wg_ref, wm_ref, wo_ref, wf_ref, wout_ref, w1_ref,
     w2_ref, o_ref) = refs[4 * TAIL_SPLIT:]
    tm = x_ref.shape[0]
    sub = tm // TAIL_SPLIT
    tiles = [(dict(idx=i), slice(i * sub, (i + 1) * sub)) for i in range(TAIL_SPLIT)]
    dot = functools.partial(jnp.dot, preferred_element_type=F32)
    nchunk = D_FF // FFN_CHUNK

    def norm_in(s, r):
        s["x"] = x_ref[r, :]
        s["xn"] = _rms(s["x"], nmix_ref[...]).astype(BF16)

    def gates(s, r):
        s["g"] = [dot(s["xn"], wg_ref[:, h * HEAD_DV:(h + 1) * HEAD_DV].astype(BF16))
                  for h in range(GLA_HEADS)]
        s["ga"] = dot(s["xn"], wm_ref[:, 0:D_MODEL])
        s["gf"] = dot(s["xn"], wm_ref[:, D_MODEL:2 * D_MODEL])
        seq_tile = (pl.program_id(0) * TAIL_SPLIT + s["idx"]) % tiles_per_seq
        y = _unfold_tile(seq_tile, *unfold_refs[s["idx"]], r_last)
        s["y_fnet"] = dot(y, wf_ref[...])

    def gla_out(s, r):
        y_gla = None
        for h in range(GLA_HEADS):
            hv = slice(h * HEAD_DV, (h + 1) * HEAD_DV)
            o = (ol_ref[r, hv].astype(F32) + of_ref[r, hv].astype(F32)
                 + ob_ref[r, hv].astype(F32))
            g = s["g"][h]
            og = (_rms(o, gn_ref[...]) * (g * _sigmoid(g))).astype(BF16)
            part = dot(og, wo_ref[hv, :])
            y_gla = part if y_gla is None else y_gla + part
        s["y_gla"] = y_gla

    def merge(s, r):
        merged = _sigmoid(s["ga"]) * s["y_gla"] + _sigmoid(s["gf"]) * s["y_fnet"]
        s["h1"] = s["x"] + dot(merged.astype(BF16), wout_ref[...])
        s["u"] = _rms(s["h1"], nffn_ref[...]).astype(BF16)
        s["acc"] = s["h1"]

    def ffn_up(c):
        def stage(s, r):
            a = jnp.maximum(dot(s["u"], w1_ref[:, c * FFN_CHUNK:(c + 1) * FFN_CHUNK]), 0.0)
            s["a", c] = (a * a).astype(BF16)
        return stage

    def ffn_down(c):
        def stage(s, r):
            s["acc"] = s["acc"] + dot(s.pop(("a", c)),
                                      w2_ref[c * FFN_CHUNK:(c + 1) * FFN_CHUNK, :])
        return stage

    def norm_out(s, r):
        o_ref[r, :] = _rms(s["acc"], nfin_ref[...])

    stages = [norm_in, gates, gla_out, merge, ffn_up(0)]
    for c in range(nchunk):
        if c + 1 < nchunk:
            stages.append(ffn_up(c + 1))
        stages.append(ffn_down(c))
    stages.append(norm_out)
    _run_interleaved(stages, tiles)


def _tail_call(x2d, ol, o_f, o_b, dft_u, dft_w, dft_j, nmix, gn, nffn, nfin, w_in_bf, w_merge,
               wo, wf, wout, w1, w2, seq, n_meta, tm):
    m = x2d.shape[0]
    assert tm == TAIL_SPLIT * DFT_TILE
    row = lambda n: pl.BlockSpec((tm, n), lambda i: (i, 0))
    norms = [nmix, gn, nffn, nfin]
    consts = [w_merge, wo, wf, wout, w1, w2]
    dft = _dft_constants(seq, n_meta)
    tiles_per_seq = seq // DFT_TILE
    unfold_specs, unfold_args = [], []
    for sidx in range(TAIL_SPLIT):
        tile_of = lambda i, sidx=sidx: (i * TAIL_SPLIT + sidx) % tiles_per_seq
        col_of = lambda i, sidx=sidx: (i * TAIL_SPLIT + sidx) // tiles_per_seq
        unfold_specs += _unfold_specs(dft, col_of, tile_of)
        unfold_args += [dft_u, dft_w, dft_w, dft_j]
    return pl.pallas_call(
        functools.partial(_tail_kernel, tiles_per_seq=tiles_per_seq, r_last=dft["r_last"]),
        grid=(m // tm,),
        in_specs=[row(D_MODEL), row(GLA_DV), row(GLA_DV), row(GLA_DV)] + unfold_specs
                 + [_resident(c.shape) for c in norms]
                 + [_window((w_in_bf.shape[0], W_WINDOW), (0, 2))]
                 + [_resident(c.shape) for c in consts],
        out_specs=row(D_MODEL),
        out_shape=jax.ShapeDtypeStruct((m, D_MODEL), F32),
        compiler_params=pltpu.CompilerParams(
            dimension_semantics=("arbitrary",), vmem_limit_bytes=V7X_VMEM_LIMIT),
        name="tail",
    )(x2d, ol, o_f, o_b, *unfold_args, *norms, w_in_bf, *consts)


@functools.lru_cache(maxsize=None)
def _gla_constants(rows):
    idx = np.arange(rows)
    same = (idx[:, None] // SUB) == (idx[None, :] // SUB)
    tri_f = (same & (idx[None, :] <= idx[:, None])).astype(np.float32)
    return tri_f, np.ascontiguousarray(tri_f.T)


@functools.lru_cache(maxsize=None)
def _channel_dft():
    d = np.arange(FNET_GROUP_DIM, dtype=np.int64)
    ang = 2.0 * np.pi * ((d[:, None] * d[None, :]) % FNET_GROUP_DIM) / FNET_GROUP_DIM
    cs = np.zeros((2 * FNET_GROUP_DIM, 2 * FNET_GROUP_DIM), np.float32)
    cs[F_LEAD:F_LEAD + FNET_GROUP_DIM] = np.concatenate([np.cos(ang), np.sin(ang)], axis=1)
    return cs


def kernel(x, meta_tokens, norm_mix, w_in, w_decay_fwd, b_decay_fwd, w_decay_bwd, b_decay_bwd,
           gla_norm, w_o_gla, w_fnet, w_out, norm_ffn, w_ff1, w_ff2, norm_final):
    batch, seq, d = x.shape
    assert d == D_MODEL and meta_tokens.shape == (N_META, D_MODEL)
    assert w_in.shape[0] == 1 and seq % (2 * SCAN_RB) == 0 and seq % DFT_TILE == 0
    wi = w_in[0]
    gate0 = 2 * GLA_DK + 2 * GLA_DV + 2 * DECAY_RANK + FNET_WIDTH
    assert wi.shape[1] == gate0 + 2 * D_MODEL
    w_merge = wi[:, gate0:].astype(BF16)
    wdec = jnp.zeros((R_PAD, 2 * GLA_DK), F32)
    wdec = wdec.at[:DECAY_RANK, :GLA_DK].set(w_decay_fwd[0])
    wdec = wdec.at[DECAY_RANK:2 * DECAY_RANK, GLA_DK:].set(w_decay_bwd[0]).astype(BF16)
    bdec = jnp.concatenate([b_decay_fwd[0], b_decay_bwd[0]])[None, :]
    cs = jnp.asarray(_channel_dft())
    nw_mix = norm_mix[0][None, :]

    x2d = x.reshape(batch * seq, D_MODEL)
    v, ol, qe, ke, bend, p, pq_q = _mixin_call(x2d, nw_mix, wi, wdec, bdec, cs,
                                               *_gla_constants(GLA_C), seq, MIX_TM, GLA_C)
    meta_chunk = jnp.pad(meta_tokens, ((SUB - N_META, 0), (0, 0)))
    vm, _, _, kem, _, pm, qm = _mixin_call(meta_chunk, nw_mix, wi, wdec, bdec, cs,
                                           *_gla_constants(SUB), SUB, SUB, SUB)
    pm, qm = pm[SUB - N_META:], qm[SUB - N_META:]
    o_f, o_b = _gla_scan_call(qe, ke, v, bend, kem, vm, seq, batch)

    dft_u, dft_w, dft_j = _seq_dft(p, pq_q, jnp.tile(pm, (1, batch)), jnp.tile(qm, (1, batch)),
                                   seq, N_META)

    out = _tail_call(x2d, ol, o_f, o_b, dft_u, dft_w, dft_j, nw_mix, gla_norm[0][None, :],
                     norm_ffn[0][None, :], norm_final[None, :], wi, w_merge,
                     w_o_gla[0].astype(BF16), w_fnet[0].astype(BF16), w_out[0].astype(BF16),
                     w_ff1[0].astype(BF16), w_ff2[0].astype(BF16), seq, N_META, TAIL_TM)
    return out.reshape(batch, seq, D_MODEL)
```

```python
import functools

import numpy as np
import jax
import jax.numpy as jnp
from jax import lax
from jax.experimental import pallas as pl
from jax.experimental.pallas import tpu as pltpu

F32 = jnp.float32
BF16 = jnp.bfloat16

D_MODEL = 1024
N_META = 16
GLA_HEADS = 4
GLA_DK = 512
GLA_DV = 1024
HEAD_DK = GLA_DK // GLA_HEADS
HEAD_DV = GLA_DV // GLA_HEADS
DECAY_RANK = 16
GATE_TEMP = 16.0
SUB = 64
FNET_GROUPS = 4
FNET_WIDTH = 512
FNET_GROUP_DIM = FNET_WIDTH // FNET_GROUPS
D_FF = 4 * D_MODEL
EPS = 1e-6
LOG2E = 1.4426950408889634
LANE = 128

_OFF_Q, _OFF_K, _OFF_V, _OFF_QKV_END = 0, 512, 1024, 2048
W_WINDOW = 1024
R_PAD = 128
F_LEAD = 2 * DECAY_RANK
F_EXT = F_LEAD + FNET_WIDTH + (LANE - F_LEAD)

GLA_C = 4 * SUB
MIX_TM = 1024

V7X_VMEM_LIMIT = 56 * 1024 * 1024


def _rms(x, w):
    ms = jnp.mean(x * x, axis=-1, keepdims=True)
    return x * lax.rsqrt(ms + EPS) * w


def _sigmoid(x):
    return 1.0 / (1.0 + jnp.exp(-x))


def _run_interleaved(stages, tiles):
    for stage in stages:
        for state, rows in tiles:
            stage(state, rows)


def _window(block, index):
    return pl.BlockSpec(block, lambda *_: index, pipeline_mode=pl.Buffered(1))


def _resident(shape):
    return _window(shape, (0,) * len(shape))


_NT = (((1,), (1,)), ((), ()))
_TN = (((0,), (0,)), ((), ()))


def _mixin_kernel(x_ref, nw_ref, wqkv_ref, wrf_ref, wdec_ref, bdec_ref, cs_ref,
                  trif_ref, trib_ref, maskf_ref, maskb_ref,
                  v_ref, ol_ref, qe_ref, ke_ref, bend_ref, pq_ref, *, chunk):
    tm = x_ref.shape[0]
    n, s = chunk // SUB, SUB
    tiles = [(dict(), slice(i * chunk, (i + 1) * chunk)) for i in range(tm // chunk)]
    dot = functools.partial(jnp.dot, preferred_element_type=F32)
    cs = cs_ref[...].astype(BF16)
    tris = (trif_ref[...], trib_ref[...])
    masks = (maskf_ref[0:s, 0:s] != 0.0, maskb_ref[0:s, 0:s] != 0.0)
    half = GLA_DK // 2

    def norm_in(st, r):
        st["xn"] = _rms(x_ref[r, :], nw_ref[...]).astype(BF16)

    def decay_rank(st, r):
        st["f"] = dot(st["xn"], wrf_ref[:, 0:F_EXT]).astype(BF16)
        st["r"] = st["f"][:, 0:R_PAD]

    def log_decay(st, j):
        cols = slice(j * half, (j + 1) * half)
        x = dot(st["r"], wdec_ref[:, cols]) + bdec_ref[:, cols]
        la = (jnp.minimum(x, 0.0) - jnp.log(1.0 + jnp.exp(-jnp.abs(x)))) * (LOG2E / GATE_TEMP)
        hi = la.astype(BF16)
        st["la", j] = (hi, (la - hi.astype(F32)).astype(BF16))

    def project(st, r):
        mm = lambda lo, hi: dot(st["xn"], wqkv_ref[:, lo:hi])
        vmid = (_OFF_V + _OFF_QKV_END) // 2
        log_decay(st, 0)
        st["q"] = mm(_OFF_Q, _OFF_K) * (HEAD_DK ** -0.5)
        log_decay(st, 1)
        st["k"] = mm(_OFF_K, _OFF_V)
        log_decay(st, 2)
        v_ref[r, 0:GLA_DV // 2] = mm(_OFF_V, vmid).astype(BF16)
        log_decay(st, 3)
        v_ref[r, GLA_DV // 2:GLA_DV] = mm(vmid, _OFF_QKV_END).astype(BF16)

    def decay_factors(st, r):
        c0 = r.start // s
        for j in range(4):
            d, qcols = j // 2, slice((j % 2) * half, (j % 2 + 1) * half)
            cols = slice(j * half, (j + 1) * half)
            hi, lo = st.pop(("la", j))
            b = (dot(tris[d], hi) + dot(tris[d], lo)).reshape(n, s, half)
            if d == 0:
                b_mid, b_end = b[:, s // 2 - 1:s // 2, :], b[:, s - 1:s, :]
            else:
                b_mid, b_end = b[:, s // 2:s // 2 + 1, :], b[:, 0:1, :]
            qd = st["q"][:, qcols].reshape(n, s, half) * jnp.exp2(b - b_mid)
            kd = st["k"][:, qcols].reshape(n, s, half) * jnp.exp2(b_mid - b)
            qe = qd * jnp.exp2(b_mid)
            ke = kd * jnp.exp2(b_end - b_mid)
            st["qd", d, j % 2] = qd.reshape(chunk, half).astype(BF16)
            st["kd", d, j % 2] = kd.reshape(chunk, half).astype(BF16)
            qe_ref[r, cols] = qe.reshape(chunk, half).astype(BF16)
            ke_ref[r, cols] = ke.reshape(chunk, half).astype(BF16)
            bend_ref[c0:c0 + n, cols] = b_end.reshape(n, half)

    def intra_chunk(st, r):
        for h in range(GLA_HEADS):
            hk = slice((h % 2) * HEAD_DK, (h % 2 + 1) * HEAD_DK)
            hv = slice(h * HEAD_DV, (h + 1) * HEAD_DV)
            sc = [[lax.dot_general(st["qd", d, h // 2][c * s:(c + 1) * s, hk],
                                   st["kd", d, h // 2][c * s:(c + 1) * s, hk], _NT,
                                   preferred_element_type=F32) for d in range(2)]
                  for c in range(n)]
            for c, (sf, sb) in enumerate(sc):
                rows = slice(r.start + c * s, r.start + (c + 1) * s)
                both = (jnp.where(masks[0], sf, 0.0) + jnp.where(masks[1], sb, 0.0)).astype(BF16)
                ol_ref[rows, hv] = dot(both, v_ref[rows, hv]).astype(BF16)

    def channel_dft(st, r):
        for grp in range(FNET_GROUPS):
            sl = slice(grp * FNET_GROUP_DIM, (grp + 1) * FNET_GROUP_DIM)
            pq = dot(st["f"][:, grp * FNET_GROUP_DIM:(grp + 2) * FNET_GROUP_DIM], cs)
            pq_ref[0, r, sl] = pq[:, :FNET_GROUP_DIM].astype(BF16)
            pq_ref[1, r, sl] = pq[:, FNET_GROUP_DIM:].astype(BF16)

    _run_interleaved((norm_in, decay_rank, project, decay_factors, intra_chunk, channel_dft),
                     tiles)


def _mixin_call(x2d, nw, w_in_bf, wdec, bdec, cs, tri_f, tri_b, rows_per_batch, tm, chunk):
    m = x2d.shape[0]
    d_model = w_in_bf.shape[0]
    nt = rows_per_batch // tm
    nb = m // rows_per_batch
    row = lambda n: pl.BlockSpec((tm, n), lambda i: (i, 0))
    pq_spec = pl.BlockSpec((2, tm, FNET_WIDTH), lambda i: (0, i % nt, i // nt))
    wide = jax.ShapeDtypeStruct((m, 2 * GLA_DK), BF16)
    pq_shape = jax.ShapeDtypeStruct((2, rows_per_batch, nb * FNET_WIDTH), BF16)
    consts = [wdec, bdec, cs,
              jnp.asarray(tri_f).astype(BF16), jnp.asarray(tri_b).astype(BF16),
              jnp.asarray(tri_f), jnp.asarray(tri_b)]
    return pl.pallas_call(
        functools.partial(_mixin_kernel, chunk=chunk),
        grid=(m // tm,),
        in_specs=[row(D_MODEL), _resident(nw.shape),
                  _window((d_model, _OFF_QKV_END), (0, 0)),
                  _window((d_model, W_WINDOW), (0, 3))]
                 + [_resident(c.shape) for c in consts],
        out_specs=(row(GLA_DV), row(GLA_DV), row(2 * GLA_DK), row(2 * GLA_DK),
                   pl.BlockSpec((tm // SUB, 2 * GLA_DK), lambda i: (i, 0)), pq_spec),
        out_shape=(jax.ShapeDtypeStruct((m, GLA_DV), BF16), jax.ShapeDtypeStruct((m, GLA_DV), BF16),
                   wide, wide, jax.ShapeDtypeStruct((m // SUB, 2 * GLA_DK), F32), pq_shape),
        compiler_params=pltpu.CompilerParams(
            dimension_semantics=("arbitrary",), vmem_limit_bytes=V7X_VMEM_LIMIT),
        name="mixin",
    )(x2d, nw, w_in_bf, w_in_bf, *consts)


SCAN_G = 512
SCAN_RB = 4096


def _scan_group(dirs):
    s, n = SUB, SCAN_G // SUB
    ahead = 2
    dec, st, inter = [], [], [{}, {}]
    for qe_ref, ke_ref, v_ref, bend_ref, st_ref, row0, chunk0, backward in dirs:
        dec_rows = jnp.tile(jnp.exp2(bend_ref[chunk0:chunk0 + n, :]), (HEAD_DK // n, 1))
        dec.append(dec_rows.T)
        st.append(st_ref[...])

    def order(step, backward):
        return n - 1 - step if backward else step

    def increment(d, step):
        _, ke_ref, v_ref, _, _, row0, _, backward = dirs[d]
        i = order(step, backward)
        sub = slice(row0 + i * s, row0 + (i + 1) * s)
        return lax.dot_general(ke_ref[sub, :], v_ref[sub, :], _TN,
                               preferred_element_type=F32)

    inc = {(d, step): increment(d, step) for step in range(ahead) for d in range(2)}
    for step in range(n):
        for d, (qe_ref, _, _, _, _, row0, _, backward) in enumerate(dirs):
            i = order(step, backward)
            sub = slice(row0 + i * s, row0 + (i + 1) * s)
            inter[d][i] = jnp.dot(qe_ref[sub, :], st[d].astype(BF16),
                                  preferred_element_type=F32)
            if step + ahead < n:
                inc[d, step + ahead] = increment(d, step + ahead)
            decay = jnp.broadcast_to(dec[d][:, i:i + 1], (HEAD_DK, HEAD_DV))
            st[d] = st[d] * decay + inc.pop((d, step))
    for d, spec in enumerate(dirs):
        spec[4][...] = st[d]
    return [jnp.concatenate([inter[d][i] for i in range(n)], axis=0) for d in range(2)]


def _gla_scan_kernel(qef_ref, kef_ref, vf_ref, bendf_ref,
                     qeb_ref, keb_ref, vb_ref, bendb_ref, kem_ref, vm_ref,
                     of_ref, ob_ref, stf_ref, stb_ref):
    rb = qef_ref.shape[0]
    ng = rb // SCAN_G

    @pl.when(pl.program_id(2) == 0)
    def _():
        stf_ref[...] = lax.dot_general(kem_ref[...], vm_ref[...], _TN,
                                       preferred_element_type=F32)
        stb_ref[...] = jnp.zeros_like(stb_ref)

    for j in range(ng):
        jb = ng - 1 - j
        cpg = SCAN_G // SUB
        in_f, in_b = _scan_group([
            (qef_ref, kef_ref, vf_ref, bendf_ref, stf_ref, j * SCAN_G, j * cpg, False),
            (qeb_ref, keb_ref, vb_ref, bendb_ref, stb_ref, jb * SCAN_G, jb * cpg, True)])
        of_ref[j * SCAN_G:(j + 1) * SCAN_G, :] = in_f.astype(BF16)
        ob_ref[jb * SCAN_G:(jb + 1) * SCAN_G, :] = in_b.astype(BF16)


def _gla_scan_call(qe, ke, v, bend, kem, vm, seq, batch):
    h, rb = GLA_HEADS, SCAN_RB
    nt = seq // rb
    fwd = lambda b, hh, t: b * nt + t
    bwd = lambda b, hh, t: b * nt + nt - 1 - t
    cb = rb // SUB

    def dk_blk(rowfn, dircol, rows):
        return pl.BlockSpec((rows, HEAD_DK), lambda b, hh, t: (rowfn(b, hh, t), dircol + hh))

    def dv_blk(rowfn):
        return pl.BlockSpec((rb, HEAD_DV), lambda b, hh, t: (rowfn(b, hh, t), hh))

    in_specs = [
        dk_blk(fwd, 0, rb), dk_blk(fwd, 0, rb), dv_blk(fwd), dk_blk(fwd, 0, cb),
        dk_blk(bwd, h, rb), dk_blk(bwd, h, rb), dv_blk(bwd), dk_blk(bwd, h, cb),
        pl.BlockSpec((SUB, HEAD_DK), lambda b, hh, t: (0, hh)),
        pl.BlockSpec((SUB, HEAD_DV), lambda b, hh, t: (0, hh)),
    ]
    out = jax.ShapeDtypeStruct((batch * seq, GLA_DV), BF16)
    return pl.pallas_call(
        _gla_scan_kernel,
        grid=(batch, h, nt),
        in_specs=in_specs,
        out_specs=(dv_blk(fwd), dv_blk(bwd)),
        out_shape=(out, out),
        scratch_shapes=[pltpu.VMEM((HEAD_DK, HEAD_DV), F32), pltpu.VMEM((HEAD_DK, HEAD_DV), F32)],
        compiler_params=pltpu.CompilerParams(
            dimension_semantics=("arbitrary", "arbitrary", "arbitrary"),
            vmem_limit_bytes=V7X_VMEM_LIMIT),
        name="gla_scan",
    )(qe, ke, v, bend, qe, ke, v, bend, kem, vm)


DFT_TILE = 256


def _mirror(j_ref, a_ref, b_ref):
    t = DFT_TILE
    return (jnp.dot(j_ref[0, :, :t], a_ref[...], preferred_element_type=F32)
            + jnp.dot(j_ref[0, :, t:], b_ref[...], preferred_element_type=F32))


def _fold_kernel(x_ref, z_ref, j_ref, o_ref, *, m_last):
    t = DFT_TILE
    n_real = x_ref.shape[1] // t
    sign = jnp.where(pl.program_id(0) == 0, 1.0, -1.0)
    real = lambda i: x_ref[0, i * t:(i + 1) * t, :]
    for i in range(o_ref.shape[1] // t):
        direct = (z_ref[0] if i == 0 else real(i - 1)).astype(F32)
        mirror = (jnp.dot(j_ref[i, :, :t], real(min(n_real - i, n_real - 1)),
                          preferred_element_type=F32)
                  + jnp.dot(j_ref[i, :, t:], real(n_real - 1 - i), preferred_element_type=F32))
        m = i * t + lax.broadcasted_iota(jnp.int32, (t, 1), 0)
        folded = jnp.where(m <= m_last, direct + sign * mirror, 0.0)
        o_ref[0, i * t:(i + 1) * t, :] = folded.astype(BF16)


def _dft_kernel(peqo_ref, dc_ref, ds_ref, rowc_ref, rows_ref, u_ref, w_ref, *, scale):
    t = DFT_TILE
    acc_a = acc_b = None
    for c in range(peqo_ref.shape[1] // t):
        sl = slice(c * t, (c + 1) * t)
        rc, rs = rowc_ref[0, 0:1, sl], rows_ref[0, 0:1, sl]
        dc, ds = dc_ref[:, sl], ds_ref[:, sl]
        tc = (dc * rc - ds * rs).astype(BF16)
        ts = (ds * rc + dc * rs).astype(BF16)
        a = jnp.dot(tc, peqo_ref[0, sl, :], preferred_element_type=F32)
        b = jnp.dot(ts, peqo_ref[1, sl, :], preferred_element_type=F32)
        acc_a = a if acc_a is None else acc_a + a
        acc_b = b if acc_b is None else acc_b + b
    u_ref[...] = ((acc_a - acc_b) * scale).astype(BF16)
    w_ref[...] = ((acc_a + acc_b) * scale).astype(BF16)


def _unfold_tile(tile, ud_ref, wa_ref, wb_ref, j_ref, r_last):
    r = tile * DFT_TILE + lax.broadcasted_iota(jnp.int32, (DFT_TILE, 1), 0)
    return jnp.where(r <= r_last, ud_ref[...], _mirror(j_ref, wa_ref, wb_ref).astype(BF16))


def _unfold_specs(c, col_of, tile_of):
    t, nf, n_real, i_mixed = DFT_TILE, c["nf"], c["n_real"], c["i_mixed"]
    clip = lambda v: jnp.clip(v, 0, nf - 1)
    blk = lambda fn: pl.BlockSpec((t, FNET_WIDTH), lambda i: (clip(fn(tile_of(i))), col_of(i)))
    variant = lambda tl: jnp.where(tl < i_mixed, 0, jnp.where(tl == i_mixed, 1, 2))
    return [blk(lambda tl: tl + 1), blk(lambda tl: n_real - tl), blk(lambda tl: n_real - 1 - tl),
            pl.BlockSpec((1, t, 2 * t), lambda i: (variant(tile_of(i)), 0, 0))]


@functools.lru_cache(maxsize=None)
def _dft_constants(seq, n_meta):
    t = DFT_TILE
    off = t - n_meta
    big_l = seq + n_meta
    half = big_l // 2
    n_real = seq // t
    nf = -(-(half + 1 + off) // t)
    hp = nf * t
    jf = np.zeros((nf, t, 2 * t), np.float32)
    for i in range(nf):
        a_idx, b_idx = min(n_real - i, n_real - 1), n_real - 1 - i
        for j in range(t):
            pos = i * t + j - off
            if 1 <= pos <= half - 1:
                tau, rho = divmod(big_l - pos - n_meta, t)
                assert tau in (a_idx, b_idx)
                jf[i, j, (t if tau == b_idx else 0) + rho] = 1.0
    r_last = half - n_meta
    i_mixed = r_last // t
    ju = np.zeros((3, t, 2 * t), np.float32)
    for i in range(i_mixed, n_real):
        a_idx, b_idx = n_real - i, n_real - 1 - i
        blk = np.zeros((t, 2 * t), np.float32)
        for j in range(t):
            r = i * t + j
            if r > r_last:
                tau, rho = divmod(big_l - (r + n_meta) + off, t)
                assert tau in (a_idx, b_idx)
                blk[j, (t if tau == b_idx else 0) + rho] = 1.0
        var = 1 if i == i_mixed else 2
        assert var == 1 or not ju[2].any() or (ju[2] == blk).all()
        ju[var] = blk
    ang = lambda prod: 2.0 * np.pi * (prod % big_l).astype(np.float64) / big_l
    pos = np.arange(hp, dtype=np.int64) - off
    d = ang(np.arange(t, dtype=np.int64)[:, None] * pos[None, :])
    k0 = t * np.arange(nf, dtype=np.int64) - off
    row = np.broadcast_to(ang(k0[:, None] * pos[None, :])[:, None, :], (nf, 8, hp))
    f = lambda a: np.ascontiguousarray(a, dtype=np.float32)
    return dict(jf=jf, ju=ju, dc=f(np.cos(d)), ds=f(np.sin(d)), rowc=f(np.cos(row)),
                rows=f(np.sin(row)), nf=nf, hp=hp, off=off, n_real=n_real,
                m_last=half + off, r_last=r_last, i_mixed=i_mixed)


def _seq_dft(pq, pqm, seq, n_meta):
    c = _dft_constants(seq, n_meta)
    t, nf, hp, ncol = DFT_TILE, c["nf"], c["hp"], pq.shape[2]
    ncb = ncol // FNET_WIDTH
    col_blk = lambda rows: pl.BlockSpec((1, rows, FNET_WIDTH), lambda g, b: (g, 0, b))
    peqo = pl.pallas_call(
        functools.partial(_fold_kernel, m_last=c["m_last"]),
        grid=(2, ncb),
        in_specs=[col_blk(seq), col_blk(t), _resident(c["jf"].shape)],
        out_specs=col_blk(hp),
        out_shape=jax.ShapeDtypeStruct((2, hp, ncol), BF16),
        compiler_params=pltpu.CompilerParams(
            dimension_semantics=("arbitrary", "arbitrary"), vmem_limit_bytes=V7X_VMEM_LIMIT),
        name="dft_fold",
    )(pq, jnp.pad(pqm, ((0, 0), (c["off"], 0), (0, 0))), jnp.asarray(c["jf"]).astype(BF16))

    scale = float((seq + n_meta) * FNET_GROUP_DIM) ** -0.5
    tile = pl.BlockSpec((t, ncol), lambda i: (i, 0))
    rowspec = pl.BlockSpec((1, 8, hp), lambda i: (i, 0, 0))
    u, w = pl.pallas_call(
        functools.partial(_dft_kernel, scale=scale),
        grid=(nf,),
        in_specs=[_resident((2, hp, ncol)), _resident((t, hp)), _resident((t, hp)),
                  rowspec, rowspec],
        out_specs=(tile, tile),
        out_shape=(jax.ShapeDtypeStruct((hp, ncol), BF16),) * 2,
        compiler_params=pltpu.CompilerParams(
            dimension_semantics=("arbitrary",), vmem_limit_bytes=V7X_VMEM_LIMIT),
        name="seq_dft",
    )(peqo, jnp.asarray(c["dc"]), jnp.asarray(c["ds"]), jnp.asarray(c["rowc"]),
      jnp.asarray(c["rows"]))

    return u, w, jnp.asarray(c["ju"]).astype(BF16)


FFN_CHUNK = 1024
TAIL_SPLIT = 2
TAIL_TM = 512


def _tail_kernel(x_ref, ol_ref, of_ref, ob_ref, *refs, tiles_per_seq, r_last):
    unfold_refs = [refs[4 * i:4 * i + 4] for i in range(TAIL_SPLIT)]
    (nmix_ref, gn_ref, nffn_ref, nfin_ref, wg_ref, wm_ref, wo_ref, wf_ref, wout_ref, w1_ref,
     w2_ref, o_ref) = refs[4 * TAIL_SPLIT:]
    tm = x_ref.shape[0]
    sub = tm // TAIL_SPLIT
    tiles = [(dict(idx=i), slice(i * sub, (i + 1) * sub)) for i in range(TAIL_SPLIT)]
    dot = functools.partial(jnp.dot, preferred_element_type=F32)
    nchunk = D_FF // FFN_CHUNK

    def norm_in(s, r):
        s["x"] = x_ref[r, :]
        s["xn"] = _rms(s["x"], nmix_ref[...]).astype(BF16)

    def gates(s, r):
        s["g"] = [dot(s["xn"], wg_ref[:, h * HEAD_DV:(h + 1) * HEAD_DV])
                  for h in range(GLA_HEADS)]
        s["ga"] = dot(s["xn"], wm_ref[:, 0:D_MODEL])
        s["gf"] = dot(s["xn"], wm_ref[:, D_MODEL:2 * D_MODEL])
        seq_tile = (pl.program_id(0) * TAIL_SPLIT + s["idx"]) % tiles_per_seq
        y = _unfold_tile(seq_tile, *unfold_refs[s["idx"]], r_last)
        s["y_fnet"] = dot(y, wf_ref[...])

    def gla_out(s, r):
        y_gla = None
        for h in range(GLA_HEADS):
            hv = slice(h * HEAD_DV, (h + 1) * HEAD_DV)
            o = (ol_ref[r, hv].astype(F32) + of_ref[r, hv].astype(F32)
                 + ob_ref[r, hv].astype(F32))
            g = s["g"][h]
            og = (_rms(o, gn_ref[...]) * (g * _sigmoid(g))).astype(BF16)
            part = dot(og, wo_ref[hv, :])
            y_gla = part if y_gla is None else y_gla + part
        s["y_gla"] = y_gla

    def merge(s, r):
        merged = _sigmoid(s["ga"]) * s["y_gla"] + _sigmoid(s["gf"]) * s["y_fnet"]
        s["h1"] = s["x"] + dot(merged.astype(BF16), wout_ref[...])
        s["u"] = _rms(s["h1"], nffn_ref[...]).astype(BF16)
        s["acc"] = s["h1"]

    def ffn_up(c):
        def stage(s, r):
            a = jnp.maximum(dot(s["u"], w1_ref[:, c * FFN_CHUNK:(c + 1) * FFN_CHUNK]), 0.0)
            s["a", c] = (a * a).astype(BF16)
        return stage

    def ffn_down(c):
        def stage(s, r):
            s["acc"] = s["acc"] + dot(s.pop(("a", c)),
                                      w2_ref[c * FFN_CHUNK:(c + 1) * FFN_CHUNK, :])
        return stage

    def norm_out(s, r):
        o_ref[r, :] = _rms(s["acc"], nfin_ref[...])

    stages = [norm_in, gates, gla_out, merge, ffn_up(0)]
    for c in range(nchunk):
        if c + 1 < nchunk:
            stages.append(ffn_up(c + 1))
        stages.append(ffn_down(c))
    stages.append(norm_out)
    _run_interleaved(stages, tiles)


def _tail_call(x2d, ol, o_f, o_b, dft_u, dft_w, dft_j, nmix, gn, nffn, nfin, w_in_bf, w_merge,
               wo, wf, wout, w1, w2, seq, n_meta, tm):
    m = x2d.shape[0]
    assert tm == TAIL_SPLIT * DFT_TILE
    row = lambda n: pl.BlockSpec((tm, n), lambda i: (i, 0))
    norms = [nmix, gn, nffn, nfin]
    consts = [w_merge, wo, wf, wout, w1, w2]
    dft = _dft_constants(seq, n_meta)
    tiles_per_seq = seq // DFT_TILE
    unfold_specs, unfold_args = [], []
    for sidx in range(TAIL_SPLIT):
        tile_of = lambda i, sidx=sidx: (i * TAIL_SPLIT + sidx) % tiles_per_seq
        col_of = lambda i, sidx=sidx: (i * TAIL_SPLIT + sidx) // tiles_per_seq
        unfold_specs += _unfold_specs(dft, col_of, tile_of)
        unfold_args += [dft_u, dft_w, dft_w, dft_j]
    return pl.pallas_call(
        functools.partial(_tail_kernel, tiles_per_seq=tiles_per_seq, r_last=dft["r_last"]),
        grid=(m // tm,),
        in_specs=[row(D_MODEL), row(GLA_DV), row(GLA_DV), row(GLA_DV)] + unfold_specs
                 + [_resident(c.shape) for c in norms]
                 + [_window((w_in_bf.shape[0], W_WINDOW), (0, 2))]
                 + [_resident(c.shape) for c in consts],
        out_specs=row(D_MODEL),
        out_shape=jax.ShapeDtypeStruct((m, D_MODEL), F32),
        compiler_params=pltpu.CompilerParams(
            dimension_semantics=("arbitrary",), vmem_limit_bytes=V7X_VMEM_LIMIT),
        name="tail",
    )(x2d, ol, o_f, o_b, *unfold_args, *norms, w_in_bf, *consts)


@functools.lru_cache(maxsize=None)
def _gla_constants(rows):
    idx = np.arange(rows)
    same = (idx[:, None] // SUB) == (idx[None, :] // SUB)
    tri_f = (same & (idx[None, :] <= idx[:, None])).astype(np.float32)
    return tri_f, np.ascontiguousarray(tri_f.T)


@functools.lru_cache(maxsize=None)
def _channel_dft():
    d = np.arange(FNET_GROUP_DIM, dtype=np.int64)
    ang = 2.0 * np.pi * ((d[:, None] * d[None, :]) % FNET_GROUP_DIM) / FNET_GROUP_DIM
    cs = np.zeros((2 * FNET_GROUP_DIM, 2 * FNET_GROUP_DIM), np.float32)
    cs[F_LEAD:F_LEAD + FNET_GROUP_DIM] = np.concatenate([np.cos(ang), np.sin(ang)], axis=1)
    return cs


def kernel(x, meta_tokens, norm_mix, w_in, w_decay_fwd, b_decay_fwd, w_decay_bwd, b_decay_bwd,
           gla_norm, w_o_gla, w_fnet, w_out, norm_ffn, w_ff1, w_ff2, norm_final):
    batch, seq, d = x.shape
    assert d == D_MODEL and meta_tokens.shape == (N_META, D_MODEL)
    assert w_in.shape[0] == 1 and seq % (2 * SCAN_RB) == 0 and seq % DFT_TILE == 0
    wi = w_in[0].astype(BF16)
    gate0 = 2 * GLA_DK + 2 * GLA_DV + 2 * DECAY_RANK + FNET_WIDTH
    assert wi.shape[1] == gate0 + 2 * D_MODEL
    w_merge = wi[:, gate0:]
    wdec = jnp.zeros((R_PAD, 2 * GLA_DK), F32)
    wdec = wdec.at[:DECAY_RANK, :GLA_DK].set(w_decay_fwd[0])
    wdec = wdec.at[DECAY_RANK:2 * DECAY_RANK, GLA_DK:].set(w_decay_bwd[0]).astype(BF16)
    bdec = jnp.concatenate([b_decay_fwd[0], b_decay_bwd[0]])[None, :]
    cs = jnp.asarray(_channel_dft())
    nw_mix = norm_mix[0][None, :]

    x2d = x.reshape(batch * seq, D_MODEL)
    v, ol, qe, ke, bend, pq = _mixin_call(x2d, nw_mix, wi, wdec, bdec, cs,
                                          *_gla_constants(GLA_C), seq, MIX_TM, GLA_C)
    meta_chunk = jnp.pad(meta_tokens, ((SUB - N_META, 0), (0, 0)))
    vm, _, _, kem, _, pqm = _mixin_call(meta_chunk, nw_mix, wi, wdec, bdec, cs,
                                        *_gla_constants(SUB), SUB, SUB, SUB)
    o_f, o_b = _gla_scan_call(qe, ke, v, bend, kem, vm, seq, batch)

    pqm = jnp.tile(pqm[:, SUB - N_META:], (1, 1, batch))
    dft_u, dft_w, dft_j = _seq_dft(pq, pqm, seq, N_META)

    out = _tail_call(x2d, ol, o_f, o_b, dft_u, dft_w, dft_j, nw_mix, gla_norm[0][None, :],
                     norm_ffn[0][None, :], norm_final[None, :], wi, w_merge,
                     w_o_gla[0].astype(BF16), w_fnet[0].astype(BF16), w_out[0].astype(BF16),
                     w_ff1[0].astype(BF16), w_ff2[0].astype(BF16), seq, N_META, TAIL_TM)
    return out.reshape(batch, seq, D_MODEL)
```

```python
import functools

import numpy as np
import jax
import jax.numpy as jnp
from jax import lax
from jax.experimental import pallas as pl
from jax.experimental.pallas import tpu as pltpu

F32 = jnp.float32
BF16 = jnp.bfloat16

D_MODEL = 1024
N_META = 16
GLA_HEADS = 4
GLA_DK = 512
GLA_DV = 1024
HEAD_DK = GLA_DK // GLA_HEADS
HEAD_DV = GLA_DV // GLA_HEADS
DECAY_RANK = 16
GATE_TEMP = 16.0
SUB = 64
FNET_GROUPS = 4
FNET_WIDTH = 512
FNET_GROUP_DIM = FNET_WIDTH // FNET_GROUPS
D_FF = 4 * D_MODEL
EPS = 1e-6
LOG2E = 1.4426950408889634
LANE = 128

_OFF_Q, _OFF_K, _OFF_V, _OFF_QKV_END = 0, 512, 1024, 2048
W_WINDOW = 1024
R_PAD = 128
F_LEAD = 2 * DECAY_RANK
F_EXT = F_LEAD + FNET_WIDTH + (LANE - F_LEAD)

GLA_C = 4 * SUB
MIX_TM = 1024

V7X_VMEM_LIMIT = 56 * 1024 * 1024


def _rms(x, w):
    ms = jnp.mean(x * x, axis=-1, keepdims=True)
    return x * lax.rsqrt(ms + EPS) * w


def _sigmoid(x):
    return 1.0 / (1.0 + jnp.exp(-x))


def _run_interleaved(stages, tiles):
    for stage in stages:
        for state, rows in tiles:
            stage(state, rows)


def _window(block, index):
    return pl.BlockSpec(block, lambda *_: index, pipeline_mode=pl.Buffered(1))


def _resident(shape):
    return _window(shape, (0,) * len(shape))


_NT = (((1,), (1,)), ((), ()))
_TN = (((0,), (0,)), ((), ()))


def _mixin_kernel(x_ref, nw_ref, wqkv_ref, wrf_ref, wdec_ref, bdec_ref, cs_ref,
                  trif_ref, trib_ref, maskf_ref, maskb_ref,
                  v_ref, ol_ref, qe_ref, ke_ref, bend_ref, pq_ref, *, chunk):
    tm = x_ref.shape[0]
    n, s = chunk // SUB, SUB
    tiles = [(dict(), slice(i * chunk, (i + 1) * chunk)) for i in range(tm // chunk)]
    dot = functools.partial(jnp.dot, preferred_element_type=F32)
    cs = cs_ref[...].astype(BF16)
    tris = (trif_ref[...], trib_ref[...])
    masks = (maskf_ref[0:s, 0:s] != 0.0, maskb_ref[0:s, 0:s] != 0.0)
    half = GLA_DK // 2

    def norm_in(st, r):
        st["xn"] = _rms(x_ref[r, :], nw_ref[...]).astype(BF16)

    def decay_rank(st, r):
        st["f"] = dot(st["xn"], wrf_ref[:, 0:F_EXT]).astype(BF16)
        st["r"] = st["f"][:, 0:R_PAD]

    def log_decay(st, j):
        cols = slice(j * half, (j + 1) * half)
        x = dot(st["r"], wdec_ref[:, cols]) + bdec_ref[:, cols]
        la = (jnp.minimum(x, 0.0) - jnp.log(1.0 + jnp.exp(-jnp.abs(x)))) * (LOG2E / GATE_TEMP)
        hi = la.astype(BF16)
        st["la", j] = (hi, (la - hi.astype(F32)).astype(BF16))

    def project(st, r):
        mm = lambda lo, hi: dot(st["xn"], wqkv_ref[:, lo:hi])
        vmid = (_OFF_V + _OFF_QKV_END) // 2
        log_decay(st, 0)
        st["q"] = mm(_OFF_Q, _OFF_K) * (HEAD_DK ** -0.5)
        log_decay(st, 1)
        st["k"] = mm(_OFF_K, _OFF_V)
        log_decay(st, 2)
        v_ref[r, 0:GLA_DV // 2] = mm(_OFF_V, vmid).astype(BF16)
        log_decay(st, 3)
        v_ref[r, GLA_DV // 2:GLA_DV] = mm(vmid, _OFF_QKV_END).astype(BF16)

    def decay_factors(st, r):
        c0 = r.start // s
        for j in range(4):
            d, qcols = j // 2, slice((j % 2) * half, (j % 2 + 1) * half)
            cols = slice(j * half, (j + 1) * half)
            hi, lo = st.pop(("la", j))
            b = (dot(tris[d], hi) + dot(tris[d], lo)).reshape(n, s, half)
            if d == 0:
                b_mid, b_end = b[:, s // 2 - 1:s // 2, :], b[:, s - 1:s, :]
            else:
                b_mid, b_end = b[:, s // 2:s // 2 + 1, :], b[:, 0:1, :]
            qd = st["q"][:, qcols].reshape(n, s, half) * jnp.exp2(b - b_mid)
            kd = st["k"][:, qcols].reshape(n, s, half) * jnp.exp2(b_mid - b)
            qe = qd * jnp.exp2(b_mid)
            ke = kd * jnp.exp2(b_end - b_mid)
            st["qd", d, j % 2] = qd.reshape(chunk, half).astype(BF16)
            st["kd", d, j % 2] = kd.reshape(chunk, half).astype(BF16)
            qe_ref[r, cols] = qe.reshape(chunk, half).astype(BF16)
            ke_ref[r, cols] = ke.reshape(chunk, half).astype(BF16)
            bend_ref[c0:c0 + n, cols] = b_end.reshape(n, half)

    def intra_chunk(st, r):
        for h in range(GLA_HEADS):
            hk = slice((h % 2) * HEAD_DK, (h % 2 + 1) * HEAD_DK)
            hv = slice(h * HEAD_DV, (h + 1) * HEAD_DV)
            sc = [[lax.dot_general(st["qd", d, h // 2][c * s:(c + 1) * s, hk],
                                   st["kd", d, h // 2][c * s:(c + 1) * s, hk], _NT,
                                   preferred_element_type=F32) for d in range(2)]
                  for c in range(n)]
            for c, (sf, sb) in enumerate(sc):
                rows = slice(r.start + c * s, r.start + (c + 1) * s)
                both = (jnp.where(masks[0], sf, 0.0) + jnp.where(masks[1], sb, 0.0)).astype(BF16)
                ol_ref[rows, hv] = dot(both, v_ref[rows, hv]).astype(BF16)

    def channel_dft(st, r):
        for grp in range(FNET_GROUPS):
            sl = slice(grp * FNET_GROUP_DIM, (grp + 1) * FNET_GROUP_DIM)
            pq = dot(st["f"][:, grp * FNET_GROUP_DIM:(grp + 2) * FNET_GROUP_DIM], cs)
            pq_ref[0, r, sl] = pq[:, :FNET_GROUP_DIM].astype(BF16)
            pq_ref[1, r, sl] = pq[:, FNET_GROUP_DIM:].astype(BF16)

    _run_interleaved((norm_in, decay_rank, project, decay_factors, intra_chunk, channel_dft),
                     tiles)


def _mixin_call(x2d, nw, w_in_bf, wdec, bdec, cs, tri_f, tri_b, rows_per_batch, tm, chunk):
    m = x2d.shape[0]
    d_model = w_in_bf.shape[0]
    nt = rows_per_batch // tm
    nb = m // rows_per_batch
    row = lambda n: pl.BlockSpec((tm, n), lambda i: (i, 0))
    pq_spec = pl.BlockSpec((2, tm, FNET_WIDTH), lambda i: (0, i % nt, i // nt))
    wide = jax.ShapeDtypeStruct((m, 2 * GLA_DK), BF16)
    pq_shape = jax.ShapeDtypeStruct((2, rows_per_batch, nb * FNET_WIDTH), BF16)
    consts = [wdec, bdec, cs,
              jnp.asarray(tri_f).astype(BF16), jnp.asarray(tri_b).astype(BF16),
              jnp.asarray(tri_f), jnp.asarray(tri_b)]
    return pl.pallas_call(
        functools.partial(_mixin_kernel, chunk=chunk),
        grid=(m // tm,),
        in_specs=[row(D_MODEL), _resident(nw.shape),
                  _window((d_model, _OFF_QKV_END), (0, 0)),
                  _window((d_model, W_WINDOW), (0, 3))]
                 + [_resident(c.shape) for c in consts],
        out_specs=(row(GLA_DV), row(GLA_DV), row(2 * GLA_DK), row(2 * GLA_DK),
                   pl.BlockSpec((tm // SUB, 2 * GLA_DK), lambda i: (i, 0)), pq_spec),
        out_shape=(jax.ShapeDtypeStruct((m, GLA_DV), BF16), jax.ShapeDtypeStruct((m, GLA_DV), BF16),
                   wide, wide, jax.ShapeDtypeStruct((m // SUB, 2 * GLA_DK), F32), pq_shape),
        compiler_params=pltpu.CompilerParams(
            dimension_semantics=("arbitrary",), vmem_limit_bytes=V7X_VMEM_LIMIT),
        name="mixin",
    )(x2d, nw, w_in_bf, w_in_bf, *consts)


SCAN_G = 512
SCAN_RB = 4096


def _scan_group(dirs):
    s, n = SUB, SCAN_G // SUB
    ahead = 2
    dec, st, inter = [], [], [{}, {}]
    for qe_ref, ke_ref, v_ref, bend_ref, st_ref, row0, chunk0, backward in dirs:
        dec_rows = jnp.tile(jnp.exp2(bend_ref[chunk0:chunk0 + n, :]), (HEAD_DK // n, 1))
        dec.append(dec_rows.T)
        st.append(st_ref[...])

    def order(step, backward):
        return n - 1 - step if backward else step

    def increment(d, step):
        _, ke_ref, v_ref, _, _, row0, _, backward = dirs[d]
        i = order(step, backward)
        sub = slice(row0 + i * s, row0 + (i + 1) * s)
        return lax.dot_general(ke_ref[sub, :], v_ref[sub, :], _TN,
                               preferred_element_type=F32)

    inc = {(d, step): increment(d, step) for step in range(ahead) for d in range(2)}
    for step in range(n):
        for d, (qe_ref, _, _, _, _, row0, _, backward) in enumerate(dirs):
            i = order(step, backward)
            sub = slice(row0 + i * s, row0 + (i + 1) * s)
            inter[d][i] = jnp.dot(qe_ref[sub, :], st[d].astype(BF16),
                                  preferred_element_type=F32)
            if step + ahead < n:
                inc[d, step + ahead] = increment(d, step + ahead)
            decay = jnp.broadcast_to(dec[d][:, i:i + 1], (HEAD_DK, HEAD_DV))
            st[d] = st[d] * decay + inc.pop((d, step))
    for d, spec in enumerate(dirs):
        spec[4][...] = st[d]
    return [jnp.concatenate([inter[d][i] for i in range(n)], axis=0) for d in range(2)]


def _gla_scan_kernel(qef_ref, kef_ref, vf_ref, bendf_ref,
                     qeb_ref, keb_ref, vb_ref, bendb_ref, kem_ref, vm_ref,
                     of_ref, ob_ref, stf_ref, stb_ref):
    rb = qef_ref.shape[0]
    ng = rb // SCAN_G

    @pl.when(pl.program_id(2) == 0)
    def _():
        stf_ref[...] = lax.dot_general(kem_ref[...], vm_ref[...], _TN,
                                       preferred_element_type=F32)
        stb_ref[...] = jnp.zeros_like(stb_ref)

    for j in range(ng):
        jb = ng - 1 - j
        cpg = SCAN_G // SUB
        in_f, in_b = _scan_group([
            (qef_ref, kef_ref, vf_ref, bendf_ref, stf_ref, j * SCAN_G, j * cpg, False),
            (qeb_ref, keb_ref, vb_ref, bendb_ref, stb_ref, jb * SCAN_G, jb * cpg, True)])
        of_ref[j * SCAN_G:(j + 1) * SCAN_G, :] = in_f.astype(BF16)
        ob_ref[jb * SCAN_G:(jb + 1) * SCAN_G, :] = in_b.astype(BF16)


def _gla_scan_call(qe, ke, v, bend, kem, vm, seq, batch):
    h, rb = GLA_HEADS, SCAN_RB
    nt = seq // rb
    fwd = lambda b, hh, t: b * nt + t
    bwd = lambda b, hh, t: b * nt + nt - 1 - t
    cb = rb // SUB

    def dk_blk(rowfn, dircol, rows):
        return pl.BlockSpec((rows, HEAD_DK), lambda b, hh, t: (rowfn(b, hh, t), dircol + hh))

    def dv_blk(rowfn):
        return pl.BlockSpec((rb, HEAD_DV), lambda b, hh, t: (rowfn(b, hh, t), hh))

    in_specs = [
        dk_blk(fwd, 0, rb), dk_blk(fwd, 0, rb), dv_blk(fwd), dk_blk(fwd, 0, cb),
        dk_blk(bwd, h, rb), dk_blk(bwd, h, rb), dv_blk(bwd), dk_blk(bwd, h, cb),
        pl.BlockSpec((SUB, HEAD_DK), lambda b, hh, t: (0, hh)),
        pl.BlockSpec((SUB, HEAD_DV), lambda b, hh, t: (0, hh)),
    ]
    out = jax.ShapeDtypeStruct((batch * seq, GLA_DV), BF16)
    return pl.pallas_call(
        _gla_scan_kernel,
        grid=(batch, h, nt),
        in_specs=in_specs,
        out_specs=(dv_blk(fwd), dv_blk(bwd)),
        out_shape=(out, out),
        scratch_shapes=[pltpu.VMEM((HEAD_DK, HEAD_DV), F32), pltpu.VMEM((HEAD_DK, HEAD_DV), F32)],
        compiler_params=pltpu.CompilerParams(
            dimension_semantics=("arbitrary", "arbitrary", "arbitrary"),
            vmem_limit_bytes=V7X_VMEM_LIMIT),
        name="gla_scan",
    )(qe, ke, v, bend, qe, ke, v, bend, kem, vm)


DFT_TILE = 256


def _fold_kernel(x_ref, z_ref, j_ref, o_ref, *, m_last):
    t = DFT_TILE
    n_real = x_ref.shape[1] // t
    sign = jnp.where(pl.program_id(0) == 0, 1.0, -1.0)
    real = lambda i: x_ref[0, i * t:(i + 1) * t, :]
    for i in range(o_ref.shape[1] // t):
        direct = (z_ref[0] if i == 0 else real(i - 1)).astype(F32)
        mirror = (jnp.dot(j_ref[i, :, :t], real(min(n_real - i, n_real - 1)),
                          preferred_element_type=F32)
                  + jnp.dot(j_ref[i, :, t:], real(n_real - 1 - i), preferred_element_type=F32))
        m = i * t + lax.broadcasted_iota(jnp.int32, (t, 1), 0)
        folded = jnp.where(m <= m_last, direct + sign * mirror, 0.0)
        o_ref[0, i * t:(i + 1) * t, :] = folded.astype(BF16)


def _dft_kernel(peqo_ref, dc_ref, ds_ref, rowc_ref, rows_ref, u_ref, w_ref, *, scale):
    t = DFT_TILE
    acc_a = acc_b = None
    for c in range(peqo_ref.shape[1] // t):
        sl = slice(c * t, (c + 1) * t)
        rc, rs = rowc_ref[0, 0:1, sl], rows_ref[0, 0:1, sl]
        dc, ds = dc_ref[:, sl], ds_ref[:, sl]
        tc = (dc * rc - ds * rs).astype(BF16)
        ts = (ds * rc + dc * rs).astype(BF16)
        a = jnp.dot(tc, peqo_ref[0, sl, :], preferred_element_type=F32)
        b = jnp.dot(ts, peqo_ref[1, sl, :], preferred_element_type=F32)
        acc_a = a if acc_a is None else acc_a + a
        acc_b = b if acc_b is None else acc_b + b
    u_ref[...] = ((acc_a - acc_b) * scale).astype(BF16)
    w_ref[...] = ((acc_a + acc_b) * scale).astype(BF16)


def _unfold_kernel(u_ref, w_ref, j_ref, y_ref, *, r_last, i_mixed):
    t = DFT_TILE
    tile = lambda ref, i: ref[i * t:(i + 1) * t, :]
    n_real = y_ref.shape[0] // t
    for i in range(n_real):
        if i < i_mixed:
            y_ref[i * t:(i + 1) * t, :] = tile(u_ref, i + 1)
            continue
        j = 0 if i == i_mixed else 1
        mirror = jnp.dot(j_ref[j, :, t:], tile(w_ref, n_real - 1 - i), preferred_element_type=F32)
        if (n_real - i + 1) * t <= w_ref.shape[0]:
            mirror += jnp.dot(j_ref[j, :, :t], tile(w_ref, n_real - i),
                              preferred_element_type=F32)
        mirror = mirror.astype(BF16)
        if i == i_mixed:
            r = i * t + lax.broadcasted_iota(jnp.int32, (t, 1), 0)
            mirror = jnp.where(r <= r_last, tile(u_ref, i + 1), mirror)
        y_ref[i * t:(i + 1) * t, :] = mirror


@functools.lru_cache(maxsize=None)
def _dft_constants(seq, n_meta):
    t = DFT_TILE
    off = t - n_meta
    big_l = seq + n_meta
    half = big_l // 2
    n_real = seq // t
    nf = -(-(half + 1 + off) // t)
    hp = nf * t
    jf = np.zeros((nf, t, 2 * t), np.float32)
    for i in range(nf):
        a_idx, b_idx = min(n_real - i, n_real - 1), n_real - 1 - i
        for j in range(t):
            pos = i * t + j - off
            if 1 <= pos <= half - 1:
                tau, rho = divmod(big_l - pos - n_meta, t)
                assert tau in (a_idx, b_idx)
                jf[i, j, (t if tau == b_idx else 0) + rho] = 1.0
    r_last = half - n_meta
    i_mixed = r_last // t
    ju = np.zeros((3, t, 2 * t), np.float32)
    for i in range(i_mixed, n_real):
        a_idx, b_idx = n_real - i, n_real - 1 - i
        blk = np.zeros((t, 2 * t), np.float32)
        for j in range(t):
            r = i * t + j
            if r > r_last:
                tau, rho = divmod(big_l - (r + n_meta) + off, t)
                assert tau in (a_idx, b_idx)
                blk[j, (t if tau == b_idx else 0) + rho] = 1.0
        var = 1 if i == i_mixed else 2
        assert var == 1 or not ju[2].any() or (ju[2] == blk).all()
        ju[var] = blk
    ang = lambda prod: 2.0 * np.pi * (prod % big_l).astype(np.float64) / big_l
    pos = np.arange(hp, dtype=np.int64) - off
    d = ang(np.arange(t, dtype=np.int64)[:, None] * pos[None, :])
    k0 = t * np.arange(nf, dtype=np.int64) - off
    row = np.broadcast_to(ang(k0[:, None] * pos[None, :])[:, None, :], (nf, 8, hp))
    f = lambda a: np.ascontiguousarray(a, dtype=np.float32)
    return dict(jf=jf, ju=ju, dc=f(np.cos(d)), ds=f(np.sin(d)), rowc=f(np.cos(row)),
                rows=f(np.sin(row)), nf=nf, hp=hp, off=off, n_real=n_real,
                m_last=half + off, r_last=r_last, i_mixed=i_mixed)


def _seq_dft(pq, pqm, seq, n_meta):
    c = _dft_constants(seq, n_meta)
    t, nf, hp, ncol = DFT_TILE, c["nf"], c["hp"], pq.shape[2]
    ncb = ncol // FNET_WIDTH
    col_blk = lambda rows: pl.BlockSpec((1, rows, FNET_WIDTH), lambda g, b: (g, 0, b))
    peqo = pl.pallas_call(
        functools.partial(_fold_kernel, m_last=c["m_last"]),
        grid=(2, ncb),
        in_specs=[col_blk(seq), col_blk(t), _resident(c["jf"].shape)],
        out_specs=col_blk(hp),
        out_shape=jax.ShapeDtypeStruct((2, hp, ncol), BF16),
        compiler_params=pltpu.CompilerParams(
            dimension_semantics=("arbitrary", "arbitrary"), vmem_limit_bytes=V7X_VMEM_LIMIT),
        name="dft_fold",
    )(pq, jnp.pad(pqm, ((0, 0), (c["off"], 0), (0, 0))), jnp.asarray(c["jf"]).astype(BF16))

    scale = float((seq + n_meta) * FNET_GROUP_DIM) ** -0.5
    tile = pl.BlockSpec((t, ncol), lambda i: (i, 0))
    rowspec = pl.BlockSpec((1, 8, hp), lambda i: (i, 0, 0))
    u, w = pl.pallas_call(
        functools.partial(_dft_kernel, scale=scale),
        grid=(nf,),
        in_specs=[_resident((2, hp, ncol)), _resident((t, hp)), _resident((t, hp)),
                  rowspec, rowspec],
        out_specs=(tile, tile),
        out_shape=(jax.ShapeDtypeStruct((hp, ncol), BF16),) * 2,
        compiler_params=pltpu.CompilerParams(
            dimension_semantics=("arbitrary",), vmem_limit_bytes=V7X_VMEM_LIMIT),
        name="seq_dft",
    )(peqo, jnp.asarray(c["dc"]), jnp.asarray(c["ds"]), jnp.asarray(c["rowc"]),
      jnp.asarray(c["rows"]))

    col = lambda rows: pl.BlockSpec((rows, FNET_WIDTH), lambda b: (0, b))
    return pl.pallas_call(
        functools.partial(_unfold_kernel, r_last=c["r_last"], i_mixed=c["i_mixed"]),
        grid=(ncb,),
        in_specs=[col(hp), col(hp), _resident((2, t, 2 * t))],
        out_specs=col(seq),
        out_shape=jax.ShapeDtypeStruct((seq, ncol), BF16),
        compiler_params=pltpu.CompilerParams(
            dimension_semantics=("arbitrary",), vmem_limit_bytes=V7X_VMEM_LIMIT),
        name="dft_unfold",
    )(u, w, jnp.asarray(c["ju"][1:]).astype(BF16))


FFN_CHUNK = 1024
TAIL_SPLIT = 2
TAIL_TM = 512


def _tail_kernel(x_ref, ol_ref, of_ref, ob_ref, y_ref, nmix_ref, gn_ref, nffn_ref, nfin_ref,
                 wg_ref, wm_ref, wo_ref, wf_ref, wout_ref, w1_ref, w2_ref, o_ref):
    tm = x_ref.shape[0]
    sub = tm // TAIL_SPLIT
    tiles = [(dict(), slice(i * sub, (i + 1) * sub)) for i in range(TAIL_SPLIT)]
    dot = functools.partial(jnp.dot, preferred_element_type=F32)
    nchunk = D_FF // FFN_CHUNK

    def norm_in(s, r):
        s["x"] = x_ref[r, :]
        s["xn"] = _rms(s["x"], nmix_ref[...]).astype(BF16)

    def gates(s, r):
        s["g"] = [dot(s["xn"], wg_ref[:, h * HEAD_DV:(h + 1) * HEAD_DV])
                  for h in range(GLA_HEADS)]
        s["ga"] = dot(s["xn"], wm_ref[:, 0:D_MODEL])
        s["gf"] = dot(s["xn"], wm_ref[:, D_MODEL:2 * D_MODEL])
        s["y_fnet"] = dot(y_ref[r, :], wf_ref[...])

    def gla_out(s, r):
        y_gla = None
        for h in range(GLA_HEADS):
            hv = slice(h * HEAD_DV, (h + 1) * HEAD_DV)
            o = (ol_ref[r, hv].astype(F32) + of_ref[r, hv].astype(F32)
                 + ob_ref[r, hv].astype(F32))
            g = s["g"][h]
            og = (_rms(o, gn_ref[...]) * (g * _sigmoid(g))).astype(BF16)
            part = dot(og, wo_ref[hv, :])
            y_gla = part if y_gla is None else y_gla + part
        s["y_gla"] = y_gla

    def merge(s, r):
        merged = _sigmoid(s["ga"]) * s["y_gla"] + _sigmoid(s["gf"]) * s["y_fnet"]
        s["h1"] = s["x"] + dot(merged.astype(BF16), wout_ref[...])
        s["u"] = _rms(s["h1"], nffn_ref[...]).astype(BF16)
        s["acc"] = s["h1"]

    def ffn_up(c):
        def stage(s, r):
            a = jnp.maximum(dot(s["u"], w1_ref[:, c * FFN_CHUNK:(c + 1) * FFN_CHUNK]), 0.0)
            s["a", c] = (a * a).astype(BF16)
        return stage

    def ffn_down(c):
        def stage(s, r):
            s["acc"] = s["acc"] + dot(s.pop(("a", c)),
                                      w2_ref[c * FFN_CHUNK:(c + 1) * FFN_CHUNK, :])
        return stage

    def norm_out(s, r):
        o_ref[r, :] = _rms(s["acc"], nfin_ref[...])

    stages = [norm_in, gates, gla_out, merge, ffn_up(0)]
    for c in range(nchunk):
        if c + 1 < nchunk:
            stages.append(ffn_up(c + 1))
        stages.append(ffn_down(c))
    stages.append(norm_out)
    _run_interleaved(stages, tiles)


def _tail_call(x2d, ol, o_f, o_b, y, nmix, gn, nffn, nfin, w_in_bf, w_merge, wo, wf, wout, w1,
               w2, seq, tm):
    m = x2d.shape[0]
    nt = seq // tm
    row = lambda n: pl.BlockSpec((tm, n), lambda i: (i, 0))
    norms = [nmix, gn, nffn, nfin]
    consts = [w_merge, wo, wf, wout, w1, w2]
    return pl.pallas_call(
        _tail_kernel,
        grid=(m // tm,),
        in_specs=[row(D_MODEL), row(GLA_DV), row(GLA_DV), row(GLA_DV),
                  pl.BlockSpec((tm, FNET_WIDTH), lambda i: (i % nt, i // nt))]
                 + [_resident(c.shape) for c in norms]
                 + [_window((w_in_bf.shape[0], W_WINDOW), (0, 2))]
                 + [_resident(c.shape) for c in consts],
        out_specs=row(D_MODEL),
        out_shape=jax.ShapeDtypeStruct((m, D_MODEL), F32),
        compiler_params=pltpu.CompilerParams(
            dimension_semantics=("arbitrary",), vmem_limit_bytes=V7X_VMEM_LIMIT),
        name="tail",
    )(x2d, ol, o_f, o_b, y, *norms, w_in_bf, *consts)


@functools.lru_cache(maxsize=None)
def _gla_constants(rows):
    idx = np.arange(rows)
    same = (idx[:, None] // SUB) == (idx[None, :] // SUB)
    tri_f = (same & (idx[None, :] <= idx[:, None])).astype(np.float32)
    return tri_f, np.ascontiguousarray(tri_f.T)


@functools.lru_cache(maxsize=None)
def _channel_dft():
    d = np.arange(FNET_GROUP_DIM, dtype=np.int64)
    ang = 2.0 * np.pi * ((d[:, None] * d[None, :]) % FNET_GROUP_DIM) / FNET_GROUP_DIM
    cs = np.zeros((2 * FNET_GROUP_DIM, 2 * FNET_GROUP_DIM), np.float32)
    cs[F_LEAD:F_LEAD + FNET_GROUP_DIM] = np.concatenate([np.cos(ang), np.sin(ang)], axis=1)
    return cs


def kernel(x, meta_tokens, norm_mix, w_in, w_decay_fwd, b_decay_fwd, w_decay_bwd, b_decay_bwd,
           gla_norm, w_o_gla, w_fnet, w_out, norm_ffn, w_ff1, w_ff2, norm_final):
    batch, seq, d = x.shape
    assert d == D_MODEL and meta_tokens.shape == (N_META, D_MODEL)
    assert w_in.shape[0] == 1 and seq % (2 * SCAN_RB) == 0 and seq % DFT_TILE == 0
    wi = w_in[0].astype(BF16)
    gate0 = 2 * GLA_DK + 2 * GLA_DV + 2 * DECAY_RANK + FNET_WIDTH
    assert wi.shape[1] == gate0 + 2 * D_MODEL
    w_merge = wi[:, gate0:]
    wdec = jnp.zeros((R_PAD, 2 * GLA_DK), F32)
    wdec = wdec.at[:DECAY_RANK, :GLA_DK].set(w_decay_fwd[0])
    wdec = wdec.at[DECAY_RANK:2 * DECAY_RANK, GLA_DK:].set(w_decay_bwd[0]).astype(BF16)
    bdec = jnp.concatenate([b_decay_fwd[0], b_decay_bwd[0]])[None, :]
    cs = jnp.asarray(_channel_dft())
    nw_mix = norm_mix[0][None, :]

    x2d = x.reshape(batch * seq, D_MODEL)
    v, ol, qe, ke, bend, pq = _mixin_call(x2d, nw_mix, wi, wdec, bdec, cs,
                                          *_gla_constants(GLA_C), seq, MIX_TM, GLA_C)
    meta_chunk = jnp.pad(meta_tokens, ((SUB - N_META, 0), (0, 0)))
    vm, _, _, kem, _, pqm = _mixin_call(meta_chunk, nw_mix, wi, wdec, bdec, cs,
                                        *_gla_constants(SUB), SUB, SUB, SUB)
    o_f, o_b = _gla_scan_call(qe, ke, v, bend, kem, vm, seq, batch)

    pqm = jnp.tile(pqm[:, SUB - N_META:], (1, 1, batch))
    y = _seq_dft(pq, pqm, seq, N_META)

    out = _tail_call(x2d, ol, o_f, o_b, y, nw_mix, gla_norm[0][None, :], norm_ffn[0][None, :],
                     norm_final[None, :], wi, w_merge, w_o_gla[0].astype(BF16),
                     w_fnet[0].astype(BF16), w_out[0].astype(BF16), w_ff1[0].astype(BF16),
                     w_ff2[0].astype(BF16), seq, TAIL_TM)
    return out.reshape(batch, seq, D_MODEL)
```

```python
import functools

import numpy as np
import jax
import jax.numpy as jnp
from jax import lax
from jax.experimental import pallas as pl
from jax.experimental.pallas import tpu as pltpu

F32 = jnp.float32
BF16 = jnp.bfloat16

D_MODEL = 1024
N_META = 16
GLA_HEADS = 4
GLA_DK = 512
GLA_DV = 1024
HEAD_DK = GLA_DK // GLA_HEADS
HEAD_DV = GLA_DV // GLA_HEADS
DECAY_RANK = 16
GATE_TEMP = 16.0
SUB = 64
FNET_GROUPS = 4
FNET_WIDTH = 512
FNET_GROUP_DIM = FNET_WIDTH // FNET_GROUPS
D_FF = 4 * D_MODEL
EPS = 1e-6
LOG2E = 1.4426950408889634
LANE = 128

_OFF_Q, _OFF_K, _OFF_V, _OFF_QKV_END = 0, 512, 1024, 2048
W_WINDOW = 1024
R_PAD = 128
F_LEAD = 2 * DECAY_RANK
F_EXT = F_LEAD + FNET_WIDTH + (LANE - F_LEAD)

GLA_C = 4 * SUB
MIX_TM = 1024

V7X_VMEM_LIMIT = 56 * 1024 * 1024


def _rms(x, w):
    ms = jnp.mean(x * x, axis=-1, keepdims=True)
    return x * lax.rsqrt(ms + EPS) * w


def _sigmoid(x):
    return 1.0 / (1.0 + jnp.exp(-x))


def _run_interleaved(stages, tiles):
    for stage in stages:
        for state, rows in tiles:
            stage(state, rows)


def _window(block, index):
    return pl.BlockSpec(block, lambda *_: index, pipeline_mode=pl.Buffered(1))


def _resident(shape):
    return _window(shape, (0,) * len(shape))


_NT = (((1,), (1,)), ((), ()))
_TN = (((0,), (0,)), ((), ()))


def _mixin_kernel(x_ref, nw_ref, wqkv_ref, wrf_ref, wdec_ref, bdec_ref, cs_ref,
                  trif_ref, trib_ref, maskf_ref, maskb_ref,
                  v_ref, ol_ref, qe_ref, ke_ref, bend_ref, pq_ref, *, chunk):
    tm = x_ref.shape[0]
    n, s = chunk // SUB, SUB
    tiles = [(dict(), slice(i * chunk, (i + 1) * chunk)) for i in range(tm // chunk)]
    dot = functools.partial(jnp.dot, preferred_element_type=F32)
    cs = cs_ref[...].astype(BF16)
    tris = (trif_ref[...], trib_ref[...])
    masks = (maskf_ref[0:s, 0:s] != 0.0, maskb_ref[0:s, 0:s] != 0.0)
    half = GLA_DK // 2

    def norm_in(st, r):
        st["xn"] = _rms(x_ref[r, :], nw_ref[...]).astype(BF16)

    def decay_rank(st, r):
        st["f"] = lax.dot_general(st["xn"], wrf_ref[0:F_EXT, :], _NT,
                                  preferred_element_type=F32).astype(BF16)
        st["r"] = st["f"][:, 0:R_PAD]

    def log_decay(st, j):
        cols = slice(j * half, (j + 1) * half)
        x = dot(st["r"], wdec_ref[:, cols]) + bdec_ref[:, cols]
        la = (jnp.minimum(x, 0.0) - jnp.log(1.0 + jnp.exp(-jnp.abs(x)))) * (LOG2E / GATE_TEMP)
        hi = la.astype(BF16)
        st["la", j] = (hi, (la - hi.astype(F32)).astype(BF16))

    def project(st, r):
        mm = lambda lo, hi: lax.dot_general(st["xn"], wqkv_ref[lo:hi, :], _NT,
                                            preferred_element_type=F32)
        vmid = (_OFF_V + _OFF_QKV_END) // 2
        log_decay(st, 0)
        st["q"] = mm(_OFF_Q, _OFF_K) * (HEAD_DK ** -0.5)
        log_decay(st, 1)
        st["k"] = mm(_OFF_K, _OFF_V)
        log_decay(st, 2)
        v_ref[r, 0:GLA_DV // 2] = mm(_OFF_V, vmid).astype(BF16)
        log_decay(st, 3)
        v_ref[r, GLA_DV // 2:GLA_DV] = mm(vmid, _OFF_QKV_END).astype(BF16)

    def decay_factors(st, r):
        c0 = r.start // s
        for j in range(4):
            d, qcols = j // 2, slice((j % 2) * half, (j % 2 + 1) * half)
            cols = slice(j * half, (j + 1) * half)
            hi, lo = st.pop(("la", j))
            b = (dot(tris[d], hi) + dot(tris[d], lo)).reshape(n, s, half)
            if d == 0:
                b_mid, b_end = b[:, s // 2 - 1:s // 2, :], b[:, s - 1:s, :]
            else:
                b_mid, b_end = b[:, s // 2:s // 2 + 1, :], b[:, 0:1, :]
            qd = st["q"][:, qcols].reshape(n, s, half) * jnp.exp2(b - b_mid)
            kd = st["k"][:, qcols].reshape(n, s, half) * jnp.exp2(b_mid - b)
            qe = qd * jnp.exp2(b_mid)
            ke = kd * jnp.exp2(b_end - b_mid)
            st["qd", d, j % 2] = qd.reshape(chunk, half).astype(BF16)
            st["kd", d, j % 2] = kd.reshape(chunk, half).astype(BF16)
            qe_ref[r, cols] = qe.reshape(chunk, half).astype(BF16)
            ke_ref[r, cols] = ke.reshape(chunk, half).astype(BF16)
            bend_ref[c0:c0 + n, cols] = b_end.reshape(n, half)

    def intra_chunk(st, r):
        for h in range(GLA_HEADS):
            hk = slice((h % 2) * HEAD_DK, (h % 2 + 1) * HEAD_DK)
            hv = slice(h * HEAD_DV, (h + 1) * HEAD_DV)
            sc = [[lax.dot_general(st["qd", d, h // 2][c * s:(c + 1) * s, hk],
                                   st["kd", d, h // 2][c * s:(c + 1) * s, hk], _NT,
                                   preferred_element_type=F32) for d in range(2)]
                  for c in range(n)]
            for c, (sf, sb) in enumerate(sc):
                rows = slice(r.start + c * s, r.start + (c + 1) * s)
                both = (jnp.where(masks[0], sf, 0.0) + jnp.where(masks[1], sb, 0.0)).astype(BF16)
                ol_ref[rows, hv] = dot(both, v_ref[rows, hv]).astype(BF16)

    def channel_dft(st, r):
        for grp in range(FNET_GROUPS):
            sl = slice(grp * FNET_GROUP_DIM, (grp + 1) * FNET_GROUP_DIM)
            pq = dot(st["f"][:, grp * FNET_GROUP_DIM:(grp + 2) * FNET_GROUP_DIM], cs)
            pq_ref[0, r, sl] = pq[:, :FNET_GROUP_DIM].astype(BF16)
            pq_ref[1, r, sl] = pq[:, FNET_GROUP_DIM:].astype(BF16)

    _run_interleaved((norm_in, decay_rank, project, decay_factors, intra_chunk, channel_dft),
                     tiles)


def _mixin_call(x2d, nw, w_in_bf, wdec, bdec, cs, tri_f, tri_b, rows_per_batch, tm, chunk):
    m = x2d.shape[0]
    d_model = w_in_bf.shape[1]
    nt = rows_per_batch // tm
    nb = m // rows_per_batch
    row = lambda n: pl.BlockSpec((tm, n), lambda i: (i, 0))
    pq_spec = pl.BlockSpec((2, tm, FNET_WIDTH), lambda i: (0, i % nt, i // nt))
    wide = jax.ShapeDtypeStruct((m, 2 * GLA_DK), BF16)
    pq_shape = jax.ShapeDtypeStruct((2, rows_per_batch, nb * FNET_WIDTH), BF16)
    consts = [wdec, bdec, cs,
              jnp.asarray(tri_f).astype(BF16), jnp.asarray(tri_b).astype(BF16),
              jnp.asarray(tri_f), jnp.asarray(tri_b)]
    return pl.pallas_call(
        functools.partial(_mixin_kernel, chunk=chunk),
        grid=(m // tm,),
        in_specs=[row(D_MODEL), _resident(nw.shape),
                  _window((_OFF_QKV_END, d_model), (0, 0)),
                  _window((W_WINDOW, d_model), (3, 0))]
                 + [_resident(c.shape) for c in consts],
        out_specs=(row(GLA_DV), row(GLA_DV), row(2 * GLA_DK), row(2 * GLA_DK),
                   pl.BlockSpec((tm // SUB, 2 * GLA_DK), lambda i: (i, 0)), pq_spec),
        out_shape=(jax.ShapeDtypeStruct((m, GLA_DV), BF16), jax.ShapeDtypeStruct((m, GLA_DV), BF16),
                   wide, wide, jax.ShapeDtypeStruct((m // SUB, 2 * GLA_DK), F32), pq_shape),
        compiler_params=pltpu.CompilerParams(
            dimension_semantics=("arbitrary",), vmem_limit_bytes=V7X_VMEM_LIMIT),
        name="mixin",
    )(x2d, nw, w_in_bf, w_in_bf, *consts)


SCAN_G = 512
SCAN_RB = 4096


def _scan_group(dirs):
    s, n = SUB, SCAN_G // SUB
    ahead = 2
    dec, st, inter = [], [], [{}, {}]
    for qe_ref, ke_ref, v_ref, bend_ref, st_ref, row0, chunk0, backward in dirs:
        dec_rows = jnp.tile(jnp.exp2(bend_ref[chunk0:chunk0 + n, :]), (HEAD_DK // n, 1))
        dec.append(dec_rows.T)
        st.append(st_ref[...])

    def order(step, backward):
        return n - 1 - step if backward else step

    def increment(d, step):
        _, ke_ref, v_ref, _, _, row0, _, backward = dirs[d]
        i = order(step, backward)
        sub = slice(row0 + i * s, row0 + (i + 1) * s)
        return lax.dot_general(ke_ref[sub, :], v_ref[sub, :], _TN,
                               preferred_element_type=F32)

    inc = {(d, step): increment(d, step) for step in range(ahead) for d in range(2)}
    for step in range(n):
        for d, (qe_ref, _, _, _, _, row0, _, backward) in enumerate(dirs):
            i = order(step, backward)
            sub = slice(row0 + i * s, row0 + (i + 1) * s)
            inter[d][i] = jnp.dot(qe_ref[sub, :], st[d].astype(BF16),
                                  preferred_element_type=F32)
            if step + ahead < n:
                inc[d, step + ahead] = increment(d, step + ahead)
            decay = jnp.broadcast_to(dec[d][:, i:i + 1], (HEAD_DK, HEAD_DV))
            st[d] = st[d] * decay + inc.pop((d, step))
    for d, spec in enumerate(dirs):
        spec[4][...] = st[d]
    return [jnp.concatenate([inter[d][i] for i in range(n)], axis=0) for d in range(2)]


def _gla_scan_kernel(qef_ref, kef_ref, vf_ref, bendf_ref,
                     qeb_ref, keb_ref, vb_ref, bendb_ref, kem_ref, vm_ref,
                     of_ref, ob_ref, stf_ref, stb_ref):
    rb = qef_ref.shape[0]
    ng = rb // SCAN_G

    @pl.when(pl.program_id(2) == 0)
    def _():
        stf_ref[...] = lax.dot_general(kem_ref[...], vm_ref[...], _TN,
                                       preferred_element_type=F32)
        stb_ref[...] = jnp.zeros_like(stb_ref)

    for j in range(ng):
        jb = ng - 1 - j
        cpg = SCAN_G // SUB
        in_f, in_b = _scan_group([
            (qef_ref, kef_ref, vf_ref, bendf_ref, stf_ref, j * SCAN_G, j * cpg, False),
            (qeb_ref, keb_ref, vb_ref, bendb_ref, stb_ref, jb * SCAN_G, jb * cpg, True)])
        of_ref[j * SCAN_G:(j + 1) * SCAN_G, :] = in_f.astype(BF16)
        ob_ref[jb * SCAN_G:(jb + 1) * SCAN_G, :] = in_b.astype(BF16)


def _gla_scan_call(qe, ke, v, bend, kem, vm, seq, batch):
    h, rb = GLA_HEADS, SCAN_RB
    nt = seq // rb
    fwd = lambda b, hh, t: b * nt + t
    bwd = lambda b, hh, t: b * nt + nt - 1 - t
    cb = rb // SUB

    def dk_blk(rowfn, dircol, rows):
        return pl.BlockSpec((rows, HEAD_DK), lambda b, hh, t: (rowfn(b, hh, t), dircol + hh))

    def dv_blk(rowfn):
        return pl.BlockSpec((rb, HEAD_DV), lambda b, hh, t: (rowfn(b, hh, t), hh))

    in_specs = [
        dk_blk(fwd, 0, rb), dk_blk(fwd, 0, rb), dv_blk(fwd), dk_blk(fwd, 0, cb),
        dk_blk(bwd, h, rb), dk_blk(bwd, h, rb), dv_blk(bwd), dk_blk(bwd, h, cb),
        pl.BlockSpec((SUB, HEAD_DK), lambda b, hh, t: (0, hh)),
        pl.BlockSpec((SUB, HEAD_DV), lambda b, hh, t: (0, hh)),
    ]
    out = jax.ShapeDtypeStruct((batch * seq, GLA_DV), BF16)
    return pl.pallas_call(
        _gla_scan_kernel,
        grid=(batch, h, nt),
        in_specs=in_specs,
        out_specs=(dv_blk(fwd), dv_blk(bwd)),
        out_shape=(out, out),
        scratch_shapes=[pltpu.VMEM((HEAD_DK, HEAD_DV), F32), pltpu.VMEM((HEAD_DK, HEAD_DV), F32)],
        compiler_params=pltpu.CompilerParams(
            dimension_semantics=("arbitrary", "arbitrary", "arbitrary"),
            vmem_limit_bytes=V7X_VMEM_LIMIT),
        name="gla_scan",
    )(qe, ke, v, bend, qe, ke, v, bend, kem, vm)


DFT_TILE = 256


def _fold_kernel(x_ref, z_ref, j_ref, o_ref, *, m_last):
    t = DFT_TILE
    n_real = x_ref.shape[1] // t
    sign = jnp.where(pl.program_id(0) == 0, 1.0, -1.0)
    real = lambda i: x_ref[0, i * t:(i + 1) * t, :]
    for i in range(o_ref.shape[1] // t):
        direct = (z_ref[0] if i == 0 else real(i - 1)).astype(F32)
        mirror = (jnp.dot(j_ref[i, :, :t], real(min(n_real - i, n_real - 1)),
                          preferred_element_type=F32)
                  + jnp.dot(j_ref[i, :, t:], real(n_real - 1 - i), preferred_element_type=F32))
        m = i * t + lax.broadcasted_iota(jnp.int32, (t, 1), 0)
        folded = jnp.where(m <= m_last, direct + sign * mirror, 0.0)
        o_ref[0, i * t:(i + 1) * t, :] = folded.astype(BF16)


def _dft_kernel(peqo_ref, dc_ref, ds_ref, rowc_ref, rows_ref, u_ref, w_ref, *, scale):
    t = DFT_TILE
    acc_a = acc_b = None
    for c in range(peqo_ref.shape[1] // t):
        sl = slice(c * t, (c + 1) * t)
        rc, rs = rowc_ref[0, 0:1, sl], rows_ref[0, 0:1, sl]
        dc, ds = dc_ref[:, sl], ds_ref[:, sl]
        tc = (dc * rc - ds * rs).astype(BF16)
        ts = (ds * rc + dc * rs).astype(BF16)
        a = jnp.dot(tc, peqo_ref[0, sl, :], preferred_element_type=F32)
        b = jnp.dot(ts, peqo_ref[1, sl, :], preferred_element_type=F32)
        acc_a = a if acc_a is None else acc_a + a
        acc_b = b if acc_b is None else acc_b + b
    u_ref[...] = ((acc_a - acc_b) * scale).astype(BF16)
    w_ref[...] = ((acc_a + acc_b) * scale).astype(BF16)


def _unfold_kernel(u_ref, w_ref, j_ref, y_ref, *, r_last, i_mixed):
    t = DFT_TILE
    tile = lambda ref, i: ref[i * t:(i + 1) * t, :]
    n_real = y_ref.shape[0] // t
    for i in range(n_real):
        if i < i_mixed:
            y_ref[i * t:(i + 1) * t, :] = tile(u_ref, i + 1)
            continue
        j = 0 if i == i_mixed else 1
        mirror = jnp.dot(j_ref[j, :, t:], tile(w_ref, n_real - 1 - i), preferred_element_type=F32)
        if (n_real - i + 1) * t <= w_ref.shape[0]:
            mirror += jnp.dot(j_ref[j, :, :t], tile(w_ref, n_real - i),
                              preferred_element_type=F32)
        mirror = mirror.astype(BF16)
        if i == i_mixed:
            r = i * t + lax.broadcasted_iota(jnp.int32, (t, 1), 0)
            mirror = jnp.where(r <= r_last, tile(u_ref, i + 1), mirror)
        y_ref[i * t:(i + 1) * t, :] = mirror


@functools.lru_cache(maxsize=None)
def _dft_constants(seq, n_meta):
    t = DFT_TILE
    off = t - n_meta
    big_l = seq + n_meta
    half = big_l // 2
    n_real = seq // t
    nf = -(-(half + 1 + off) // t)
    hp = nf * t
    jf = np.zeros((nf, t, 2 * t), np.float32)
    for i in range(nf):
        a_idx, b_idx = min(n_real - i, n_real - 1), n_real - 1 - i
        for j in range(t):
            pos = i * t + j - off
            if 1 <= pos <= half - 1:
                tau, rho = divmod(big_l - pos - n_meta, t)
                assert tau in (a_idx, b_idx)
                jf[i, j, (t if tau == b_idx else 0) + rho] = 1.0
    r_last = half - n_meta
    i_mixed = r_last // t
    ju = np.zeros((3, t, 2 * t), np.float32)
    for i in range(i_mixed, n_real):
        a_idx, b_idx = n_real - i, n_real - 1 - i
        blk = np.zeros((t, 2 * t), np.float32)
        for j in range(t):
            r = i * t + j
            if r > r_last:
                tau, rho = divmod(big_l - (r + n_meta) + off, t)
                assert tau in (a_idx, b_idx)
                blk[j, (t if tau == b_idx else 0) + rho] = 1.0
        var = 1 if i == i_mixed else 2
        assert var == 1 or not ju[2].any() or (ju[2] == blk).all()
        ju[var] = blk
    ang = lambda prod: 2.0 * np.pi * (prod % big_l).astype(np.float64) / big_l
    pos = np.arange(hp, dtype=np.int64) - off
    d = ang(np.arange(t, dtype=np.int64)[:, None] * pos[None, :])
    k0 = t * np.arange(nf, dtype=np.int64) - off
    row = np.broadcast_to(ang(k0[:, None] * pos[None, :])[:, None, :], (nf, 8, hp))
    f = lambda a: np.ascontiguousarray(a, dtype=np.float32)
    return dict(jf=jf, ju=ju, dc=f(np.cos(d)), ds=f(np.sin(d)), rowc=f(np.cos(row)),
                rows=f(np.sin(row)), nf=nf, hp=hp, off=off, n_real=n_real,
                m_last=half + off, r_last=r_last, i_mixed=i_mixed)


def _seq_dft(pq, pqm, seq, n_meta):
    c = _dft_constants(seq, n_meta)
    t, nf, hp, ncol = DFT_TILE, c["nf"], c["hp"], pq.shape[2]
    ncb = ncol // FNET_WIDTH
    col_blk = lambda rows: pl.BlockSpec((1, rows, FNET_WIDTH), lambda g, b: (g, 0, b))
    peqo = pl.pallas_call(
        functools.partial(_fold_kernel, m_last=c["m_last"]),
        grid=(2, ncb),
        in_specs=[col_blk(seq), col_blk(t), _resident(c["jf"].shape)],
        out_specs=col_blk(hp),
        out_shape=jax.ShapeDtypeStruct((2, hp, ncol), BF16),
        compiler_params=pltpu.CompilerParams(
            dimension_semantics=("arbitrary", "arbitrary"), vmem_limit_bytes=V7X_VMEM_LIMIT),
        name="dft_fold",
    )(pq, jnp.pad(pqm, ((0, 0), (c["off"], 0), (0, 0))), jnp.asarray(c["jf"]).astype(BF16))

    scale = float((seq + n_meta) * FNET_GROUP_DIM) ** -0.5
    tile = pl.BlockSpec((t, ncol), lambda i: (i, 0))
    rowspec = pl.BlockSpec((1, 8, hp), lambda i: (i, 0, 0))
    u, w = pl.pallas_call(
        functools.partial(_dft_kernel, scale=scale),
        grid=(nf,),
        in_specs=[_resident((2, hp, ncol)), _resident((t, hp)), _resident((t, hp)),
                  rowspec, rowspec],
        out_specs=(tile, tile),
        out_shape=(jax.ShapeDtypeStruct((hp, ncol), BF16),) * 2,
        compiler_params=pltpu.CompilerParams(
            dimension_semantics=("arbitrary",), vmem_limit_bytes=V7X_VMEM_LIMIT),
        name="seq_dft",
    )(peqo, jnp.asarray(c["dc"]), jnp.asarray(c["ds"]), jnp.asarray(c["rowc"]),
      jnp.asarray(c["rows"]))

    col = lambda rows: pl.BlockSpec((rows, FNET_WIDTH), lambda b: (0, b))
    return pl.pallas_call(
        functools.partial(_unfold_kernel, r_last=c["r_last"], i_mixed=c["i_mixed"]),
        grid=(ncb,),
        in_specs=[col(hp), col(hp), _resident((2, t, 2 * t))],
        out_specs=col(seq),
        out_shape=jax.ShapeDtypeStruct((seq, ncol), BF16),
        compiler_params=pltpu.CompilerParams(
            dimension_semantics=("arbitrary",), vmem_limit_bytes=V7X_VMEM_LIMIT),
        name="dft_unfold",
    )(u, w, jnp.asarray(c["ju"][1:]).astype(BF16))


FFN_CHUNK = 1024
TAIL_SPLIT = 2
TAIL_TM = 512


def _tail_kernel(x_ref, ol_ref, of_ref, ob_ref, y_ref, nmix_ref, gn_ref, nffn_ref, nfin_ref,
                 wg_ref, wm_ref, wo_ref, wf_ref, wout_ref, w1_ref, w2_ref, o_ref):
    tm = x_ref.shape[0]
    sub = tm // TAIL_SPLIT
    tiles = [(dict(), slice(i * sub, (i + 1) * sub)) for i in range(TAIL_SPLIT)]
    dot = functools.partial(jnp.dot, preferred_element_type=F32)
    nchunk = D_FF // FFN_CHUNK

    def norm_in(s, r):
        s["x"] = x_ref[r, :]
        s["xn"] = _rms(s["x"], nmix_ref[...]).astype(BF16)

    def gates(s, r):
        proj = lambda w_rows: lax.dot_general(s["xn"], w_rows, _NT, preferred_element_type=F32)
        s["g"] = [proj(wg_ref[h * HEAD_DV:(h + 1) * HEAD_DV, :]) for h in range(GLA_HEADS)]
        s["ga"] = proj(wm_ref[0:D_MODEL, :])
        s["gf"] = proj(wm_ref[D_MODEL:2 * D_MODEL, :])
        s["y_fnet"] = dot(y_ref[r, :], wf_ref[...])

    def gla_out(s, r):
        y_gla = None
        for h in range(GLA_HEADS):
            hv = slice(h * HEAD_DV, (h + 1) * HEAD_DV)
            o = (ol_ref[r, hv].astype(F32) + of_ref[r, hv].astype(F32)
                 + ob_ref[r, hv].astype(F32))
            g = s["g"][h]
            og = (_rms(o, gn_ref[...]) * (g * _sigmoid(g))).astype(BF16)
            part = dot(og, wo_ref[hv, :])
            y_gla = part if y_gla is None else y_gla + part
        s["y_gla"] = y_gla

    def merge(s, r):
        merged = _sigmoid(s["ga"]) * s["y_gla"] + _sigmoid(s["gf"]) * s["y_fnet"]
        s["h1"] = s["x"] + dot(merged.astype(BF16), wout_ref[...])
        s["u"] = _rms(s["h1"], nffn_ref[...]).astype(BF16)
        s["acc"] = s["h1"]

    def ffn_up(c):
        def stage(s, r):
            a = jnp.maximum(dot(s["u"], w1_ref[:, c * FFN_CHUNK:(c + 1) * FFN_CHUNK]), 0.0)
            s["a", c] = (a * a).astype(BF16)
        return stage

    def ffn_down(c):
        def stage(s, r):
            s["acc"] = s["acc"] + dot(s.pop(("a", c)),
                                      w2_ref[c * FFN_CHUNK:(c + 1) * FFN_CHUNK, :])
        return stage

    def norm_out(s, r):
        o_ref[r, :] = _rms(s["acc"], nfin_ref[...])

    stages = [norm_in, gates, gla_out, merge, ffn_up(0)]
    for c in range(nchunk):
        if c + 1 < nchunk:
            stages.append(ffn_up(c + 1))
        stages.append(ffn_down(c))
    stages.append(norm_out)
    _run_interleaved(stages, tiles)


def _tail_call(x2d, ol, o_f, o_b, y, nmix, gn, nffn, nfin, w_in_bf, w_merge, wo, wf, wout, w1,
               w2, seq, tm):
    m = x2d.shape[0]
    nt = seq // tm
    row = lambda n: pl.BlockSpec((tm, n), lambda i: (i, 0))
    norms = [nmix, gn, nffn, nfin]
    consts = [w_merge, wo, wf, wout, w1, w2]
    return pl.pallas_call(
        _tail_kernel,
        grid=(m // tm,),
        in_specs=[row(D_MODEL), row(GLA_DV), row(GLA_DV), row(GLA_DV),
                  pl.BlockSpec((tm, FNET_WIDTH), lambda i: (i % nt, i // nt))]
                 + [_resident(c.shape) for c in norms]
                 + [_window((W_WINDOW, w_in_bf.shape[1]), (2, 0))]
                 + [_resident(c.shape) for c in consts],
        out_specs=row(D_MODEL),
        out_shape=jax.ShapeDtypeStruct((m, D_MODEL), F32),
        compiler_params=pltpu.CompilerParams(
            dimension_semantics=("arbitrary",), vmem_limit_bytes=V7X_VMEM_LIMIT),
        name="tail",
    )(x2d, ol, o_f, o_b, y, *norms, w_in_bf, *consts)


@functools.lru_cache(maxsize=None)
def _gla_constants(rows):
    idx = np.arange(rows)
    same = (idx[:, None] // SUB) == (idx[None, :] // SUB)
    tri_f = (same & (idx[None, :] <= idx[:, None])).astype(np.float32)
    return tri_f, np.ascontiguousarray(tri_f.T)


@functools.lru_cache(maxsize=None)
def _channel_dft():
    d = np.arange(FNET_GROUP_DIM, dtype=np.int64)
    ang = 2.0 * np.pi * ((d[:, None] * d[None, :]) % FNET_GROUP_DIM) / FNET_GROUP_DIM
    cs = np.zeros((2 * FNET_GROUP_DIM, 2 * FNET_GROUP_DIM), np.float32)
    cs[F_LEAD:F_LEAD + FNET_GROUP_DIM] = np.concatenate([np.cos(ang), np.sin(ang)], axis=1)
    return cs


def kernel(x, meta_tokens, norm_mix, w_in, w_decay_fwd, b_decay_fwd, w_decay_bwd, b_decay_bwd,
           gla_norm, w_o_gla, w_fnet, w_out, norm_ffn, w_ff1, w_ff2, norm_final):
    batch, seq, d = x.shape
    assert d == D_MODEL and meta_tokens.shape == (N_META, D_MODEL)
    assert w_in.shape[0] == 1 and seq % (2 * SCAN_RB) == 0 and seq % DFT_TILE == 0
    wit = jnp.swapaxes(w_in[0], 0, 1).astype(BF16)
    gate0 = 2 * GLA_DK + 2 * GLA_DV + 2 * DECAY_RANK + FNET_WIDTH
    assert wit.shape[0] == gate0 + 2 * D_MODEL
    w_merge = wit[gate0:]
    wdec = jnp.zeros((R_PAD, 2 * GLA_DK), F32)
    wdec = wdec.at[:DECAY_RANK, :GLA_DK].set(w_decay_fwd[0])
    wdec = wdec.at[DECAY_RANK:2 * DECAY_RANK, GLA_DK:].set(w_decay_bwd[0]).astype(BF16)
    bdec = jnp.concatenate([b_decay_fwd[0], b_decay_bwd[0]])[None, :]
    cs = jnp.asarray(_channel_dft())
    nw_mix = norm_mix[0][None, :]

    x2d = x.reshape(batch * seq, D_MODEL)
    v, ol, qe, ke, bend, pq = _mixin_call(x2d, nw_mix, wit, wdec, bdec, cs,
                                          *_gla_constants(GLA_C), seq, MIX_TM, GLA_C)
    meta_chunk = jnp.pad(meta_tokens, ((SUB - N_META, 0), (0, 0)))
    vm, _, _, kem, _, pqm = _mixin_call(meta_chunk, nw_mix, wit, wdec, bdec, cs,
                                        *_gla_constants(SUB), SUB, SUB, SUB)
    o_f, o_b = _gla_scan_call(qe, ke, v, bend, kem, vm, seq, batch)

    pqm = jnp.tile(pqm[:, SUB - N_META:], (1, 1, batch))
    y = _seq_dft(pq, pqm, seq, N_META)

    out = _tail_call(x2d, ol, o_f, o_b, y, nw_mix, gla_norm[0][None, :], norm_ffn[0][None, :],
                     norm_final[None, :], wit, w_merge, w_o_gla[0].astype(BF16),
                     w_fnet[0].astype(BF16), w_out[0].astype(BF16), w_ff1[0].astype(BF16),
                     w_ff2[0].astype(BF16), seq, TAIL_TM)
    return out.reshape(batch, seq, D_MODEL)
```

```python
import functools

import numpy as np
import jax
import jax.numpy as jnp
from jax import lax
from jax.experimental import pallas as pl
from jax.experimental.pallas import tpu as pltpu

F32 = jnp.float32
BF16 = jnp.bfloat16

D_MODEL = 1024
N_META = 16
GLA_HEADS = 4
GLA_DK = 512
GLA_DV = 1024
HEAD_DK = GLA_DK // GLA_HEADS
HEAD_DV = GLA_DV // GLA_HEADS
DECAY_RANK = 16
GATE_TEMP = 16.0
SUB = 64
FNET_GROUPS = 4
FNET_WIDTH = 512
FNET_GROUP_DIM = FNET_WIDTH // FNET_GROUPS
D_FF = 4 * D_MODEL
EPS = 1e-6
LOG2E = 1.4426950408889634
LANE = 128

_OFF_Q, _OFF_K, _OFF_V, _OFF_QKV_END = 0, 512, 1024, 2048
W_WINDOW = 1024
R_PAD = 128
F_LEAD = 2 * DECAY_RANK
F_EXT = F_LEAD + FNET_WIDTH + (LANE - F_LEAD)

GLA_C = 4 * SUB
MIX_TM = 1024

V7X_VMEM_LIMIT = 56 * 1024 * 1024


def _rms(x, w):
    ms = jnp.mean(x * x, axis=-1, keepdims=True)
    return x * lax.rsqrt(ms + EPS) * w


def _sigmoid(x):
    return 1.0 / (1.0 + jnp.exp(-x))


def _run_interleaved(stages, tiles):
    for stage in stages:
        for state, rows in tiles:
            stage(state, rows)


def _window(block, index):
    return pl.BlockSpec(block, lambda *_: index, pipeline_mode=pl.Buffered(1))


def _resident(shape):
    return _window(shape, (0,) * len(shape))


_NT = (((1,), (1,)), ((), ()))
_TN = (((0,), (0,)), ((), ()))


def _mixin_kernel(x_ref, nw_ref, wqkv_ref, wrf_ref, wdec_ref, bdec_ref, cs_ref,
                  trif_ref, trib_ref, maskf_ref, maskb_ref, *refs, chunk, n_cast):
    cast_in, cast_out = refs[:n_cast], refs[n_cast + 6:]
    v_ref, ol_ref, qe_ref, ke_ref, bend_ref, pq_ref = refs[n_cast:n_cast + 6]
    _mixin_body(x_ref, nw_ref, wqkv_ref, wrf_ref, wdec_ref, bdec_ref, cs_ref, trif_ref, trib_ref,
                maskf_ref, maskb_ref, v_ref, ol_ref, qe_ref, ke_ref, bend_ref, pq_ref, chunk=chunk)
    for src, dst in zip(cast_in, cast_out):
        dst[...] = src[...].astype(BF16)


def _mixin_body(x_ref, nw_ref, wqkv_ref, wrf_ref, wdec_ref, bdec_ref, cs_ref,
                trif_ref, trib_ref, maskf_ref, maskb_ref,
                v_ref, ol_ref, qe_ref, ke_ref, bend_ref, pq_ref, *, chunk):
    tm = x_ref.shape[0]
    n, s = chunk // SUB, SUB
    tiles = [(dict(), slice(i * chunk, (i + 1) * chunk)) for i in range(tm // chunk)]
    dot = functools.partial(jnp.dot, preferred_element_type=F32)
    cs = cs_ref[...].astype(BF16)
    tris = (trif_ref[...], trib_ref[...])
    masks = (maskf_ref[0:s, 0:s] != 0.0, maskb_ref[0:s, 0:s] != 0.0)
    half = GLA_DK // 2

    def norm_in(st, r):
        st["xn"] = _rms(x_ref[r, :], nw_ref[...]).astype(BF16)

    def decay_rank(st, r):
        st["f"] = dot(st["xn"], wrf_ref[:, 0:F_EXT]).astype(BF16)
        st["r"] = st["f"][:, 0:R_PAD]

    def log_decay(st, j):
        cols = slice(j * half, (j + 1) * half)
        x = dot(st["r"], wdec_ref[:, cols]) + bdec_ref[:, cols]
        la = (jnp.minimum(x, 0.0) - jnp.log(1.0 + jnp.exp(-jnp.abs(x)))) * (LOG2E / GATE_TEMP)
        hi = la.astype(BF16)
        st["la", j] = (hi, (la - hi.astype(F32)).astype(BF16))

    def project(st, r):
        mm = lambda lo, hi: dot(st["xn"], wqkv_ref[:, lo:hi])
        vmid = (_OFF_V + _OFF_QKV_END) // 2
        log_decay(st, 0)
        st["q"] = mm(_OFF_Q, _OFF_K) * (HEAD_DK ** -0.5)
        log_decay(st, 1)
        st["k"] = mm(_OFF_K, _OFF_V)
        log_decay(st, 2)
        v_ref[r, 0:GLA_DV // 2] = mm(_OFF_V, vmid).astype(BF16)
        log_decay(st, 3)
        v_ref[r, GLA_DV // 2:GLA_DV] = mm(vmid, _OFF_QKV_END).astype(BF16)

    def decay_factors(st, r):
        c0 = r.start // s
        for j in range(4):
            d, qcols = j // 2, slice((j % 2) * half, (j % 2 + 1) * half)
            cols = slice(j * half, (j + 1) * half)
            hi, lo = st.pop(("la", j))
            b = (dot(tris[d], hi) + dot(tris[d], lo)).reshape(n, s, half)
            if d == 0:
                b_mid, b_end = b[:, s // 2 - 1:s // 2, :], b[:, s - 1:s, :]
            else:
                b_mid, b_end = b[:, s // 2:s // 2 + 1, :], b[:, 0:1, :]
            qd = st["q"][:, qcols].reshape(n, s, half) * jnp.exp2(b - b_mid)
            kd = st["k"][:, qcols].reshape(n, s, half) * jnp.exp2(b_mid - b)
            qe = qd * jnp.exp2(b_mid)
            ke = kd * jnp.exp2(b_end - b_mid)
            st["qd", d, j % 2] = qd.reshape(chunk, half).astype(BF16)
            st["kd", d, j % 2] = kd.reshape(chunk, half).astype(BF16)
            qe_ref[r, cols] = qe.reshape(chunk, half).astype(BF16)
            ke_ref[r, cols] = ke.reshape(chunk, half).astype(BF16)
            bend_ref[c0:c0 + n, cols] = b_end.reshape(n, half)

    def intra_chunk(st, r):
        for h in range(GLA_HEADS):
            hk = slice((h % 2) * HEAD_DK, (h % 2 + 1) * HEAD_DK)
            hv = slice(h * HEAD_DV, (h + 1) * HEAD_DV)
            sc = [[lax.dot_general(st["qd", d, h // 2][c * s:(c + 1) * s, hk],
                                   st["kd", d, h // 2][c * s:(c + 1) * s, hk], _NT,
                                   preferred_element_type=F32) for d in range(2)]
                  for c in range(n)]
            for c, (sf, sb) in enumerate(sc):
                rows = slice(r.start + c * s, r.start + (c + 1) * s)
                both = (jnp.where(masks[0], sf, 0.0) + jnp.where(masks[1], sb, 0.0)).astype(BF16)
                ol_ref[rows, hv] = dot(both, v_ref[rows, hv]).astype(BF16)

    def channel_dft(st, r):
        for grp in range(FNET_GROUPS):
            sl = slice(grp * FNET_GROUP_DIM, (grp + 1) * FNET_GROUP_DIM)
            pq = dot(st["f"][:, grp * FNET_GROUP_DIM:(grp + 2) * FNET_GROUP_DIM], cs)
            pq_ref[0, r, sl] = pq[:, :FNET_GROUP_DIM].astype(BF16)
            pq_ref[1, r, sl] = pq[:, FNET_GROUP_DIM:].astype(BF16)

    _run_interleaved((norm_in, decay_rank, project, decay_factors, intra_chunk, channel_dft),
                     tiles)


def _mixin_call(x2d, nw, w_in_bf, wdec, bdec, cs, tri_f, tri_b, rows_per_batch, tm, chunk,
                cast=()):
    m = x2d.shape[0]
    d_model = w_in_bf.shape[0]
    nt = rows_per_batch // tm
    nb = m // rows_per_batch
    steps = m // tm
    slab = lambda w: pl.BlockSpec((w.shape[0] // steps, w.shape[1]), lambda i: (i, 0))
    assert all(w.shape[0] % (16 * steps) == 0 for w in cast)
    row = lambda n: pl.BlockSpec((tm, n), lambda i: (i, 0))
    pq_spec = pl.BlockSpec((2, tm, FNET_WIDTH), lambda i: (0, i % nt, i // nt))
    wide = jax.ShapeDtypeStruct((m, 2 * GLA_DK), BF16)
    pq_shape = jax.ShapeDtypeStruct((2, rows_per_batch, nb * FNET_WIDTH), BF16)
    consts = [wdec, bdec, cs,
              jnp.asarray(tri_f).astype(BF16), jnp.asarray(tri_b).astype(BF16),
              jnp.asarray(tri_f), jnp.asarray(tri_b)]
    return pl.pallas_call(
        functools.partial(_mixin_kernel, chunk=chunk, n_cast=len(cast)),
        grid=(steps,),
        in_specs=[row(D_MODEL), _resident(nw.shape),
                  _window((d_model, _OFF_QKV_END), (0, 0)),
                  _window((d_model, W_WINDOW), (0, 3))]
                 + [_resident(c.shape) for c in consts] + [slab(w) for w in cast],
        out_specs=(row(GLA_DV), row(GLA_DV), row(2 * GLA_DK), row(2 * GLA_DK),
                   pl.BlockSpec((tm // SUB, 2 * GLA_DK), lambda i: (i, 0)), pq_spec)
                  + tuple(slab(w) for w in cast),
        out_shape=(jax.ShapeDtypeStruct((m, GLA_DV), BF16), jax.ShapeDtypeStruct((m, GLA_DV), BF16),
                   wide, wide, jax.ShapeDtypeStruct((m // SUB, 2 * GLA_DK), F32), pq_shape)
                  + tuple(jax.ShapeDtypeStruct(w.shape, BF16) for w in cast),
        compiler_params=pltpu.CompilerParams(
            dimension_semantics=("arbitrary",), vmem_limit_bytes=V7X_VMEM_LIMIT),
        name="mixin",
    )(x2d, nw, w_in_bf, w_in_bf, *consts, *cast)


SCAN_G = 512
SCAN_RB = 4096


def _scan_group(dirs):
    s, n = SUB, SCAN_G // SUB
    ahead = 2
    dec, st, inter = [], [], [{}, {}]
    for qe_ref, ke_ref, v_ref, bend_ref, st_ref, row0, chunk0, backward in dirs:
        dec_rows = jnp.tile(jnp.exp2(bend_ref[chunk0:chunk0 + n, :]), (HEAD_DK // n, 1))
        dec.append(dec_rows.T)
        st.append(st_ref[...])

    def order(step, backward):
        return n - 1 - step if backward else step

    def increment(d, step):
        _, ke_ref, v_ref, _, _, row0, _, backward = dirs[d]
        i = order(step, backward)
        sub = slice(row0 + i * s, row0 + (i + 1) * s)
        return lax.dot_general(ke_ref[sub, :], v_ref[sub, :], _TN,
                               preferred_element_type=F32)

    inc = {(d, step): increment(d, step) for step in range(ahead) for d in range(2)}
    for step in range(n):
        for d, (qe_ref, _, _, _, _, row0, _, backward) in enumerate(dirs):
            i = order(step, backward)
            sub = slice(row0 + i * s, row0 + (i + 1) * s)
            inter[d][i] = jnp.dot(qe_ref[sub, :], st[d].astype(BF16),
                                  preferred_element_type=F32)
            if step + ahead < n:
                inc[d, step + ahead] = increment(d, step + ahead)
            decay = jnp.broadcast_to(dec[d][:, i:i + 1], (HEAD_DK, HEAD_DV))
            st[d] = st[d] * decay + inc.pop((d, step))
    for d, spec in enumerate(dirs):
        spec[4][...] = st[d]
    return [jnp.concatenate([inter[d][i] for i in range(n)], axis=0) for d in range(2)]


def _gla_scan_kernel(qef_ref, kef_ref, vf_ref, bendf_ref,
                     qeb_ref, keb_ref, vb_ref, bendb_ref, kem_ref, vm_ref,
                     of_ref, ob_ref, stf_ref, stb_ref):
    rb = qef_ref.shape[0]
    ng = rb // SCAN_G

    @pl.when(pl.program_id(2) == 0)
    def _():
        stf_ref[...] = lax.dot_general(kem_ref[...], vm_ref[...], _TN,
                                       preferred_element_type=F32)
        stb_ref[...] = jnp.zeros_like(stb_ref)

    for j in range(ng):
        jb = ng - 1 - j
        cpg = SCAN_G // SUB
        in_f, in_b = _scan_group([
            (qef_ref, kef_ref, vf_ref, bendf_ref, stf_ref, j * SCAN_G, j * cpg, False),
            (qeb_ref, keb_ref, vb_ref, bendb_ref, stb_ref, jb * SCAN_G, jb * cpg, True)])
        of_ref[j * SCAN_G:(j + 1) * SCAN_G, :] = in_f.astype(BF16)
        ob_ref[jb * SCAN_G:(jb + 1) * SCAN_G, :] = in_b.astype(BF16)


def _gla_scan_call(qe, ke, v, bend, kem, vm, seq, batch):
    h, rb = GLA_HEADS, SCAN_RB
    nt = seq // rb
    fwd = lambda b, hh, t: b * nt + t
    bwd = lambda b, hh, t: b * nt + nt - 1 - t
    cb = rb // SUB

    def dk_blk(rowfn, dircol, rows):
        return pl.BlockSpec((rows, HEAD_DK), lambda b, hh, t: (rowfn(b, hh, t), dircol + hh))

    def dv_blk(rowfn):
        return pl.BlockSpec((rb, HEAD_DV), lambda b, hh, t: (rowfn(b, hh, t), hh))

    in_specs = [
        dk_blk(fwd, 0, rb), dk_blk(fwd, 0, rb), dv_blk(fwd), dk_blk(fwd, 0, cb),
        dk_blk(bwd, h, rb), dk_blk(bwd, h, rb), dv_blk(bwd), dk_blk(bwd, h, cb),
        pl.BlockSpec((SUB, HEAD_DK), lambda b, hh, t: (0, hh)),
        pl.BlockSpec((SUB, HEAD_DV), lambda b, hh, t: (0, hh)),
    ]
    out = jax.ShapeDtypeStruct((batch * seq, GLA_DV), BF16)
    return pl.pallas_call(
        _gla_scan_kernel,
        grid=(batch, h, nt),
        in_specs=in_specs,
        out_specs=(dv_blk(fwd), dv_blk(bwd)),
        out_shape=(out, out),
        scratch_shapes=[pltpu.VMEM((HEAD_DK, HEAD_DV), F32), pltpu.VMEM((HEAD_DK, HEAD_DV), F32)],
        compiler_params=pltpu.CompilerParams(
            dimension_semantics=("arbitrary", "arbitrary", "arbitrary"),
            vmem_limit_bytes=V7X_VMEM_LIMIT),
        name="gla_scan",
    )(qe, ke, v, bend, qe, ke, v, bend, kem, vm)


DFT_TILE = 256


def _fold_kernel(x_ref, z_ref, j_ref, o_ref, *, m_last):
    t = DFT_TILE
    n_real = x_ref.shape[1] // t
    sign = jnp.where(pl.program_id(0) == 0, 1.0, -1.0)
    real = lambda i: x_ref[0, i * t:(i + 1) * t, :]
    for i in range(o_ref.shape[1] // t):
        direct = (z_ref[0] if i == 0 else real(i - 1)).astype(F32)
        mirror = (jnp.dot(j_ref[i, :, :t], real(min(n_real - i, n_real - 1)),
                          preferred_element_type=F32)
                  + jnp.dot(j_ref[i, :, t:], real(n_real - 1 - i), preferred_element_type=F32))
        m = i * t + lax.broadcasted_iota(jnp.int32, (t, 1), 0)
        folded = jnp.where(m <= m_last, direct + sign * mirror, 0.0)
        o_ref[0, i * t:(i + 1) * t, :] = folded.astype(BF16)


def _dft_kernel(peqo_ref, dc_ref, ds_ref, rowc_ref, rows_ref, u_ref, w_ref, *, scale):
    t = DFT_TILE
    acc_a = acc_b = None
    for c in range(peqo_ref.shape[1] // t):
        sl = slice(c * t, (c + 1) * t)
        rc, rs = rowc_ref[0, 0:1, sl], rows_ref[0, 0:1, sl]
        dc, ds = dc_ref[:, sl], ds_ref[:, sl]
        tc = (dc * rc - ds * rs).astype(BF16)
        ts = (ds * rc + dc * rs).astype(BF16)
        a = jnp.dot(tc, peqo_ref[0, sl, :], preferred_element_type=F32)
        b = jnp.dot(ts, peqo_ref[1, sl, :], preferred_element_type=F32)
        acc_a = a if acc_a is None else acc_a + a
        acc_b = b if acc_b is None else acc_b + b
    u_ref[...] = ((acc_a - acc_b) * scale).astype(BF16)
    w_ref[...] = ((acc_a + acc_b) * scale).astype(BF16)


def _unfold_kernel(u_ref, w_ref, j_ref, y_ref, *, r_last, i_mixed):
    t = DFT_TILE
    tile = lambda ref, i: ref[i * t:(i + 1) * t, :]
    n_real = y_ref.shape[0] // t
    for i in range(n_real):
        if i < i_mixed:
            y_ref[i * t:(i + 1) * t, :] = tile(u_ref, i + 1)
            continue
        j = 0 if i == i_mixed else 1
        mirror = jnp.dot(j_ref[j, :, t:], tile(w_ref, n_real - 1 - i), preferred_element_type=F32)
        if (n_real - i + 1) * t <= w_ref.shape[0]:
            mirror += jnp.dot(j_ref[j, :, :t], tile(w_ref, n_real - i),
                              preferred_element_type=F32)
        mirror = mirror.astype(BF16)
        if i == i_mixed:
            r = i * t + lax.broadcasted_iota(jnp.int32, (t, 1), 0)
            mirror = jnp.where(r <= r_last, tile(u_ref, i + 1), mirror)
        y_ref[i * t:(i + 1) * t, :] = mirror


@functools.lru_cache(maxsize=None)
def _dft_constants(seq, n_meta):
    t = DFT_TILE
    off = t - n_meta
    big_l = seq + n_meta
    half = big_l // 2
    n_real = seq // t
    nf = -(-(half + 1 + off) // t)
    hp = nf * t
    jf = np.zeros((nf, t, 2 * t), np.float32)
    for i in range(nf):
        a_idx, b_idx = min(n_real - i, n_real - 1), n_real - 1 - i
        for j in range(t):
            pos = i * t + j - off
            if 1 <= pos <= half - 1:
                tau, rho = divmod(big_l - pos - n_meta, t)
                assert tau in (a_idx, b_idx)
                jf[i, j, (t if tau == b_idx else 0) + rho] = 1.0
    r_last = half - n_meta
    i_mixed = r_last // t
    ju = np.zeros((3, t, 2 * t), np.float32)
    for i in range(i_mixed, n_real):
        a_idx, b_idx = n_real - i, n_real - 1 - i
        blk = np.zeros((t, 2 * t), np.float32)
        for j in range(t):
            r = i * t + j
            if r > r_last:
                tau, rho = divmod(big_l - (r + n_meta) + off, t)
                assert tau in (a_idx, b_idx)
                blk[j, (t if tau == b_idx else 0) + rho] = 1.0
        var = 1 if i == i_mixed else 2
        assert var == 1 or not ju[2].any() or (ju[2] == blk).all()
        ju[var] = blk
    ang = lambda prod: 2.0 * np.pi * (prod % big_l).astype(np.float64) / big_l
    pos = np.arange(hp, dtype=np.int64) - off
    d = ang(np.arange(t, dtype=np.int64)[:, None] * pos[None, :])
    k0 = t * np.arange(nf, dtype=np.int64) - off
    row = np.broadcast_to(ang(k0[:, None] * pos[None, :])[:, None, :], (nf, 8, hp))
    f = lambda a: np.ascontiguousarray(a, dtype=np.float32)
    return dict(jf=jf, ju=ju, dc=f(np.cos(d)), ds=f(np.sin(d)), rowc=f(np.cos(row)),
                rows=f(np.sin(row)), nf=nf, hp=hp, off=off, n_real=n_real,
                m_last=half + off, r_last=r_last, i_mixed=i_mixed)


def _seq_dft(pq, pqm, seq, n_meta):
    c = _dft_constants(seq, n_meta)
    t, nf, hp, ncol = DFT_TILE, c["nf"], c["hp"], pq.shape[2]
    ncb = ncol // FNET_WIDTH
    col_blk = lambda rows: pl.BlockSpec((1, rows, FNET_WIDTH), lambda g, b: (g, 0, b))
    peqo = pl.pallas_call(
        functools.partial(_fold_kernel, m_last=c["m_last"]),
        grid=(2, ncb),
        in_specs=[col_blk(seq), col_blk(t), _resident(c["jf"].shape)],
        out_specs=col_blk(hp),
        out_shape=jax.ShapeDtypeStruct((2, hp, ncol), BF16),
        compiler_params=pltpu.CompilerParams(
            dimension_semantics=("arbitrary", "arbitrary"), vmem_limit_bytes=V7X_VMEM_LIMIT),
        name="dft_fold",
    )(pq, jnp.pad(pqm, ((0, 0), (c["off"], 0), (0, 0))), jnp.asarray(c["jf"]).astype(BF16))

    scale = float((seq + n_meta) * FNET_GROUP_DIM) ** -0.5
    tile = pl.BlockSpec((t, ncol), lambda i: (i, 0))
    rowspec = pl.BlockSpec((1, 8, hp), lambda i: (i, 0, 0))
    u, w = pl.pallas_call(
        functools.partial(_dft_kernel, scale=scale),
        grid=(nf,),
        in_specs=[_resident((2, hp, ncol)), _resident((t, hp)), _resident((t, hp)),
                  rowspec, rowspec],
        out_specs=(tile, tile),
        out_shape=(jax.ShapeDtypeStruct((hp, ncol), BF16),) * 2,
        compiler_params=pltpu.CompilerParams(
            dimension_semantics=("arbitrary",), vmem_limit_bytes=V7X_VMEM_LIMIT),
        name="seq_dft",
    )(peqo, jnp.asarray(c["dc"]), jnp.asarray(c["ds"]), jnp.asarray(c["rowc"]),
      jnp.asarray(c["rows"]))

    col = lambda rows: pl.BlockSpec((rows, FNET_WIDTH), lambda b: (0, b))
    return pl.pallas_call(
        functools.partial(_unfold_kernel, r_last=c["r_last"], i_mixed=c["i_mixed"]),
        grid=(ncb,),
        in_specs=[col(hp), col(hp), _resident((2, t, 2 * t))],
        out_specs=col(seq),
        out_shape=jax.ShapeDtypeStruct((seq, ncol), BF16),
        compiler_params=pltpu.CompilerParams(
            dimension_semantics=("arbitrary",), vmem_limit_bytes=V7X_VMEM_LIMIT),
        name="dft_unfold",
    )(u, w, jnp.asarray(c["ju"][1:]).astype(BF16))


FFN_CHUNK = 1024
TAIL_SPLIT = 2
TAIL_TM = 512


def _tail_kernel(x_ref, ol_ref, of_ref, ob_ref, y_ref, nmix_ref, gn_ref, nffn_ref, nfin_ref,
                 wg_ref, wm_ref, wo_ref, wf_ref, wout_ref, w1_ref, w2_ref, o_ref):
    tm = x_ref.shape[0]
    sub = tm // TAIL_SPLIT
    tiles = [(dict(), slice(i * sub, (i + 1) * sub)) for i in range(TAIL_SPLIT)]
    dot = functools.partial(jnp.dot, preferred_element_type=F32)
    nchunk = D_FF // FFN_CHUNK

    def norm_in(s, r):
        s["x"] = x_ref[r, :]
        s["xn"] = _rms(s["x"], nmix_ref[...]).astype(BF16)

    def gates(s, r):
        s["g"] = [dot(s["xn"], wg_ref[:, h * HEAD_DV:(h + 1) * HEAD_DV])
                  for h in range(GLA_HEADS)]
        s["ga"] = dot(s["xn"], wm_ref[:, 0:D_MODEL])
        s["gf"] = dot(s["xn"], wm_ref[:, D_MODEL:2 * D_MODEL])
        s["y_fnet"] = dot(y_ref[r, :], wf_ref[...])

    def gla_out(s, r):
        y_gla = None
        for h in range(GLA_HEADS):
            hv = slice(h * HEAD_DV, (h + 1) * HEAD_DV)
            o = (ol_ref[r, hv].astype(F32) + of_ref[r, hv].astype(F32)
                 + ob_ref[r, hv].astype(F32))
            g = s["g"][h]
            og = (_rms(o, gn_ref[...]) * (g * _sigmoid(g))).astype(BF16)
            part = dot(og, wo_ref[hv, :])
            y_gla = part if y_gla is None else y_gla + part
        s["y_gla"] = y_gla

    def merge(s, r):
        merged = _sigmoid(s["ga"]) * s["y_gla"] + _sigmoid(s["gf"]) * s["y_fnet"]
        s["h1"] = s["x"] + dot(merged.astype(BF16), wout_ref[...])
        s["u"] = _rms(s["h1"], nffn_ref[...]).astype(BF16)
        s["acc"] = s["h1"]

    def ffn_up(c):
        def stage(s, r):
            a = jnp.maximum(dot(s["u"], w1_ref[:, c * FFN_CHUNK:(c + 1) * FFN_CHUNK]), 0.0)
            s["a", c] = (a * a).astype(BF16)
        return stage

    def ffn_down(c):
        def stage(s, r):
            s["acc"] = s["acc"] + dot(s.pop(("a", c)),
                                      w2_ref[c * FFN_CHUNK:(c + 1) * FFN_CHUNK, :])
        return stage

    def norm_out(s, r):
        o_ref[r, :] = _rms(s["acc"], nfin_ref[...])

    stages = [norm_in, gates, gla_out, merge, ffn_up(0)]
    for c in range(nchunk):
        if c + 1 < nchunk:
            stages.append(ffn_up(c + 1))
        stages.append(ffn_down(c))
    stages.append(norm_out)
    _run_interleaved(stages, tiles)


def _tail_call(x2d, ol, o_f, o_b, y, nmix, gn, nffn, nfin, w_in_bf, w_merge, wo, wf, wout, w1,
               w2, seq, tm):
    m = x2d.shape[0]
    nt = seq // tm
    row = lambda n: pl.BlockSpec((tm, n), lambda i: (i, 0))
    norms = [nmix, gn, nffn, nfin]
    consts = [w_merge, wo, wf, wout, w1, w2]
    return pl.pallas_call(
        _tail_kernel,
        grid=(m // tm,),
        in_specs=[row(D_MODEL), row(GLA_DV), row(GLA_DV), row(GLA_DV),
                  pl.BlockSpec((tm, FNET_WIDTH), lambda i: (i % nt, i // nt))]
                 + [_resident(c.shape) for c in norms]
                 + [_window((w_in_bf.shape[0], W_WINDOW), (0, 2))]
                 + [_resident(c.shape) for c in consts],
        out_specs=row(D_MODEL),
        out_shape=jax.ShapeDtypeStruct((m, D_MODEL), F32),
        compiler_params=pltpu.CompilerParams(
            dimension_semantics=("arbitrary",), vmem_limit_bytes=V7X_VMEM_LIMIT),
        name="tail",
    )(x2d, ol, o_f, o_b, y, *norms, w_in_bf, *consts)


@functools.lru_cache(maxsize=None)
def _gla_constants(rows):
    idx = np.arange(rows)
    same = (idx[:, None] // SUB) == (idx[None, :] // SUB)
    tri_f = (same & (idx[None, :] <= idx[:, None])).astype(np.float32)
    return tri_f, np.ascontiguousarray(tri_f.T)


@functools.lru_cache(maxsize=None)
def _channel_dft():
    d = np.arange(FNET_GROUP_DIM, dtype=np.int64)
    ang = 2.0 * np.pi * ((d[:, None] * d[None, :]) % FNET_GROUP_DIM) / FNET_GROUP_DIM
    cs = np.zeros((2 * FNET_GROUP_DIM, 2 * FNET_GROUP_DIM), np.float32)
    cs[F_LEAD:F_LEAD + FNET_GROUP_DIM] = np.concatenate([np.cos(ang), np.sin(ang)], axis=1)
    return cs


def kernel(x, meta_tokens, norm_mix, w_in, w_decay_fwd, b_decay_fwd, w_decay_bwd, b_decay_bwd,
           gla_norm, w_o_gla, w_fnet, w_out, norm_ffn, w_ff1, w_ff2, norm_final):
    batch, seq, d = x.shape
    assert d == D_MODEL and meta_tokens.shape == (N_META, D_MODEL)
    assert w_in.shape[0] == 1 and seq % (2 * SCAN_RB) == 0 and seq % DFT_TILE == 0
    wi = w_in[0].astype(BF16)
    gate0 = 2 * GLA_DK + 2 * GLA_DV + 2 * DECAY_RANK + FNET_WIDTH
    assert wi.shape[1] == gate0 + 2 * D_MODEL
    w_merge = wi[:, gate0:]
    wdec = jnp.zeros((R_PAD, 2 * GLA_DK), F32)
    wdec = wdec.at[:DECAY_RANK, :GLA_DK].set(w_decay_fwd[0])
    wdec = wdec.at[DECAY_RANK:2 * DECAY_RANK, GLA_DK:].set(w_decay_bwd[0]).astype(BF16)
    bdec = jnp.concatenate([b_decay_fwd[0], b_decay_bwd[0]])[None, :]
    cs = jnp.asarray(_channel_dft())
    nw_mix = norm_mix[0][None, :]

    x2d = x.reshape(batch * seq, D_MODEL)
    tail_weights = (w_o_gla[0], w_fnet[0], w_out[0], w_ff1[0], w_ff2[0])
    v, ol, qe, ke, bend, pq, *tail_weights = _mixin_call(
        x2d, nw_mix, wi, wdec, bdec, cs, *_gla_constants(GLA_C), seq, MIX_TM, GLA_C,
        cast=tail_weights)
    meta_chunk = jnp.pad(meta_tokens, ((SUB - N_META, 0), (0, 0)))
    vm, _, _, kem, _, pqm = _mixin_call(meta_chunk, nw_mix, wi, wdec, bdec, cs,
                                        *_gla_constants(SUB), SUB, SUB, SUB)
    o_f, o_b = _gla_scan_call(qe, ke, v, bend, kem, vm, seq, batch)

    pqm = jnp.tile(pqm[:, SUB - N_META:], (1, 1, batch))
    y = _seq_dft(pq, pqm, seq, N_META)

    out = _tail_call(x2d, ol, o_f, o_b, y, nw_mix, gla_norm[0][None, :], norm_ffn[0][None, :],
                     norm_final[None, :], wi, w_merge, *tail_weights, seq, TAIL_TM)
    return out.reshape(batch, seq, D_MODEL)
```

```python
import functools

import numpy as np
import jax
import jax.numpy as jnp
from jax import lax
from jax.experimental import pallas as pl
from jax.experimental.pallas import tpu as pltpu

F32 = jnp.float32
BF16 = jnp.bfloat16

D_MODEL = 1024
N_META = 16
GLA_HEADS = 4
GLA_DK = 512
GLA_DV = 1024
HEAD_DK = GLA_DK // GLA_HEADS
HEAD_DV = GLA_DV // GLA_HEADS
DECAY_RANK = 16
GATE_TEMP = 16.0
SUB = 64
FNET_GROUPS = 4
FNET_WIDTH = 512
FNET_GROUP_DIM = FNET_WIDTH // FNET_GROUPS
D_FF = 4 * D_MODEL
EPS = 1e-6
LOG2E = 1.4426950408889634
LANE = 128

_OFF_Q, _OFF_K, _OFF_V, _OFF_QKV_END = 0, 512, 1024, 2048
W_WINDOW = 1024
R_PAD = 128
F_LEAD = 2 * DECAY_RANK
F_EXT = F_LEAD + FNET_WIDTH + (LANE - F_LEAD)

GLA_C = 4 * SUB
MIX_TM = 1024
GATE_SUB = 32

V7X_VMEM_LIMIT = 58 * 1024 * 1024


def _rms(x, w):
    ms = jnp.mean(x * x, axis=-1, keepdims=True)
    return x * lax.rsqrt(ms + EPS) * w


def _sigmoid(x):
    return 1.0 / (1.0 + jnp.exp(-x))


def _run_interleaved(stages, tiles):
    for stage in stages:
        for state, rows in tiles:
            stage(state, rows)


def _window(block, index):
    return pl.BlockSpec(block, lambda *_: index, pipeline_mode=pl.Buffered(1))


def _resident(shape):
    return _window(shape, (0,) * len(shape))


_NT = (((1,), (1,)), ((), ()))
_TN = (((0,), (0,)), ((), ()))


def _mixin_kernel(x_ref, nw_ref, wqkv_ref, wrf_ref, wdec_ref, bdec_ref, cs_ref,
                  trif_ref, trib_ref, maskf_ref, maskb_ref, *refs, chunk, n_cast, n_gate):
    n_in = n_cast + n_gate
    cast_in, gate_in = refs[:n_cast], refs[n_cast:n_in]
    v_ref, ol_ref, qe_ref, ke_ref, bend_ref, pq_ref = refs[n_in:n_in + 6]
    cast_out = refs[n_in + 6:n_in + 6 + n_cast]
    _mixin_body(x_ref, nw_ref, wqkv_ref, wrf_ref, wdec_ref, bdec_ref, cs_ref, trif_ref, trib_ref,
                maskf_ref, maskb_ref, v_ref, ol_ref, qe_ref, ke_ref, bend_ref, pq_ref, chunk=chunk)
    for src, dst in zip(cast_in, cast_out):
        dst[...] = src[...].astype(BF16)
    if n_gate:
        feat_major = jnp.concatenate([r[...] for r in gate_in], axis=0)
        refs[-1][...] = feat_major.T.astype(BF16)


def _mixin_body(x_ref, nw_ref, wqkv_ref, wrf_ref, wdec_ref, bdec_ref, cs_ref,
                trif_ref, trib_ref, maskf_ref, maskb_ref,
                v_ref, ol_ref, qe_ref, ke_ref, bend_ref, pq_ref, *, chunk):
    tm = x_ref.shape[0]
    n, s = chunk // SUB, SUB
    tiles = [(dict(), slice(i * chunk, (i + 1) * chunk)) for i in range(tm // chunk)]
    dot = functools.partial(jnp.dot, preferred_element_type=F32)
    cs = cs_ref[...].astype(BF16)
    tris = (trif_ref[...], trib_ref[...])
    masks = (maskf_ref[0:s, 0:s] != 0.0, maskb_ref[0:s, 0:s] != 0.0)
    half = GLA_DK // 2

    def norm_in(st, r):
        st["xn"] = _rms(x_ref[r, :], nw_ref[...]).astype(BF16)

    def decay_rank(st, r):
        st["f"] = dot(st["xn"], wrf_ref[:, 0:F_EXT]).astype(BF16)
        st["r"] = st["f"][:, 0:R_PAD]

    def log_decay(st, j):
        cols = slice(j * half, (j + 1) * half)
        x = dot(st["r"], wdec_ref[:, cols]) + bdec_ref[:, cols]
        la = (jnp.minimum(x, 0.0) - jnp.log(1.0 + jnp.exp(-jnp.abs(x)))) * (LOG2E / GATE_TEMP)
        hi = la.astype(BF16)
        st["la", j] = (hi, (la - hi.astype(F32)).astype(BF16))

    def project(st, r):
        mm = lambda lo, hi: dot(st["xn"], wqkv_ref[:, lo:hi])
        vmid = (_OFF_V + _OFF_QKV_END) // 2
        log_decay(st, 0)
        st["q"] = mm(_OFF_Q, _OFF_K) * (HEAD_DK ** -0.5)
        log_decay(st, 1)
        st["k"] = mm(_OFF_K, _OFF_V)
        log_decay(st, 2)
        v_ref[r, 0:GLA_DV // 2] = mm(_OFF_V, vmid).astype(BF16)
        log_decay(st, 3)
        v_ref[r, GLA_DV // 2:GLA_DV] = mm(vmid, _OFF_QKV_END).astype(BF16)

    def decay_factors(st, r):
        c0 = r.start // s
        for j in range(4):
            d, qcols = j // 2, slice((j % 2) * half, (j % 2 + 1) * half)
            cols = slice(j * half, (j + 1) * half)
            hi, lo = st.pop(("la", j))
            b = (dot(tris[d], hi) + dot(tris[d], lo)).reshape(n, s, half)
            if d == 0:
                b_mid, b_end = b[:, s // 2 - 1:s // 2, :], b[:, s - 1:s, :]
            else:
                b_mid, b_end = b[:, s // 2:s // 2 + 1, :], b[:, 0:1, :]
            qd = st["q"][:, qcols].reshape(n, s, half) * jnp.exp2(b - b_mid)
            kd = st["k"][:, qcols].reshape(n, s, half) * jnp.exp2(b_mid - b)
            qe = qd * jnp.exp2(b_mid)
            ke = kd * jnp.exp2(b_end - b_mid)
            st["qd", d, j % 2] = qd.reshape(chunk, half).astype(BF16)
            st["kd", d, j % 2] = kd.reshape(chunk, half).astype(BF16)
            qe_ref[r, cols] = qe.reshape(chunk, half).astype(BF16)
            ke_ref[r, cols] = ke.reshape(chunk, half).astype(BF16)
            bend_ref[c0:c0 + n, cols] = b_end.reshape(n, half)

    def intra_chunk(st, r):
        for h in range(GLA_HEADS):
            hk = slice((h % 2) * HEAD_DK, (h % 2 + 1) * HEAD_DK)
            hv = slice(h * HEAD_DV, (h + 1) * HEAD_DV)
            sc = [[lax.dot_general(st["qd", d, h // 2][c * s:(c + 1) * s, hk],
                                   st["kd", d, h // 2][c * s:(c + 1) * s, hk], _NT,
                                   preferred_element_type=F32) for d in range(2)]
                  for c in range(n)]
            for c, (sf, sb) in enumerate(sc):
                rows = slice(r.start + c * s, r.start + (c + 1) * s)
                both = (jnp.where(masks[0], sf, 0.0) + jnp.where(masks[1], sb, 0.0)).astype(BF16)
                ol_ref[rows, hv] = dot(both, v_ref[rows, hv]).astype(BF16)

    def channel_dft(st, r):
        for grp in range(FNET_GROUPS):
            sl = slice(grp * FNET_GROUP_DIM, (grp + 1) * FNET_GROUP_DIM)
            pq = dot(st["f"][:, grp * FNET_GROUP_DIM:(grp + 2) * FNET_GROUP_DIM], cs)
            pq_ref[0, r, sl] = pq[:, :FNET_GROUP_DIM].astype(BF16)
            pq_ref[1, r, sl] = pq[:, FNET_GROUP_DIM:].astype(BF16)

    _run_interleaved((norm_in, decay_rank, project, decay_factors, intra_chunk, channel_dft),
                     tiles)


def _mixin_call(x2d, nw, w_in_bf, wdec, bdec, cs, tri_f, tri_b, rows_per_batch, tm, chunk,
                cast=(), gate_t=None, gate_row0=0):
    m = x2d.shape[0]
    d_model = w_in_bf.shape[0]
    nt = rows_per_batch // tm
    nb = m // rows_per_batch
    steps = m // tm
    slab = lambda w: pl.BlockSpec((w.shape[0] // steps, w.shape[1]), lambda i: (i, 0))
    assert all(w.shape[0] % (16 * steps) == 0 for w in cast)
    gate_specs, gate_out_spec, gate_out_shape = [], (), ()
    if gate_t is not None:
        feat = 2 * D_MODEL // steps
        assert feat % LANE == 0 and feat % GATE_SUB == 0 and gate_row0 % GATE_SUB == 0
        assert gate_t.shape == (gate_row0 + 2 * D_MODEL, d_model)
        nsub, blk0 = feat // GATE_SUB, gate_row0 // GATE_SUB
        gate_specs = [pl.BlockSpec((GATE_SUB, d_model), lambda i, j=j: (blk0 + i * nsub + j, 0))
                      for j in range(nsub)]
        gate_out_spec = (pl.BlockSpec((d_model, feat), lambda i: (0, i)),)
        gate_out_shape = (jax.ShapeDtypeStruct((d_model, 2 * D_MODEL), BF16),)
    row = lambda n: pl.BlockSpec((tm, n), lambda i: (i, 0))
    pq_spec = pl.BlockSpec((2, tm, FNET_WIDTH), lambda i: (0, i % nt, i // nt))
    wide = jax.ShapeDtypeStruct((m, 2 * GLA_DK), BF16)
    pq_shape = jax.ShapeDtypeStruct((2, rows_per_batch, nb * FNET_WIDTH), BF16)
    consts = [wdec, bdec, cs,
              jnp.asarray(tri_f).astype(BF16), jnp.asarray(tri_b).astype(BF16),
              jnp.asarray(tri_f), jnp.asarray(tri_b)]
    return pl.pallas_call(
        functools.partial(_mixin_kernel, chunk=chunk, n_cast=len(cast), n_gate=len(gate_specs)),
        grid=(steps,),
        in_specs=[row(D_MODEL), _resident(nw.shape),
                  _window((d_model, _OFF_QKV_END), (0, 0)),
                  _window((d_model, W_WINDOW), (0, 3))]
                 + [_resident(c.shape) for c in consts] + [slab(w) for w in cast] + gate_specs,
        out_specs=(row(GLA_DV), row(GLA_DV), row(2 * GLA_DK), row(2 * GLA_DK),
                   pl.BlockSpec((tm // SUB, 2 * GLA_DK), lambda i: (i, 0)), pq_spec)
                  + tuple(slab(w) for w in cast) + gate_out_spec,
        out_shape=(jax.ShapeDtypeStruct((m, GLA_DV), BF16), jax.ShapeDtypeStruct((m, GLA_DV), BF16),
                   wide, wide, jax.ShapeDtypeStruct((m // SUB, 2 * GLA_DK), F32), pq_shape)
                  + tuple(jax.ShapeDtypeStruct(w.shape, BF16) for w in cast) + gate_out_shape,
        compiler_params=pltpu.CompilerParams(
            dimension_semantics=("arbitrary",), vmem_limit_bytes=V7X_VMEM_LIMIT),
        name="mixin",
    )(x2d, nw, w_in_bf, w_in_bf, *consts, *cast, *([gate_t] * len(gate_specs)))


SCAN_G = 512
SCAN_RB = 4096


def _scan_group(dirs):
    s, n = SUB, SCAN_G // SUB
    ahead = 2
    dec, st, inter = [], [], [{}, {}]
    for qe_ref, ke_ref, v_ref, bend_ref, st_ref, row0, chunk0, backward in dirs:
        dec_rows = jnp.tile(jnp.exp2(bend_ref[chunk0:chunk0 + n, :]), (HEAD_DK // n, 1))
        dec.append(dec_rows.T)
        st.append(st_ref[...])

    def order(step, backward):
        return n - 1 - step if backward else step

    def increment(d, step):
        _, ke_ref, v_ref, _, _, row0, _, backward = dirs[d]
        i = order(step, backward)
        sub = slice(row0 + i * s, row0 + (i + 1) * s)
        return lax.dot_general(ke_ref[sub, :], v_ref[sub, :], _TN,
                               preferred_element_type=F32)

    inc = {(d, step): increment(d, step) for step in range(ahead) for d in range(2)}
    for step in range(n):
        for d, (qe_ref, _, _, _, _, row0, _, backward) in enumerate(dirs):
            i = order(step, backward)
            sub = slice(row0 + i * s, row0 + (i + 1) * s)
            inter[d][i] = jnp.dot(qe_ref[sub, :], st[d].astype(BF16),
                                  preferred_element_type=F32)
            if step + ahead < n:
                inc[d, step + ahead] = increment(d, step + ahead)
            decay = jnp.broadcast_to(dec[d][:, i:i + 1], (HEAD_DK, HEAD_DV))
            st[d] = st[d] * decay + inc.pop((d, step))
    for d, spec in enumerate(dirs):
        spec[4][...] = st[d]
    return [jnp.concatenate([inter[d][i] for i in range(n)], axis=0) for d in range(2)]


def _gla_scan_kernel(qef_ref, kef_ref, vf_ref, bendf_ref,
                     qeb_ref, keb_ref, vb_ref, bendb_ref, kem_ref, vm_ref,
                     of_ref, ob_ref, stf_ref, stb_ref):
    rb = qef_ref.shape[0]
    ng = rb // SCAN_G

    @pl.when(pl.program_id(2) == 0)
    def _():
        stf_ref[...] = lax.dot_general(kem_ref[...], vm_ref[...], _TN,
                                       preferred_element_type=F32)
        stb_ref[...] = jnp.zeros_like(stb_ref)

    for j in range(ng):
        jb = ng - 1 - j
        cpg = SCAN_G // SUB
        in_f, in_b = _scan_group([
            (qef_ref, kef_ref, vf_ref, bendf_ref, stf_ref, j * SCAN_G, j * cpg, False),
            (qeb_ref, keb_ref, vb_ref, bendb_ref, stb_ref, jb * SCAN_G, jb * cpg, True)])
        of_ref[j * SCAN_G:(j + 1) * SCAN_G, :] = in_f.astype(BF16)
        ob_ref[jb * SCAN_G:(jb + 1) * SCAN_G, :] = in_b.astype(BF16)


def _gla_scan_call(qe, ke, v, bend, kem, vm, seq, batch):
    h, rb = GLA_HEADS, SCAN_RB
    nt = seq // rb
    fwd = lambda b, hh, t: b * nt + t
    bwd = lambda b, hh, t: b * nt + nt - 1 - t
    cb = rb // SUB

    def dk_blk(rowfn, dircol, rows):
        return pl.BlockSpec((rows, HEAD_DK), lambda b, hh, t: (rowfn(b, hh, t), dircol + hh))

    def dv_blk(rowfn):
        return pl.BlockSpec((rb, HEAD_DV), lambda b, hh, t: (rowfn(b, hh, t), hh))

    in_specs = [
        dk_blk(fwd, 0, rb), dk_blk(fwd, 0, rb), dv_blk(fwd), dk_blk(fwd, 0, cb),
        dk_blk(bwd, h, rb), dk_blk(bwd, h, rb), dv_blk(bwd), dk_blk(bwd, h, cb),
        pl.BlockSpec((SUB, HEAD_DK), lambda b, hh, t: (0, hh)),
        pl.BlockSpec((SUB, HEAD_DV), lambda b, hh, t: (0, hh)),
    ]
    out = jax.ShapeDtypeStruct((batch * seq, GLA_DV), BF16)
    return pl.pallas_call(
        _gla_scan_kernel,
        grid=(batch, h, nt),
        in_specs=in_specs,
        out_specs=(dv_blk(fwd), dv_blk(bwd)),
        out_shape=(out, out),
        scratch_shapes=[pltpu.VMEM((HEAD_DK, HEAD_DV), F32), pltpu.VMEM((HEAD_DK, HEAD_DV), F32)],
        compiler_params=pltpu.CompilerParams(
            dimension_semantics=("arbitrary", "arbitrary", "arbitrary"),
            vmem_limit_bytes=V7X_VMEM_LIMIT),
        name="gla_scan",
    )(qe, ke, v, bend, qe, ke, v, bend, kem, vm)


DFT_TILE = 256


def _fold_kernel(x_ref, z_ref, j_ref, o_ref, *, m_last):
    t = DFT_TILE
    n_real = x_ref.shape[1] // t
    sign = jnp.where(pl.program_id(0) == 0, 1.0, -1.0)
    real = lambda i: x_ref[0, i * t:(i + 1) * t, :]
    for i in range(o_ref.shape[1] // t):
        direct = (z_ref[0] if i == 0 else real(i - 1)).astype(F32)
        mirror = (jnp.dot(j_ref[i, :, :t], real(min(n_real - i, n_real - 1)),
                          preferred_element_type=F32)
                  + jnp.dot(j_ref[i, :, t:], real(n_real - 1 - i), preferred_element_type=F32))
        m = i * t + lax.broadcasted_iota(jnp.int32, (t, 1), 0)
        folded = jnp.where(m <= m_last, direct + sign * mirror, 0.0)
        o_ref[0, i * t:(i + 1) * t, :] = folded.astype(BF16)


def _dft_kernel(peqo_ref, dc_ref, ds_ref, rowc_ref, rows_ref, u_ref, w_ref, *, scale):
    t = DFT_TILE
    acc_a = acc_b = None
    for c in range(peqo_ref.shape[1] // t):
        sl = slice(c * t, (c + 1) * t)
        rc, rs = rowc_ref[0, 0:1, sl], rows_ref[0, 0:1, sl]
        dc, ds = dc_ref[:, sl], ds_ref[:, sl]
        tc = (dc * rc - ds * rs).astype(BF16)
        ts = (ds * rc + dc * rs).astype(BF16)
        a = jnp.dot(tc, peqo_ref[0, sl, :], preferred_element_type=F32)
        b = jnp.dot(ts, peqo_ref[1, sl, :], preferred_element_type=F32)
        acc_a = a if acc_a is None else acc_a + a
        acc_b = b if acc_b is None else acc_b + b
    u_ref[...] = ((acc_a - acc_b) * scale).astype(BF16)
    w_ref[...] = ((acc_a + acc_b) * scale).astype(BF16)


def _unfold_kernel(u_ref, w_ref, j_ref, y_ref, *, r_last, i_mixed):
    t = DFT_TILE
    tile = lambda ref, i: ref[i * t:(i + 1) * t, :]
    n_real = y_ref.shape[0] // t
    for i in range(n_real):
        if i < i_mixed:
            y_ref[i * t:(i + 1) * t, :] = tile(u_ref, i + 1)
            continue
        j = 0 if i == i_mixed else 1
        mirror = jnp.dot(j_ref[j, :, t:], tile(w_ref, n_real - 1 - i), preferred_element_type=F32)
        if (n_real - i + 1) * t <= w_ref.shape[0]:
            mirror += jnp.dot(j_ref[j, :, :t], tile(w_ref, n_real - i),
                              preferred_element_type=F32)
        mirror = mirror.astype(BF16)
        if i == i_mixed:
            r = i * t + lax.broadcasted_iota(jnp.int32, (t, 1), 0)
            mirror = jnp.where(r <= r_last, tile(u_ref, i + 1), mirror)
        y_ref[i * t:(i + 1) * t, :] = mirror


@functools.lru_cache(maxsize=None)
def _dft_constants(seq, n_meta):
    t = DFT_TILE
    off = t - n_meta
    big_l = seq + n_meta
    half = big_l // 2
    n_real = seq // t
    nf = -(-(half + 1 + off) // t)
    hp = nf * t
    jf = np.zeros((nf, t, 2 * t), np.float32)
    for i in range(nf):
        a_idx, b_idx = min(n_real - i, n_real - 1), n_real - 1 - i
        for j in range(t):
            pos = i * t + j - off
            if 1 <= pos <= half - 1:
                tau, rho = divmod(big_l - pos - n_meta, t)
                assert tau in (a_idx, b_idx)
                jf[i, j, (t if tau == b_idx else 0) + rho] = 1.0
    r_last = half - n_meta
    i_mixed = r_last // t
    ju = np.zeros((3, t, 2 * t), np.float32)
    for i in range(i_mixed, n_real):
        a_idx, b_idx = n_real - i, n_real - 1 - i
        blk = np.zeros((t, 2 * t), np.float32)
        for j in range(t):
            r = i * t + j
            if r > r_last:
                tau, rho = divmod(big_l - (r + n_meta) + off, t)
                assert tau in (a_idx, b_idx)
                blk[j, (t if tau == b_idx else 0) + rho] = 1.0
        var = 1 if i == i_mixed else 2
        assert var == 1 or not ju[2].any() or (ju[2] == blk).all()
        ju[var] = blk
    ang = lambda prod: 2.0 * np.pi * (prod % big_l).astype(np.float64) / big_l
    pos = np.arange(hp, dtype=np.int64) - off
    d = ang(np.arange(t, dtype=np.int64)[:, None] * pos[None, :])
    k0 = t * np.arange(nf, dtype=np.int64) - off
    row = np.broadcast_to(ang(k0[:, None] * pos[None, :])[:, None, :], (nf, 8, hp))
    f = lambda a: np.ascontiguousarray(a, dtype=np.float32)
    return dict(jf=jf, ju=ju, dc=f(np.cos(d)), ds=f(np.sin(d)), rowc=f(np.cos(row)),
                rows=f(np.sin(row)), nf=nf, hp=hp, off=off, n_real=n_real,
                m_last=half + off, r_last=r_last, i_mixed=i_mixed)


def _seq_dft(pq, pqm, seq, n_meta):
    c = _dft_constants(seq, n_meta)
    t, nf, hp, ncol = DFT_TILE, c["nf"], c["hp"], pq.shape[2]
    ncb = ncol // FNET_WIDTH
    col_blk = lambda rows: pl.BlockSpec((1, rows, FNET_WIDTH), lambda g, b: (g, 0, b))
    peqo = pl.pallas_call(
        functools.partial(_fold_kernel, m_last=c["m_last"]),
        grid=(2, ncb),
        in_specs=[col_blk(seq), col_blk(t), _resident(c["jf"].shape)],
        out_specs=col_blk(hp),
        out_shape=jax.ShapeDtypeStruct((2, hp, ncol), BF16),
        compiler_params=pltpu.CompilerParams(
            dimension_semantics=("arbitrary", "arbitrary"), vmem_limit_bytes=V7X_VMEM_LIMIT),
        name="dft_fold",
    )(pq, jnp.pad(pqm, ((0, 0), (c["off"], 0), (0, 0))), jnp.asarray(c["jf"]).astype(BF16))

    scale = float((seq + n_meta) * FNET_GROUP_DIM) ** -0.5
    tile = pl.BlockSpec((t, ncol), lambda i: (i, 0))
    rowspec = pl.BlockSpec((1, 8, hp), lambda i: (i, 0, 0))
    u, w = pl.pallas_call(
        functools.partial(_dft_kernel, scale=scale),
        grid=(nf,),
        in_specs=[_resident((2, hp, ncol)), _resident((t, hp)), _resident((t, hp)),
                  rowspec, rowspec],
        out_specs=(tile, tile),
        out_shape=(jax.ShapeDtypeStruct((hp, ncol), BF16),) * 2,
        compiler_params=pltpu.CompilerParams(
            dimension_semantics=("arbitrary",), vmem_limit_bytes=V7X_VMEM_LIMIT),
        name="seq_dft",
    )(peqo, jnp.asarray(c["dc"]), jnp.asarray(c["ds"]), jnp.asarray(c["rowc"]),
      jnp.asarray(c["rows"]))

    col = lambda rows: pl.BlockSpec((rows, FNET_WIDTH), lambda b: (0, b))
    return pl.pallas_call(
        functools.partial(_unfold_kernel, r_last=c["r_last"], i_mixed=c["i_mixed"]),
        grid=(ncb,),
        in_specs=[col(hp), col(hp), _resident((2, t, 2 * t))],
        out_specs=col(seq),
        out_shape=jax.ShapeDtypeStruct((seq, ncol), BF16),
        compiler_params=pltpu.CompilerParams(
            dimension_semantics=("arbitrary",), vmem_limit_bytes=V7X_VMEM_LIMIT),
        name="dft_unfold",
    )(u, w, jnp.asarray(c["ju"][1:]).astype(BF16))


FFN_CHUNK = 1024
TAIL_SPLIT = 2
TAIL_TM = 512


def _tail_kernel(x_ref, ol_ref, of_ref, ob_ref, y_ref, nmix_ref, gn_ref, nffn_ref, nfin_ref,
                 wg_ref, wm_ref, wo_ref, wf_ref, wout_ref, w1_ref, w2_ref, o_ref):
    tm = x_ref.shape[0]
    sub = tm // TAIL_SPLIT
    tiles = [(dict(), slice(i * sub, (i + 1) * sub)) for i in range(TAIL_SPLIT)]
    dot = functools.partial(jnp.dot, preferred_element_type=F32)
    nchunk = D_FF // FFN_CHUNK

    def norm_in(s, r):
        s["x"] = x_ref[r, :]
        s["xn"] = _rms(s["x"], nmix_ref[...]).astype(BF16)

    def gates(s, r):
        s["g"] = [dot(s["xn"], wg_ref[:, h * HEAD_DV:(h + 1) * HEAD_DV])
                  for h in range(GLA_HEADS)]
        s["ga"] = dot(s["xn"], wm_ref[:, 0:D_MODEL])
        s["gf"] = dot(s["xn"], wm_ref[:, D_MODEL:2 * D_MODEL])
        s["y_fnet"] = dot(y_ref[r, :], wf_ref[...])

    def gla_out(s, r):
        y_gla = None
        for h in range(GLA_HEADS):
            hv = slice(h * HEAD_DV, (h + 1) * HEAD_DV)
            o = (ol_ref[r, hv].astype(F32) + of_ref[r, hv].astype(F32)
                 + ob_ref[r, hv].astype(F32))
            g = s["g"][h]
            og = (_rms(o, gn_ref[...]) * (g * _sigmoid(g))).astype(BF16)
            part = dot(og, wo_ref[hv, :])
            y_gla = part if y_gla is None else y_gla + part
        s["y_gla"] = y_gla

    def merge(s, r):
        merged = _sigmoid(s["ga"]) * s["y_gla"] + _sigmoid(s["gf"]) * s["y_fnet"]
        s["h1"] = s["x"] + dot(merged.astype(BF16), wout_ref[...])
        s["u"] = _rms(s["h1"], nffn_ref[...]).astype(BF16)
        s["acc"] = s["h1"]

    def ffn_up(c):
        def stage(s, r):
            a = jnp.maximum(dot(s["u"], w1_ref[:, c * FFN_CHUNK:(c + 1) * FFN_CHUNK]), 0.0)
            s["a", c] = (a * a).astype(BF16)
        return stage

    def ffn_down(c):
        def stage(s, r):
            s["acc"] = s["acc"] + dot(s.pop(("a", c)),
                                      w2_ref[c * FFN_CHUNK:(c + 1) * FFN_CHUNK, :])
        return stage

    def norm_out(s, r):
        o_ref[r, :] = _rms(s["acc"], nfin_ref[...])

    stages = [norm_in, gates, gla_out, merge, ffn_up(0)]
    for c in range(nchunk):
        if c + 1 < nchunk:
            stages.append(ffn_up(c + 1))
        stages.append(ffn_down(c))
    stages.append(norm_out)
    _run_interleaved(stages, tiles)


def _tail_call(x2d, ol, o_f, o_b, y, nmix, gn, nffn, nfin, w_in_bf, w_merge, wo, wf, wout, w1,
               w2, seq, tm):
    m = x2d.shape[0]
    nt = seq // tm
    row = lambda n: pl.BlockSpec((tm, n), lambda i: (i, 0))
    norms = [nmix, gn, nffn, nfin]
    consts = [w_merge, wo, wf, wout, w1, w2]
    return pl.pallas_call(
        _tail_kernel,
        grid=(m // tm,),
        in_specs=[row(D_MODEL), row(GLA_DV), row(GLA_DV), row(GLA_DV),
                  pl.BlockSpec((tm, FNET_WIDTH), lambda i: (i % nt, i // nt))]
                 + [_resident(c.shape) for c in norms]
                 + [_window((w_in_bf.shape[0], W_WINDOW), (0, 2))]
                 + [_resident(c.shape) for c in consts],
        out_specs=row(D_MODEL),
        out_shape=jax.ShapeDtypeStruct((m, D_MODEL), F32),
        compiler_params=pltpu.CompilerParams(
            dimension_semantics=("arbitrary",), vmem_limit_bytes=V7X_VMEM_LIMIT),
        name="tail",
    )(x2d, ol, o_f, o_b, y, *norms, w_in_bf, *consts)


@functools.lru_cache(maxsize=None)
def _gla_constants(rows):
    idx = np.arange(rows)
    same = (idx[:, None] // SUB) == (idx[None, :] // SUB)
    tri_f = (same & (idx[None, :] <= idx[:, None])).astype(np.float32)
    return tri_f, np.ascontiguousarray(tri_f.T)


@functools.lru_cache(maxsize=None)
def _channel_dft():
    d = np.arange(FNET_GROUP_DIM, dtype=np.int64)
    ang = 2.0 * np.pi * ((d[:, None] * d[None, :]) % FNET_GROUP_DIM) / FNET_GROUP_DIM
    cs = np.zeros((2 * FNET_GROUP_DIM, 2 * FNET_GROUP_DIM), np.float32)
    cs[F_LEAD:F_LEAD + FNET_GROUP_DIM] = np.concatenate([np.cos(ang), np.sin(ang)], axis=1)
    return cs


def kernel(x, meta_tokens, norm_mix, w_in, w_decay_fwd, b_decay_fwd, w_decay_bwd, b_decay_bwd,
           gla_norm, w_o_gla, w_fnet, w_out, norm_ffn, w_ff1, w_ff2, norm_final):
    batch, seq, d = x.shape
    assert d == D_MODEL and meta_tokens.shape == (N_META, D_MODEL)
    assert w_in.shape[0] == 1 and seq % (2 * SCAN_RB) == 0 and seq % DFT_TILE == 0
    gate0 = 2 * GLA_DK + 2 * GLA_DV + 2 * DECAY_RANK + FNET_WIDTH
    assert w_in.shape[2] == gate0 + 2 * D_MODEL and gate0 <= 4 * W_WINDOW
    wi = w_in[0][:, :4 * W_WINDOW].astype(BF16)
    w_in_t = jnp.swapaxes(w_in[0], 0, 1)
    wdec = jnp.zeros((R_PAD, 2 * GLA_DK), F32)
    wdec = wdec.at[:DECAY_RANK, :GLA_DK].set(w_decay_fwd[0])
    wdec = wdec.at[DECAY_RANK:2 * DECAY_RANK, GLA_DK:].set(w_decay_bwd[0]).astype(BF16)
    bdec = jnp.concatenate([b_decay_fwd[0], b_decay_bwd[0]])[None, :]
    cs = jnp.asarray(_channel_dft())
    nw_mix = norm_mix[0][None, :]

    x2d = x.reshape(batch * seq, D_MODEL)
    tail_weights = (w_o_gla[0], w_fnet[0], w_out[0], w_ff1[0], w_ff2[0])
    v, ol, qe, ke, bend, pq, *tail_weights, w_merge = _mixin_call(
        x2d, nw_mix, wi, wdec, bdec, cs, *_gla_constants(GLA_C), seq, MIX_TM, GLA_C,
        cast=tail_weights, gate_t=w_in_t, gate_row0=gate0)
    meta_chunk = jnp.pad(meta_tokens, ((SUB - N_META, 0), (0, 0)))
    vm, _, _, kem, _, pqm = _mixin_call(meta_chunk, nw_mix, wi, wdec, bdec, cs,
                                        *_gla_constants(SUB), SUB, SUB, SUB)
    o_f, o_b = _gla_scan_call(qe, ke, v, bend, kem, vm, seq, batch)

    pqm = jnp.tile(pqm[:, SUB - N_META:], (1, 1, batch))
    y = _seq_dft(pq, pqm, seq, N_META)

    out = _tail_call(x2d, ol, o_f, o_b, y, nw_mix, gla_norm[0][None, :], norm_ffn[0][None, :],
                     norm_final[None, :], wi, w_merge, *tail_weights, seq, TAIL_TM)
    return out.reshape(batch, seq, D_MODEL)
```

```python
import functools

import numpy as np
import jax
import jax.numpy as jnp
from jax import lax
from jax.experimental import pallas as pl
from jax.experimental.pallas import tpu as pltpu

F32 = jnp.float32
BF16 = jnp.bfloat16

D_MODEL = 1024
N_META = 16
GLA_HEADS = 4
GLA_DK = 512
GLA_DV = 1024
HEAD_DK = GLA_DK // GLA_HEADS
HEAD_DV = GLA_DV // GLA_HEADS
DECAY_RANK = 16
GATE_TEMP = 16.0
SUB = 64
FNET_GROUPS = 4
FNET_WIDTH = 512
FNET_GROUP_DIM = FNET_WIDTH // FNET_GROUPS
D_FF = 4 * D_MODEL
EPS = 1e-6
LOG2E = 1.4426950408889634
LANE = 128

_OFF_Q, _OFF_K, _OFF_V, _OFF_QKV_END = 0, 512, 1024, 2048
W_WINDOW = 1024
R_PAD = 128
F_LEAD = 2 * DECAY_RANK
F_EXT = F_LEAD + FNET_WIDTH + (LANE - F_LEAD)

GLA_C = 4 * SUB
MIX_TM = 1024
GATE_SUB = 32

V7X_VMEM_LIMIT = 58 * 1024 * 1024


def _rms(x, w):
    ms = jnp.mean(x * x, axis=-1, keepdims=True)
    return x * lax.rsqrt(ms + EPS) * w


def _sigmoid(x):
    return 1.0 / (1.0 + jnp.exp(-x))


def _run_interleaved(stages, tiles):
    for stage in stages:
        for state, rows in tiles:
            stage(state, rows)


def _window(block, index):
    return pl.BlockSpec(block, lambda *_: index, pipeline_mode=pl.Buffered(1))


def _resident(shape):
    return _window(shape, (0,) * len(shape))


_NT = (((1,), (1,)), ((), ()))
_TN = (((0,), (0,)), ((), ()))


def _mixin_kernel(x_ref, nw_ref, wqkv_ref, wrf_ref, wdec_ref, bdec_ref, cs_ref,
                  trif_ref, trib_ref, maskf_ref, maskb_ref, *refs, chunk, n_cast, n_gate):
    n_in = n_cast + n_gate
    cast_in, gate_in = refs[:n_cast], refs[n_cast:n_in]
    v_ref, ol_ref, qe_ref, ke_ref, bend_ref, pq_ref = refs[n_in:n_in + 6]
    cast_out = refs[n_in + 6:n_in + 6 + n_cast]
    _mixin_body(x_ref, nw_ref, wqkv_ref, wrf_ref, wdec_ref, bdec_ref, cs_ref, trif_ref, trib_ref,
                maskf_ref, maskb_ref, v_ref, ol_ref, qe_ref, ke_ref, bend_ref, pq_ref, chunk=chunk)
    for src, dst in zip(cast_in, cast_out):
        dst[...] = src[...].astype(BF16)
    if n_gate:
        feat_major = jnp.concatenate([r[...] for r in gate_in], axis=0)
        refs[-1][...] = feat_major.T.astype(BF16)


def _mixin_body(x_ref, nw_ref, wqkv_ref, wrf_ref, wdec_ref, bdec_ref, cs_ref,
                trif_ref, trib_ref, maskf_ref, maskb_ref,
                v_ref, ol_ref, qe_ref, ke_ref, bend_ref, pq_ref, *, chunk):
    tm = x_ref.shape[0]
    n, s = chunk // SUB, SUB
    tiles = [(dict(), slice(i * chunk, (i + 1) * chunk)) for i in range(tm // chunk)]
    dot = functools.partial(jnp.dot, preferred_element_type=F32)
    cs = cs_ref[...].astype(BF16)
    tris = (trif_ref[...], trib_ref[...])
    masks = (maskf_ref[0:s, 0:s] != 0.0, maskb_ref[0:s, 0:s] != 0.0)
    half = GLA_DK // 2

    def norm_in(st, r):
        st["xn"] = _rms(x_ref[r, :], nw_ref[...]).astype(BF16)

    def decay_rank(st, r):
        st["f"] = dot(st["xn"], wrf_ref[:, 0:F_EXT]).astype(BF16)
        st["r"] = st["f"][:, 0:R_PAD]

    def log_decay(st, j):
        cols = slice(j * half, (j + 1) * half)
        x = dot(st["r"], wdec_ref[:, cols]) + bdec_ref[:, cols]
        la = (jnp.minimum(x, 0.0) - jnp.log(1.0 + jnp.exp(-jnp.abs(x)))) * (LOG2E / GATE_TEMP)
        hi = la.astype(BF16)
        st["la", j] = (hi, (la - hi.astype(F32)).astype(BF16))

    def project(st, r):
        mm = lambda lo, hi: dot(st["xn"], wqkv_ref[:, lo:hi])
        vmid = (_OFF_V + _OFF_QKV_END) // 2
        log_decay(st, 0)
        st["q"] = mm(_OFF_Q, _OFF_K) * (HEAD_DK ** -0.5)
        log_decay(st, 1)
        st["k"] = mm(_OFF_K, _OFF_V)
        log_decay(st, 2)
        v_ref[r, 0:GLA_DV // 2] = mm(_OFF_V, vmid).astype(BF16)
        log_decay(st, 3)
        v_ref[r, GLA_DV // 2:GLA_DV] = mm(vmid, _OFF_QKV_END).astype(BF16)

    def decay_factors(st, r):
        c0 = r.start // s
        for j in range(4):
            d, qcols = j // 2, slice((j % 2) * half, (j % 2 + 1) * half)
            cols = slice(j * half, (j + 1) * half)
            hi, lo = st.pop(("la", j))
            b = (dot(tris[d], hi) + dot(tris[d], lo)).reshape(n, s, half)
            if d == 0:
                b_mid, b_end = b[:, s // 2 - 1:s // 2, :], b[:, s - 1:s, :]
            else:
                b_mid, b_end = b[:, s // 2:s // 2 + 1, :], b[:, 0:1, :]
            qd = st["q"][:, qcols].reshape(n, s, half) * jnp.exp2(b - b_mid)
            kd = st["k"][:, qcols].reshape(n, s, half) * jnp.exp2(b_mid - b)
            qe = qd * jnp.exp2(b_mid)
            ke = kd * jnp.exp2(b_end - b_mid)
            st["qd", d, j % 2] = qd.reshape(chunk, half).astype(BF16)
            st["kd", d, j % 2] = kd.reshape(chunk, half).astype(BF16)
            qe_ref[r, cols] = qe.reshape(chunk, half).astype(BF16)
            ke_ref[r, cols] = ke.reshape(chunk, half).astype(BF16)
            bend_ref[c0:c0 + n, cols] = b_end.reshape(n, half)

    def intra_chunk(st, r):
        for h in range(GLA_HEADS):
            hk = slice((h % 2) * HEAD_DK, (h % 2 + 1) * HEAD_DK)
            hv = slice(h * HEAD_DV, (h + 1) * HEAD_DV)
            sc = [[lax.dot_general(st["qd", d, h // 2][c * s:(c + 1) * s, hk],
                                   st["kd", d, h // 2][c * s:(c + 1) * s, hk], _NT,
                                   preferred_element_type=F32) for d in range(2)]
                  for c in range(n)]
            for c, (sf, sb) in enumerate(sc):
                rows = slice(r.start + c * s, r.start + (c + 1) * s)
                both = (jnp.where(masks[0], sf, 0.0) + jnp.where(masks[1], sb, 0.0)).astype(BF16)
                ol_ref[rows, hv] = dot(both, v_ref[rows, hv]).astype(BF16)

    def channel_dft(st, r):
        for grp in range(FNET_GROUPS):
            sl = slice(grp * FNET_GROUP_DIM, (grp + 1) * FNET_GROUP_DIM)
            pq = dot(st["f"][:, grp * FNET_GROUP_DIM:(grp + 2) * FNET_GROUP_DIM], cs)
            pq_ref[0, r, sl] = pq[:, :FNET_GROUP_DIM].astype(BF16)
            pq_ref[1, r, sl] = pq[:, FNET_GROUP_DIM:].astype(BF16)

    _run_interleaved((norm_in, decay_rank, project, decay_factors, intra_chunk, channel_dft),
                     tiles)


def _mixin_call(x2d, nw, w_in_bf, wdec, bdec, cs, tri_f, tri_b, rows_per_batch, tm, chunk,
                cast=(), gate_t=None, gate_row0=0):
    m = x2d.shape[0]
    d_model = w_in_bf.shape[0]
    nt = rows_per_batch // tm
    nb = m // rows_per_batch
    steps = m // tm
    slab = lambda w: pl.BlockSpec((w.shape[0] // steps, w.shape[1]), lambda i: (i, 0))
    assert all(w.shape[0] % (16 * steps) == 0 for w in cast)
    gate_specs, gate_out_spec, gate_out_shape = [], (), ()
    if gate_t is not None:
        feat = 2 * D_MODEL // steps
        assert feat % LANE == 0 and feat % GATE_SUB == 0 and gate_row0 % GATE_SUB == 0
        assert gate_t.shape == (gate_row0 + 2 * D_MODEL, d_model)
        nsub, blk0 = feat // GATE_SUB, gate_row0 // GATE_SUB
        gate_specs = [pl.BlockSpec((GATE_SUB, d_model), lambda i, j=j: (blk0 + i * nsub + j, 0))
                      for j in range(nsub)]
        gate_out_spec = (pl.BlockSpec((d_model, feat), lambda i: (0, i)),)
        gate_out_shape = (jax.ShapeDtypeStruct((d_model, 2 * D_MODEL), BF16),)
    row = lambda n: pl.BlockSpec((tm, n), lambda i: (i, 0))
    pq_spec = pl.BlockSpec((2, tm, FNET_WIDTH), lambda i: (0, i % nt, i // nt))
    wide = jax.ShapeDtypeStruct((m, 2 * GLA_DK), BF16)
    pq_shape = jax.ShapeDtypeStruct((2, rows_per_batch, nb * FNET_WIDTH), BF16)
    consts = [wdec, bdec, cs,
              jnp.asarray(tri_f).astype(BF16), jnp.asarray(tri_b).astype(BF16),
              jnp.asarray(tri_f), jnp.asarray(tri_b)]
    return pl.pallas_call(
        functools.partial(_mixin_kernel, chunk=chunk, n_cast=len(cast), n_gate=len(gate_specs)),
        grid=(steps,),
        in_specs=[row(D_MODEL), _resident(nw.shape),
                  _window((d_model, _OFF_QKV_END), (0, 0)),
                  _window((d_model, W_WINDOW), (0, 3))]
                 + [_resident(c.shape) for c in consts] + [slab(w) for w in cast] + gate_specs,
        out_specs=(row(GLA_DV), row(GLA_DV), row(2 * GLA_DK), row(2 * GLA_DK),
                   pl.BlockSpec((tm // SUB, 2 * GLA_DK), lambda i: (i, 0)), pq_spec)
                  + tuple(slab(w) for w in cast) + gate_out_spec,
        out_shape=(jax.ShapeDtypeStruct((m, GLA_DV), BF16), jax.ShapeDtypeStruct((m, GLA_DV), BF16),
                   wide, wide, jax.ShapeDtypeStruct((m // SUB, 2 * GLA_DK), F32), pq_shape)
                  + tuple(jax.ShapeDtypeStruct(w.shape, BF16) for w in cast) + gate_out_shape,
        compiler_params=pltpu.CompilerParams(
            dimension_semantics=("arbitrary",), vmem_limit_bytes=V7X_VMEM_LIMIT),
        name="mixin",
    )(x2d, nw, w_in_bf, w_in_bf, *consts, *cast, *([gate_t] * len(gate_specs)))


def _transpose_cast_kernel(x_ref, o_ref):
    o_ref[...] = x_ref[...].T.astype(BF16)


def _transpose_cast_call(w_t, n_rows, block):
    d_model = w_t.shape[1]
    assert n_rows % block == 0 and n_rows <= w_t.shape[0]
    return pl.pallas_call(
        _transpose_cast_kernel,
        grid=(n_rows // block,),
        in_specs=[pl.BlockSpec((block, d_model), lambda i: (i, 0))],
        out_specs=pl.BlockSpec((d_model, block), lambda i: (0, i)),
        out_shape=jax.ShapeDtypeStruct((d_model, n_rows), BF16),
        compiler_params=pltpu.CompilerParams(
            dimension_semantics=("arbitrary",), vmem_limit_bytes=V7X_VMEM_LIMIT),
        name="w_in_prep",
    )(w_t)


SCAN_G = 512
SCAN_RB = 4096


def _scan_group(dirs):
    s, n = SUB, SCAN_G // SUB
    ahead = 2
    dec, st, inter = [], [], [{}, {}]
    for qe_ref, ke_ref, v_ref, bend_ref, st_ref, row0, chunk0, backward in dirs:
        dec_rows = jnp.tile(jnp.exp2(bend_ref[chunk0:chunk0 + n, :]), (HEAD_DK // n, 1))
        dec.append(dec_rows.T)
        st.append(st_ref[...])

    def order(step, backward):
        return n - 1 - step if backward else step

    def increment(d, step):
        _, ke_ref, v_ref, _, _, row0, _, backward = dirs[d]
        i = order(step, backward)
        sub = slice(row0 + i * s, row0 + (i + 1) * s)
        return lax.dot_general(ke_ref[sub, :], v_ref[sub, :], _TN,
                               preferred_element_type=F32)

    inc = {(d, step): increment(d, step) for step in range(ahead) for d in range(2)}
    for step in range(n):
        for d, (qe_ref, _, _, _, _, row0, _, backward) in enumerate(dirs):
            i = order(step, backward)
            sub = slice(row0 + i * s, row0 + (i + 1) * s)
            inter[d][i] = jnp.dot(qe_ref[sub, :], st[d].astype(BF16),
                                  preferred_element_type=F32)
            if step + ahead < n:
                inc[d, step + ahead] = increment(d, step + ahead)
            decay = jnp.broadcast_to(dec[d][:, i:i + 1], (HEAD_DK, HEAD_DV))
            st[d] = st[d] * decay + inc.pop((d, step))
    for d, spec in enumerate(dirs):
        spec[4][...] = st[d]
    return [jnp.concatenate([inter[d][i] for i in range(n)], axis=0) for d in range(2)]


def _gla_scan_kernel(qef_ref, kef_ref, vf_ref, bendf_ref,
                     qeb_ref, keb_ref, vb_ref, bendb_ref, kem_ref, vm_ref,
                     of_ref, ob_ref, stf_ref, stb_ref):
    rb = qef_ref.shape[0]
    ng = rb // SCAN_G

    @pl.when(pl.program_id(2) == 0)
    def _():
        stf_ref[...] = lax.dot_general(kem_ref[...], vm_ref[...], _TN,
                                       preferred_element_type=F32)
        stb_ref[...] = jnp.zeros_like(stb_ref)

    for j in range(ng):
        jb = ng - 1 - j
        cpg = SCAN_G // SUB
        in_f, in_b = _scan_group([
            (qef_ref, kef_ref, vf_ref, bendf_ref, stf_ref, j * SCAN_G, j * cpg, False),
            (qeb_ref, keb_ref, vb_ref, bendb_ref, stb_ref, jb * SCAN_G, jb * cpg, True)])
        of_ref[j * SCAN_G:(j + 1) * SCAN_G, :] = in_f.astype(BF16)
        ob_ref[jb * SCAN_G:(jb + 1) * SCAN_G, :] = in_b.astype(BF16)


def _gla_scan_call(qe, ke, v, bend, kem, vm, seq, batch):
    h, rb = GLA_HEADS, SCAN_RB
    nt = seq // rb
    fwd = lambda b, hh, t: b * nt + t
    bwd = lambda b, hh, t: b * nt + nt - 1 - t
    cb = rb // SUB

    def dk_blk(rowfn, dircol, rows):
        return pl.BlockSpec((rows, HEAD_DK), lambda b, hh, t: (rowfn(b, hh, t), dircol + hh))

    def dv_blk(rowfn):
        return pl.BlockSpec((rb, HEAD_DV), lambda b, hh, t: (rowfn(b, hh, t), hh))

    in_specs = [
        dk_blk(fwd, 0, rb), dk_blk(fwd, 0, rb), dv_blk(fwd), dk_blk(fwd, 0, cb),
        dk_blk(bwd, h, rb), dk_blk(bwd, h, rb), dv_blk(bwd), dk_blk(bwd, h, cb),
        pl.BlockSpec((SUB, HEAD_DK), lambda b, hh, t: (0, hh)),
        pl.BlockSpec((SUB, HEAD_DV), lambda b, hh, t: (0, hh)),
    ]
    out = jax.ShapeDtypeStruct((batch * seq, GLA_DV), BF16)
    return pl.pallas_call(
        _gla_scan_kernel,
        grid=(batch, h, nt),
        in_specs=in_specs,
        out_specs=(dv_blk(fwd), dv_blk(bwd)),
        out_shape=(out, out),
        scratch_shapes=[pltpu.VMEM((HEAD_DK, HEAD_DV), F32), pltpu.VMEM((HEAD_DK, HEAD_DV), F32)],
        compiler_params=pltpu.CompilerParams(
            dimension_semantics=("arbitrary", "arbitrary", "arbitrary"),
            vmem_limit_bytes=V7X_VMEM_LIMIT),
        name="gla_scan",
    )(qe, ke, v, bend, qe, ke, v, bend, kem, vm)


DFT_TILE = 256


def _fold_kernel(x_ref, z_ref, j_ref, o_ref, *, m_last):
    t = DFT_TILE
    n_real = x_ref.shape[1] // t
    sign = jnp.where(pl.program_id(0) == 0, 1.0, -1.0)
    real = lambda i: x_ref[0, i * t:(i + 1) * t, :]
    for i in range(o_ref.shape[1] // t):
        direct = (z_ref[0] if i == 0 else real(i - 1)).astype(F32)
        mirror = (jnp.dot(j_ref[i, :, :t], real(min(n_real - i, n_real - 1)),
                          preferred_element_type=F32)
                  + jnp.dot(j_ref[i, :, t:], real(n_real - 1 - i), preferred_element_type=F32))
        m = i * t + lax.broadcasted_iota(jnp.int32, (t, 1), 0)
        folded = jnp.where(m <= m_last, direct + sign * mirror, 0.0)
        o_ref[0, i * t:(i + 1) * t, :] = folded.astype(BF16)


def _dft_kernel(peqo_ref, dc_ref, ds_ref, rowc_ref, rows_ref, u_ref, w_ref, *, scale):
    t = DFT_TILE
    acc_a = acc_b = None
    for c in range(peqo_ref.shape[1] // t):
        sl = slice(c * t, (c + 1) * t)
        rc, rs = rowc_ref[0, 0:1, sl], rows_ref[0, 0:1, sl]
        dc, ds = dc_ref[:, sl], ds_ref[:, sl]
        tc = (dc * rc - ds * rs).astype(BF16)
        ts = (ds * rc + dc * rs).astype(BF16)
        a = jnp.dot(tc, peqo_ref[0, sl, :], preferred_element_type=F32)
        b = jnp.dot(ts, peqo_ref[1, sl, :], preferred_element_type=F32)
        acc_a = a if acc_a is None else acc_a + a
        acc_b = b if acc_b is None else acc_b + b
    u_ref[...] = ((acc_a - acc_b) * scale).astype(BF16)
    w_ref[...] = ((acc_a + acc_b) * scale).astype(BF16)


def _unfold_kernel(u_ref, w_ref, j_ref, y_ref, *, r_last, i_mixed):
    t = DFT_TILE
    tile = lambda ref, i: ref[i * t:(i + 1) * t, :]
    n_real = y_ref.shape[0] // t
    for i in range(n_real):
        if i < i_mixed:
            y_ref[i * t:(i + 1) * t, :] = tile(u_ref, i + 1)
            continue
        j = 0 if i == i_mixed else 1
        mirror = jnp.dot(j_ref[j, :, t:], tile(w_ref, n_real - 1 - i), preferred_element_type=F32)
        if (n_real - i + 1) * t <= w_ref.shape[0]:
            mirror += jnp.dot(j_ref[j, :, :t], tile(w_ref, n_real - i),
                              preferred_element_type=F32)
        mirror = mirror.astype(BF16)
        if i == i_mixed:
            r = i * t + lax.broadcasted_iota(jnp.int32, (t, 1), 0)
            mirror = jnp.where(r <= r_last, tile(u_ref, i + 1), mirror)
        y_ref[i * t:(i + 1) * t, :] = mirror


@functools.lru_cache(maxsize=None)
def _dft_constants(seq, n_meta):
    t = DFT_TILE
    off = t - n_meta
    big_l = seq + n_meta
    half = big_l // 2
    n_real = seq // t
    nf = -(-(half + 1 + off) // t)
    hp = nf * t
    jf = np.zeros((nf, t, 2 * t), np.float32)
    for i in range(nf):
        a_idx, b_idx = min(n_real - i, n_real - 1), n_real - 1 - i
        for j in range(t):
            pos = i * t + j - off
            if 1 <= pos <= half - 1:
                tau, rho = divmod(big_l - pos - n_meta, t)
                assert tau in (a_idx, b_idx)
                jf[i, j, (t if tau == b_idx else 0) + rho] = 1.0
    r_last = half - n_meta
    i_mixed = r_last // t
    ju = np.zeros((3, t, 2 * t), np.float32)
    for i in range(i_mixed, n_real):
        a_idx, b_idx = n_real - i, n_real - 1 - i
        blk = np.zeros((t, 2 * t), np.float32)
        for j in range(t):
            r = i * t + j
            if r > r_last:
                tau, rho = divmod(big_l - (r + n_meta) + off, t)
                assert tau in (a_idx, b_idx)
                blk[j, (t if tau == b_idx else 0) + rho] = 1.0
        var = 1 if i == i_mixed else 2
        assert var == 1 or not ju[2].any() or (ju[2] == blk).all()
        ju[var] = blk
    ang = lambda prod: 2.0 * np.pi * (prod % big_l).astype(np.float64) / big_l
    pos = np.arange(hp, dtype=np.int64) - off
    d = ang(np.arange(t, dtype=np.int64)[:, None] * pos[None, :])
    k0 = t * np.arange(nf, dtype=np.int64) - off
    row = np.broadcast_to(ang(k0[:, None] * pos[None, :])[:, None, :], (nf, 8, hp))
    f = lambda a: np.ascontiguousarray(a, dtype=np.float32)
    return dict(jf=jf, ju=ju, dc=f(np.cos(d)), ds=f(np.sin(d)), rowc=f(np.cos(row)),
                rows=f(np.sin(row)), nf=nf, hp=hp, off=off, n_real=n_real,
                m_last=half + off, r_last=r_last, i_mixed=i_mixed)


def _seq_dft(pq, pqm, seq, n_meta):
    c = _dft_constants(seq, n_meta)
    t, nf, hp, ncol = DFT_TILE, c["nf"], c["hp"], pq.shape[2]
    ncb = ncol // FNET_WIDTH
    col_blk = lambda rows: pl.BlockSpec((1, rows, FNET_WIDTH), lambda g, b: (g, 0, b))
    peqo = pl.pallas_call(
        functools.partial(_fold_kernel, m_last=c["m_last"]),
        grid=(2, ncb),
        in_specs=[col_blk(seq), col_blk(t), _resident(c["jf"].shape)],
        out_specs=col_blk(hp),
        out_shape=jax.ShapeDtypeStruct((2, hp, ncol), BF16),
        compiler_params=pltpu.CompilerParams(
            dimension_semantics=("arbitrary", "arbitrary"), vmem_limit_bytes=V7X_VMEM_LIMIT),
        name="dft_fold",
    )(pq, jnp.pad(pqm, ((0, 0), (c["off"], 0), (0, 0))), jnp.asarray(c["jf"]).astype(BF16))

    scale = float((seq + n_meta) * FNET_GROUP_DIM) ** -0.5
    tile = pl.BlockSpec((t, ncol), lambda i: (i, 0))
    rowspec = pl.BlockSpec((1, 8, hp), lambda i: (i, 0, 0))
    u, w = pl.pallas_call(
        functools.partial(_dft_kernel, scale=scale),
        grid=(nf,),
        in_specs=[_resident((2, hp, ncol)), _resident((t, hp)), _resident((t, hp)),
                  rowspec, rowspec],
        out_specs=(tile, tile),
        out_shape=(jax.ShapeDtypeStruct((hp, ncol), BF16),) * 2,
        compiler_params=pltpu.CompilerParams(
            dimension_semantics=("arbitrary",), vmem_limit_bytes=V7X_VMEM_LIMIT),
        name="seq_dft",
    )(peqo, jnp.asarray(c["dc"]), jnp.asarray(c["ds"]), jnp.asarray(c["rowc"]),
      jnp.asarray(c["rows"]))

    col = lambda rows: pl.BlockSpec((rows, FNET_WIDTH), lambda b: (0, b))
    return pl.pallas_call(
        functools.partial(_unfold_kernel, r_last=c["r_last"], i_mixed=c["i_mixed"]),
        grid=(ncb,),
        in_specs=[col(hp), col(hp), _resident((2, t, 2 * t))],
        out_specs=col(seq),
        out_shape=jax.ShapeDtypeStruct((seq, ncol), BF16),
        compiler_params=pltpu.CompilerParams(
            dimension_semantics=("arbitrary",), vmem_limit_bytes=V7X_VMEM_LIMIT),
        name="dft_unfold",
    )(u, w, jnp.asarray(c["ju"][1:]).astype(BF16))


FFN_CHUNK = 1024
TAIL_SPLIT = 2
TAIL_TM = 512


def _tail_kernel(x_ref, ol_ref, of_ref, ob_ref, y_ref, nmix_ref, gn_ref, nffn_ref, nfin_ref,
                 wg_ref, wm_ref, wo_ref, wf_ref, wout_ref, w1_ref, w2_ref, o_ref):
    tm = x_ref.shape[0]
    sub = tm // TAIL_SPLIT
    tiles = [(dict(), slice(i * sub, (i + 1) * sub)) for i in range(TAIL_SPLIT)]
    dot = functools.partial(jnp.dot, preferred_element_type=F32)
    nchunk = D_FF // FFN_CHUNK

    def norm_in(s, r):
        s["x"] = x_ref[r, :]
        s["xn"] = _rms(s["x"], nmix_ref[...]).astype(BF16)

    def gates(s, r):
        s["g"] = [dot(s["xn"], wg_ref[:, h * HEAD_DV:(h + 1) * HEAD_DV])
                  for h in range(GLA_HEADS)]
        s["ga"] = dot(s["xn"], wm_ref[:, 0:D_MODEL])
        s["gf"] = dot(s["xn"], wm_ref[:, D_MODEL:2 * D_MODEL])
        s["y_fnet"] = dot(y_ref[r, :], wf_ref[...])

    def gla_out(s, r):
        y_gla = None
        for h in range(GLA_HEADS):
            hv = slice(h * HEAD_DV, (h + 1) * HEAD_DV)
            o = (ol_ref[r, hv].astype(F32) + of_ref[r, hv].astype(F32)
                 + ob_ref[r, hv].astype(F32))
            g = s["g"][h]
            og = (_rms(o, gn_ref[...]) * (g * _sigmoid(g))).astype(BF16)
            part = dot(og, wo_ref[hv, :])
            y_gla = part if y_gla is None else y_gla + part
        s["y_gla"] = y_gla

    def merge(s, r):
        merged = _sigmoid(s["ga"]) * s["y_gla"] + _sigmoid(s["gf"]) * s["y_fnet"]
        s["h1"] = s["x"] + dot(merged.astype(BF16), wout_ref[...])
        s["u"] = _rms(s["h1"], nffn_ref[...]).astype(BF16)
        s["acc"] = s["h1"]

    def ffn_up(c):
        def stage(s, r):
            a = jnp.maximum(dot(s["u"], w1_ref[:, c * FFN_CHUNK:(c + 1) * FFN_CHUNK]), 0.0)
            s["a", c] = (a * a).astype(BF16)
        return stage

    def ffn_down(c):
        def stage(s, r):
            s["acc"] = s["acc"] + dot(s.pop(("a", c)),
                                      w2_ref[c * FFN_CHUNK:(c + 1) * FFN_CHUNK, :])
        return stage

    def norm_out(s, r):
        o_ref[r, :] = _rms(s["acc"], nfin_ref[...])

    stages = [norm_in, gates, gla_out, merge, ffn_up(0)]
    for c in range(nchunk):
        if c + 1 < nchunk:
            stages.append(ffn_up(c + 1))
        stages.append(ffn_down(c))
    stages.append(norm_out)
    _run_interleaved(stages, tiles)


def _tail_call(x2d, ol, o_f, o_b, y, nmix, gn, nffn, nfin, w_in_bf, w_merge, wo, wf, wout, w1,
               w2, seq, tm):
    m = x2d.shape[0]
    nt = seq // tm
    row = lambda n: pl.BlockSpec((tm, n), lambda i: (i, 0))
    norms = [nmix, gn, nffn, nfin]
    consts = [w_merge, wo, wf, wout, w1, w2]
    return pl.pallas_call(
        _tail_kernel,
        grid=(m // tm,),
        in_specs=[row(D_MODEL), row(GLA_DV), row(GLA_DV), row(GLA_DV),
                  pl.BlockSpec((tm, FNET_WIDTH), lambda i: (i % nt, i // nt))]
                 + [_resident(c.shape) for c in norms]
                 + [_window((w_in_bf.shape[0], W_WINDOW), (0, 2))]
                 + [_resident(c.shape) for c in consts],
        out_specs=row(D_MODEL),
        out_shape=jax.ShapeDtypeStruct((m, D_MODEL), F32),
        compiler_params=pltpu.CompilerParams(
            dimension_semantics=("arbitrary",), vmem_limit_bytes=V7X_VMEM_LIMIT),
        name="tail",
    )(x2d, ol, o_f, o_b, y, *norms, w_in_bf, *consts)


@functools.lru_cache(maxsize=None)
def _gla_constants(rows):
    idx = np.arange(rows)
    same = (idx[:, None] // SUB) == (idx[None, :] // SUB)
    tri_f = (same & (idx[None, :] <= idx[:, None])).astype(np.float32)
    return tri_f, np.ascontiguousarray(tri_f.T)


@functools.lru_cache(maxsize=None)
def _channel_dft():
    d = np.arange(FNET_GROUP_DIM, dtype=np.int64)
    ang = 2.0 * np.pi * ((d[:, None] * d[None, :]) % FNET_GROUP_DIM) / FNET_GROUP_DIM
    cs = np.zeros((2 * FNET_GROUP_DIM, 2 * FNET_GROUP_DIM), np.float32)
    cs[F_LEAD:F_LEAD + FNET_GROUP_DIM] = np.concatenate([np.cos(ang), np.sin(ang)], axis=1)
    return cs


def kernel(x, meta_tokens, norm_mix, w_in, w_decay_fwd, b_decay_fwd, w_decay_bwd, b_decay_bwd,
           gla_norm, w_o_gla, w_fnet, w_out, norm_ffn, w_ff1, w_ff2, norm_final):
    batch, seq, d = x.shape
    assert d == D_MODEL and meta_tokens.shape == (N_META, D_MODEL)
    assert w_in.shape[0] == 1 and seq % (2 * SCAN_RB) == 0 and seq % DFT_TILE == 0
    gate0 = 2 * GLA_DK + 2 * GLA_DV + 2 * DECAY_RANK + FNET_WIDTH
    assert w_in.shape[2] == gate0 + 2 * D_MODEL and gate0 <= 4 * W_WINDOW
    w_in_t = jnp.swapaxes(w_in[0], 0, 1)
    wi = _transpose_cast_call(w_in_t, 4 * W_WINDOW, W_WINDOW // 2)
    wdec = jnp.zeros((R_PAD, 2 * GLA_DK), F32)
    wdec = wdec.at[:DECAY_RANK, :GLA_DK].set(w_decay_fwd[0])
    wdec = wdec.at[DECAY_RANK:2 * DECAY_RANK, GLA_DK:].set(w_decay_bwd[0]).astype(BF16)
    bdec = jnp.concatenate([b_decay_fwd[0], b_decay_bwd[0]])[None, :]
    cs = jnp.asarray(_channel_dft())
    nw_mix = norm_mix[0][None, :]

    x2d = x.reshape(batch * seq, D_MODEL)
    tail_weights = (w_o_gla[0], w_fnet[0], w_out[0], w_ff1[0], w_ff2[0])
    v, ol, qe, ke, bend, pq, *tail_weights, w_merge = _mixin_call(
        x2d, nw_mix, wi, wdec, bdec, cs, *_gla_constants(GLA_C), seq, MIX_TM, GLA_C,
        cast=tail_weights, gate_t=w_in_t, gate_row0=gate0)
    meta_chunk = jnp.pad(meta_tokens, ((SUB - N_META, 0), (0, 0)))
    vm, _, _, kem, _, pqm = _mixin_call(meta_chunk, nw_mix, wi, wdec, bdec, cs,
                                        *_gla_constants(SUB), SUB, SUB, SUB)
    o_f, o_b = _gla_scan_call(qe, ke, v, bend, kem, vm, seq, batch)

    pqm = jnp.tile(pqm[:, SUB - N_META:], (1, 1, batch))
    y = _seq_dft(pq, pqm, seq, N_META)

    out = _tail_call(x2d, ol, o_f, o_b, y, nw_mix, gla_norm[0][None, :], norm_ffn[0][None, :],
                     norm_final[None, :], wi, w_merge, *tail_weights, seq, TAIL_TM)
    return out.reshape(batch, seq, D_MODEL)
```

```python
import functools

import numpy as np
import jax
import jax.numpy as jnp
from jax import lax
from jax.experimental import pallas as pl
from jax.experimental.pallas import tpu as pltpu

F32 = jnp.float32
BF16 = jnp.bfloat16

D_MODEL = 1024
N_META = 16
GLA_HEADS = 4
GLA_DK = 512
GLA_DV = 1024
HEAD_DK = GLA_DK // GLA_HEADS
HEAD_DV = GLA_DV // GLA_HEADS
DECAY_RANK = 16
GATE_TEMP = 16.0
SUB = 64
FNET_GROUPS = 4
FNET_WIDTH = 512
FNET_GROUP_DIM = FNET_WIDTH // FNET_GROUPS
D_FF = 4 * D_MODEL
EPS = 1e-6
LOG2E = 1.4426950408889634
LANE = 128

_OFF_Q, _OFF_K, _OFF_V, _OFF_QKV_END = 0, 512, 1024, 2048
W_WINDOW = 1024
R_PAD = 128
F_LEAD = 2 * DECAY_RANK
F_EXT = F_LEAD + FNET_WIDTH + (LANE - F_LEAD)

GLA_C = 4 * SUB
MIX_TM = 1024
GATE_SUB = 32

V7X_VMEM_LIMIT = 58 * 1024 * 1024


def _rms(x, w):
    ms = jnp.mean(x * x, axis=-1, keepdims=True)
    return x * lax.rsqrt(ms + EPS) * w


def _sigmoid(x):
    return 1.0 / (1.0 + jnp.exp(-x))


def _run_interleaved(stages, tiles):
    for stage in stages:
        for state, rows in tiles:
            stage(state, rows)


def _window(block, index):
    return pl.BlockSpec(block, lambda *_: index, pipeline_mode=pl.Buffered(1))


def _resident(shape):
    return _window(shape, (0,) * len(shape))


_NT = (((1,), (1,)), ((), ()))
_TN = (((0,), (0,)), ((), ()))


def _mixin_kernel(x_ref, nw_ref, wqkv_ref, wrf_ref, wdf_ref, wdb_ref, bdf_ref, bdb_ref, cs_ref,
                  trif_ref, trib_ref, maskf_ref, maskb_ref, *refs, chunk, n_cast, n_gate):
    n_in = n_cast + n_gate
    cast_in, gate_in = refs[:n_cast], refs[n_cast:n_in]
    v_ref, ol_ref, qe_ref, ke_ref, bend_ref, pq_ref = refs[n_in:n_in + 6]
    cast_out = refs[n_in + 6:n_in + 6 + n_cast]
    _mixin_body(x_ref, nw_ref, wqkv_ref, wrf_ref, (wdf_ref, wdb_ref), (bdf_ref, bdb_ref), cs_ref,
                trif_ref, trib_ref,
                maskf_ref, maskb_ref, v_ref, ol_ref, qe_ref, ke_ref, bend_ref, pq_ref, chunk=chunk)
    for src, dst in zip(cast_in, cast_out):
        dst[...] = src[...].astype(BF16)
    if n_gate:
        feat_major = jnp.concatenate([r[...] for r in gate_in], axis=0)
        refs[-1][...] = feat_major.T.astype(BF16)


def _mixin_body(x_ref, nw_ref, wqkv_ref, wrf_ref, wdec_refs, bdec_refs, cs_ref,
                trif_ref, trib_ref, maskf_ref, maskb_ref,
                v_ref, ol_ref, qe_ref, ke_ref, bend_ref, pq_ref, *, chunk):
    tm = x_ref.shape[0]
    n, s = chunk // SUB, SUB
    tiles = [(dict(), slice(i * chunk, (i + 1) * chunk)) for i in range(tm // chunk)]
    dot = functools.partial(jnp.dot, preferred_element_type=F32)
    cs = cs_ref[...].astype(BF16)
    tris = (trif_ref[...], trib_ref[...])
    masks = (maskf_ref[0:s, 0:s] != 0.0, maskb_ref[0:s, 0:s] != 0.0)
    half = GLA_DK // 2
    zeros = lambda rows: jnp.zeros((rows, GLA_DK), F32)
    wdec = jnp.concatenate([
        jnp.concatenate([wdec_refs[0][...], zeros(R_PAD - DECAY_RANK)], axis=0),
        jnp.concatenate([zeros(DECAY_RANK), wdec_refs[1][...], zeros(R_PAD - 2 * DECAY_RANK)],
                        axis=0)], axis=1).astype(BF16)
    bdec = jnp.concatenate([bdec_refs[0][...], bdec_refs[1][...]], axis=1)

    def norm_in(st, r):
        st["xn"] = _rms(x_ref[r, :], nw_ref[...]).astype(BF16)

    def decay_rank(st, r):
        st["f"] = dot(st["xn"], wrf_ref[:, 0:F_EXT]).astype(BF16)
        st["r"] = st["f"][:, 0:R_PAD]

    def log_decay(st, j):
        cols = slice(j * half, (j + 1) * half)
        x = dot(st["r"], wdec[:, cols]) + bdec[:, cols]
        la = (jnp.minimum(x, 0.0) - jnp.log(1.0 + jnp.exp(-jnp.abs(x)))) * (LOG2E / GATE_TEMP)
        hi = la.astype(BF16)
        st["la", j] = (hi, (la - hi.astype(F32)).astype(BF16))

    def project(st, r):
        mm = lambda lo, hi: dot(st["xn"], wqkv_ref[:, lo:hi])
        vmid = (_OFF_V + _OFF_QKV_END) // 2
        log_decay(st, 0)
        st["q"] = mm(_OFF_Q, _OFF_K) * (HEAD_DK ** -0.5)
        log_decay(st, 1)
        st["k"] = mm(_OFF_K, _OFF_V)
        log_decay(st, 2)
        v_ref[r, 0:GLA_DV // 2] = mm(_OFF_V, vmid).astype(BF16)
        log_decay(st, 3)
        v_ref[r, GLA_DV // 2:GLA_DV] = mm(vmid, _OFF_QKV_END).astype(BF16)

    def decay_factors(st, r):
        c0 = r.start // s
        for j in range(4):
            d, qcols = j // 2, slice((j % 2) * half, (j % 2 + 1) * half)
            cols = slice(j * half, (j + 1) * half)
            hi, lo = st.pop(("la", j))
            b = (dot(tris[d], hi) + dot(tris[d], lo)).reshape(n, s, half)
            if d == 0:
                b_mid, b_end = b[:, s // 2 - 1:s // 2, :], b[:, s - 1:s, :]
            else:
                b_mid, b_end = b[:, s // 2:s // 2 + 1, :], b[:, 0:1, :]
            qd = st["q"][:, qcols].reshape(n, s, half) * jnp.exp2(b - b_mid)
            kd = st["k"][:, qcols].reshape(n, s, half) * jnp.exp2(b_mid - b)
            qe = qd * jnp.exp2(b_mid)
            ke = kd * jnp.exp2(b_end - b_mid)
            st["qd", d, j % 2] = qd.reshape(chunk, half).astype(BF16)
            st["kd", d, j % 2] = kd.reshape(chunk, half).astype(BF16)
            qe_ref[r, cols] = qe.reshape(chunk, half).astype(BF16)
            ke_ref[r, cols] = ke.reshape(chunk, half).astype(BF16)
            bend_ref[c0:c0 + n, cols] = b_end.reshape(n, half)

    def intra_chunk(st, r):
        for h in range(GLA_HEADS):
            hk = slice((h % 2) * HEAD_DK, (h % 2 + 1) * HEAD_DK)
            hv = slice(h * HEAD_DV, (h + 1) * HEAD_DV)
            sc = [[lax.dot_general(st["qd", d, h // 2][c * s:(c + 1) * s, hk],
                                   st["kd", d, h // 2][c * s:(c + 1) * s, hk], _NT,
                                   preferred_element_type=F32) for d in range(2)]
                  for c in range(n)]
            for c, (sf, sb) in enumerate(sc):
                rows = slice(r.start + c * s, r.start + (c + 1) * s)
                both = (jnp.where(masks[0], sf, 0.0) + jnp.where(masks[1], sb, 0.0)).astype(BF16)
                ol_ref[rows, hv] = dot(both, v_ref[rows, hv]).astype(BF16)

    def channel_dft(st, r):
        for grp in range(FNET_GROUPS):
            sl = slice(grp * FNET_GROUP_DIM, (grp + 1) * FNET_GROUP_DIM)
            pq = dot(st["f"][:, grp * FNET_GROUP_DIM:(grp + 2) * FNET_GROUP_DIM], cs)
            pq_ref[0, r, sl] = pq[:, :FNET_GROUP_DIM].astype(BF16)
            pq_ref[1, r, sl] = pq[:, FNET_GROUP_DIM:].astype(BF16)

    _run_interleaved((norm_in, decay_rank, project, decay_factors, intra_chunk, channel_dft),
                     tiles)


def _mixin_call(x2d, nw, w_in_bf, wdec, bdec, cs, tri_f, tri_b, rows_per_batch, tm, chunk,
                cast=(), gate_t=None, gate_row0=0):
    m = x2d.shape[0]
    d_model = w_in_bf.shape[0]
    nt = rows_per_batch // tm
    nb = m // rows_per_batch
    steps = m // tm
    slab = lambda w: pl.BlockSpec((w.shape[0] // steps, w.shape[1]), lambda i: (i, 0))
    assert all(w.shape[0] % (16 * steps) == 0 for w in cast)
    gate_specs, gate_out_spec, gate_out_shape = [], (), ()
    if gate_t is not None:
        feat = 2 * D_MODEL // steps
        assert feat % LANE == 0 and feat % GATE_SUB == 0 and gate_row0 % GATE_SUB == 0
        assert gate_t.shape == (gate_row0 + 2 * D_MODEL, d_model)
        nsub, blk0 = feat // GATE_SUB, gate_row0 // GATE_SUB
        gate_specs = [pl.BlockSpec((GATE_SUB, d_model), lambda i, j=j: (blk0 + i * nsub + j, 0))
                      for j in range(nsub)]
        gate_out_spec = (pl.BlockSpec((d_model, feat), lambda i: (0, i)),)
        gate_out_shape = (jax.ShapeDtypeStruct((d_model, 2 * D_MODEL), BF16),)
    row = lambda n: pl.BlockSpec((tm, n), lambda i: (i, 0))
    pq_spec = pl.BlockSpec((2, tm, FNET_WIDTH), lambda i: (0, i % nt, i // nt))
    wide = jax.ShapeDtypeStruct((m, 2 * GLA_DK), BF16)
    pq_shape = jax.ShapeDtypeStruct((2, rows_per_batch, nb * FNET_WIDTH), BF16)
    consts = [*wdec, *bdec, cs,
              jnp.asarray(tri_f).astype(BF16), jnp.asarray(tri_b).astype(BF16),
              jnp.asarray(tri_f), jnp.asarray(tri_b)]
    return pl.pallas_call(
        functools.partial(_mixin_kernel, chunk=chunk, n_cast=len(cast), n_gate=len(gate_specs)),
        grid=(steps,),
        in_specs=[row(D_MODEL), _resident(nw.shape),
                  _window((d_model, _OFF_QKV_END), (0, 0)),
                  _window((d_model, W_WINDOW), (0, 3))]
                 + [_resident(c.shape) for c in consts] + [slab(w) for w in cast] + gate_specs,
        out_specs=(row(GLA_DV), row(GLA_DV), row(2 * GLA_DK), row(2 * GLA_DK),
                   pl.BlockSpec((tm // SUB, 2 * GLA_DK), lambda i: (i, 0)), pq_spec)
                  + tuple(slab(w) for w in cast) + gate_out_spec,
        out_shape=(jax.ShapeDtypeStruct((m, GLA_DV), BF16), jax.ShapeDtypeStruct((m, GLA_DV), BF16),
                   wide, wide, jax.ShapeDtypeStruct((m // SUB, 2 * GLA_DK), F32), pq_shape)
                  + tuple(jax.ShapeDtypeStruct(w.shape, BF16) for w in cast) + gate_out_shape,
        compiler_params=pltpu.CompilerParams(
            dimension_semantics=("arbitrary",), vmem_limit_bytes=V7X_VMEM_LIMIT),
        name="mixin",
    )(x2d, nw, w_in_bf, w_in_bf, *consts, *cast, *([gate_t] * len(gate_specs)))


def _transpose_cast_kernel(x_ref, o_ref):
    o_ref[...] = x_ref[...].T.astype(BF16)


def _transpose_cast_call(w_t, n_rows, block):
    d_model = w_t.shape[1]
    assert n_rows % block == 0 and n_rows <= w_t.shape[0]
    return pl.pallas_call(
        _transpose_cast_kernel,
        grid=(n_rows // block,),
        in_specs=[pl.BlockSpec((block, d_model), lambda i: (i, 0))],
        out_specs=pl.BlockSpec((d_model, block), lambda i: (0, i)),
        out_shape=jax.ShapeDtypeStruct((d_model, n_rows), BF16),
        compiler_params=pltpu.CompilerParams(
            dimension_semantics=("arbitrary",), vmem_limit_bytes=V7X_VMEM_LIMIT),
        name="w_in_prep",
    )(w_t)


SCAN_G = 512
SCAN_RB = 4096


def _scan_group(dirs):
    s, n = SUB, SCAN_G // SUB
    ahead = 2
    dec, st, inter = [], [], [{}, {}]
    for qe_ref, ke_ref, v_ref, bend_ref, st_ref, row0, chunk0, backward in dirs:
        dec_rows = jnp.tile(jnp.exp2(bend_ref[chunk0:chunk0 + n, :]), (HEAD_DK // n, 1))
        dec.append(dec_rows.T)
        st.append(st_ref[...])

    def order(step, backward):
        return n - 1 - step if backward else step

    def increment(d, step):
        _, ke_ref, v_ref, _, _, row0, _, backward = dirs[d]
        i = order(step, backward)
        sub = slice(row0 + i * s, row0 + (i + 1) * s)
        return lax.dot_general(ke_ref[sub, :], v_ref[sub, :], _TN,
                               preferred_element_type=F32)

    inc = {(d, step): increment(d, step) for step in range(ahead) for d in range(2)}
    for step in range(n):
        for d, (qe_ref, _, _, _, _, row0, _, backward) in enumerate(dirs):
            i = order(step, backward)
            sub = slice(row0 + i * s, row0 + (i + 1) * s)
            inter[d][i] = jnp.dot(qe_ref[sub, :], st[d].astype(BF16),
                                  preferred_element_type=F32)
            if step + ahead < n:
                inc[d, step + ahead] = increment(d, step + ahead)
            decay = jnp.broadcast_to(dec[d][:, i:i + 1], (HEAD_DK, HEAD_DV))
            st[d] = st[d] * decay + inc.pop((d, step))
    for d, spec in enumerate(dirs):
        spec[4][...] = st[d]
    return [jnp.concatenate([inter[d][i] for i in range(n)], axis=0) for d in range(2)]


def _gla_scan_kernel(qef_ref, kef_ref, vf_ref, bendf_ref,
                     qeb_ref, keb_ref, vb_ref, bendb_ref, kem_ref, vm_ref,
                     of_ref, ob_ref, stf_ref, stb_ref):
    rb = qef_ref.shape[0]
    ng = rb // SCAN_G

    @pl.when(pl.program_id(2) == 0)
    def _():
        stf_ref[...] = lax.dot_general(kem_ref[...], vm_ref[...], _TN,
                                       preferred_element_type=F32)
        stb_ref[...] = jnp.zeros_like(stb_ref)

    for j in range(ng):
        jb = ng - 1 - j
        cpg = SCAN_G // SUB
        in_f, in_b = _scan_group([
            (qef_ref, kef_ref, vf_ref, bendf_ref, stf_ref, j * SCAN_G, j * cpg, False),
            (qeb_ref, keb_ref, vb_ref, bendb_ref, stb_ref, jb * SCAN_G, jb * cpg, True)])
        of_ref[j * SCAN_G:(j + 1) * SCAN_G, :] = in_f.astype(BF16)
        ob_ref[jb * SCAN_G:(jb + 1) * SCAN_G, :] = in_b.astype(BF16)


def _gla_scan_call(qe, ke, v, bend, kem, vm, seq, batch):
    h, rb = GLA_HEADS, SCAN_RB
    nt = seq // rb
    fwd = lambda b, hh, t: b * nt + t
    bwd = lambda b, hh, t: b * nt + nt - 1 - t
    cb = rb // SUB

    def dk_blk(rowfn, dircol, rows):
        return pl.BlockSpec((rows, HEAD_DK), lambda b, hh, t: (rowfn(b, hh, t), dircol + hh))

    def dv_blk(rowfn):
        return pl.BlockSpec((rb, HEAD_DV), lambda b, hh, t: (rowfn(b, hh, t), hh))

    in_specs = [
        dk_blk(fwd, 0, rb), dk_blk(fwd, 0, rb), dv_blk(fwd), dk_blk(fwd, 0, cb),
        dk_blk(bwd, h, rb), dk_blk(bwd, h, rb), dv_blk(bwd), dk_blk(bwd, h, cb),
        pl.BlockSpec((SUB, HEAD_DK), lambda b, hh, t: (0, hh)),
        pl.BlockSpec((SUB, HEAD_DV), lambda b, hh, t: (0, hh)),
    ]
    out = jax.ShapeDtypeStruct((batch * seq, GLA_DV), BF16)
    return pl.pallas_call(
        _gla_scan_kernel,
        grid=(batch, h, nt),
        in_specs=in_specs,
        out_specs=(dv_blk(fwd), dv_blk(bwd)),
        out_shape=(out, out),
        scratch_shapes=[pltpu.VMEM((HEAD_DK, HEAD_DV), F32), pltpu.VMEM((HEAD_DK, HEAD_DV), F32)],
        compiler_params=pltpu.CompilerParams(
            dimension_semantics=("arbitrary", "arbitrary", "arbitrary"),
            vmem_limit_bytes=V7X_VMEM_LIMIT),
        name="gla_scan",
    )(qe, ke, v, bend, qe, ke, v, bend, kem, vm)


DFT_TILE = 256


def _fold_kernel(x_ref, z_ref, j_ref, o_ref, *, m_last):
    t = DFT_TILE
    n_real = x_ref.shape[1] // t
    sign = jnp.where(pl.program_id(0) == 0, 1.0, -1.0)
    real = lambda i: x_ref[0, i * t:(i + 1) * t, :]
    lead = jnp.concatenate([jnp.zeros((t - z_ref.shape[1], z_ref.shape[2]), BF16), z_ref[0]],
                           axis=0)
    for i in range(o_ref.shape[1] // t):
        direct = (lead if i == 0 else real(i - 1)).astype(F32)
        mirror = (jnp.dot(j_ref[i, :, :t], real(min(n_real - i, n_real - 1)),
                          preferred_element_type=F32)
                  + jnp.dot(j_ref[i, :, t:], real(n_real - 1 - i), preferred_element_type=F32))
        m = i * t + lax.broadcasted_iota(jnp.int32, (t, 1), 0)
        folded = jnp.where(m <= m_last, direct + sign * mirror, 0.0)
        o_ref[0, i * t:(i + 1) * t, :] = folded.astype(BF16)


def _dft_kernel(peqo_ref, dc_ref, ds_ref, rowc_ref, rows_ref, u_ref, w_ref, *, scale):
    t = DFT_TILE
    acc_a = acc_b = None
    for c in range(peqo_ref.shape[1] // t):
        sl = slice(c * t, (c + 1) * t)
        rc, rs = rowc_ref[0, 0:1, sl], rows_ref[0, 0:1, sl]
        dc, ds = dc_ref[:, sl], ds_ref[:, sl]
        tc = (dc * rc - ds * rs).astype(BF16)
        ts = (ds * rc + dc * rs).astype(BF16)
        a = jnp.dot(tc, peqo_ref[0, sl, :], preferred_element_type=F32)
        b = jnp.dot(ts, peqo_ref[1, sl, :], preferred_element_type=F32)
        acc_a = a if acc_a is None else acc_a + a
        acc_b = b if acc_b is None else acc_b + b
    u_ref[...] = ((acc_a - acc_b) * scale).astype(BF16)
    w_ref[...] = ((acc_a + acc_b) * scale).astype(BF16)


def _unfold_kernel(u_ref, w_ref, j_ref, y_ref, *, r_last, i_mixed):
    t = DFT_TILE
    tile = lambda ref, i: ref[i * t:(i + 1) * t, :]
    n_real = y_ref.shape[0] // t
    for i in range(n_real):
        if i < i_mixed:
            y_ref[i * t:(i + 1) * t, :] = tile(u_ref, i + 1)
            continue
        j = 0 if i == i_mixed else 1
        mirror = jnp.dot(j_ref[j, :, t:], tile(w_ref, n_real - 1 - i), preferred_element_type=F32)
        if (n_real - i + 1) * t <= w_ref.shape[0]:
            mirror += jnp.dot(j_ref[j, :, :t], tile(w_ref, n_real - i),
                              preferred_element_type=F32)
        mirror = mirror.astype(BF16)
        if i == i_mixed:
            r = i * t + lax.broadcasted_iota(jnp.int32, (t, 1), 0)
            mirror = jnp.where(r <= r_last, tile(u_ref, i + 1), mirror)
        y_ref[i * t:(i + 1) * t, :] = mirror


@functools.lru_cache(maxsize=None)
def _dft_constants(seq, n_meta):
    t = DFT_TILE
    off = t - n_meta
    big_l = seq + n_meta
    half = big_l // 2
    n_real = seq // t
    nf = -(-(half + 1 + off) // t)
    hp = nf * t
    jf = np.zeros((nf, t, 2 * t), np.float32)
    for i in range(nf):
        a_idx, b_idx = min(n_real - i, n_real - 1), n_real - 1 - i
        for j in range(t):
            pos = i * t + j - off
            if 1 <= pos <= half - 1:
                tau, rho = divmod(big_l - pos - n_meta, t)
                assert tau in (a_idx, b_idx)
                jf[i, j, (t if tau == b_idx else 0) + rho] = 1.0
    r_last = half - n_meta
    i_mixed = r_last // t
    ju = np.zeros((3, t, 2 * t), np.float32)
    for i in range(i_mixed, n_real):
        a_idx, b_idx = n_real - i, n_real - 1 - i
        blk = np.zeros((t, 2 * t), np.float32)
        for j in range(t):
            r = i * t + j
            if r > r_last:
                tau, rho = divmod(big_l - (r + n_meta) + off, t)
                assert tau in (a_idx, b_idx)
                blk[j, (t if tau == b_idx else 0) + rho] = 1.0
        var = 1 if i == i_mixed else 2
        assert var == 1 or not ju[2].any() or (ju[2] == blk).all()
        ju[var] = blk
    ang = lambda prod: 2.0 * np.pi * (prod % big_l).astype(np.float64) / big_l
    pos = np.arange(hp, dtype=np.int64) - off
    d = ang(np.arange(t, dtype=np.int64)[:, None] * pos[None, :])
    k0 = t * np.arange(nf, dtype=np.int64) - off
    row = np.broadcast_to(ang(k0[:, None] * pos[None, :])[:, None, :], (nf, 8, hp))
    f = lambda a: np.ascontiguousarray(a, dtype=np.float32)
    return dict(jf=jf, ju=ju, dc=f(np.cos(d)), ds=f(np.sin(d)), rowc=f(np.cos(row)),
                rows=f(np.sin(row)), nf=nf, hp=hp, off=off, n_real=n_real,
                m_last=half + off, r_last=r_last, i_mixed=i_mixed)


def _seq_dft(pq, pqm, seq, n_meta):
    c = _dft_constants(seq, n_meta)
    t, nf, hp, ncol = DFT_TILE, c["nf"], c["hp"], pq.shape[2]
    ncb = ncol // FNET_WIDTH
    assert c["off"] == t - n_meta and pqm.shape[2] == FNET_WIDTH and pqm.shape[1] <= t
    col_blk = lambda rows: pl.BlockSpec((1, rows, FNET_WIDTH), lambda g, b: (g, 0, b))
    meta_blk = pl.BlockSpec((1, pqm.shape[1], FNET_WIDTH), lambda g, b: (g, 0, 0))
    peqo = pl.pallas_call(
        functools.partial(_fold_kernel, m_last=c["m_last"]),
        grid=(2, ncb),
        in_specs=[col_blk(seq), meta_blk, _resident(c["jf"].shape)],
        out_specs=col_blk(hp),
        out_shape=jax.ShapeDtypeStruct((2, hp, ncol), BF16),
        compiler_params=pltpu.CompilerParams(
            dimension_semantics=("arbitrary", "arbitrary"), vmem_limit_bytes=V7X_VMEM_LIMIT),
        name="dft_fold",
    )(pq, pqm, jnp.asarray(c["jf"]).astype(BF16))

    scale = float((seq + n_meta) * FNET_GROUP_DIM) ** -0.5
    tile = pl.BlockSpec((t, ncol), lambda i: (i, 0))
    rowspec = pl.BlockSpec((1, 8, hp), lambda i: (i, 0, 0))
    u, w = pl.pallas_call(
        functools.partial(_dft_kernel, scale=scale),
        grid=(nf,),
        in_specs=[_resident((2, hp, ncol)), _resident((t, hp)), _resident((t, hp)),
                  rowspec, rowspec],
        out_specs=(tile, tile),
        out_shape=(jax.ShapeDtypeStruct((hp, ncol), BF16),) * 2,
        compiler_params=pltpu.CompilerParams(
            dimension_semantics=("arbitrary",), vmem_limit_bytes=V7X_VMEM_LIMIT),
        name="seq_dft",
    )(peqo, jnp.asarray(c["dc"]), jnp.asarray(c["ds"]), jnp.asarray(c["rowc"]),
      jnp.asarray(c["rows"]))

    col = lambda rows: pl.BlockSpec((rows, FNET_WIDTH), lambda b: (0, b))
    return pl.pallas_call(
        functools.partial(_unfold_kernel, r_last=c["r_last"], i_mixed=c["i_mixed"]),
        grid=(ncb,),
        in_specs=[col(hp), col(hp), _resident((2, t, 2 * t))],
        out_specs=col(seq),
        out_shape=jax.ShapeDtypeStruct((seq, ncol), BF16),
        compiler_params=pltpu.CompilerParams(
            dimension_semantics=("arbitrary",), vmem_limit_bytes=V7X_VMEM_LIMIT),
        name="dft_unfold",
    )(u, w, jnp.asarray(c["ju"][1:]).astype(BF16))


FFN_CHUNK = 1024
TAIL_SPLIT = 2
TAIL_TM = 512


def _tail_kernel(x_ref, ol_ref, of_ref, ob_ref, y_ref, nmix_ref, gn_ref, nffn_ref, nfin_ref,
                 wg_ref, wm_ref, wo_ref, wf_ref, wout_ref, w1_ref, w2_ref, o_ref):
    tm = x_ref.shape[0]
    sub = tm // TAIL_SPLIT
    tiles = [(dict(), slice(i * sub, (i + 1) * sub)) for i in range(TAIL_SPLIT)]
    dot = functools.partial(jnp.dot, preferred_element_type=F32)
    nchunk = D_FF // FFN_CHUNK

    def norm_in(s, r):
        s["x"] = x_ref[r, :]
        s["xn"] = _rms(s["x"], nmix_ref[...]).astype(BF16)

    def gates(s, r):
        s["g"] = [dot(s["xn"], wg_ref[:, h * HEAD_DV:(h + 1) * HEAD_DV])
                  for h in range(GLA_HEADS)]
        s["ga"] = dot(s["xn"], wm_ref[:, 0:D_MODEL])
        s["gf"] = dot(s["xn"], wm_ref[:, D_MODEL:2 * D_MODEL])
        s["y_fnet"] = dot(y_ref[r, :], wf_ref[...])

    def gla_out(s, r):
        y_gla = None
        for h in range(GLA_HEADS):
            hv = slice(h * HEAD_DV, (h + 1) * HEAD_DV)
            o = (ol_ref[r, hv].astype(F32) + of_ref[r, hv].astype(F32)
                 + ob_ref[r, hv].astype(F32))
            g = s["g"][h]
            og = (_rms(o, gn_ref[...]) * (g * _sigmoid(g))).astype(BF16)
            part = dot(og, wo_ref[hv, :])
            y_gla = part if y_gla is None else y_gla + part
        s["y_gla"] = y_gla

    def merge(s, r):
        merged = _sigmoid(s["ga"]) * s["y_gla"] + _sigmoid(s["gf"]) * s["y_fnet"]
        s["h1"] = s["x"] + dot(merged.astype(BF16), wout_ref[...])
        s["u"] = _rms(s["h1"], nffn_ref[...]).astype(BF16)
        s["acc"] = s["h1"]

    def ffn_up(c):
        def stage(s, r):
            a = jnp.maximum(dot(s["u"], w1_ref[:, c * FFN_CHUNK:(c + 1) * FFN_CHUNK]), 0.0)
            s["a", c] = (a * a).astype(BF16)
        return stage

    def ffn_down(c):
        def stage(s, r):
            s["acc"] = s["acc"] + dot(s.pop(("a", c)),
                                      w2_ref[c * FFN_CHUNK:(c + 1) * FFN_CHUNK, :])
        return stage

    def norm_out(s, r):
        o_ref[r, :] = _rms(s["acc"], nfin_ref[...])

    stages = [norm_in, gates, gla_out, merge, ffn_up(0)]
    for c in range(nchunk):
        if c + 1 < nchunk:
            stages.append(ffn_up(c + 1))
        stages.append(ffn_down(c))
    stages.append(norm_out)
    _run_interleaved(stages, tiles)


def _tail_call(x2d, ol, o_f, o_b, y, nmix, gn, nffn, nfin, w_in_bf, w_merge, wo, wf, wout, w1,
               w2, seq, tm):
    m = x2d.shape[0]
    nt = seq // tm
    row = lambda n: pl.BlockSpec((tm, n), lambda i: (i, 0))
    norms = [nmix, gn, nffn, nfin]
    consts = [w_merge, wo, wf, wout, w1, w2]
    return pl.pallas_call(
        _tail_kernel,
        grid=(m // tm,),
        in_specs=[row(D_MODEL), row(GLA_DV), row(GLA_DV), row(GLA_DV),
                  pl.BlockSpec((tm, FNET_WIDTH), lambda i: (i % nt, i // nt))]
                 + [_resident(c.shape) for c in norms]
                 + [_window((w_in_bf.shape[0], W_WINDOW), (0, 2))]
                 + [_resident(c.shape) for c in consts],
        out_specs=row(D_MODEL),
        out_shape=jax.ShapeDtypeStruct((m, D_MODEL), F32),
        compiler_params=pltpu.CompilerParams(
            dimension_semantics=("arbitrary",), vmem_limit_bytes=V7X_VMEM_LIMIT),
        name="tail",
    )(x2d, ol, o_f, o_b, y, *norms, w_in_bf, *consts)


@functools.lru_cache(maxsize=None)
def _gla_constants(rows):
    idx = np.arange(rows)
    same = (idx[:, None] // SUB) == (idx[None, :] // SUB)
    tri_f = (same & (idx[None, :] <= idx[:, None])).astype(np.float32)
    return tri_f, np.ascontiguousarray(tri_f.T)


@functools.lru_cache(maxsize=None)
def _channel_dft():
    d = np.arange(FNET_GROUP_DIM, dtype=np.int64)
    ang = 2.0 * np.pi * ((d[:, None] * d[None, :]) % FNET_GROUP_DIM) / FNET_GROUP_DIM
    cs = np.zeros((2 * FNET_GROUP_DIM, 2 * FNET_GROUP_DIM), np.float32)
    cs[F_LEAD:F_LEAD + FNET_GROUP_DIM] = np.concatenate([np.cos(ang), np.sin(ang)], axis=1)
    return cs


def kernel(x, meta_tokens, norm_mix, w_in, w_decay_fwd, b_decay_fwd, w_decay_bwd, b_decay_bwd,
           gla_norm, w_o_gla, w_fnet, w_out, norm_ffn, w_ff1, w_ff2, norm_final):
    batch, seq, d = x.shape
    assert d == D_MODEL and meta_tokens.shape == (N_META, D_MODEL)
    assert w_in.shape[0] == 1 and seq % (2 * SCAN_RB) == 0 and seq % DFT_TILE == 0
    gate0 = 2 * GLA_DK + 2 * GLA_DV + 2 * DECAY_RANK + FNET_WIDTH
    assert w_in.shape[2] == gate0 + 2 * D_MODEL and gate0 <= 4 * W_WINDOW
    w_in_t = jnp.swapaxes(w_in[0], 0, 1)
    wi = _transpose_cast_call(w_in_t, 4 * W_WINDOW, W_WINDOW // 2)
    wdec = (w_decay_fwd[0], w_decay_bwd[0])
    bdec = (b_decay_fwd, b_decay_bwd)
    cs = jnp.asarray(_channel_dft())
    nw_mix = norm_mix[0][None, :]

    x2d = x.reshape(batch * seq, D_MODEL)
    tail_weights = (w_o_gla[0], w_fnet[0], w_out[0], w_ff1[0], w_ff2[0])
    v, ol, qe, ke, bend, pq, *tail_weights, w_merge = _mixin_call(
        x2d, nw_mix, wi, wdec, bdec, cs, *_gla_constants(GLA_C), seq, MIX_TM, GLA_C,
        cast=tail_weights, gate_t=w_in_t, gate_row0=gate0)
    meta_chunk = jnp.pad(meta_tokens, ((SUB - N_META, 0), (0, 0)))
    vm, _, _, kem, _, pqm = _mixin_call(meta_chunk, nw_mix, wi, wdec, bdec, cs,
                                        *_gla_constants(SUB), SUB, SUB, SUB)
    o_f, o_b = _gla_scan_call(qe, ke, v, bend, kem, vm, seq, batch)

    y = _seq_dft(pq, pqm, seq, N_META)

    out = _tail_call(x2d, ol, o_f, o_b, y, nw_mix, gla_norm[0][None, :], norm_ffn[0][None, :],
                     norm_final[None, :], wi, w_merge, *tail_weights, seq, TAIL_TM)
    return out.reshape(batch, seq, D_MODEL)
```

```python
import functools

import numpy as np
import jax
import jax.numpy as jnp
from jax import lax
from jax.experimental import pallas as pl
from jax.experimental.pallas import tpu as pltpu

F32 = jnp.float32
BF16 = jnp.bfloat16

D_MODEL = 1024
N_META = 16
GLA_HEADS = 4
GLA_DK = 512
GLA_DV = 1024
HEAD_DK = GLA_DK // GLA_HEADS
HEAD_DV = GLA_DV // GLA_HEADS
DECAY_RANK = 16
GATE_TEMP = 16.0
SUB = 64
FNET_GROUPS = 4
FNET_WIDTH = 512
FNET_GROUP_DIM = FNET_WIDTH // FNET_GROUPS
D_FF = 4 * D_MODEL
EPS = 1e-6
LOG2E = 1.4426950408889634
LANE = 128

_OFF_Q, _OFF_K, _OFF_V, _OFF_QKV_END = 0, 512, 1024, 2048
W_WINDOW = 1024
R_PAD = 128
F_LEAD = 2 * DECAY_RANK
F_EXT = F_LEAD + FNET_WIDTH + (LANE - F_LEAD)

GLA_C = 4 * SUB
MIX_TM = 1024
GATE_SUB = 32

V7X_VMEM_LIMIT = 58 * 1024 * 1024


def _rms(x, w):
    ms = jnp.mean(x * x, axis=-1, keepdims=True)
    return x * lax.rsqrt(ms + EPS) * w


def _sigmoid(x):
    return 1.0 / (1.0 + jnp.exp(-x))


def _run_interleaved(stages, tiles):
    for stage in stages:
        for state, rows in tiles:
            stage(state, rows)


def _window(block, index):
    return pl.BlockSpec(block, lambda *_: index, pipeline_mode=pl.Buffered(1))


def _resident(shape):
    return _window(shape, (0,) * len(shape))


_NT = (((1,), (1,)), ((), ()))
_TN = (((0,), (0,)), ((), ()))


def _mixin_kernel(x_ref, nw_ref, wqkv_ref, wrf_ref, wdf_ref, wdb_ref, bdf_ref, bdb_ref, cs_ref,
                  trif_ref, trib_ref, maskf_ref, maskb_ref, *refs, chunk, n_cast, n_gate, lead):
    if lead:
        *refs, tile_ref = refs
        tile_ref[...] = jnp.concatenate([jnp.zeros((lead, x_ref.shape[1]), F32), x_ref[...]],
                                        axis=0)
        x_ref = tile_ref
    n_in = n_cast + n_gate
    cast_in, gate_in = refs[:n_cast], refs[n_cast:n_in]
    v_ref, ol_ref, qe_ref, ke_ref, bend_ref, pq_ref = refs[n_in:n_in + 6]
    cast_out = refs[n_in + 6:n_in + 6 + n_cast]
    _mixin_body(x_ref, nw_ref, wqkv_ref, wrf_ref, (wdf_ref, wdb_ref), (bdf_ref, bdb_ref), cs_ref,
                trif_ref, trib_ref,
                maskf_ref, maskb_ref, v_ref, ol_ref, qe_ref, ke_ref, bend_ref, pq_ref, chunk=chunk)
    for src, dst in zip(cast_in, cast_out):
        dst[...] = src[...].astype(BF16)
    if n_gate:
        feat_major = jnp.concatenate([r[...] for r in gate_in], axis=0)
        refs[-1][...] = feat_major.T.astype(BF16)


def _mixin_body(x_ref, nw_ref, wqkv_ref, wrf_ref, wdec_refs, bdec_refs, cs_ref,
                trif_ref, trib_ref, maskf_ref, maskb_ref,
                v_ref, ol_ref, qe_ref, ke_ref, bend_ref, pq_ref, *, chunk):
    tm = x_ref.shape[0]
    n, s = chunk // SUB, SUB
    tiles = [(dict(), slice(i * chunk, (i + 1) * chunk)) for i in range(tm // chunk)]
    dot = functools.partial(jnp.dot, preferred_element_type=F32)
    cs = cs_ref[...].astype(BF16)
    tris = (trif_ref[...], trib_ref[...])
    masks = (maskf_ref[0:s, 0:s] != 0.0, maskb_ref[0:s, 0:s] != 0.0)
    half = GLA_DK // 2
    zeros = lambda rows: jnp.zeros((rows, GLA_DK), F32)
    wdec = jnp.concatenate([
        jnp.concatenate([wdec_refs[0][...], zeros(R_PAD - DECAY_RANK)], axis=0),
        jnp.concatenate([zeros(DECAY_RANK), wdec_refs[1][...], zeros(R_PAD - 2 * DECAY_RANK)],
                        axis=0)], axis=1).astype(BF16)
    bdec = jnp.concatenate([bdec_refs[0][...], bdec_refs[1][...]], axis=1)

    def norm_in(st, r):
        st["xn"] = _rms(x_ref[r, :], nw_ref[...]).astype(BF16)

    def decay_rank(st, r):
        st["f"] = dot(st["xn"], wrf_ref[:, 0:F_EXT]).astype(BF16)
        st["r"] = st["f"][:, 0:R_PAD]

    def log_decay(st, j):
        cols = slice(j * half, (j + 1) * half)
        x = dot(st["r"], wdec[:, cols]) + bdec[:, cols]
        la = (jnp.minimum(x, 0.0) - jnp.log(1.0 + jnp.exp(-jnp.abs(x)))) * (LOG2E / GATE_TEMP)
        hi = la.astype(BF16)
        st["la", j] = (hi, (la - hi.astype(F32)).astype(BF16))

    def project(st, r):
        mm = lambda lo, hi: dot(st["xn"], wqkv_ref[:, lo:hi])
        vmid = (_OFF_V + _OFF_QKV_END) // 2
        log_decay(st, 0)
        st["q"] = mm(_OFF_Q, _OFF_K) * (HEAD_DK ** -0.5)
        log_decay(st, 1)
        st["k"] = mm(_OFF_K, _OFF_V)
        log_decay(st, 2)
        v_ref[r, 0:GLA_DV // 2] = mm(_OFF_V, vmid).astype(BF16)
        log_decay(st, 3)
        v_ref[r, GLA_DV // 2:GLA_DV] = mm(vmid, _OFF_QKV_END).astype(BF16)

    def decay_factors(st, r):
        c0 = r.start // s
        for j in range(4):
            d, qcols = j // 2, slice((j % 2) * half, (j % 2 + 1) * half)
            cols = slice(j * half, (j + 1) * half)
            hi, lo = st.pop(("la", j))
            b = (dot(tris[d], hi) + dot(tris[d], lo)).reshape(n, s, half)
            if d == 0:
                b_mid, b_end = b[:, s // 2 - 1:s // 2, :], b[:, s - 1:s, :]
            else:
                b_mid, b_end = b[:, s // 2:s // 2 + 1, :], b[:, 0:1, :]
            qd = st["q"][:, qcols].reshape(n, s, half) * jnp.exp2(b - b_mid)
            kd = st["k"][:, qcols].reshape(n, s, half) * jnp.exp2(b_mid - b)
            qe = qd * jnp.exp2(b_mid)
            ke = kd * jnp.exp2(b_end - b_mid)
            st["qd", d, j % 2] = qd.reshape(chunk, half).astype(BF16)
            st["kd", d, j % 2] = kd.reshape(chunk, half).astype(BF16)
            qe_ref[r, cols] = qe.reshape(chunk, half).astype(BF16)
            ke_ref[r, cols] = ke.reshape(chunk, half).astype(BF16)
            bend_ref[c0:c0 + n, cols] = b_end.reshape(n, half)

    def intra_chunk(st, r):
        for h in range(GLA_HEADS):
            hk = slice((h % 2) * HEAD_DK, (h % 2 + 1) * HEAD_DK)
            hv = slice(h * HEAD_DV, (h + 1) * HEAD_DV)
            sc = [[lax.dot_general(st["qd", d, h // 2][c * s:(c + 1) * s, hk],
                                   st["kd", d, h // 2][c * s:(c + 1) * s, hk], _NT,
                                   preferred_element_type=F32) for d in range(2)]
                  for c in range(n)]
            for c, (sf, sb) in enumerate(sc):
                rows = slice(r.start + c * s, r.start + (c + 1) * s)
                both = (jnp.where(masks[0], sf, 0.0) + jnp.where(masks[1], sb, 0.0)).astype(BF16)
                ol_ref[rows, hv] = dot(both, v_ref[rows, hv]).astype(BF16)

    def channel_dft(st, r):
        for grp in range(FNET_GROUPS):
            sl = slice(grp * FNET_GROUP_DIM, (grp + 1) * FNET_GROUP_DIM)
            pq = dot(st["f"][:, grp * FNET_GROUP_DIM:(grp + 2) * FNET_GROUP_DIM], cs)
            pq_ref[0, r, sl] = pq[:, :FNET_GROUP_DIM].astype(BF16)
            pq_ref[1, r, sl] = pq[:, FNET_GROUP_DIM:].astype(BF16)

    _run_interleaved((norm_in, decay_rank, project, decay_factors, intra_chunk, channel_dft),
                     tiles)


def _mixin_call(x2d, nw, w_in_bf, wdec, bdec, cs, tri_f, tri_b, rows_per_batch, tm, chunk,
                cast=(), gate_t=None, gate_row0=0, lead=0):
    m = x2d.shape[0] + lead
    d_model = w_in_bf.shape[0]
    nt = rows_per_batch // tm
    nb = m // rows_per_batch
    steps = m // tm
    assert not lead or (steps == 1 and lead % 8 == 0)
    slab = lambda w: pl.BlockSpec((w.shape[0] // steps, w.shape[1]), lambda i: (i, 0))
    assert all(w.shape[0] % (16 * steps) == 0 for w in cast)
    gate_specs, gate_out_spec, gate_out_shape = [], (), ()
    if gate_t is not None:
        feat = 2 * D_MODEL // steps
        assert feat % LANE == 0 and feat % GATE_SUB == 0 and gate_row0 % GATE_SUB == 0
        assert gate_t.shape == (gate_row0 + 2 * D_MODEL, d_model)
        nsub, blk0 = feat // GATE_SUB, gate_row0 // GATE_SUB
        gate_specs = [pl.BlockSpec((GATE_SUB, d_model), lambda i, j=j: (blk0 + i * nsub + j, 0))
                      for j in range(nsub)]
        gate_out_spec = (pl.BlockSpec((d_model, feat), lambda i: (0, i)),)
        gate_out_shape = (jax.ShapeDtypeStruct((d_model, 2 * D_MODEL), BF16),)
    row = lambda n: pl.BlockSpec((tm, n), lambda i: (i, 0))
    pq_spec = pl.BlockSpec((2, tm, FNET_WIDTH), lambda i: (0, i % nt, i // nt))
    wide = jax.ShapeDtypeStruct((m, 2 * GLA_DK), BF16)
    pq_shape = jax.ShapeDtypeStruct((2, rows_per_batch, nb * FNET_WIDTH), BF16)
    consts = [*wdec, *bdec, cs,
              jnp.asarray(tri_f).astype(BF16), jnp.asarray(tri_b).astype(BF16),
              jnp.asarray(tri_f), jnp.asarray(tri_b)]
    return pl.pallas_call(
        functools.partial(_mixin_kernel, chunk=chunk, n_cast=len(cast), n_gate=len(gate_specs),
                          lead=lead),
        grid=(steps,),
        in_specs=[_resident(x2d.shape) if lead else row(D_MODEL), _resident(nw.shape),
                  _window((d_model, _OFF_QKV_END), (0, 0)),
                  _window((d_model, W_WINDOW), (0, 3))]
                 + [_resident(c.shape) for c in consts] + [slab(w) for w in cast] + gate_specs,
        out_specs=(row(GLA_DV), row(GLA_DV), row(2 * GLA_DK), row(2 * GLA_DK),
                   pl.BlockSpec((tm // SUB, 2 * GLA_DK), lambda i: (i, 0)), pq_spec)
                  + tuple(slab(w) for w in cast) + gate_out_spec,
        out_shape=(jax.ShapeDtypeStruct((m, GLA_DV), BF16), jax.ShapeDtypeStruct((m, GLA_DV), BF16),
                   wide, wide, jax.ShapeDtypeStruct((m // SUB, 2 * GLA_DK), F32), pq_shape)
                  + tuple(jax.ShapeDtypeStruct(w.shape, BF16) for w in cast) + gate_out_shape,
        scratch_shapes=[pltpu.VMEM((tm, D_MODEL), F32)] if lead else [],
        compiler_params=pltpu.CompilerParams(
            dimension_semantics=("arbitrary",), vmem_limit_bytes=V7X_VMEM_LIMIT),
        name="mixin",
    )(x2d, nw, w_in_bf, w_in_bf, *consts, *cast, *([gate_t] * len(gate_specs)))


def _transpose_cast_kernel(x_ref, o_ref):
    o_ref[...] = x_ref[...].T.astype(BF16)


def _transpose_cast_call(w_t, n_rows, block):
    d_model = w_t.shape[1]
    assert n_rows % block == 0 and n_rows <= w_t.shape[0]
    return pl.pallas_call(
        _transpose_cast_kernel,
        grid=(n_rows // block,),
        in_specs=[pl.BlockSpec((block, d_model), lambda i: (i, 0))],
        out_specs=pl.BlockSpec((d_model, block), lambda i: (0, i)),
        out_shape=jax.ShapeDtypeStruct((d_model, n_rows), BF16),
        compiler_params=pltpu.CompilerParams(
            dimension_semantics=("arbitrary",), vmem_limit_bytes=V7X_VMEM_LIMIT),
        name="w_in_prep",
    )(w_t)


SCAN_G = 512
SCAN_RB = 4096


def _scan_group(dirs):
    s, n = SUB, SCAN_G // SUB
    ahead = 2
    dec, st, inter = [], [], [{}, {}]
    for qe_ref, ke_ref, v_ref, bend_ref, st_ref, row0, chunk0, backward in dirs:
        dec_rows = jnp.tile(jnp.exp2(bend_ref[chunk0:chunk0 + n, :]), (HEAD_DK // n, 1))
        dec.append(dec_rows.T)
        st.append(st_ref[...])

    def order(step, backward):
        return n - 1 - step if backward else step

    def increment(d, step):
        _, ke_ref, v_ref, _, _, row0, _, backward = dirs[d]
        i = order(step, backward)
        sub = slice(row0 + i * s, row0 + (i + 1) * s)
        return lax.dot_general(ke_ref[sub, :], v_ref[sub, :], _TN,
                               preferred_element_type=F32)

    inc = {(d, step): increment(d, step) for step in range(ahead) for d in range(2)}
    for step in range(n):
        for d, (qe_ref, _, _, _, _, row0, _, backward) in enumerate(dirs):
            i = order(step, backward)
            sub = slice(row0 + i * s, row0 + (i + 1) * s)
            inter[d][i] = jnp.dot(qe_ref[sub, :], st[d].astype(BF16),
                                  preferred_element_type=F32)
            if step + ahead < n:
                inc[d, step + ahead] = increment(d, step + ahead)
            decay = jnp.broadcast_to(dec[d][:, i:i + 1], (HEAD_DK, HEAD_DV))
            st[d] = st[d] * decay + inc.pop((d, step))
    for d, spec in enumerate(dirs):
        spec[4][...] = st[d]
    return [jnp.concatenate([inter[d][i] for i in range(n)], axis=0) for d in range(2)]


def _gla_scan_kernel(qef_ref, kef_ref, vf_ref, bendf_ref,
                     qeb_ref, keb_ref, vb_ref, bendb_ref, kem_ref, vm_ref,
                     of_ref, ob_ref, stf_ref, stb_ref):
    rb = qef_ref.shape[0]
    ng = rb // SCAN_G

    @pl.when(pl.program_id(2) == 0)
    def _():
        stf_ref[...] = lax.dot_general(kem_ref[...], vm_ref[...], _TN,
                                       preferred_element_type=F32)
        stb_ref[...] = jnp.zeros_like(stb_ref)

    for j in range(ng):
        jb = ng - 1 - j
        cpg = SCAN_G // SUB
        in_f, in_b = _scan_group([
            (qef_ref, kef_ref, vf_ref, bendf_ref, stf_ref, j * SCAN_G, j * cpg, False),
            (qeb_ref, keb_ref, vb_ref, bendb_ref, stb_ref, jb * SCAN_G, jb * cpg, True)])
        of_ref[j * SCAN_G:(j + 1) * SCAN_G, :] = in_f.astype(BF16)
        ob_ref[jb * SCAN_G:(jb + 1) * SCAN_G, :] = in_b.astype(BF16)


def _gla_scan_call(qe, ke, v, bend, kem, vm, seq, batch):
    h, rb = GLA_HEADS, SCAN_RB
    nt = seq // rb
    fwd = lambda b, hh, t: b * nt + t
    bwd = lambda b, hh, t: b * nt + nt - 1 - t
    cb = rb // SUB

    def dk_blk(rowfn, dircol, rows):
        return pl.BlockSpec((rows, HEAD_DK), lambda b, hh, t: (rowfn(b, hh, t), dircol + hh))

    def dv_blk(rowfn):
        return pl.BlockSpec((rb, HEAD_DV), lambda b, hh, t: (rowfn(b, hh, t), hh))

    in_specs = [
        dk_blk(fwd, 0, rb), dk_blk(fwd, 0, rb), dv_blk(fwd), dk_blk(fwd, 0, cb),
        dk_blk(bwd, h, rb), dk_blk(bwd, h, rb), dv_blk(bwd), dk_blk(bwd, h, cb),
        pl.BlockSpec((SUB, HEAD_DK), lambda b, hh, t: (0, hh)),
        pl.BlockSpec((SUB, HEAD_DV), lambda b, hh, t: (0, hh)),
    ]
    out = jax.ShapeDtypeStruct((batch * seq, GLA_DV), BF16)
    return pl.pallas_call(
        _gla_scan_kernel,
        grid=(batch, h, nt),
        in_specs=in_specs,
        out_specs=(dv_blk(fwd), dv_blk(bwd)),
        out_shape=(out, out),
        scratch_shapes=[pltpu.VMEM((HEAD_DK, HEAD_DV), F32), pltpu.VMEM((HEAD_DK, HEAD_DV), F32)],
        compiler_params=pltpu.CompilerParams(
            dimension_semantics=("arbitrary", "arbitrary", "arbitrary"),
            vmem_limit_bytes=V7X_VMEM_LIMIT),
        name="gla_scan",
    )(qe, ke, v, bend, qe, ke, v, bend, kem, vm)


DFT_TILE = 256


def _fold_kernel(x_ref, z_ref, j_ref, o_ref, *, m_last):
    t = DFT_TILE
    n_real = x_ref.shape[1] // t
    sign = jnp.where(pl.program_id(0) == 0, 1.0, -1.0)
    real = lambda i: x_ref[0, i * t:(i + 1) * t, :]
    lead = jnp.concatenate([jnp.zeros((t - z_ref.shape[1], z_ref.shape[2]), BF16), z_ref[0]],
                           axis=0)
    for i in range(o_ref.shape[1] // t):
        direct = (lead if i == 0 else real(i - 1)).astype(F32)
        mirror = (jnp.dot(j_ref[i, :, :t], real(min(n_real - i, n_real - 1)),
                          preferred_element_type=F32)
                  + jnp.dot(j_ref[i, :, t:], real(n_real - 1 - i), preferred_element_type=F32))
        m = i * t + lax.broadcasted_iota(jnp.int32, (t, 1), 0)
        folded = jnp.where(m <= m_last, direct + sign * mirror, 0.0)
        o_ref[0, i * t:(i + 1) * t, :] = folded.astype(BF16)


def _dft_kernel(peqo_ref, dc_ref, ds_ref, rowc_ref, rows_ref, u_ref, w_ref, *, scale):
    t = DFT_TILE
    acc_a = acc_b = None
    for c in range(peqo_ref.shape[1] // t):
        sl = slice(c * t, (c + 1) * t)
        rc, rs = rowc_ref[0, 0:1, sl], rows_ref[0, 0:1, sl]
        dc, ds = dc_ref[:, sl], ds_ref[:, sl]
        tc = (dc * rc - ds * rs).astype(BF16)
        ts = (ds * rc + dc * rs).astype(BF16)
        a = jnp.dot(tc, peqo_ref[0, sl, :], preferred_element_type=F32)
        b = jnp.dot(ts, peqo_ref[1, sl, :], preferred_element_type=F32)
        acc_a = a if acc_a is None else acc_a + a
        acc_b = b if acc_b is None else acc_b + b
    u_ref[...] = ((acc_a - acc_b) * scale).astype(BF16)
    w_ref[...] = ((acc_a + acc_b) * scale).astype(BF16)


def _unfold_kernel(u_ref, w_ref, j_ref, y_ref, *, r_last, i_mixed):
    t = DFT_TILE
    tile = lambda ref, i: ref[i * t:(i + 1) * t, :]
    n_real = y_ref.shape[0] // t
    for i in range(n_real):
        if i < i_mixed:
            y_ref[i * t:(i + 1) * t, :] = tile(u_ref, i + 1)
            continue
        j = 0 if i == i_mixed else 1
        mirror = jnp.dot(j_ref[j, :, t:], tile(w_ref, n_real - 1 - i), preferred_element_type=F32)
        if (n_real - i + 1) * t <= w_ref.shape[0]:
            mirror += jnp.dot(j_ref[j, :, :t], tile(w_ref, n_real - i),
                              preferred_element_type=F32)
        mirror = mirror.astype(BF16)
        if i == i_mixed:
            r = i * t + lax.broadcasted_iota(jnp.int32, (t, 1), 0)
            mirror = jnp.where(r <= r_last, tile(u_ref, i + 1), mirror)
        y_ref[i * t:(i + 1) * t, :] = mirror


@functools.lru_cache(maxsize=None)
def _dft_constants(seq, n_meta):
    t = DFT_TILE
    off = t - n_meta
    big_l = seq + n_meta
    half = big_l // 2
    n_real = seq // t
    nf = -(-(half + 1 + off) // t)
    hp = nf * t
    jf = np.zeros((nf, t, 2 * t), np.float32)
    for i in range(nf):
        a_idx, b_idx = min(n_real - i, n_real - 1), n_real - 1 - i
        for j in range(t):
            pos = i * t + j - off
            if 1 <= pos <= half - 1:
                tau, rho = divmod(big_l - pos - n_meta, t)
                assert tau in (a_idx, b_idx)
                jf[i, j, (t if tau == b_idx else 0) + rho] = 1.0
    r_last = half - n_meta
    i_mixed = r_last // t
    ju = np.zeros((3, t, 2 * t), np.float32)
    for i in range(i_mixed, n_real):
        a_idx, b_idx = n_real - i, n_real - 1 - i
        blk = np.zeros((t, 2 * t), np.float32)
        for j in range(t):
            r = i * t + j
            if r > r_last:
                tau, rho = divmod(big_l - (r + n_meta) + off, t)
                assert tau in (a_idx, b_idx)
                blk[j, (t if tau == b_idx else 0) + rho] = 1.0
        var = 1 if i == i_mixed else 2
        assert var == 1 or not ju[2].any() or (ju[2] == blk).all()
        ju[var] = blk
    ang = lambda prod: 2.0 * np.pi * (prod % big_l).astype(np.float64) / big_l
    pos = np.arange(hp, dtype=np.int64) - off
    d = ang(np.arange(t, dtype=np.int64)[:, None] * pos[None, :])
    k0 = t * np.arange(nf, dtype=np.int64) - off
    row = np.broadcast_to(ang(k0[:, None] * pos[None, :])[:, None, :], (nf, 8, hp))
    f = lambda a: np.ascontiguousarray(a, dtype=np.float32)
    return dict(jf=jf, ju=ju, dc=f(np.cos(d)), ds=f(np.sin(d)), rowc=f(np.cos(row)),
                rows=f(np.sin(row)), nf=nf, hp=hp, off=off, n_real=n_real,
                m_last=half + off, r_last=r_last, i_mixed=i_mixed)


def _seq_dft(pq, pqm, seq, n_meta):
    c = _dft_constants(seq, n_meta)
    t, nf, hp, ncol = DFT_TILE, c["nf"], c["hp"], pq.shape[2]
    ncb = ncol // FNET_WIDTH
    assert c["off"] == t - n_meta and pqm.shape[2] == FNET_WIDTH and pqm.shape[1] <= t
    col_blk = lambda rows: pl.BlockSpec((1, rows, FNET_WIDTH), lambda g, b: (g, 0, b))
    meta_blk = pl.BlockSpec((1, pqm.shape[1], FNET_WIDTH), lambda g, b: (g, 0, 0))
    peqo = pl.pallas_call(
        functools.partial(_fold_kernel, m_last=c["m_last"]),
        grid=(2, ncb),
        in_specs=[col_blk(seq), meta_blk, _resident(c["jf"].shape)],
        out_specs=col_blk(hp),
        out_shape=jax.ShapeDtypeStruct((2, hp, ncol), BF16),
        compiler_params=pltpu.CompilerParams(
            dimension_semantics=("arbitrary", "arbitrary"), vmem_limit_bytes=V7X_VMEM_LIMIT),
        name="dft_fold",
    )(pq, pqm, jnp.asarray(c["jf"]).astype(BF16))

    scale = float((seq + n_meta) * FNET_GROUP_DIM) ** -0.5
    tile = pl.BlockSpec((t, ncol), lambda i: (i, 0))
    rowspec = pl.BlockSpec((1, 8, hp), lambda i: (i, 0, 0))
    u, w = pl.pallas_call(
        functools.partial(_dft_kernel, scale=scale),
        grid=(nf,),
        in_specs=[_resident((2, hp, ncol)), _resident((t, hp)), _resident((t, hp)),
                  rowspec, rowspec],
        out_specs=(tile, tile),
        out_shape=(jax.ShapeDtypeStruct((hp, ncol), BF16),) * 2,
        compiler_params=pltpu.CompilerParams(
            dimension_semantics=("arbitrary",), vmem_limit_bytes=V7X_VMEM_LIMIT),
        name="seq_dft",
    )(peqo, jnp.asarray(c["dc"]), jnp.asarray(c["ds"]), jnp.asarray(c["rowc"]),
      jnp.asarray(c["rows"]))

    col = lambda rows: pl.BlockSpec((rows, FNET_WIDTH), lambda b: (0, b))
    return pl.pallas_call(
        functools.partial(_unfold_kernel, r_last=c["r_last"], i_mixed=c["i_mixed"]),
        grid=(ncb,),
        in_specs=[col(hp), col(hp), _resident((2, t, 2 * t))],
        out_specs=col(seq),
        out_shape=jax.ShapeDtypeStruct((seq, ncol), BF16),
        compiler_params=pltpu.CompilerParams(
            dimension_semantics=("arbitrary",), vmem_limit_bytes=V7X_VMEM_LIMIT),
        name="dft_unfold",
    )(u, w, jnp.asarray(c["ju"][1:]).astype(BF16))


FFN_CHUNK = 1024
TAIL_SPLIT = 2
TAIL_TM = 512


def _tail_kernel(x_ref, ol_ref, of_ref, ob_ref, y_ref, nmix_ref, gn_ref, nffn_ref, nfin_ref,
                 wg_ref, wm_ref, wo_ref, wf_ref, wout_ref, w1_ref, w2_ref, o_ref):
    tm = x_ref.shape[0]
    sub = tm // TAIL_SPLIT
    tiles = [(dict(), slice(i * sub, (i + 1) * sub)) for i in range(TAIL_SPLIT)]
    dot = functools.partial(jnp.dot, preferred_element_type=F32)
    nchunk = D_FF // FFN_CHUNK

    def norm_in(s, r):
        s["x"] = x_ref[r, :]
        s["xn"] = _rms(s["x"], nmix_ref[...]).astype(BF16)

    def gates(s, r):
        s["g"] = [dot(s["xn"], wg_ref[:, h * HEAD_DV:(h + 1) * HEAD_DV])
                  for h in range(GLA_HEADS)]
        s["ga"] = dot(s["xn"], wm_ref[:, 0:D_MODEL])
        s["gf"] = dot(s["xn"], wm_ref[:, D_MODEL:2 * D_MODEL])
        s["y_fnet"] = dot(y_ref[r, :], wf_ref[...])

    def gla_out(s, r):
        y_gla = None
        for h in range(GLA_HEADS):
            hv = slice(h * HEAD_DV, (h + 1) * HEAD_DV)
            o = (ol_ref[r, hv].astype(F32) + of_ref[r, hv].astype(F32)
                 + ob_ref[r, hv].astype(F32))
            g = s["g"][h]
            og = (_rms(o, gn_ref[...]) * (g * _sigmoid(g))).astype(BF16)
            part = dot(og, wo_ref[hv, :])
            y_gla = part if y_gla is None else y_gla + part
        s["y_gla"] = y_gla

    def merge(s, r):
        merged = _sigmoid(s["ga"]) * s["y_gla"] + _sigmoid(s["gf"]) * s["y_fnet"]
        s["h1"] = s["x"] + dot(merged.astype(BF16), wout_ref[...])
        s["u"] = _rms(s["h1"], nffn_ref[...]).astype(BF16)
        s["acc"] = s["h1"]

    def ffn_up(c):
        def stage(s, r):
            a = jnp.maximum(dot(s["u"], w1_ref[:, c * FFN_CHUNK:(c + 1) * FFN_CHUNK]), 0.0)
            s["a", c] = (a * a).astype(BF16)
        return stage

    def ffn_down(c):
        def stage(s, r):
            s["acc"] = s["acc"] + dot(s.pop(("a", c)),
                                      w2_ref[c * FFN_CHUNK:(c + 1) * FFN_CHUNK, :])
        return stage

    def norm_out(s, r):
        o_ref[r, :] = _rms(s["acc"], nfin_ref[...])

    stages = [norm_in, gates, gla_out, merge, ffn_up(0)]
    for c in range(nchunk):
        if c + 1 < nchunk:
            stages.append(ffn_up(c + 1))
        stages.append(ffn_down(c))
    stages.append(norm_out)
    _run_interleaved(stages, tiles)


def _tail_call(x2d, ol, o_f, o_b, y, nmix, gn, nffn, nfin, w_in_bf, w_merge, wo, wf, wout, w1,
               w2, seq, tm):
    m = x2d.shape[0]
    nt = seq // tm
    row = lambda n: pl.BlockSpec((tm, n), lambda i: (i, 0))
    norms = [nmix, gn, nffn, nfin]
    consts = [w_merge, wo, wf, wout, w1, w2]
    return pl.pallas_call(
        _tail_kernel,
        grid=(m // tm,),
        in_specs=[row(D_MODEL), row(GLA_DV), row(GLA_DV), row(GLA_DV),
                  pl.BlockSpec((tm, FNET_WIDTH), lambda i: (i % nt, i // nt))]
                 + [_resident(c.shape) for c in norms]
                 + [_window((w_in_bf.shape[0], W_WINDOW), (0, 2))]
                 + [_resident(c.shape) for c in consts],
        out_specs=row(D_MODEL),
        out_shape=jax.ShapeDtypeStruct((m, D_MODEL), F32),
        compiler_params=pltpu.CompilerParams(
            dimension_semantics=("arbitrary",), vmem_limit_bytes=V7X_VMEM_LIMIT),
        name="tail",
    )(x2d, ol, o_f, o_b, y, *norms, w_in_bf, *consts)


@functools.lru_cache(maxsize=None)
def _gla_constants(rows):
    idx = np.arange(rows)
    same = (idx[:, None] // SUB) == (idx[None, :] // SUB)
    tri_f = (same & (idx[None, :] <= idx[:, None])).astype(np.float32)
    return tri_f, np.ascontiguousarray(tri_f.T)


@functools.lru_cache(maxsize=None)
def _channel_dft():
    d = np.arange(FNET_GROUP_DIM, dtype=np.int64)
    ang = 2.0 * np.pi * ((d[:, None] * d[None, :]) % FNET_GROUP_DIM) / FNET_GROUP_DIM
    cs = np.zeros((2 * FNET_GROUP_DIM, 2 * FNET_GROUP_DIM), np.float32)
    cs[F_LEAD:F_LEAD + FNET_GROUP_DIM] = np.concatenate([np.cos(ang), np.sin(ang)], axis=1)
    return cs


def kernel(x, meta_tokens, norm_mix, w_in, w_decay_fwd, b_decay_fwd, w_decay_bwd, b_decay_bwd,
           gla_norm, w_o_gla, w_fnet, w_out, norm_ffn, w_ff1, w_ff2, norm_final):
    batch, seq, d = x.shape
    assert d == D_MODEL and meta_tokens.shape == (N_META, D_MODEL)
    assert w_in.shape[0] == 1 and seq % (2 * SCAN_RB) == 0 and seq % DFT_TILE == 0
    gate0 = 2 * GLA_DK + 2 * GLA_DV + 2 * DECAY_RANK + FNET_WIDTH
    assert w_in.shape[2] == gate0 + 2 * D_MODEL and gate0 <= 4 * W_WINDOW
    w_in_t = jnp.swapaxes(w_in[0], 0, 1)
    wi = _transpose_cast_call(w_in_t, 4 * W_WINDOW, W_WINDOW // 2)
    wdec = (w_decay_fwd[0], w_decay_bwd[0])
    bdec = (b_decay_fwd, b_decay_bwd)
    cs = jnp.asarray(_channel_dft())
    nw_mix = norm_mix[0][None, :]

    x2d = x.reshape(batch * seq, D_MODEL)
    tail_weights = (w_o_gla[0], w_fnet[0], w_out[0], w_ff1[0], w_ff2[0])
    v, ol, qe, ke, bend, pq, *tail_weights, w_merge = _mixin_call(
        x2d, nw_mix, wi, wdec, bdec, cs, *_gla_constants(GLA_C), seq, MIX_TM, GLA_C,
        cast=tail_weights, gate_t=w_in_t, gate_row0=gate0)
    vm, _, _, kem, _, pqm = _mixin_call(meta_tokens, nw_mix, wi, wdec, bdec, cs,
                                        *_gla_constants(SUB), SUB, SUB, SUB, lead=SUB - N_META)
    o_f, o_b = _gla_scan_call(qe, ke, v, bend, kem, vm, seq, batch)

    y = _seq_dft(pq, pqm, seq, N_META)

    out = _tail_call(x2d, ol, o_f, o_b, y, nw_mix, gla_norm[0][None, :], norm_ffn[0][None, :],
                     norm_final[None, :], wi, w_merge, *tail_weights, seq, TAIL_TM)
    return out.reshape(batch, seq, D_MODEL)
```

```python
import functools

import numpy as np
import jax
import jax.numpy as jnp
from jax import lax
from jax.experimental import pallas as pl
from jax.experimental.pallas import tpu as pltpu

F32 = jnp.float32
BF16 = jnp.bfloat16

D_MODEL = 1024
N_META = 16
GLA_HEADS = 4
GLA_DK = 512
GLA_DV = 1024
HEAD_DK = GLA_DK // GLA_HEADS
HEAD_DV = GLA_DV // GLA_HEADS
DECAY_RANK = 16
GATE_TEMP = 16.0
SUB = 64
FNET_GROUPS = 4
FNET_WIDTH = 512
FNET_GROUP_DIM = FNET_WIDTH // FNET_GROUPS
D_FF = 4 * D_MODEL
EPS = 1e-6
LOG2E = 1.4426950408889634
LANE = 128

_OFF_Q, _OFF_K, _OFF_V, _OFF_QKV_END = 0, 512, 1024, 2048
W_WINDOW = 1024
R_PAD = 128
F_LEAD = 2 * DECAY_RANK
F_EXT = F_LEAD + FNET_WIDTH + (LANE - F_LEAD)

GLA_C = 4 * SUB
MIX_TM = 1024
GATE_SUB = 32

V7X_VMEM_LIMIT = 58 * 1024 * 1024


def _rms(x, w):
    ms = jnp.mean(x * x, axis=-1, keepdims=True)
    return x * lax.rsqrt(ms + EPS) * w


def _sigmoid(x):
    return 1.0 / (1.0 + jnp.exp(-x))


def _run_interleaved(stages, tiles):
    for stage in stages:
        for state, rows in tiles:
            stage(state, rows)


def _window(block, index):
    return pl.BlockSpec(block, lambda *_: index, pipeline_mode=pl.Buffered(1))


def _resident(shape):
    return _window(shape, (0,) * len(shape))


_NT = (((1,), (1,)), ((), ()))
_TN = (((0,), (0,)), ((), ()))


def _mixin_kernel(x_ref, nw_ref, wqkv_ref, wrf_ref, wdf_ref, wdb_ref, bdf_ref, bdb_ref, cs_ref,
                  trif_ref, trib_ref, maskf_ref, maskb_ref, *refs, chunk, n_cast, n_gate, lead):
    if lead:
        *refs, tile_ref = refs
        tile_ref[...] = jnp.concatenate([jnp.zeros((lead, x_ref.shape[1]), F32), x_ref[...]],
                                        axis=0)
        x_ref = tile_ref
    n_in = n_cast + n_gate
    cast_in, gate_in = refs[:n_cast], refs[n_cast:n_in]
    v_ref, ol_ref, qe_ref, ke_ref, bend_ref, pq_ref = refs[n_in:n_in + 6]
    cast_out = refs[n_in + 6:n_in + 6 + n_cast]
    _mixin_body(x_ref, nw_ref, wqkv_ref, wrf_ref, (wdf_ref, wdb_ref), (bdf_ref, bdb_ref), cs_ref,
                trif_ref, trib_ref,
                maskf_ref, maskb_ref, v_ref, ol_ref, qe_ref, ke_ref, bend_ref, pq_ref, chunk=chunk)
    for src, dst in zip(cast_in, cast_out):
        dst[...] = src[...].astype(BF16)
    if n_gate:
        feat_major = jnp.concatenate([r[...] for r in gate_in], axis=0)
        refs[-1][...] = feat_major.T.astype(BF16)


def _mixin_body(x_ref, nw_ref, wqkv_ref, wrf_ref, wdec_refs, bdec_refs, cs_ref,
                trif_ref, trib_ref, maskf_ref, maskb_ref,
                v_ref, ol_ref, qe_ref, ke_ref, bend_ref, pq_ref, *, chunk):
    tm = x_ref.shape[0]
    n, s = chunk // SUB, SUB
    tiles = [(dict(), slice(i * chunk, (i + 1) * chunk)) for i in range(tm // chunk)]
    dot = functools.partial(jnp.dot, preferred_element_type=F32)
    cs = cs_ref[...].astype(BF16)
    tris = (trif_ref[...], trib_ref[...])
    masks = (maskf_ref[0:s, 0:s] != 0.0, maskb_ref[0:s, 0:s] != 0.0)
    half = GLA_DK // 2
    zeros = lambda rows: jnp.zeros((rows, GLA_DK), F32)
    wdec = jnp.concatenate([
        jnp.concatenate([wdec_refs[0][...], zeros(R_PAD - DECAY_RANK)], axis=0),
        jnp.concatenate([zeros(DECAY_RANK), wdec_refs[1][...], zeros(R_PAD - 2 * DECAY_RANK)],
                        axis=0)], axis=1).astype(BF16)
    bdec = jnp.concatenate([bdec_refs[0][...], bdec_refs[1][...]], axis=1)

    def norm_in(st, r):
        st["xn"] = _rms(x_ref[r, :], nw_ref[...]).astype(BF16)

    def decay_rank(st, r):
        st["f"] = dot(st["xn"], wrf_ref[:, 0:F_EXT]).astype(BF16)
        st["r"] = st["f"][:, 0:R_PAD]

    def log_decay(st, j):
        cols = slice(j * half, (j + 1) * half)
        x = dot(st["r"], wdec[:, cols]) + bdec[:, cols]
        la = (jnp.minimum(x, 0.0) - jnp.log(1.0 + jnp.exp(-jnp.abs(x)))) * (LOG2E / GATE_TEMP)
        hi = la.astype(BF16)
        st["la", j] = (hi, (la - hi.astype(F32)).astype(BF16))

    def project(st, r):
        mm = lambda lo, hi: dot(st["xn"], wqkv_ref[:, lo:hi])
        vmid = (_OFF_V + _OFF_QKV_END) // 2
        log_decay(st, 0)
        st["q"] = mm(_OFF_Q, _OFF_K) * (HEAD_DK ** -0.5)
        log_decay(st, 1)
        st["k"] = mm(_OFF_K, _OFF_V)
        log_decay(st, 2)
        v_ref[r, 0:GLA_DV // 2] = mm(_OFF_V, vmid).astype(BF16)
        log_decay(st, 3)
        v_ref[r, GLA_DV // 2:GLA_DV] = mm(vmid, _OFF_QKV_END).astype(BF16)

    def decay_factors(st, r):
        c0 = r.start // s
        for j in range(4):
            d, qcols = j // 2, slice((j % 2) * half, (j % 2 + 1) * half)
            cols = slice(j * half, (j + 1) * half)
            hi, lo = st.pop(("la", j))
            b = (dot(tris[d], hi) + dot(tris[d], lo)).reshape(n, s, half)
            if d == 0:
                b_mid, b_end = b[:, s // 2 - 1:s // 2, :], b[:, s - 1:s, :]
            else:
                b_mid, b_end = b[:, s // 2:s // 2 + 1, :], b[:, 0:1, :]
            qd = st["q"][:, qcols].reshape(n, s, half) * jnp.exp2(b - b_mid)
            kd = st["k"][:, qcols].reshape(n, s, half) * jnp.exp2(b_mid - b)
            qe = qd * jnp.exp2(b_mid)
            ke = kd * jnp.exp2(b_end - b_mid)
            st["qd", d, j % 2] = qd.reshape(chunk, half).astype(BF16)
            st["kd", d, j % 2] = kd.reshape(chunk, half).astype(BF16)
            qe_ref[r, cols] = qe.reshape(chunk, half).astype(BF16)
            ke_ref[r, cols] = ke.reshape(chunk, half).astype(BF16)
            bend_ref[c0:c0 + n, cols] = b_end.reshape(n, half)

    def intra_chunk(st, r):
        for h in range(GLA_HEADS):
            hk = slice((h % 2) * HEAD_DK, (h % 2 + 1) * HEAD_DK)
            hv = slice(h * HEAD_DV, (h + 1) * HEAD_DV)
            sc = [[lax.dot_general(st["qd", d, h // 2][c * s:(c + 1) * s, hk],
                                   st["kd", d, h // 2][c * s:(c + 1) * s, hk], _NT,
                                   preferred_element_type=F32) for d in range(2)]
                  for c in range(n)]
            for c, (sf, sb) in enumerate(sc):
                rows = slice(r.start + c * s, r.start + (c + 1) * s)
                both = (jnp.where(masks[0], sf, 0.0) + jnp.where(masks[1], sb, 0.0)).astype(BF16)
                ol_ref[rows, hv] = dot(both, v_ref[rows, hv]).astype(BF16)

    def channel_dft(st, r):
        for grp in range(FNET_GROUPS):
            sl = slice(grp * FNET_GROUP_DIM, (grp + 1) * FNET_GROUP_DIM)
            pq = dot(st["f"][:, grp * FNET_GROUP_DIM:(grp + 2) * FNET_GROUP_DIM], cs)
            pq_ref[0, r, sl] = pq[:, :FNET_GROUP_DIM].astype(BF16)
            pq_ref[1, r, sl] = pq[:, FNET_GROUP_DIM:].astype(BF16)

    _run_interleaved((norm_in, decay_rank, project, decay_factors, intra_chunk, channel_dft),
                     tiles)


def _mixin_call(x2d, nw, w_in_bf, wdec, bdec, cs, tri_f, tri_b, rows_per_batch, tm, chunk,
                cast=(), gate_t=None, gate_row0=0, lead=0):
    m = x2d.shape[0] + lead
    d_model = w_in_bf.shape[0]
    nt = rows_per_batch // tm
    nb = m // rows_per_batch
    steps = m // tm
    assert not lead or (steps == 1 and lead % 8 == 0)
    slab = lambda w: pl.BlockSpec((w.shape[0] // steps, w.shape[1]), lambda i: (i, 0))
    assert all(w.shape[0] % (16 * steps) == 0 for w in cast)
    gate_specs, gate_out_spec, gate_out_shape = [], (), ()
    if gate_t is not None:
        feat = 2 * D_MODEL // steps
        assert feat % LANE == 0 and feat % GATE_SUB == 0 and gate_row0 % GATE_SUB == 0
        assert gate_t.shape == (gate_row0 + 2 * D_MODEL, d_model)
        nsub, blk0 = feat // GATE_SUB, gate_row0 // GATE_SUB
        gate_specs = [pl.BlockSpec((GATE_SUB, d_model), lambda i, j=j: (blk0 + i * nsub + j, 0))
                      for j in range(nsub)]
        gate_out_spec = (pl.BlockSpec((d_model, feat), lambda i: (0, i)),)
        gate_out_shape = (jax.ShapeDtypeStruct((d_model, 2 * D_MODEL), BF16),)
    row = lambda n: pl.BlockSpec((tm, n), lambda i: (i, 0))
    pq_spec = pl.BlockSpec((2, tm, FNET_WIDTH), lambda i: (0, i % nt, i // nt))
    wide = jax.ShapeDtypeStruct((m, 2 * GLA_DK), BF16)
    pq_shape = jax.ShapeDtypeStruct((2, rows_per_batch, nb * FNET_WIDTH), BF16)
    consts = [*wdec, *bdec, cs,
              jnp.asarray(tri_f).astype(BF16), jnp.asarray(tri_b).astype(BF16),
              jnp.asarray(tri_f), jnp.asarray(tri_b)]
    return pl.pallas_call(
        functools.partial(_mixin_kernel, chunk=chunk, n_cast=len(cast), n_gate=len(gate_specs),
                          lead=lead),
        grid=(steps,),
        in_specs=[_resident(x2d.shape) if lead else row(D_MODEL), _resident(nw.shape),
                  _window((d_model, _OFF_QKV_END), (0, 0)),
                  _window((d_model, W_WINDOW), (0, 3))]
                 + [_resident(c.shape) for c in consts] + [slab(w) for w in cast] + gate_specs,
        out_specs=(row(GLA_DV), row(GLA_DV), row(2 * GLA_DK), row(2 * GLA_DK),
                   pl.BlockSpec((tm // SUB, 2 * GLA_DK), lambda i: (i, 0)), pq_spec)
                  + tuple(slab(w) for w in cast) + gate_out_spec,
        out_shape=(jax.ShapeDtypeStruct((m, GLA_DV), BF16), jax.ShapeDtypeStruct((m, GLA_DV), BF16),
                   wide, wide, jax.ShapeDtypeStruct((m // SUB, 2 * GLA_DK), F32), pq_shape)
                  + tuple(jax.ShapeDtypeStruct(w.shape, BF16) for w in cast) + gate_out_shape,
        scratch_shapes=[pltpu.VMEM((tm, D_MODEL), F32)] if lead else [],
        compiler_params=pltpu.CompilerParams(
            dimension_semantics=("arbitrary",), vmem_limit_bytes=V7X_VMEM_LIMIT),
        name="mixin",
    )(x2d, nw, w_in_bf, w_in_bf, *consts, *cast, *([gate_t] * len(gate_specs)))


def _transpose_cast_kernel(x_ref, o_ref):
    o_ref[...] = x_ref[...].T.astype(BF16)


def _transpose_cast_call(w_t, n_rows, block):
    d_model = w_t.shape[1]
    assert n_rows % block == 0 and n_rows <= w_t.shape[0]
    return pl.pallas_call(
        _transpose_cast_kernel,
        grid=(n_rows // block,),
        in_specs=[pl.BlockSpec((block, d_model), lambda i: (i, 0))],
        out_specs=pl.BlockSpec((d_model, block), lambda i: (0, i)),
        out_shape=jax.ShapeDtypeStruct((d_model, n_rows), BF16),
        compiler_params=pltpu.CompilerParams(
            dimension_semantics=("arbitrary",), vmem_limit_bytes=V7X_VMEM_LIMIT),
        name="w_in_prep",
    )(w_t)


SCAN_G = 512
SCAN_RB = 4096


def _scan_group(dirs):
    s, n = SUB, SCAN_G // SUB
    ahead = 2
    dec, st, inter = [], [], [{}, {}]
    for qe_ref, ke_ref, v_ref, bend_ref, st_ref, row0, chunk0, backward in dirs:
        dec_rows = jnp.tile(jnp.exp2(bend_ref[chunk0:chunk0 + n, :]), (HEAD_DK // n, 1))
        dec.append(dec_rows.T)
        st.append(st_ref[...])

    def order(step, backward):
        return n - 1 - step if backward else step

    def increment(d, step):
        _, ke_ref, v_ref, _, _, row0, _, backward = dirs[d]
        i = order(step, backward)
        sub = slice(row0 + i * s, row0 + (i + 1) * s)
        return lax.dot_general(ke_ref[sub, :], v_ref[sub, :], _TN,
                               preferred_element_type=F32)

    inc = {(d, step): increment(d, step) for step in range(ahead) for d in range(2)}
    for step in range(n):
        for d, (qe_ref, _, _, _, _, row0, _, backward) in enumerate(dirs):
            i = order(step, backward)
            sub = slice(row0 + i * s, row0 + (i + 1) * s)
            inter[d][i] = jnp.dot(qe_ref[sub, :], st[d].astype(BF16),
                                  preferred_element_type=F32)
            if step + ahead < n:
                inc[d, step + ahead] = increment(d, step + ahead)
            decay = jnp.broadcast_to(dec[d][:, i:i + 1], (HEAD_DK, HEAD_DV))
            st[d] = st[d] * decay + inc.pop((d, step))
    for d, spec in enumerate(dirs):
        spec[4][...] = st[d]
    return [jnp.concatenate([inter[d][i] for i in range(n)], axis=0) for d in range(2)]


def _gla_scan_kernel(qef_ref, kef_ref, vf_ref, bendf_ref,
                     qeb_ref, keb_ref, vb_ref, bendb_ref, kem_ref, vm_ref,
                     of_ref, ob_ref, stf_ref, stb_ref):
    rb = qef_ref.shape[0]
    ng = rb // SCAN_G

    @pl.when(pl.program_id(2) == 0)
    def _():
        stf_ref[...] = lax.dot_general(kem_ref[...], vm_ref[...], _TN,
                                       preferred_element_type=F32)
        stb_ref[...] = jnp.zeros_like(stb_ref)

    for j in range(ng):
        jb = ng - 1 - j
        cpg = SCAN_G // SUB
        in_f, in_b = _scan_group([
            (qef_ref, kef_ref, vf_ref, bendf_ref, stf_ref, j * SCAN_G, j * cpg, False),
            (qeb_ref, keb_ref, vb_ref, bendb_ref, stb_ref, jb * SCAN_G, jb * cpg, True)])
        of_ref[j * SCAN_G:(j + 1) * SCAN_G, :] = in_f.astype(BF16)
        ob_ref[jb * SCAN_G:(jb + 1) * SCAN_G, :] = in_b.astype(BF16)


def _gla_scan_call(qe, ke, v, bend, kem, vm, seq, batch):
    h, rb = GLA_HEADS, SCAN_RB
    nt = seq // rb
    fwd = lambda b, hh, t: b * nt + t
    bwd = lambda b, hh, t: b * nt + nt - 1 - t
    cb = rb // SUB

    def dk_blk(rowfn, dircol, rows):
        return pl.BlockSpec((rows, HEAD_DK), lambda b, hh, t: (rowfn(b, hh, t), dircol + hh))

    def dv_blk(rowfn):
        return pl.BlockSpec((rb, HEAD_DV), lambda b, hh, t: (rowfn(b, hh, t), hh))

    in_specs = [
        dk_blk(fwd, 0, rb), dk_blk(fwd, 0, rb), dv_blk(fwd), dk_blk(fwd, 0, cb),
        dk_blk(bwd, h, rb), dk_blk(bwd, h, rb), dv_blk(bwd), dk_blk(bwd, h, cb),
        pl.BlockSpec((SUB, HEAD_DK), lambda b, hh, t: (0, hh)),
        pl.BlockSpec((SUB, HEAD_DV), lambda b, hh, t: (0, hh)),
    ]
    out = jax.ShapeDtypeStruct((batch * seq, GLA_DV), BF16)
    return pl.pallas_call(
        _gla_scan_kernel,
        grid=(batch, h, nt),
        in_specs=in_specs,
        out_specs=(dv_blk(fwd), dv_blk(bwd)),
        out_shape=(out, out),
        scratch_shapes=[pltpu.VMEM((HEAD_DK, HEAD_DV), F32), pltpu.VMEM((HEAD_DK, HEAD_DV), F32)],
        compiler_params=pltpu.CompilerParams(
            dimension_semantics=("arbitrary", "arbitrary", "arbitrary"),
            vmem_limit_bytes=V7X_VMEM_LIMIT),
        name="gla_scan",
    )(qe, ke, v, bend, qe, ke, v, bend, kem, vm)


DFT_TILE = 256
DFT_COLS = 256


def _fold_kernel(x_ref, z_ref, j_ref, o_ref, *, m_last):
    t = DFT_TILE
    n_real = x_ref.shape[1] // t
    sign = jnp.where(pl.program_id(0) == 0, 1.0, -1.0)
    real = lambda i: x_ref[0, i * t:(i + 1) * t, :]
    lead = jnp.concatenate([jnp.zeros((t - z_ref.shape[1], z_ref.shape[2]), BF16), z_ref[0]],
                           axis=0)
    for i in range(o_ref.shape[1] // t):
        direct = (lead if i == 0 else real(i - 1)).astype(F32)
        mirror = (jnp.dot(j_ref[i, :, :t], real(min(n_real - i, n_real - 1)),
                          preferred_element_type=F32)
                  + jnp.dot(j_ref[i, :, t:], real(n_real - 1 - i), preferred_element_type=F32))
        m = i * t + lax.broadcasted_iota(jnp.int32, (t, 1), 0)
        folded = jnp.where(m <= m_last, direct + sign * mirror, 0.0)
        o_ref[0, i * t:(i + 1) * t, :] = folded.astype(BF16)


def _dft_kernel(peqo_ref, dc_ref, ds_ref, rowc_ref, rows_ref, u_ref, w_ref, *, scale):
    t = DFT_TILE
    acc_a = acc_b = None
    for c in range(peqo_ref.shape[1] // t):
        sl = slice(c * t, (c + 1) * t)
        rc, rs = rowc_ref[0, 0:1, sl], rows_ref[0, 0:1, sl]
        dc, ds = dc_ref[:, sl], ds_ref[:, sl]
        tc = (dc * rc - ds * rs).astype(BF16)
        ts = (ds * rc + dc * rs).astype(BF16)
        a = jnp.dot(tc, peqo_ref[0, sl, :], preferred_element_type=F32)
        b = jnp.dot(ts, peqo_ref[1, sl, :], preferred_element_type=F32)
        acc_a = a if acc_a is None else acc_a + a
        acc_b = b if acc_b is None else acc_b + b
    u_ref[...] = ((acc_a - acc_b) * scale).astype(BF16)
    w_ref[...] = ((acc_a + acc_b) * scale).astype(BF16)


def _unfold_kernel(u_ref, w_ref, j_ref, y_ref, *, r_last, i_mixed):
    t = DFT_TILE
    tile = lambda ref, i: ref[i * t:(i + 1) * t, :]
    n_real = y_ref.shape[0] // t
    for i in range(n_real):
        if i < i_mixed:
            y_ref[i * t:(i + 1) * t, :] = tile(u_ref, i + 1)
            continue
        j = 0 if i == i_mixed else 1
        mirror = jnp.dot(j_ref[j, :, t:], tile(w_ref, n_real - 1 - i), preferred_element_type=F32)
        if (n_real - i + 1) * t <= w_ref.shape[0]:
            mirror += jnp.dot(j_ref[j, :, :t], tile(w_ref, n_real - i),
                              preferred_element_type=F32)
        mirror = mirror.astype(BF16)
        if i == i_mixed:
            r = i * t + lax.broadcasted_iota(jnp.int32, (t, 1), 0)
            mirror = jnp.where(r <= r_last, tile(u_ref, i + 1), mirror)
        y_ref[i * t:(i + 1) * t, :] = mirror


@functools.lru_cache(maxsize=None)
def _dft_constants(seq, n_meta):
    t = DFT_TILE
    off = t - n_meta
    big_l = seq + n_meta
    half = big_l // 2
    n_real = seq // t
    nf = -(-(half + 1 + off) // t)
    hp = nf * t
    jf = np.zeros((nf, t, 2 * t), np.float32)
    for i in range(nf):
        a_idx, b_idx = min(n_real - i, n_real - 1), n_real - 1 - i
        for j in range(t):
            pos = i * t + j - off
            if 1 <= pos <= half - 1:
                tau, rho = divmod(big_l - pos - n_meta, t)
                assert tau in (a_idx, b_idx)
                jf[i, j, (t if tau == b_idx else 0) + rho] = 1.0
    r_last = half - n_meta
    i_mixed = r_last // t
    ju = np.zeros((3, t, 2 * t), np.float32)
    for i in range(i_mixed, n_real):
        a_idx, b_idx = n_real - i, n_real - 1 - i
        blk = np.zeros((t, 2 * t), np.float32)
        for j in range(t):
            r = i * t + j
            if r > r_last:
                tau, rho = divmod(big_l - (r + n_meta) + off, t)
                assert tau in (a_idx, b_idx)
                blk[j, (t if tau == b_idx else 0) + rho] = 1.0
        var = 1 if i == i_mixed else 2
        assert var == 1 or not ju[2].any() or (ju[2] == blk).all()
        ju[var] = blk
    ang = lambda prod: 2.0 * np.pi * (prod % big_l).astype(np.float64) / big_l
    pos = np.arange(hp, dtype=np.int64) - off
    d = ang(np.arange(t, dtype=np.int64)[:, None] * pos[None, :])
    k0 = t * np.arange(nf, dtype=np.int64) - off
    row = np.broadcast_to(ang(k0[:, None] * pos[None, :])[:, None, :], (nf, 8, hp))
    f = lambda a: np.ascontiguousarray(a, dtype=np.float32)
    return dict(jf=jf, ju=ju, dc=f(np.cos(d)), ds=f(np.sin(d)), rowc=f(np.cos(row)),
                rows=f(np.sin(row)), nf=nf, hp=hp, off=off, n_real=n_real,
                m_last=half + off, r_last=r_last, i_mixed=i_mixed)


def _seq_dft(pq, pqm, seq, n_meta):
    c = _dft_constants(seq, n_meta)
    t, nf, hp, ncol = DFT_TILE, c["nf"], c["hp"], pq.shape[2]
    cw = DFT_COLS
    ncb, mcb = ncol // cw, pqm.shape[2] // cw
    assert c["off"] == t - n_meta and pqm.shape[2] % cw == 0 and pqm.shape[1] <= t
    col_blk = lambda rows: pl.BlockSpec((1, rows, cw), lambda g, b: (g, 0, b))
    meta_blk = pl.BlockSpec((1, pqm.shape[1], cw), lambda g, b: (g, 0, b % mcb))
    peqo = pl.pallas_call(
        functools.partial(_fold_kernel, m_last=c["m_last"]),
        grid=(2, ncb),
        in_specs=[col_blk(seq), meta_blk, _resident(c["jf"].shape)],
        out_specs=col_blk(hp),
        out_shape=jax.ShapeDtypeStruct((2, hp, ncol), BF16),
        compiler_params=pltpu.CompilerParams(
            dimension_semantics=("arbitrary", "arbitrary"), vmem_limit_bytes=V7X_VMEM_LIMIT),
        name="dft_fold",
    )(pq, pqm, jnp.asarray(c["jf"]).astype(BF16))

    scale = float((seq + n_meta) * FNET_GROUP_DIM) ** -0.5
    tile = pl.BlockSpec((t, ncol), lambda i: (i, 0))
    rowspec = pl.BlockSpec((1, 8, hp), lambda i: (i, 0, 0))
    u, w = pl.pallas_call(
        functools.partial(_dft_kernel, scale=scale),
        grid=(nf,),
        in_specs=[_resident((2, hp, ncol)), _resident((t, hp)), _resident((t, hp)),
                  rowspec, rowspec],
        out_specs=(tile, tile),
        out_shape=(jax.ShapeDtypeStruct((hp, ncol), BF16),) * 2,
        compiler_params=pltpu.CompilerParams(
            dimension_semantics=("arbitrary",), vmem_limit_bytes=V7X_VMEM_LIMIT),
        name="seq_dft",
    )(peqo, jnp.asarray(c["dc"]), jnp.asarray(c["ds"]), jnp.asarray(c["rowc"]),
      jnp.asarray(c["rows"]))

    col = lambda rows: pl.BlockSpec((rows, cw), lambda b: (0, b))
    return pl.pallas_call(
        functools.partial(_unfold_kernel, r_last=c["r_last"], i_mixed=c["i_mixed"]),
        grid=(ncb,),
        in_specs=[col(hp), col(hp), _resident((2, t, 2 * t))],
        out_specs=col(seq),
        out_shape=jax.ShapeDtypeStruct((seq, ncol), BF16),
        compiler_params=pltpu.CompilerParams(
            dimension_semantics=("arbitrary",), vmem_limit_bytes=V7X_VMEM_LIMIT),
        name="dft_unfold",
    )(u, w, jnp.asarray(c["ju"][1:]).astype(BF16))


FFN_CHUNK = 1024
TAIL_SPLIT = 2
TAIL_TM = 512


def _tail_kernel(x_ref, ol_ref, of_ref, ob_ref, y_ref, nmix_ref, gn_ref, nffn_ref, nfin_ref,
                 wg_ref, wm_ref, wo_ref, wf_ref, wout_ref, w1_ref, w2_ref, o_ref):
    tm = x_ref.shape[0]
    sub = tm // TAIL_SPLIT
    tiles = [(dict(), slice(i * sub, (i + 1) * sub)) for i in range(TAIL_SPLIT)]
    dot = functools.partial(jnp.dot, preferred_element_type=F32)
    nchunk = D_FF // FFN_CHUNK

    def norm_in(s, r):
        s["x"] = x_ref[r, :]
        s["xn"] = _rms(s["x"], nmix_ref[...]).astype(BF16)

    def gates(s, r):
        s["g"] = [dot(s["xn"], wg_ref[:, h * HEAD_DV:(h + 1) * HEAD_DV])
                  for h in range(GLA_HEADS)]
        s["ga"] = dot(s["xn"], wm_ref[:, 0:D_MODEL])
        s["gf"] = dot(s["xn"], wm_ref[:, D_MODEL:2 * D_MODEL])
        s["y_fnet"] = dot(y_ref[r, :], wf_ref[...])

    def gla_out(s, r):
        y_gla = None
        for h in range(GLA_HEADS):
            hv = slice(h * HEAD_DV, (h + 1) * HEAD_DV)
            o = (ol_ref[r, hv].astype(F32) + of_ref[r, hv].astype(F32)
                 + ob_ref[r, hv].astype(F32))
            g = s["g"][h]
            og = (_rms(o, gn_ref[...]) * (g * _sigmoid(g))).astype(BF16)
            part = dot(og, wo_ref[hv, :])
            y_gla = part if y_gla is None else y_gla + part
        s["y_gla"] = y_gla

    def merge(s, r):
        merged = _sigmoid(s["ga"]) * s["y_gla"] + _sigmoid(s["gf"]) * s["y_fnet"]
        s["h1"] = s["x"] + dot(merged.astype(BF16), wout_ref[...])
        s["u"] = _rms(s["h1"], nffn_ref[...]).astype(BF16)
        s["acc"] = s["h1"]

    def ffn_up(c):
        def stage(s, r):
            a = jnp.maximum(dot(s["u"], w1_ref[:, c * FFN_CHUNK:(c + 1) * FFN_CHUNK]), 0.0)
            s["a", c] = (a * a).astype(BF16)
        return stage

    def ffn_down(c):
        def stage(s, r):
            s["acc"] = s["acc"] + dot(s.pop(("a", c)),
                                      w2_ref[c * FFN_CHUNK:(c + 1) * FFN_CHUNK, :])
        return stage

    def norm_out(s, r):
        o_ref[r, :] = _rms(s["acc"], nfin_ref[...])

    stages = [norm_in, gates, gla_out, merge, ffn_up(0)]
    for c in range(nchunk):
        if c + 1 < nchunk:
            stages.append(ffn_up(c + 1))
        stages.append(ffn_down(c))
    stages.append(norm_out)
    _run_interleaved(stages, tiles)


def _tail_call(x2d, ol, o_f, o_b, y, nmix, gn, nffn, nfin, w_in_bf, w_merge, wo, wf, wout, w1,
               w2, seq, tm):
    m = x2d.shape[0]
    nt = seq // tm
    row = lambda n: pl.BlockSpec((tm, n), lambda i: (i, 0))
    norms = [nmix, gn, nffn, nfin]
    consts = [w_merge, wo, wf, wout, w1, w2]
    return pl.pallas_call(
        _tail_kernel,
        grid=(m // tm,),
        in_specs=[row(D_MODEL), row(GLA_DV), row(GLA_DV), row(GLA_DV),
                  pl.BlockSpec((tm, FNET_WIDTH), lambda i: (i % nt, i // nt))]
                 + [_resident(c.shape) for c in norms]
                 + [_window((w_in_bf.shape[0], W_WINDOW), (0, 2))]
                 + [_resident(c.shape) for c in consts],
        out_specs=row(D_MODEL),
        out_shape=jax.ShapeDtypeStruct((m, D_MODEL), F32),
        compiler_params=pltpu.CompilerParams(
            dimension_semantics=("arbitrary",), vmem_limit_bytes=V7X_VMEM_LIMIT),
        name="tail",
    )(x2d, ol, o_f, o_b, y, *norms, w_in_bf, *consts)


@functools.lru_cache(maxsize=None)
def _gla_constants(rows):
    idx = np.arange(rows)
    same = (idx[:, None] // SUB) == (idx[None, :] // SUB)
    tri_f = (same & (idx[None, :] <= idx[:, None])).astype(np.float32)
    return tri_f, np.ascontiguousarray(tri_f.T)


@functools.lru_cache(maxsize=None)
def _channel_dft():
    d = np.arange(FNET_GROUP_DIM, dtype=np.int64)
    ang = 2.0 * np.pi * ((d[:, None] * d[None, :]) % FNET_GROUP_DIM) / FNET_GROUP_DIM
    cs = np.zeros((2 * FNET_GROUP_DIM, 2 * FNET_GROUP_DIM), np.float32)
    cs[F_LEAD:F_LEAD + FNET_GROUP_DIM] = np.concatenate([np.cos(ang), np.sin(ang)], axis=1)
    return cs


def kernel(x, meta_tokens, norm_mix, w_in, w_decay_fwd, b_decay_fwd, w_decay_bwd, b_decay_bwd,
           gla_norm, w_o_gla, w_fnet, w_out, norm_ffn, w_ff1, w_ff2, norm_final):
    batch, seq, d = x.shape
    assert d == D_MODEL and meta_tokens.shape == (N_META, D_MODEL)
    assert w_in.shape[0] == 1 and seq % (2 * SCAN_RB) == 0 and seq % DFT_TILE == 0
    gate0 = 2 * GLA_DK + 2 * GLA_DV + 2 * DECAY_RANK + FNET_WIDTH
    assert w_in.shape[2] == gate0 + 2 * D_MODEL and gate0 <= 4 * W_WINDOW
    w_in_t = jnp.swapaxes(w_in[0], 0, 1)
    wi = _transpose_cast_call(w_in_t, 4 * W_WINDOW, W_WINDOW // 2)
    wdec = (w_decay_fwd[0], w_decay_bwd[0])
    bdec = (b_decay_fwd, b_decay_bwd)
    cs = jnp.asarray(_channel_dft())
    nw_mix = norm_mix[0][None, :]

    x2d = x.reshape(batch * seq, D_MODEL)
    tail_weights = (w_o_gla[0], w_fnet[0], w_out[0], w_ff1[0], w_ff2[0])
    v, ol, qe, ke, bend, pq, *tail_weights, w_merge = _mixin_call(
        x2d, nw_mix, wi, wdec, bdec, cs, *_gla_constants(GLA_C), seq, MIX_TM, GLA_C,
        cast=tail_weights, gate_t=w_in_t, gate_row0=gate0)
    vm, _, _, kem, _, pqm = _mixin_call(meta_tokens, nw_mix, wi, wdec, bdec, cs,
                                        *_gla_constants(SUB), SUB, SUB, SUB, lead=SUB - N_META)
    o_f, o_b = _gla_scan_call(qe, ke, v, bend, kem, vm, seq, batch)

    y = _seq_dft(pq, pqm, seq, N_META)

    out = _tail_call(x2d, ol, o_f, o_b, y, nw_mix, gla_norm[0][None, :], norm_ffn[0][None, :],
                     norm_final[None, :], wi, w_merge, *tail_weights, seq, TAIL_TM)
    return out.reshape(batch, seq, D_MODEL)
```

```python
import functools

import numpy as np
import jax
import jax.numpy as jnp
from jax import lax
from jax.experimental import pallas as pl
from jax.experimental.pallas import tpu as pltpu

F32 = jnp.float32
BF16 = jnp.bfloat16

D_MODEL = 1024
N_META = 16
GLA_HEADS = 4
GLA_DK = 512
GLA_DV = 1024
HEAD_DK = GLA_DK // GLA_HEADS
HEAD_DV = GLA_DV // GLA_HEADS
DECAY_RANK = 16
GATE_TEMP = 16.0
SUB = 64
FNET_GROUPS = 4
FNET_WIDTH = 512
FNET_GROUP_DIM = FNET_WIDTH // FNET_GROUPS
D_FF = 4 * D_MODEL
EPS = 1e-6
LOG2E = 1.4426950408889634
LANE = 128

_OFF_Q, _OFF_K, _OFF_V, _OFF_QKV_END = 0, 512, 1024, 2048
W_WINDOW = 1024
R_PAD = 128
F_LEAD = 2 * DECAY_RANK
F_EXT = F_LEAD + FNET_WIDTH + (LANE - F_LEAD)

GLA_C = 4 * SUB
MIX_TM = 1024
GATE_SUB = 32

V7X_VMEM_LIMIT = 58 * 1024 * 1024


def _rms(x, w):
    ms = jnp.mean(x * x, axis=-1, keepdims=True)
    return x * lax.rsqrt(ms + EPS) * w


def _sigmoid(x):
    return 1.0 / (1.0 + jnp.exp(-x))


def _run_interleaved(stages, tiles):
    for stage in stages:
        for state, rows in tiles:
            stage(state, rows)


def _window(block, index):
    return pl.BlockSpec(block, lambda *_: index, pipeline_mode=pl.Buffered(1))


def _resident(shape):
    return _window(shape, (0,) * len(shape))


_NT = (((1,), (1,)), ((), ()))
_TN = (((0,), (0,)), ((), ()))


def _mixin_kernel(x_ref, nw_ref, wqkv_ref, wrf_ref, wdf_ref, wdb_ref, bdf_ref, bdb_ref, cs_ref,
                  trif_ref, trib_ref, maskf_ref, maskb_ref, *refs, chunk, n_cast, n_gate, lead):
    if lead:
        *refs, tile_ref = refs
        tile_ref[...] = jnp.concatenate([jnp.zeros((lead, x_ref.shape[1]), F32), x_ref[...]],
                                        axis=0)
        x_ref = tile_ref
    n_in = n_cast + n_gate
    cast_in, gate_in = refs[:n_cast], refs[n_cast:n_in]
    v_ref, ol_ref, qe_ref, ke_ref, bend_ref, pq_ref = refs[n_in:n_in + 6]
    cast_out = refs[n_in + 6:n_in + 6 + n_cast]
    _mixin_body(x_ref, nw_ref, wqkv_ref, wrf_ref, (wdf_ref, wdb_ref), (bdf_ref, bdb_ref), cs_ref,
                trif_ref, trib_ref,
                maskf_ref, maskb_ref, v_ref, ol_ref, qe_ref, ke_ref, bend_ref, pq_ref, chunk=chunk)
    for src, dst in zip(cast_in, cast_out):
        dst[...] = src[...].astype(BF16)
    if n_gate:
        feat_major = jnp.concatenate([r[...] for r in gate_in], axis=0)
        refs[-1][...] = feat_major.T.astype(BF16)


def _mixin_body(x_ref, nw_ref, wqkv_ref, wrf_ref, wdec_refs, bdec_refs, cs_ref,
                trif_ref, trib_ref, maskf_ref, maskb_ref,
                v_ref, ol_ref, qe_ref, ke_ref, bend_ref, pq_ref, *, chunk):
    tm = x_ref.shape[0]
    n, s = chunk // SUB, SUB
    tiles = [(dict(), slice(i * chunk, (i + 1) * chunk)) for i in range(tm // chunk)]
    dot = functools.partial(jnp.dot, preferred_element_type=F32)
    cs = cs_ref[...].astype(BF16)
    tris = (trif_ref[...], trib_ref[...])
    masks = (maskf_ref[0:s, 0:s] != 0.0, maskb_ref[0:s, 0:s] != 0.0)
    half = GLA_DK // 2
    zeros = lambda rows: jnp.zeros((rows, GLA_DK), F32)
    wdec = jnp.concatenate([
        jnp.concatenate([wdec_refs[0][...], zeros(R_PAD - DECAY_RANK)], axis=0),
        jnp.concatenate([zeros(DECAY_RANK), wdec_refs[1][...], zeros(R_PAD - 2 * DECAY_RANK)],
                        axis=0)], axis=1).astype(BF16)
    bdec = jnp.concatenate([bdec_refs[0][...], bdec_refs[1][...]], axis=1)

    def norm_in(st, r):
        st["xn"] = _rms(x_ref[r, :], nw_ref[...]).astype(BF16)

    def decay_rank(st, r):
        st["f"] = dot(st["xn"], wrf_ref[:, 0:F_EXT]).astype(BF16)
        st["r"] = st["f"][:, 0:R_PAD]

    def log_decay(st, j):
        cols = slice(j * half, (j + 1) * half)
        x = dot(st["r"], wdec[:, cols]) + bdec[:, cols]
        la = (jnp.minimum(x, 0.0) - jnp.log(1.0 + jnp.exp(-jnp.abs(x)))) * (LOG2E / GATE_TEMP)
        hi = la.astype(BF16)
        st["la", j] = (hi, (la - hi.astype(F32)).astype(BF16))

    def project(st, r):
        mm = lambda lo, hi: dot(st["xn"], wqkv_ref[:, lo:hi])
        vmid = (_OFF_V + _OFF_QKV_END) // 2
        log_decay(st, 0)
        st["q"] = mm(_OFF_Q, _OFF_K) * (HEAD_DK ** -0.5)
        log_decay(st, 1)
        st["k"] = mm(_OFF_K, _OFF_V)
        log_decay(st, 2)
        v_ref[r, 0:GLA_DV // 2] = mm(_OFF_V, vmid).astype(BF16)
        log_decay(st, 3)
        v_ref[r, GLA_DV // 2:GLA_DV] = mm(vmid, _OFF_QKV_END).astype(BF16)

    def decay_factors(st, r):
        c0 = r.start // s
        for j in range(4):
            d, qcols = j // 2, slice((j % 2) * half, (j % 2 + 1) * half)
            cols = slice(j * half, (j + 1) * half)
            hi, lo = st.pop(("la", j))
            b = (dot(tris[d], hi) + dot(tris[d], lo)).reshape(n, s, half)
            if d == 0:
                b_mid, b_end = b[:, s // 2 - 1:s // 2, :], b[:, s - 1:s, :]
            else:
                b_mid, b_end = b[:, s // 2:s // 2 + 1, :], b[:, 0:1, :]
            qd = st["q"][:, qcols].reshape(n, s, half) * jnp.exp2(b - b_mid)
            kd = st["k"][:, qcols].reshape(n, s, half) * jnp.exp2(b_mid - b)
            qe = qd * jnp.exp2(b_mid)
            ke = kd * jnp.exp2(b_end - b_mid)
            st["qd", d, j % 2] = qd.reshape(chunk, half).astype(BF16)
            st["kd", d, j % 2] = kd.reshape(chunk, half).astype(BF16)
            qe_ref[r, cols] = qe.reshape(chunk, half).astype(BF16)
            ke_ref[r, cols] = ke.reshape(chunk, half).astype(BF16)
            bend_ref[c0:c0 + n, cols] = b_end.reshape(n, half)

    def intra_chunk(st, r):
        for h in range(GLA_HEADS):
            hk = slice((h % 2) * HEAD_DK, (h % 2 + 1) * HEAD_DK)
            hv = slice(h * HEAD_DV, (h + 1) * HEAD_DV)
            sc = [[lax.dot_general(st["qd", d, h // 2][c * s:(c + 1) * s, hk],
                                   st["kd", d, h // 2][c * s:(c + 1) * s, hk], _NT,
                                   preferred_element_type=F32) for d in range(2)]
                  for c in range(n)]
            for c, (sf, sb) in enumerate(sc):
                rows = slice(r.start + c * s, r.start + (c + 1) * s)
                both = (jnp.where(masks[0], sf, 0.0) + jnp.where(masks[1], sb, 0.0)).astype(BF16)
                ol_ref[rows, hv] = dot(both, v_ref[rows, hv]).astype(BF16)

    def channel_dft(st, r):
        for grp in range(FNET_GROUPS):
            sl = slice(grp * FNET_GROUP_DIM, (grp + 1) * FNET_GROUP_DIM)
            pq = dot(st["f"][:, grp * FNET_GROUP_DIM:(grp + 2) * FNET_GROUP_DIM], cs)
            pq_ref[0, r, sl] = pq[:, :FNET_GROUP_DIM].astype(BF16)
            pq_ref[1, r, sl] = pq[:, FNET_GROUP_DIM:].astype(BF16)

    _run_interleaved((norm_in, decay_rank, project, decay_factors, intra_chunk, channel_dft),
                     tiles)


def _mixin_call(x2d, nw, w_in_bf, wdec, bdec, cs, tri_f, tri_b, rows_per_batch, tm, chunk,
                cast=(), gate_t=None, gate_row0=0, lead=0):
    m = x2d.shape[0] + lead
    d_model = w_in_bf.shape[0]
    nt = rows_per_batch // tm
    nb = m // rows_per_batch
    steps = m // tm
    assert not lead or (steps == 1 and lead % 8 == 0)
    slab = lambda w: pl.BlockSpec((w.shape[0] // steps, w.shape[1]), lambda i: (i, 0))
    assert all(w.shape[0] % (16 * steps) == 0 for w in cast)
    gate_specs, gate_out_spec, gate_out_shape = [], (), ()
    if gate_t is not None:
        feat = 2 * D_MODEL // steps
        assert feat % LANE == 0 and feat % GATE_SUB == 0 and gate_row0 % GATE_SUB == 0
        assert gate_t.shape == (gate_row0 + 2 * D_MODEL, d_model)
        nsub, blk0 = feat // GATE_SUB, gate_row0 // GATE_SUB
        gate_specs = [pl.BlockSpec((GATE_SUB, d_model), lambda i, j=j: (blk0 + i * nsub + j, 0))
                      for j in range(nsub)]
        gate_out_spec = (pl.BlockSpec((d_model, feat), lambda i: (0, i)),)
        gate_out_shape = (jax.ShapeDtypeStruct((d_model, 2 * D_MODEL), BF16),)
    row = lambda n: pl.BlockSpec((tm, n), lambda i: (i, 0))
    pq_spec = pl.BlockSpec((2, None, tm, FNET_WIDTH), lambda i: (0, i // nt, i % nt, 0))
    wide = jax.ShapeDtypeStruct((m, 2 * GLA_DK), BF16)
    pq_shape = jax.ShapeDtypeStruct((2, nb, rows_per_batch, FNET_WIDTH), BF16)
    consts = [*wdec, *bdec, cs,
              jnp.asarray(tri_f).astype(BF16), jnp.asarray(tri_b).astype(BF16),
              jnp.asarray(tri_f), jnp.asarray(tri_b)]
    return pl.pallas_call(
        functools.partial(_mixin_kernel, chunk=chunk, n_cast=len(cast), n_gate=len(gate_specs),
                          lead=lead),
        grid=(steps,),
        in_specs=[_resident(x2d.shape) if lead else row(D_MODEL), _resident(nw.shape),
                  _window((d_model, _OFF_QKV_END), (0, 0)),
                  _window((d_model, W_WINDOW), (0, 3))]
                 + [_resident(c.shape) for c in consts] + [slab(w) for w in cast] + gate_specs,
        out_specs=(row(GLA_DV), row(GLA_DV), row(2 * GLA_DK), row(2 * GLA_DK),
                   pl.BlockSpec((tm // SUB, 2 * GLA_DK), lambda i: (i, 0)), pq_spec)
                  + tuple(slab(w) for w in cast) + gate_out_spec,
        out_shape=(jax.ShapeDtypeStruct((m, GLA_DV), BF16), jax.ShapeDtypeStruct((m, GLA_DV), BF16),
                   wide, wide, jax.ShapeDtypeStruct((m // SUB, 2 * GLA_DK), F32), pq_shape)
                  + tuple(jax.ShapeDtypeStruct(w.shape, BF16) for w in cast) + gate_out_shape,
        scratch_shapes=[pltpu.VMEM((tm, D_MODEL), F32)] if lead else [],
        compiler_params=pltpu.CompilerParams(
            dimension_semantics=("arbitrary",), vmem_limit_bytes=V7X_VMEM_LIMIT),
        name="mixin",
    )(x2d, nw, w_in_bf, w_in_bf, *consts, *cast, *([gate_t] * len(gate_specs)))


def _transpose_cast_kernel(x_ref, o_ref):
    o_ref[...] = x_ref[...].T.astype(BF16)


def _transpose_cast_call(w_t, n_rows, block):
    d_model = w_t.shape[1]
    assert n_rows % block == 0 and n_rows <= w_t.shape[0]
    return pl.pallas_call(
        _transpose_cast_kernel,
        grid=(n_rows // block,),
        in_specs=[pl.BlockSpec((block, d_model), lambda i: (i, 0))],
        out_specs=pl.BlockSpec((d_model, block), lambda i: (0, i)),
        out_shape=jax.ShapeDtypeStruct((d_model, n_rows), BF16),
        compiler_params=pltpu.CompilerParams(
            dimension_semantics=("arbitrary",), vmem_limit_bytes=V7X_VMEM_LIMIT),
        name="w_in_prep",
    )(w_t)


SCAN_G = 512
SCAN_RB = 4096


def _scan_group(dirs):
    s, n = SUB, SCAN_G // SUB
    ahead = 2
    dec, st, inter = [], [], [{}, {}]
    for qe_ref, ke_ref, v_ref, bend_ref, st_ref, row0, chunk0, backward in dirs:
        dec_rows = jnp.tile(jnp.exp2(bend_ref[chunk0:chunk0 + n, :]), (HEAD_DK // n, 1))
        dec.append(dec_rows.T)
        st.append(st_ref[...])

    def order(step, backward):
        return n - 1 - step if backward else step

    def increment(d, step):
        _, ke_ref, v_ref, _, _, row0, _, backward = dirs[d]
        i = order(step, backward)
        sub = slice(row0 + i * s, row0 + (i + 1) * s)
        return lax.dot_general(ke_ref[sub, :], v_ref[sub, :], _TN,
                               preferred_element_type=F32)

    inc = {(d, step): increment(d, step) for step in range(ahead) for d in range(2)}
    for step in range(n):
        for d, (qe_ref, _, _, _, _, row0, _, backward) in enumerate(dirs):
            i = order(step, backward)
            sub = slice(row0 + i * s, row0 + (i + 1) * s)
            inter[d][i] = jnp.dot(qe_ref[sub, :], st[d].astype(BF16),
                                  preferred_element_type=F32)
            if step + ahead < n:
                inc[d, step + ahead] = increment(d, step + ahead)
            decay = jnp.broadcast_to(dec[d][:, i:i + 1], (HEAD_DK, HEAD_DV))
            st[d] = st[d] * decay + inc.pop((d, step))
    for d, spec in enumerate(dirs):
        spec[4][...] = st[d]
    return [jnp.concatenate([inter[d][i] for i in range(n)], axis=0) for d in range(2)]


def _gla_scan_kernel(qef_ref, kef_ref, vf_ref, bendf_ref,
                     qeb_ref, keb_ref, vb_ref, bendb_ref, kem_ref, vm_ref,
                     of_ref, ob_ref, stf_ref, stb_ref):
    rb = qef_ref.shape[0]
    ng = rb // SCAN_G

    @pl.when(pl.program_id(2) == 0)
    def _():
        stf_ref[...] = lax.dot_general(kem_ref[...], vm_ref[...], _TN,
                                       preferred_element_type=F32)
        stb_ref[...] = jnp.zeros_like(stb_ref)

    for j in range(ng):
        jb = ng - 1 - j
        cpg = SCAN_G // SUB
        in_f, in_b = _scan_group([
            (qef_ref, kef_ref, vf_ref, bendf_ref, stf_ref, j * SCAN_G, j * cpg, False),
            (qeb_ref, keb_ref, vb_ref, bendb_ref, stb_ref, jb * SCAN_G, jb * cpg, True)])
        of_ref[j * SCAN_G:(j + 1) * SCAN_G, :] = in_f.astype(BF16)
        ob_ref[jb * SCAN_G:(jb + 1) * SCAN_G, :] = in_b.astype(BF16)


def _gla_scan_call(qe, ke, v, bend, kem, vm, seq, batch):
    h, rb = GLA_HEADS, SCAN_RB
    nt = seq // rb
    fwd = lambda b, hh, t: b * nt + t
    bwd = lambda b, hh, t: b * nt + nt - 1 - t
    cb = rb // SUB

    def dk_blk(rowfn, dircol, rows):
        return pl.BlockSpec((rows, HEAD_DK), lambda b, hh, t: (rowfn(b, hh, t), dircol + hh))

    def dv_blk(rowfn):
        return pl.BlockSpec((rb, HEAD_DV), lambda b, hh, t: (rowfn(b, hh, t), hh))

    in_specs = [
        dk_blk(fwd, 0, rb), dk_blk(fwd, 0, rb), dv_blk(fwd), dk_blk(fwd, 0, cb),
        dk_blk(bwd, h, rb), dk_blk(bwd, h, rb), dv_blk(bwd), dk_blk(bwd, h, cb),
        pl.BlockSpec((SUB, HEAD_DK), lambda b, hh, t: (0, hh)),
        pl.BlockSpec((SUB, HEAD_DV), lambda b, hh, t: (0, hh)),
    ]
    out = jax.ShapeDtypeStruct((batch * seq, GLA_DV), BF16)
    return pl.pallas_call(
        _gla_scan_kernel,
        grid=(batch, h, nt),
        in_specs=in_specs,
        out_specs=(dv_blk(fwd), dv_blk(bwd)),
        out_shape=(out, out),
        scratch_shapes=[pltpu.VMEM((HEAD_DK, HEAD_DV), F32), pltpu.VMEM((HEAD_DK, HEAD_DV), F32)],
        compiler_params=pltpu.CompilerParams(
            dimension_semantics=("arbitrary", "arbitrary", "arbitrary"),
            vmem_limit_bytes=V7X_VMEM_LIMIT),
        name="gla_scan",
    )(qe, ke, v, bend, qe, ke, v, bend, kem, vm)


DFT_TILE = 256


def _fold_kernel(x_ref, z_ref, j_ref, o_ref, *, m_last):
    t = DFT_TILE
    n_real = x_ref.shape[1] // t
    sign = jnp.where(pl.program_id(0) == 0, 1.0, -1.0)
    real = lambda i: x_ref[0, i * t:(i + 1) * t, :]
    lead = jnp.concatenate([jnp.zeros((t - z_ref.shape[1], z_ref.shape[2]), BF16), z_ref[0]],
                           axis=0)
    for i in range(o_ref.shape[1] // t):
        direct = (lead if i == 0 else real(i - 1)).astype(F32)
        mirror = (jnp.dot(j_ref[i, :, :t], real(min(n_real - i, n_real - 1)),
                          preferred_element_type=F32)
                  + jnp.dot(j_ref[i, :, t:], real(n_real - 1 - i), preferred_element_type=F32))
        m = i * t + lax.broadcasted_iota(jnp.int32, (t, 1), 0)
        folded = jnp.where(m <= m_last, direct + sign * mirror, 0.0)
        o_ref[0, i * t:(i + 1) * t, :] = folded.astype(BF16)


def _dft_kernel(peqo_ref, dc_ref, ds_ref, rowc_ref, rows_ref, u_ref, w_ref, *, scale):
    t = DFT_TILE
    acc_a = acc_b = None
    for c in range(peqo_ref.shape[1] // t):
        sl = slice(c * t, (c + 1) * t)
        rc, rs = rowc_ref[0, 0:1, sl], rows_ref[0, 0:1, sl]
        dc, ds = dc_ref[:, sl], ds_ref[:, sl]
        tc = (dc * rc - ds * rs).astype(BF16)
        ts = (ds * rc + dc * rs).astype(BF16)
        a = jnp.dot(tc, peqo_ref[0, sl, :], preferred_element_type=F32)
        b = jnp.dot(ts, peqo_ref[1, sl, :], preferred_element_type=F32)
        acc_a = a if acc_a is None else acc_a + a
        acc_b = b if acc_b is None else acc_b + b
    u_ref[...] = ((acc_a - acc_b) * scale).astype(BF16)
    w_ref[...] = ((acc_a + acc_b) * scale).astype(BF16)


def _unfold_kernel(u_ref, w_ref, j_ref, y_ref, *, r_last, i_mixed):
    t = DFT_TILE
    tile = lambda ref, i: ref[i * t:(i + 1) * t, :]
    n_real = y_ref.shape[0] // t
    for i in range(n_real):
        if i < i_mixed:
            y_ref[i * t:(i + 1) * t, :] = tile(u_ref, i + 1)
            continue
        j = 0 if i == i_mixed else 1
        mirror = jnp.dot(j_ref[j, :, t:], tile(w_ref, n_real - 1 - i), preferred_element_type=F32)
        if (n_real - i + 1) * t <= w_ref.shape[0]:
            mirror += jnp.dot(j_ref[j, :, :t], tile(w_ref, n_real - i),
                              preferred_element_type=F32)
        mirror = mirror.astype(BF16)
        if i == i_mixed:
            r = i * t + lax.broadcasted_iota(jnp.int32, (t, 1), 0)
            mirror = jnp.where(r <= r_last, tile(u_ref, i + 1), mirror)
        y_ref[i * t:(i + 1) * t, :] = mirror


@functools.lru_cache(maxsize=None)
def _dft_constants(seq, n_meta):
    t = DFT_TILE
    off = t - n_meta
    big_l = seq + n_meta
    half = big_l // 2
    n_real = seq // t
    nf = -(-(half + 1 + off) // t)
    hp = nf * t
    jf = np.zeros((nf, t, 2 * t), np.float32)
    for i in range(nf):
        a_idx, b_idx = min(n_real - i, n_real - 1), n_real - 1 - i
        for j in range(t):
            pos = i * t + j - off
            if 1 <= pos <= half - 1:
                tau, rho = divmod(big_l - pos - n_meta, t)
                assert tau in (a_idx, b_idx)
                jf[i, j, (t if tau == b_idx else 0) + rho] = 1.0
    r_last = half - n_meta
    i_mixed = r_last // t
    ju = np.zeros((3, t, 2 * t), np.float32)
    for i in range(i_mixed, n_real):
        a_idx, b_idx = n_real - i, n_real - 1 - i
        blk = np.zeros((t, 2 * t), np.float32)
        for j in range(t):
            r = i * t + j
            if r > r_last:
                tau, rho = divmod(big_l - (r + n_meta) + off, t)
                assert tau in (a_idx, b_idx)
                blk[j, (t if tau == b_idx else 0) + rho] = 1.0
        var = 1 if i == i_mixed else 2
        assert var == 1 or not ju[2].any() or (ju[2] == blk).all()
        ju[var] = blk
    ang = lambda prod: 2.0 * np.pi * (prod % big_l).astype(np.float64) / big_l
    pos = np.arange(hp, dtype=np.int64) - off
    d = ang(np.arange(t, dtype=np.int64)[:, None] * pos[None, :])
    k0 = t * np.arange(nf, dtype=np.int64) - off
    row = np.broadcast_to(ang(k0[:, None] * pos[None, :])[:, None, :], (nf, 8, hp))
    f = lambda a: np.ascontiguousarray(a, dtype=np.float32)
    return dict(jf=jf, ju=ju, dc=f(np.cos(d)), ds=f(np.sin(d)), rowc=f(np.cos(row)),
                rows=f(np.sin(row)), nf=nf, hp=hp, off=off, n_real=n_real,
                m_last=half + off, r_last=r_last, i_mixed=i_mixed)


def _seq_dft(pq, pqm, seq, n_meta):
    c = _dft_constants(seq, n_meta)
    t, nf, hp, ncb = DFT_TILE, c["nf"], c["hp"], pq.shape[1]
    ncol = ncb * FNET_WIDTH
    assert pq.shape[2:] == (seq, FNET_WIDTH)
    assert c["off"] == t - n_meta and pqm.shape[2] == FNET_WIDTH and pqm.shape[1] <= t
    col_blk = lambda rows: pl.BlockSpec((1, rows, FNET_WIDTH), lambda g, b: (g, 0, b))
    slab_blk = pl.BlockSpec((1, None, seq, FNET_WIDTH), lambda g, b: (g, b, 0, 0))
    meta_blk = pl.BlockSpec((1, pqm.shape[1], FNET_WIDTH), lambda g, b: (g, 0, 0))
    peqo = pl.pallas_call(
        functools.partial(_fold_kernel, m_last=c["m_last"]),
        grid=(2, ncb),
        in_specs=[slab_blk, meta_blk, _resident(c["jf"].shape)],
        out_specs=col_blk(hp),
        out_shape=jax.ShapeDtypeStruct((2, hp, ncol), BF16),
        compiler_params=pltpu.CompilerParams(
            dimension_semantics=("arbitrary", "arbitrary"), vmem_limit_bytes=V7X_VMEM_LIMIT),
        name="dft_fold",
    )(pq, pqm, jnp.asarray(c["jf"]).astype(BF16))

    scale = float((seq + n_meta) * FNET_GROUP_DIM) ** -0.5
    tile = pl.BlockSpec((t, ncol), lambda i: (i, 0))
    rowspec = pl.BlockSpec((1, 8, hp), lambda i: (i, 0, 0))
    u, w = pl.pallas_call(
        functools.partial(_dft_kernel, scale=scale),
        grid=(nf,),
        in_specs=[_resident((2, hp, ncol)), _resident((t, hp)), _resident((t, hp)),
                  rowspec, rowspec],
        out_specs=(tile, tile),
        out_shape=(jax.ShapeDtypeStruct((hp, ncol), BF16),) * 2,
        compiler_params=pltpu.CompilerParams(
            dimension_semantics=("arbitrary",), vmem_limit_bytes=V7X_VMEM_LIMIT),
        name="seq_dft",
    )(peqo, jnp.asarray(c["dc"]), jnp.asarray(c["ds"]), jnp.asarray(c["rowc"]),
      jnp.asarray(c["rows"]))

    col = lambda rows: pl.BlockSpec((rows, FNET_WIDTH), lambda b: (0, b))
    return pl.pallas_call(
        functools.partial(_unfold_kernel, r_last=c["r_last"], i_mixed=c["i_mixed"]),
        grid=(ncb,),
        in_specs=[col(hp), col(hp), _resident((2, t, 2 * t))],
        out_specs=col(seq),
        out_shape=jax.ShapeDtypeStruct((seq, ncol), BF16),
        compiler_params=pltpu.CompilerParams(
            dimension_semantics=("arbitrary",), vmem_limit_bytes=V7X_VMEM_LIMIT),
        name="dft_unfold",
    )(u, w, jnp.asarray(c["ju"][1:]).astype(BF16))


FFN_CHUNK = 1024
TAIL_SPLIT = 2
TAIL_TM = 512


def _tail_kernel(x_ref, ol_ref, of_ref, ob_ref, y_ref, nmix_ref, gn_ref, nffn_ref, nfin_ref,
                 wg_ref, wm_ref, wo_ref, wf_ref, wout_ref, w1_ref, w2_ref, o_ref):
    tm = x_ref.shape[0]
    sub = tm // TAIL_SPLIT
    tiles = [(dict(), slice(i * sub, (i + 1) * sub)) for i in range(TAIL_SPLIT)]
    dot = functools.partial(jnp.dot, preferred_element_type=F32)
    nchunk = D_FF // FFN_CHUNK

    def norm_in(s, r):
        s["x"] = x_ref[r, :]
        s["xn"] = _rms(s["x"], nmix_ref[...]).astype(BF16)

    def gates(s, r):
        s["g"] = [dot(s["xn"], wg_ref[:, h * HEAD_DV:(h + 1) * HEAD_DV])
                  for h in range(GLA_HEADS)]
        s["ga"] = dot(s["xn"], wm_ref[:, 0:D_MODEL])
        s["gf"] = dot(s["xn"], wm_ref[:, D_MODEL:2 * D_MODEL])
        s["y_fnet"] = dot(y_ref[r, :], wf_ref[...])

    def gla_out(s, r):
        y_gla = None
        for h in range(GLA_HEADS):
            hv = slice(h * HEAD_DV, (h + 1) * HEAD_DV)
            o = (ol_ref[r, hv].astype(F32) + of_ref[r, hv].astype(F32)
                 + ob_ref[r, hv].astype(F32))
            g = s["g"][h]
            og = (_rms(o, gn_ref[...]) * (g * _sigmoid(g))).astype(BF16)
            part = dot(og, wo_ref[hv, :])
            y_gla = part if y_gla is None else y_gla + part
        s["y_gla"] = y_gla

    def merge(s, r):
        merged = _sigmoid(s["ga"]) * s["y_gla"] + _sigmoid(s["gf"]) * s["y_fnet"]
        s["h1"] = s["x"] + dot(merged.astype(BF16), wout_ref[...])
        s["u"] = _rms(s["h1"], nffn_ref[...]).astype(BF16)
        s["acc"] = s["h1"]

    def ffn_up(c):
        def stage(s, r):
            a = jnp.maximum(dot(s["u"], w1_ref[:, c * FFN_CHUNK:(c + 1) * FFN_CHUNK]), 0.0)
            s["a", c] = (a * a).astype(BF16)
        return stage

    def ffn_down(c):
        def stage(s, r):
            s["acc"] = s["acc"] + dot(s.pop(("a", c)),
                                      w2_ref[c * FFN_CHUNK:(c + 1) * FFN_CHUNK, :])
        return stage

    def norm_out(s, r):
        o_ref[r, :] = _rms(s["acc"], nfin_ref[...])

    stages = [norm_in, gates, gla_out, merge, ffn_up(0)]
    for c in range(nchunk):
        if c + 1 < nchunk:
            stages.append(ffn_up(c + 1))
        stages.append(ffn_down(c))
    stages.append(norm_out)
    _run_interleaved(stages, tiles)


def _tail_call(x2d, ol, o_f, o_b, y, nmix, gn, nffn, nfin, w_in_bf, w_merge, wo, wf, wout, w1,
               w2, seq, tm):
    m = x2d.shape[0]
    nt = seq // tm
    row = lambda n: pl.BlockSpec((tm, n), lambda i: (i, 0))
    norms = [nmix, gn, nffn, nfin]
    consts = [w_merge, wo, wf, wout, w1, w2]
    return pl.pallas_call(
        _tail_kernel,
        grid=(m // tm,),
        in_specs=[row(D_MODEL), row(GLA_DV), row(GLA_DV), row(GLA_DV),
                  pl.BlockSpec((tm, FNET_WIDTH), lambda i: (i % nt, i // nt))]
                 + [_resident(c.shape) for c in norms]
                 + [_window((w_in_bf.shape[0], W_WINDOW), (0, 2))]
                 + [_resident(c.shape) for c in consts],
        out_specs=row(D_MODEL),
        out_shape=jax.ShapeDtypeStruct((m, D_MODEL), F32),
        compiler_params=pltpu.CompilerParams(
            dimension_semantics=("arbitrary",), vmem_limit_bytes=V7X_VMEM_LIMIT),
        name="tail",
    )(x2d, ol, o_f, o_b, y, *norms, w_in_bf, *consts)


@functools.lru_cache(maxsize=None)
def _gla_constants(rows):
    idx = np.arange(rows)
    same = (idx[:, None] // SUB) == (idx[None, :] // SUB)
    tri_f = (same & (idx[None, :] <= idx[:, None])).astype(np.float32)
    return tri_f, np.ascontiguousarray(tri_f.T)


@functools.lru_cache(maxsize=None)
def _channel_dft():
    d = np.arange(FNET_GROUP_DIM, dtype=np.int64)
    ang = 2.0 * np.pi * ((d[:, None] * d[None, :]) % FNET_GROUP_DIM) / FNET_GROUP_DIM
    cs = np.zeros((2 * FNET_GROUP_DIM, 2 * FNET_GROUP_DIM), np.float32)
    cs[F_LEAD:F_LEAD + FNET_GROUP_DIM] = np.concatenate([np.cos(ang), np.sin(ang)], axis=1)
    return cs


def kernel(x, meta_tokens, norm_mix, w_in, w_decay_fwd, b_decay_fwd, w_decay_bwd, b_decay_bwd,
           gla_norm, w_o_gla, w_fnet, w_out, norm_ffn, w_ff1, w_ff2, norm_final):
    batch, seq, d = x.shape
    assert d == D_MODEL and meta_tokens.shape == (N_META, D_MODEL)
    assert w_in.shape[0] == 1 and seq % (2 * SCAN_RB) == 0 and seq % DFT_TILE == 0
    gate0 = 2 * GLA_DK + 2 * GLA_DV + 2 * DECAY_RANK + FNET_WIDTH
    assert w_in.shape[2] == gate0 + 2 * D_MODEL and gate0 <= 4 * W_WINDOW
    w_in_t = jnp.swapaxes(w_in[0], 0, 1)
    wi = _transpose_cast_call(w_in_t, 4 * W_WINDOW, W_WINDOW)
    wdec = (w_decay_fwd[0], w_decay_bwd[0])
    bdec = (b_decay_fwd, b_decay_bwd)
    cs = jnp.asarray(_channel_dft())
    nw_mix = norm_mix[0][None, :]

    x2d = x.reshape(batch * seq, D_MODEL)
    tail_weights = (w_o_gla[0], w_fnet[0], w_out[0], w_ff1[0], w_ff2[0])
    v, ol, qe, ke, bend, pq, *tail_weights, w_merge = _mixin_call(
        x2d, nw_mix, wi, wdec, bdec, cs, *_gla_constants(GLA_C), seq, MIX_TM, GLA_C,
        cast=tail_weights, gate_t=w_in_t, gate_row0=gate0)
    vm, _, _, kem, _, pqm = _mixin_call(meta_tokens, nw_mix, wi, wdec, bdec, cs,
                                        *_gla_constants(SUB), SUB, SUB, SUB, lead=SUB - N_META)
    o_f, o_b = _gla_scan_call(qe, ke, v, bend, kem, vm, seq, batch)

    y = _seq_dft(pq, pqm[:, 0], seq, N_META)

    out = _tail_call(x2d, ol, o_f, o_b, y, nw_mix, gla_norm[0][None, :], norm_ffn[0][None, :],
                     norm_final[None, :], wi, w_merge, *tail_weights, seq, TAIL_TM)
    return out.reshape(batch, seq, D_MODEL)
```

```python
import functools

import numpy as np
import jax
import jax.numpy as jnp
from jax import lax
from jax.experimental import pallas as pl
from jax.experimental.pallas import tpu as pltpu

F32 = jnp.float32
BF16 = jnp.bfloat16

D_MODEL = 1024
N_META = 16
GLA_HEADS = 4
GLA_DK = 512
GLA_DV = 1024
HEAD_DK = GLA_DK // GLA_HEADS
HEAD_DV = GLA_DV // GLA_HEADS
DECAY_RANK = 16
GATE_TEMP = 16.0
SUB = 64
FNET_GROUPS = 4
FNET_WIDTH = 512
FNET_GROUP_DIM = FNET_WIDTH // FNET_GROUPS
D_FF = 4 * D_MODEL
EPS = 1e-6
LOG2E = 1.4426950408889634
LANE = 128

_OFF_Q, _OFF_K, _OFF_V, _OFF_QKV_END = 0, 512, 1024, 2048
W_WINDOW = 1024
R_PAD = 128
F_LEAD = 2 * DECAY_RANK
F_EXT = F_LEAD + FNET_WIDTH + (LANE - F_LEAD)

GLA_C = 4 * SUB
MIX_TM = 1024
GATE_SUB = 32

V7X_VMEM_LIMIT = 58 * 1024 * 1024


def _rms(x, w):
    ms = jnp.mean(x * x, axis=-1, keepdims=True)
    return x * lax.rsqrt(ms + EPS) * w


def _sigmoid(x):
    return 1.0 / (1.0 + jnp.exp(-x))


def _run_interleaved(stages, tiles):
    for stage in stages:
        for state, rows in tiles:
            stage(state, rows)


def _window(block, index):
    return pl.BlockSpec(block, lambda *_: index, pipeline_mode=pl.Buffered(1))


def _resident(shape):
    return _window(shape, (0,) * len(shape))


_NT = (((1,), (1,)), ((), ()))
_TN = (((0,), (0,)), ((), ()))


def _mixin_kernel(x_ref, nw_ref, wqkv_ref, wrf_ref, wdf_ref, wdb_ref, bdf_ref, bdb_ref, cs_ref,
                  trif_ref, trib_ref, maskf_ref, maskb_ref, *refs, chunk, n_cast, n_gate, lead):
    if lead:
        *refs, tile_ref = refs
        tile_ref[...] = jnp.concatenate([jnp.zeros((lead, x_ref.shape[1]), F32), x_ref[...]],
                                        axis=0)
        x_ref = tile_ref
    n_in = n_cast + n_gate
    cast_in, gate_in = refs[:n_cast], refs[n_cast:n_in]
    v_ref, ol_ref, qe_ref, ke_ref, bend_ref, pq_ref = refs[n_in:n_in + 6]
    cast_out = refs[n_in + 6:n_in + 6 + n_cast]
    _mixin_body(x_ref, nw_ref, wqkv_ref, wrf_ref, (wdf_ref, wdb_ref), (bdf_ref, bdb_ref), cs_ref,
                trif_ref, trib_ref,
                maskf_ref, maskb_ref, v_ref, ol_ref, qe_ref, ke_ref, bend_ref, pq_ref, chunk=chunk)
    for src, dst in zip(cast_in, cast_out):
        dst[...] = src[...].astype(BF16)
    if n_gate:
        feat_major = jnp.concatenate([r[...] for r in gate_in], axis=0)
        refs[-1][...] = feat_major.T.astype(BF16)


def _mixin_body(x_ref, nw_ref, wqkv_ref, wrf_ref, wdec_refs, bdec_refs, cs_ref,
                trif_ref, trib_ref, maskf_ref, maskb_ref,
                v_ref, ol_ref, qe_ref, ke_ref, bend_ref, pq_ref, *, chunk):
    tm = x_ref.shape[0]
    n, s = chunk // SUB, SUB
    tiles = [(dict(), slice(i * chunk, (i + 1) * chunk)) for i in range(tm // chunk)]
    dot = functools.partial(jnp.dot, preferred_element_type=F32)
    cs = cs_ref[...].astype(BF16)
    tris = (trif_ref[...], trib_ref[...])
    masks = (maskf_ref[0:s, 0:s] != 0.0, maskb_ref[0:s, 0:s] != 0.0)
    half = GLA_DK // 2
    zeros = lambda rows: jnp.zeros((rows, GLA_DK), F32)
    wdec = jnp.concatenate([
        jnp.concatenate([wdec_refs[0][...], zeros(R_PAD - DECAY_RANK)], axis=0),
        jnp.concatenate([zeros(DECAY_RANK), wdec_refs[1][...], zeros(R_PAD - 2 * DECAY_RANK)],
                        axis=0)], axis=1).astype(BF16)
    bdec = jnp.concatenate([bdec_refs[0][...], bdec_refs[1][...]], axis=1)

    def by_head(ref, head0, r, val, width):
        for i in range(val.shape[1] // width):
            ref[head0 + i, r, :] = val[:, i * width:(i + 1) * width]

    def norm_in(st, r):
        st["xn"] = _rms(x_ref[r, :], nw_ref[...]).astype(BF16)

    def decay_rank(st, r):
        st["f"] = dot(st["xn"], wrf_ref[:, 0:F_EXT]).astype(BF16)
        st["r"] = st["f"][:, 0:R_PAD]

    def log_decay(st, j):
        cols = slice(j * half, (j + 1) * half)
        x = dot(st["r"], wdec[:, cols]) + bdec[:, cols]
        la = (jnp.minimum(x, 0.0) - jnp.log(1.0 + jnp.exp(-jnp.abs(x)))) * (LOG2E / GATE_TEMP)
        hi = la.astype(BF16)
        st["la", j] = (hi, (la - hi.astype(F32)).astype(BF16))

    def project(st, r):
        mm = lambda lo, hi: dot(st["xn"], wqkv_ref[:, lo:hi])
        vmid = (_OFF_V + _OFF_QKV_END) // 2
        log_decay(st, 0)
        st["q"] = mm(_OFF_Q, _OFF_K) * (HEAD_DK ** -0.5)
        log_decay(st, 1)
        st["k"] = mm(_OFF_K, _OFF_V)
        log_decay(st, 2)
        by_head(v_ref, 0, r, mm(_OFF_V, vmid).astype(BF16), HEAD_DV)
        log_decay(st, 3)
        by_head(v_ref, GLA_HEADS // 2, r, mm(vmid, _OFF_QKV_END).astype(BF16), HEAD_DV)

    def decay_factors(st, r):
        c0 = r.start // s
        for j in range(4):
            d, qcols = j // 2, slice((j % 2) * half, (j % 2 + 1) * half)
            cols = slice(j * half, (j + 1) * half)
            hi, lo = st.pop(("la", j))
            b = (dot(tris[d], hi) + dot(tris[d], lo)).reshape(n, s, half)
            if d == 0:
                b_mid, b_end = b[:, s // 2 - 1:s // 2, :], b[:, s - 1:s, :]
            else:
                b_mid, b_end = b[:, s // 2:s // 2 + 1, :], b[:, 0:1, :]
            qd = st["q"][:, qcols].reshape(n, s, half) * jnp.exp2(b - b_mid)
            kd = st["k"][:, qcols].reshape(n, s, half) * jnp.exp2(b_mid - b)
            qe = qd * jnp.exp2(b_mid)
            ke = kd * jnp.exp2(b_end - b_mid)
            st["qd", d, j % 2] = qd.reshape(chunk, half).astype(BF16)
            st["kd", d, j % 2] = kd.reshape(chunk, half).astype(BF16)
            by_head(qe_ref, cols.start // HEAD_DK, r, qe.reshape(chunk, half).astype(BF16), HEAD_DK)
            by_head(ke_ref, cols.start // HEAD_DK, r, ke.reshape(chunk, half).astype(BF16), HEAD_DK)
            bend_ref[c0:c0 + n, cols] = b_end.reshape(n, half)

    def intra_chunk(st, r):
        for h in range(GLA_HEADS):
            hk = slice((h % 2) * HEAD_DK, (h % 2 + 1) * HEAD_DK)
            hv = slice(h * HEAD_DV, (h + 1) * HEAD_DV)
            sc = [[lax.dot_general(st["qd", d, h // 2][c * s:(c + 1) * s, hk],
                                   st["kd", d, h // 2][c * s:(c + 1) * s, hk], _NT,
                                   preferred_element_type=F32) for d in range(2)]
                  for c in range(n)]
            for c, (sf, sb) in enumerate(sc):
                rows = slice(r.start + c * s, r.start + (c + 1) * s)
                both = (jnp.where(masks[0], sf, 0.0) + jnp.where(masks[1], sb, 0.0)).astype(BF16)
                ol_ref[rows, hv] = dot(both, v_ref[h, rows, :]).astype(BF16)

    def channel_dft(st, r):
        for grp in range(FNET_GROUPS):
            sl = slice(grp * FNET_GROUP_DIM, (grp + 1) * FNET_GROUP_DIM)
            pq = dot(st["f"][:, grp * FNET_GROUP_DIM:(grp + 2) * FNET_GROUP_DIM], cs)
            pq_ref[0, r, sl] = pq[:, :FNET_GROUP_DIM].astype(BF16)
            pq_ref[1, r, sl] = pq[:, FNET_GROUP_DIM:].astype(BF16)

    _run_interleaved((norm_in, decay_rank, project, decay_factors, intra_chunk, channel_dft),
                     tiles)


def _mixin_call(x2d, nw, w_in_bf, wdec, bdec, cs, tri_f, tri_b, rows_per_batch, tm, chunk,
                cast=(), gate_t=None, gate_row0=0, lead=0):
    m = x2d.shape[0] + lead
    d_model = w_in_bf.shape[0]
    nt = rows_per_batch // tm
    nb = m // rows_per_batch
    steps = m // tm
    assert not lead or (steps == 1 and lead % 8 == 0)
    slab = lambda w: pl.BlockSpec((w.shape[0] // steps, w.shape[1]), lambda i: (i, 0))
    assert all(w.shape[0] % (16 * steps) == 0 for w in cast)
    gate_specs, gate_out_spec, gate_out_shape = [], (), ()
    if gate_t is not None:
        feat = 2 * D_MODEL // steps
        assert feat % LANE == 0 and feat % GATE_SUB == 0 and gate_row0 % GATE_SUB == 0
        assert gate_t.shape == (gate_row0 + 2 * D_MODEL, d_model)
        nsub, blk0 = feat // GATE_SUB, gate_row0 // GATE_SUB
        gate_specs = [pl.BlockSpec((GATE_SUB, d_model), lambda i, j=j: (blk0 + i * nsub + j, 0))
                      for j in range(nsub)]
        gate_out_spec = (pl.BlockSpec((d_model, feat), lambda i: (0, i)),)
        gate_out_shape = (jax.ShapeDtypeStruct((d_model, 2 * D_MODEL), BF16),)
    row = lambda n: pl.BlockSpec((tm, n), lambda i: (i, 0))
    pq_spec = pl.BlockSpec((2, None, tm, FNET_WIDTH), lambda i: (0, i // nt, i % nt, 0))
    wide = jax.ShapeDtypeStruct((2 * GLA_HEADS, m, HEAD_DK), BF16)
    heads = lambda nh, width: pl.BlockSpec((nh, tm, width), lambda i: (0, i, 0))
    pq_shape = jax.ShapeDtypeStruct((2, nb, rows_per_batch, FNET_WIDTH), BF16)
    consts = [*wdec, *bdec, cs,
              jnp.asarray(tri_f).astype(BF16), jnp.asarray(tri_b).astype(BF16),
              jnp.asarray(tri_f), jnp.asarray(tri_b)]
    return pl.pallas_call(
        functools.partial(_mixin_kernel, chunk=chunk, n_cast=len(cast), n_gate=len(gate_specs),
                          lead=lead),
        grid=(steps,),
        in_specs=[_resident(x2d.shape) if lead else row(D_MODEL), _resident(nw.shape),
                  _window((d_model, _OFF_QKV_END), (0, 0)),
                  _window((d_model, W_WINDOW), (0, 3))]
                 + [_resident(c.shape) for c in consts] + [slab(w) for w in cast] + gate_specs,
        out_specs=(heads(GLA_HEADS, HEAD_DV), row(GLA_DV),
                   heads(2 * GLA_HEADS, HEAD_DK), heads(2 * GLA_HEADS, HEAD_DK),
                   pl.BlockSpec((tm // SUB, 2 * GLA_DK), lambda i: (i, 0)), pq_spec)
                  + tuple(slab(w) for w in cast) + gate_out_spec,
        out_shape=(jax.ShapeDtypeStruct((GLA_HEADS, m, HEAD_DV), BF16),
                   jax.ShapeDtypeStruct((m, GLA_DV), BF16),
                   wide, wide, jax.ShapeDtypeStruct((m // SUB, 2 * GLA_DK), F32), pq_shape)
                  + tuple(jax.ShapeDtypeStruct(w.shape, BF16) for w in cast) + gate_out_shape,
        scratch_shapes=[pltpu.VMEM((tm, D_MODEL), F32)] if lead else [],
        compiler_params=pltpu.CompilerParams(
            dimension_semantics=("arbitrary",), vmem_limit_bytes=V7X_VMEM_LIMIT),
        name="mixin",
    )(x2d, nw, w_in_bf, w_in_bf, *consts, *cast, *([gate_t] * len(gate_specs)))


def _transpose_cast_kernel(x_ref, o_ref):
    o_ref[...] = x_ref[...].T.astype(BF16)


def _transpose_cast_call(w_t, n_rows, block):
    d_model = w_t.shape[1]
    assert n_rows % block == 0 and n_rows <= w_t.shape[0]
    return pl.pallas_call(
        _transpose_cast_kernel,
        grid=(n_rows // block,),
        in_specs=[pl.BlockSpec((block, d_model), lambda i: (i, 0))],
        out_specs=pl.BlockSpec((d_model, block), lambda i: (0, i)),
        out_shape=jax.ShapeDtypeStruct((d_model, n_rows), BF16),
        compiler_params=pltpu.CompilerParams(
            dimension_semantics=("arbitrary",), vmem_limit_bytes=V7X_VMEM_LIMIT),
        name="w_in_prep",
    )(w_t)


SCAN_G = 512
SCAN_RB = 4096


def _scan_group(dirs):
    s, n = SUB, SCAN_G // SUB
    ahead = 2
    dec, st, inter = [], [], [{}, {}]
    for qe_ref, ke_ref, v_ref, bend_ref, st_ref, row0, chunk0, backward in dirs:
        dec_rows = jnp.tile(jnp.exp2(bend_ref[chunk0:chunk0 + n, :]), (HEAD_DK // n, 1))
        dec.append(dec_rows.T)
        st.append(st_ref[...])

    def order(step, backward):
        return n - 1 - step if backward else step

    def increment(d, step):
        _, ke_ref, v_ref, _, _, row0, _, backward = dirs[d]
        i = order(step, backward)
        sub = slice(row0 + i * s, row0 + (i + 1) * s)
        return lax.dot_general(ke_ref[sub, :], v_ref[sub, :], _TN,
                               preferred_element_type=F32)

    inc = {(d, step): increment(d, step) for step in range(ahead) for d in range(2)}
    for step in range(n):
        for d, (qe_ref, _, _, _, _, row0, _, backward) in enumerate(dirs):
            i = order(step, backward)
            sub = slice(row0 + i * s, row0 + (i + 1) * s)
            inter[d][i] = jnp.dot(qe_ref[sub, :], st[d].astype(BF16),
                                  preferred_element_type=F32)
            if step + ahead < n:
                inc[d, step + ahead] = increment(d, step + ahead)
            decay = jnp.broadcast_to(dec[d][:, i:i + 1], (HEAD_DK, HEAD_DV))
            st[d] = st[d] * decay + inc.pop((d, step))
    for d, spec in enumerate(dirs):
        spec[4][...] = st[d]
    return [jnp.concatenate([inter[d][i] for i in range(n)], axis=0) for d in range(2)]


def _gla_scan_kernel(qef_ref, kef_ref, vf_ref, bendf_ref,
                     qeb_ref, keb_ref, vb_ref, bendb_ref, kem_ref, vm_ref,
                     of_ref, ob_ref, stf_ref, stb_ref):
    rb = qef_ref.shape[0]
    ng = rb // SCAN_G

    @pl.when(pl.program_id(2) == 0)
    def _():
        stf_ref[...] = lax.dot_general(kem_ref[...], vm_ref[...], _TN,
                                       preferred_element_type=F32)
        stb_ref[...] = jnp.zeros_like(stb_ref)

    for j in range(ng):
        jb = ng - 1 - j
        cpg = SCAN_G // SUB
        in_f, in_b = _scan_group([
            (qef_ref, kef_ref, vf_ref, bendf_ref, stf_ref, j * SCAN_G, j * cpg, False),
            (qeb_ref, keb_ref, vb_ref, bendb_ref, stb_ref, jb * SCAN_G, jb * cpg, True)])
        of_ref[j * SCAN_G:(j + 1) * SCAN_G, :] = in_f.astype(BF16)
        ob_ref[jb * SCAN_G:(jb + 1) * SCAN_G, :] = in_b.astype(BF16)


def _gla_scan_call(qe, ke, v, bend, kem, vm, seq, batch):
    h, rb = GLA_HEADS, SCAN_RB
    nt = seq // rb
    fwd = lambda b, hh, t: b * nt + t
    bwd = lambda b, hh, t: b * nt + nt - 1 - t
    cb = rb // SUB

    def dk_blk(rowfn, dircol, rows):
        return pl.BlockSpec((rows, HEAD_DK), lambda b, hh, t: (rowfn(b, hh, t), dircol + hh))

    def dv_blk(rowfn):
        return pl.BlockSpec((rb, HEAD_DV), lambda b, hh, t: (rowfn(b, hh, t), hh))

    def head_blk(rowfn, head0, rows, width):
        return pl.BlockSpec((None, rows, width), lambda b, hh, t: (head0 + hh, rowfn(b, hh, t), 0))

    first = lambda b, hh, t: 0
    in_specs = [
        head_blk(fwd, 0, rb, HEAD_DK), head_blk(fwd, 0, rb, HEAD_DK), head_blk(fwd, 0, rb, HEAD_DV),
        dk_blk(fwd, 0, cb),
        head_blk(bwd, h, rb, HEAD_DK), head_blk(bwd, h, rb, HEAD_DK), head_blk(bwd, 0, rb, HEAD_DV),
        dk_blk(bwd, h, cb),
        head_blk(first, 0, SUB, HEAD_DK), head_blk(first, 0, SUB, HEAD_DV),
    ]
    out = jax.ShapeDtypeStruct((batch * seq, GLA_DV), BF16)
    return pl.pallas_call(
        _gla_scan_kernel,
        grid=(batch, h, nt),
        in_specs=in_specs,
        out_specs=(dv_blk(fwd), dv_blk(bwd)),
        out_shape=(out, out),
        scratch_shapes=[pltpu.VMEM((HEAD_DK, HEAD_DV), F32), pltpu.VMEM((HEAD_DK, HEAD_DV), F32)],
        compiler_params=pltpu.CompilerParams(
            dimension_semantics=("arbitrary", "arbitrary", "arbitrary"),
            vmem_limit_bytes=V7X_VMEM_LIMIT),
        name="gla_scan",
    )(qe, ke, v, bend, qe, ke, v, bend, kem, vm)


DFT_TILE = 256


def _fold_kernel(x_ref, z_ref, j_ref, o_ref, *, m_last):
    t = DFT_TILE
    n_real = x_ref.shape[1] // t
    sign = jnp.where(pl.program_id(0) == 0, 1.0, -1.0)
    real = lambda i: x_ref[0, i * t:(i + 1) * t, :]
    lead = jnp.concatenate([jnp.zeros((t - z_ref.shape[1], z_ref.shape[2]), BF16), z_ref[0]],
                           axis=0)
    for i in range(o_ref.shape[1] // t):
        direct = (lead if i == 0 else real(i - 1)).astype(F32)
        mirror = (jnp.dot(j_ref[i, :, :t], real(min(n_real - i, n_real - 1)),
                          preferred_element_type=F32)
                  + jnp.dot(j_ref[i, :, t:], real(n_real - 1 - i), preferred_element_type=F32))
        m = i * t + lax.broadcasted_iota(jnp.int32, (t, 1), 0)
        folded = jnp.where(m <= m_last, direct + sign * mirror, 0.0)
        o_ref[0, i * t:(i + 1) * t, :] = folded.astype(BF16)


def _dft_kernel(peqo_ref, dc_ref, ds_ref, rowc_ref, rows_ref, u_ref, w_ref, *, scale):
    t = DFT_TILE
    acc_a = acc_b = None
    for c in range(peqo_ref.shape[1] // t):
        sl = slice(c * t, (c + 1) * t)
        rc, rs = rowc_ref[0, 0:1, sl], rows_ref[0, 0:1, sl]
        dc, ds = dc_ref[:, sl], ds_ref[:, sl]
        tc = (dc * rc - ds * rs).astype(BF16)
        ts = (ds * rc + dc * rs).astype(BF16)
        a = jnp.dot(tc, peqo_ref[0, sl, :], preferred_element_type=F32)
        b = jnp.dot(ts, peqo_ref[1, sl, :], preferred_element_type=F32)
        acc_a = a if acc_a is None else acc_a + a
        acc_b = b if acc_b is None else acc_b + b
    u_ref[...] = ((acc_a - acc_b) * scale).astype(BF16)
    w_ref[...] = ((acc_a + acc_b) * scale).astype(BF16)


def _unfold_kernel(u_ref, w_ref, j_ref, y_ref, *, r_last, i_mixed):
    t = DFT_TILE
    tile = lambda ref, i: ref[i * t:(i + 1) * t, :]
    n_real = y_ref.shape[0] // t
    for i in range(n_real):
        if i < i_mixed:
            y_ref[i * t:(i + 1) * t, :] = tile(u_ref, i + 1)
            continue
        j = 0 if i == i_mixed else 1
        mirror = jnp.dot(j_ref[j, :, t:], tile(w_ref, n_real - 1 - i), preferred_element_type=F32)
        if (n_real - i + 1) * t <= w_ref.shape[0]:
            mirror += jnp.dot(j_ref[j, :, :t], tile(w_ref, n_real - i),
                              preferred_element_type=F32)
        mirror = mirror.astype(BF16)
        if i == i_mixed:
            r = i * t + lax.broadcasted_iota(jnp.int32, (t, 1), 0)
            mirror = jnp.where(r <= r_last, tile(u_ref, i + 1), mirror)
        y_ref[i * t:(i + 1) * t, :] = mirror


@functools.lru_cache(maxsize=None)
def _dft_constants(seq, n_meta):
    t = DFT_TILE
    off = t - n_meta
    big_l = seq + n_meta
    half = big_l // 2
    n_real = seq // t
    nf = -(-(half + 1 + off) // t)
    hp = nf * t
    jf = np.zeros((nf, t, 2 * t), np.float32)
    for i in range(nf):
        a_idx, b_idx = min(n_real - i, n_real - 1), n_real - 1 - i
        for j in range(t):
            pos = i * t + j - off
            if 1 <= pos <= half - 1:
                tau, rho = divmod(big_l - pos - n_meta, t)
                assert tau in (a_idx, b_idx)
                jf[i, j, (t if tau == b_idx else 0) + rho] = 1.0
    r_last = half - n_meta
    i_mixed = r_last // t
    ju = np.zeros((3, t, 2 * t), np.float32)
    for i in range(i_mixed, n_real):
        a_idx, b_idx = n_real - i, n_real - 1 - i
        blk = np.zeros((t, 2 * t), np.float32)
        for j in range(t):
            r = i * t + j
            if r > r_last:
                tau, rho = divmod(big_l - (r + n_meta) + off, t)
                assert tau in (a_idx, b_idx)
                blk[j, (t if tau == b_idx else 0) + rho] = 1.0
        var = 1 if i == i_mixed else 2
        assert var == 1 or not ju[2].any() or (ju[2] == blk).all()
        ju[var] = blk
    ang = lambda prod: 2.0 * np.pi * (prod % big_l).astype(np.float64) / big_l
    pos = np.arange(hp, dtype=np.int64) - off
    d = ang(np.arange(t, dtype=np.int64)[:, None] * pos[None, :])
    k0 = t * np.arange(nf, dtype=np.int64) - off
    row = np.broadcast_to(ang(k0[:, None] * pos[None, :])[:, None, :], (nf, 8, hp))
    f = lambda a: np.ascontiguousarray(a, dtype=np.float32)
    return dict(jf=jf, ju=ju, dc=f(np.cos(d)), ds=f(np.sin(d)), rowc=f(np.cos(row)),
                rows=f(np.sin(row)), nf=nf, hp=hp, off=off, n_real=n_real,
                m_last=half + off, r_last=r_last, i_mixed=i_mixed)


def _seq_dft(pq, pqm, seq, n_meta):
    c = _dft_constants(seq, n_meta)
    t, nf, hp, ncb = DFT_TILE, c["nf"], c["hp"], pq.shape[1]
    ncol = ncb * FNET_WIDTH
    assert pq.shape[2:] == (seq, FNET_WIDTH)
    assert c["off"] == t - n_meta and pqm.shape[2] == FNET_WIDTH and pqm.shape[1] <= t
    col_blk = lambda rows: pl.BlockSpec((1, rows, FNET_WIDTH), lambda g, b: (g, 0, b))
    slab_blk = pl.BlockSpec((1, None, seq, FNET_WIDTH), lambda g, b: (g, b, 0, 0))
    meta_blk = pl.BlockSpec((1, pqm.shape[1], FNET_WIDTH), lambda g, b: (g, 0, 0))
    peqo = pl.pallas_call(
        functools.partial(_fold_kernel, m_last=c["m_last"]),
        grid=(2, ncb),
        in_specs=[slab_blk, meta_blk, _resident(c["jf"].shape)],
        out_specs=col_blk(hp),
        out_shape=jax.ShapeDtypeStruct((2, hp, ncol), BF16),
        compiler_params=pltpu.CompilerParams(
            dimension_semantics=("arbitrary", "arbitrary"), vmem_limit_bytes=V7X_VMEM_LIMIT),
        name="dft_fold",
    )(pq, pqm, jnp.asarray(c["jf"]).astype(BF16))

    scale = float((seq + n_meta) * FNET_GROUP_DIM) ** -0.5
    tile = pl.BlockSpec((t, ncol), lambda i: (i, 0))
    rowspec = pl.BlockSpec((1, 8, hp), lambda i: (i, 0, 0))
    u, w = pl.pallas_call(
        functools.partial(_dft_kernel, scale=scale),
        grid=(nf,),
        in_specs=[_resident((2, hp, ncol)), _resident((t, hp)), _resident((t, hp)),
                  rowspec, rowspec],
        out_specs=(tile, tile),
        out_shape=(jax.ShapeDtypeStruct((hp, ncol), BF16),) * 2,
        compiler_params=pltpu.CompilerParams(
            dimension_semantics=("arbitrary",), vmem_limit_bytes=V7X_VMEM_LIMIT),
        name="seq_dft",
    )(peqo, jnp.asarray(c["dc"]), jnp.asarray(c["ds"]), jnp.asarray(c["rowc"]),
      jnp.asarray(c["rows"]))

    col = lambda rows: pl.BlockSpec((rows, FNET_WIDTH), lambda b: (0, b))
    return pl.pallas_call(
        functools.partial(_unfold_kernel, r_last=c["r_last"], i_mixed=c["i_mixed"]),
        grid=(ncb,),
        in_specs=[col(hp), col(hp), _resident((2, t, 2 * t))],
        out_specs=col(seq),
        out_shape=jax.ShapeDtypeStruct((seq, ncol), BF16),
        compiler_params=pltpu.CompilerParams(
            dimension_semantics=("arbitrary",), vmem_limit_bytes=V7X_VMEM_LIMIT),
        name="dft_unfold",
    )(u, w, jnp.asarray(c["ju"][1:]).astype(BF16))


FFN_CHUNK = 1024
TAIL_SPLIT = 2
TAIL_TM = 512


def _tail_kernel(x_ref, ol_ref, of_ref, ob_ref, y_ref, nmix_ref, gn_ref, nffn_ref, nfin_ref,
                 wg_ref, wm_ref, wo_ref, wf_ref, wout_ref, w1_ref, w2_ref, o_ref):
    tm = x_ref.shape[0]
    sub = tm // TAIL_SPLIT
    tiles = [(dict(), slice(i * sub, (i + 1) * sub)) for i in range(TAIL_SPLIT)]
    dot = functools.partial(jnp.dot, preferred_element_type=F32)
    nchunk = D_FF // FFN_CHUNK

    def norm_in(s, r):
        s["x"] = x_ref[r, :]
        s["xn"] = _rms(s["x"], nmix_ref[...]).astype(BF16)

    def gates(s, r):
        s["g"] = [dot(s["xn"], wg_ref[:, h * HEAD_DV:(h + 1) * HEAD_DV])
                  for h in range(GLA_HEADS)]
        s["ga"] = dot(s["xn"], wm_ref[:, 0:D_MODEL])
        s["gf"] = dot(s["xn"], wm_ref[:, D_MODEL:2 * D_MODEL])
        s["y_fnet"] = dot(y_ref[r, :], wf_ref[...])

    def gla_out(s, r):
        y_gla = None
        for h in range(GLA_HEADS):
            hv = slice(h * HEAD_DV, (h + 1) * HEAD_DV)
            o = (ol_ref[r, hv].astype(F32) + of_ref[r, hv].astype(F32)
                 + ob_ref[r, hv].astype(F32))
            g = s["g"][h]
            og = (_rms(o, gn_ref[...]) * (g * _sigmoid(g))).astype(BF16)
            part = dot(og, wo_ref[hv, :])
            y_gla = part if y_gla is None else y_gla + part
        s["y_gla"] = y_gla

    def merge(s, r):
        merged = _sigmoid(s["ga"]) * s["y_gla"] + _sigmoid(s["gf"]) * s["y_fnet"]
        s["h1"] = s["x"] + dot(merged.astype(BF16), wout_ref[...])
        s["u"] = _rms(s["h1"], nffn_ref[...]).astype(BF16)
        s["acc"] = s["h1"]

    def ffn_up(c):
        def stage(s, r):
            a = jnp.maximum(dot(s["u"], w1_ref[:, c * FFN_CHUNK:(c + 1) * FFN_CHUNK]), 0.0)
            s["a", c] = (a * a).astype(BF16)
        return stage

    def ffn_down(c):
        def stage(s, r):
            s["acc"] = s["acc"] + dot(s.pop(("a", c)),
                                      w2_ref[c * FFN_CHUNK:(c + 1) * FFN_CHUNK, :])
        return stage

    def norm_out(s, r):
        o_ref[r, :] = _rms(s["acc"], nfin_ref[...])

    stages = [norm_in, gates, gla_out, merge, ffn_up(0)]
    for c in range(nchunk):
        if c + 1 < nchunk:
            stages.append(ffn_up(c + 1))
        stages.append(ffn_down(c))
    stages.append(norm_out)
    _run_interleaved(stages, tiles)


def _tail_call(x2d, ol, o_f, o_b, y, nmix, gn, nffn, nfin, w_in_bf, w_merge, wo, wf, wout, w1,
               w2, seq, tm):
    m = x2d.shape[0]
    nt = seq // tm
    row = lambda n: pl.BlockSpec((tm, n), lambda i: (i, 0))
    norms = [nmix, gn, nffn, nfin]
    consts = [w_merge, wo, wf, wout, w1, w2]
    return pl.pallas_call(
        _tail_kernel,
        grid=(m // tm,),
        in_specs=[row(D_MODEL), row(GLA_DV), row(GLA_DV), row(GLA_DV),
                  pl.BlockSpec((tm, FNET_WIDTH), lambda i: (i % nt, i // nt))]
                 + [_resident(c.shape) for c in norms]
                 + [_window((w_in_bf.shape[0], W_WINDOW), (0, 2))]
                 + [_resident(c.shape) for c in consts],
        out_specs=row(D_MODEL),
        out_shape=jax.ShapeDtypeStruct((m, D_MODEL), F32),
        compiler_params=pltpu.CompilerParams(
            dimension_semantics=("arbitrary",), vmem_limit_bytes=V7X_VMEM_LIMIT),
        name="tail",
    )(x2d, ol, o_f, o_b, y, *norms, w_in_bf, *consts)


@functools.lru_cache(maxsize=None)
def _gla_constants(rows):
    idx = np.arange(rows)
    same = (idx[:, None] // SUB) == (idx[None, :] // SUB)
    tri_f = (same & (idx[None, :] <= idx[:, None])).astype(np.float32)
    return tri_f, np.ascontiguousarray(tri_f.T)


@functools.lru_cache(maxsize=None)
def _channel_dft():
    d = np.arange(FNET_GROUP_DIM, dtype=np.int64)
    ang = 2.0 * np.pi * ((d[:, None] * d[None, :]) % FNET_GROUP_DIM) / FNET_GROUP_DIM
    cs = np.zeros((2 * FNET_GROUP_DIM, 2 * FNET_GROUP_DIM), np.float32)
    cs[F_LEAD:F_LEAD + FNET_GROUP_DIM] = np.concatenate([np.cos(ang), np.sin(ang)], axis=1)
    return cs


def kernel(x, meta_tokens, norm_mix, w_in, w_decay_fwd, b_decay_fwd, w_decay_bwd, b_decay_bwd,
           gla_norm, w_o_gla, w_fnet, w_out, norm_ffn, w_ff1, w_ff2, norm_final):
    batch, seq, d = x.shape
    assert d == D_MODEL and meta_tokens.shape == (N_META, D_MODEL)
    assert w_in.shape[0] == 1 and seq % (2 * SCAN_RB) == 0 and seq % DFT_TILE == 0
    gate0 = 2 * GLA_DK + 2 * GLA_DV + 2 * DECAY_RANK + FNET_WIDTH
    assert w_in.shape[2] == gate0 + 2 * D_MODEL and gate0 <= 4 * W_WINDOW
    w_in_t = jnp.swapaxes(w_in[0], 0, 1)
    wi = _transpose_cast_call(w_in_t, 4 * W_WINDOW, W_WINDOW)
    wdec = (w_decay_fwd[0], w_decay_bwd[0])
    bdec = (b_decay_fwd, b_decay_bwd)
    cs = jnp.asarray(_channel_dft())
    nw_mix = norm_mix[0][None, :]

    x2d = x.reshape(batch * seq, D_MODEL)
    tail_weights = (w_o_gla[0], w_fnet[0], w_out[0], w_ff1[0], w_ff2[0])
    v, ol, qe, ke, bend, pq, *tail_weights, w_merge = _mixin_call(
        x2d, nw_mix, wi, wdec, bdec, cs, *_gla_constants(GLA_C), seq, MIX_TM, GLA_C,
        cast=tail_weights, gate_t=w_in_t, gate_row0=gate0)
    vm, _, _, kem, _, pqm = _mixin_call(meta_tokens, nw_mix, wi, wdec, bdec, cs,
                                        *_gla_constants(SUB), SUB, SUB, SUB, lead=SUB - N_META)
    o_f, o_b = _gla_scan_call(qe, ke, v, bend, kem, vm, seq, batch)

    y = _seq_dft(pq, pqm[:, 0], seq, N_META)

    out = _tail_call(x2d, ol, o_f, o_b, y, nw_mix, gla_norm[0][None, :], norm_ffn[0][None, :],
                     norm_final[None, :], wi, w_merge, *tail_weights, seq, TAIL_TM)
    return out.reshape(batch, seq, D_MODEL)
```

```python
import functools

import numpy as np
import jax
import jax.numpy as jnp
from jax import lax
from jax.experimental import pallas as pl
from jax.experimental.pallas import tpu as pltpu

F32 = jnp.float32
BF16 = jnp.bfloat16

D_MODEL = 1024
N_META = 16
GLA_HEADS = 4
GLA_DK = 512
GLA_DV = 1024
HEAD_DK = GLA_DK // GLA_HEADS
HEAD_DV = GLA_DV // GLA_HEADS
DECAY_RANK = 16
GATE_TEMP = 16.0
SUB = 64
FNET_GROUPS = 4
FNET_WIDTH = 512
FNET_GROUP_DIM = FNET_WIDTH // FNET_GROUPS
D_FF = 4 * D_MODEL
EPS = 1e-6
LOG2E = 1.4426950408889634
LANE = 128

_OFF_Q, _OFF_K, _OFF_V, _OFF_QKV_END = 0, 512, 1024, 2048
W_WINDOW = 1024
R_PAD = 128
F_LEAD = 2 * DECAY_RANK
F_EXT = F_LEAD + FNET_WIDTH + (LANE - F_LEAD)

GLA_C = 4 * SUB
MIX_TM = 1024
GATE_SUB = 32

V7X_VMEM_LIMIT = 58 * 1024 * 1024


def _rms(x, w):
    ms = jnp.mean(x * x, axis=-1, keepdims=True)
    return x * lax.rsqrt(ms + EPS) * w


def _sigmoid(x):
    return 1.0 / (1.0 + jnp.exp(-x))


def _run_interleaved(stages, tiles):
    for stage in stages:
        for state, rows in tiles:
            stage(state, rows)


def _window(block, index):
    return pl.BlockSpec(block, lambda *_: index, pipeline_mode=pl.Buffered(1))


def _resident(shape):
    return _window(shape, (0,) * len(shape))


_NT = (((1,), (1,)), ((), ()))
_TN = (((0,), (0,)), ((), ()))


def _mixin_kernel(x_ref, nw_ref, wqkv_ref, wrf_ref, wdf_ref, wdb_ref, bdf_ref, bdb_ref, cs_ref,
                  trif_ref, trib_ref, maskf_ref, maskb_ref, *refs, chunk, n_cast, n_gate, lead):
    if lead:
        *refs, tile_ref = refs
        tile_ref[...] = jnp.concatenate([jnp.zeros((lead, x_ref.shape[1]), F32), x_ref[...]],
                                        axis=0)
        x_ref = tile_ref
    n_in = n_cast + n_gate
    cast_in, gate_in = refs[:n_cast], refs[n_cast:n_in]
    v_ref, ol_ref, qe_ref, ke_ref, bend_ref, pq_ref = refs[n_in:n_in + 6]
    cast_out = refs[n_in + 6:n_in + 6 + n_cast]
    _mixin_body(x_ref, nw_ref, wqkv_ref, wrf_ref, (wdf_ref, wdb_ref), (bdf_ref, bdb_ref), cs_ref,
                trif_ref, trib_ref,
                maskf_ref, maskb_ref, v_ref, ol_ref, qe_ref, ke_ref, bend_ref, pq_ref, chunk=chunk)
    for src, dst in zip(cast_in, cast_out):
        dst[...] = src[...].astype(BF16)
    if n_gate:
        feat_major = jnp.concatenate([r[...] for r in gate_in], axis=0)
        refs[-1][...] = feat_major.T.astype(BF16)


def _mixin_body(x_ref, nw_ref, wqkv_ref, wrf_ref, wdec_refs, bdec_refs, cs_ref,
                trif_ref, trib_ref, maskf_ref, maskb_ref,
                v_ref, ol_ref, qe_ref, ke_ref, bend_ref, pq_ref, *, chunk):
    tm = x_ref.shape[0]
    n, s = chunk // SUB, SUB
    tiles = [(dict(), slice(i * chunk, (i + 1) * chunk)) for i in range(tm // chunk)]
    dot = functools.partial(jnp.dot, preferred_element_type=F32)
    cs = cs_ref[...].astype(BF16)
    tris = (trif_ref[...], trib_ref[...])
    masks = (maskf_ref[0:s, 0:s] != 0.0, maskb_ref[0:s, 0:s] != 0.0)
    half = GLA_DK // 2
    zeros = lambda rows: jnp.zeros((rows, GLA_DK), F32)
    wdec = jnp.concatenate([
        jnp.concatenate([wdec_refs[0][...], zeros(R_PAD - DECAY_RANK)], axis=0),
        jnp.concatenate([zeros(DECAY_RANK), wdec_refs[1][...], zeros(R_PAD - 2 * DECAY_RANK)],
                        axis=0)], axis=1).astype(BF16)
    bdec = jnp.concatenate([bdec_refs[0][...], bdec_refs[1][...]], axis=1)

    def by_head(ref, head0, r, val, width):
        for i in range(val.shape[1] // width):
            ref[head0 + i, r, :] = val[:, i * width:(i + 1) * width]

    def norm_in(st, r):
        st["xn"] = _rms(x_ref[r, :], nw_ref[...]).astype(BF16)

    def decay_rank(st, r):
        st["f"] = dot(st["xn"], wrf_ref[:, 0:F_EXT]).astype(BF16)
        st["r"] = st["f"][:, 0:R_PAD]

    def log_decay(st, j):
        cols = slice(j * half, (j + 1) * half)
        x = dot(st["r"], wdec[:, cols]) + bdec[:, cols]
        la = (jnp.minimum(x, 0.0) - jnp.log(1.0 + jnp.exp(-jnp.abs(x)))) * (LOG2E / GATE_TEMP)
        hi = la.astype(BF16)
        st["la", j] = (hi, (la - hi.astype(F32)).astype(BF16))

    def project(st, r):
        mm = lambda lo, hi: dot(st["xn"], wqkv_ref[:, lo:hi])
        vmid = (_OFF_V + _OFF_QKV_END) // 2
        log_decay(st, 0)
        st["q"] = mm(_OFF_Q, _OFF_K) * (HEAD_DK ** -0.5)
        log_decay(st, 1)
        st["k"] = mm(_OFF_K, _OFF_V)
        log_decay(st, 2)
        by_head(v_ref, 0, r, mm(_OFF_V, vmid).astype(BF16), HEAD_DV)
        log_decay(st, 3)
        by_head(v_ref, GLA_HEADS // 2, r, mm(vmid, _OFF_QKV_END).astype(BF16), HEAD_DV)

    def decay_factors(st, r):
        c0 = r.start // s
        for j in range(4):
            d, qcols = j // 2, slice((j % 2) * half, (j % 2 + 1) * half)
            cols = slice(j * half, (j + 1) * half)
            hi, lo = st.pop(("la", j))
            b = (dot(tris[d], hi) + dot(tris[d], lo)).reshape(n, s, half)
            if d == 0:
                b_mid, b_end = b[:, s // 2 - 1:s // 2, :], b[:, s - 1:s, :]
            else:
                b_mid, b_end = b[:, s // 2:s // 2 + 1, :], b[:, 0:1, :]
            qd = st["q"][:, qcols].reshape(n, s, half) * jnp.exp2(b - b_mid)
            kd = st["k"][:, qcols].reshape(n, s, half) * jnp.exp2(b_mid - b)
            qe = qd * jnp.exp2(b_mid)
            ke = kd * jnp.exp2(b_end - b_mid)
            st["qd", d, j % 2] = qd.reshape(chunk, half).astype(BF16)
            st["kd", d, j % 2] = kd.reshape(chunk, half).astype(BF16)
            by_head(qe_ref, cols.start // HEAD_DK, r, qe.reshape(chunk, half).astype(BF16), HEAD_DK)
            by_head(ke_ref, cols.start // HEAD_DK, r, ke.reshape(chunk, half).astype(BF16), HEAD_DK)
            bend_ref[c0:c0 + n, cols] = b_end.reshape(n, half)

    def intra_chunk(st, r):
        for h in range(GLA_HEADS):
            hk = slice((h % 2) * HEAD_DK, (h % 2 + 1) * HEAD_DK)
            hv = slice(h * HEAD_DV, (h + 1) * HEAD_DV)
            sc = [[lax.dot_general(st["qd", d, h // 2][c * s:(c + 1) * s, hk],
                                   st["kd", d, h // 2][c * s:(c + 1) * s, hk], _NT,
                                   preferred_element_type=F32) for d in range(2)]
                  for c in range(n)]
            for c, (sf, sb) in enumerate(sc):
                rows = slice(r.start + c * s, r.start + (c + 1) * s)
                both = (jnp.where(masks[0], sf, 0.0) + jnp.where(masks[1], sb, 0.0)).astype(BF16)
                ol_ref[rows, hv] = dot(both, v_ref[h, rows, :]).astype(BF16)

    def channel_dft(st, r):
        for grp in range(FNET_GROUPS):
            sl = slice(grp * FNET_GROUP_DIM, (grp + 1) * FNET_GROUP_DIM)
            pq = dot(st["f"][:, grp * FNET_GROUP_DIM:(grp + 2) * FNET_GROUP_DIM], cs)
            pq_ref[0, r, sl] = pq[:, :FNET_GROUP_DIM].astype(BF16)
            pq_ref[1, r, sl] = pq[:, FNET_GROUP_DIM:].astype(BF16)

    _run_interleaved((norm_in, decay_rank, project, decay_factors, intra_chunk, channel_dft),
                     tiles)


def _mixin_call(x2d, nw, w_in_bf, wdec, bdec, cs, tri_f, tri_b, rows_per_batch, tm, chunk,
                cast=(), gate_t=None, gate_row0=0, lead=0):
    m = x2d.shape[0] + lead
    d_model = w_in_bf.shape[0]
    nt = rows_per_batch // tm
    nb = m // rows_per_batch
    steps = m // tm
    assert not lead or (steps == 1 and lead % 8 == 0)
    slab = lambda w: pl.BlockSpec((w.shape[0] // steps, w.shape[1]), lambda i: (i, 0))
    assert all(w.shape[0] % (16 * steps) == 0 for w in cast)
    gate_specs, gate_out_spec, gate_out_shape = [], (), ()
    if gate_t is not None:
        feat = 2 * D_MODEL // steps
        assert feat % LANE == 0 and feat % GATE_SUB == 0 and gate_row0 % GATE_SUB == 0
        assert gate_t.shape == (gate_row0 + 2 * D_MODEL, d_model)
        nsub, blk0 = feat // GATE_SUB, gate_row0 // GATE_SUB
        gate_specs = [pl.BlockSpec((GATE_SUB, d_model), lambda i, j=j: (blk0 + i * nsub + j, 0))
                      for j in range(nsub)]
        gate_out_spec = (pl.BlockSpec((d_model, feat), lambda i: (0, i)),)
        gate_out_shape = (jax.ShapeDtypeStruct((d_model, 2 * D_MODEL), BF16),)
    row = lambda n: pl.BlockSpec((tm, n), lambda i: (i, 0))
    pq_spec = pl.BlockSpec((2, None, tm, FNET_WIDTH), lambda i: (0, i // nt, i % nt, 0))
    wide = jax.ShapeDtypeStruct((2 * GLA_HEADS, m, HEAD_DK), BF16)
    heads = lambda nh, width: pl.BlockSpec((nh, tm, width), lambda i: (0, i, 0))
    pq_shape = jax.ShapeDtypeStruct((2, nb, rows_per_batch, FNET_WIDTH), BF16)
    consts = [*wdec, *bdec, cs,
              jnp.asarray(tri_f).astype(BF16), jnp.asarray(tri_b).astype(BF16),
              jnp.asarray(tri_f), jnp.asarray(tri_b)]
    return pl.pallas_call(
        functools.partial(_mixin_kernel, chunk=chunk, n_cast=len(cast), n_gate=len(gate_specs),
                          lead=lead),
        grid=(steps,),
        in_specs=[_resident(x2d.shape) if lead else row(D_MODEL), _resident(nw.shape),
                  _window((d_model, _OFF_QKV_END), (0, 0)),
                  _window((d_model, W_WINDOW), (0, 3))]
                 + [_resident(c.shape) for c in consts] + [slab(w) for w in cast] + gate_specs,
        out_specs=(heads(GLA_HEADS, HEAD_DV), row(GLA_DV),
                   heads(2 * GLA_HEADS, HEAD_DK), heads(2 * GLA_HEADS, HEAD_DK),
                   pl.BlockSpec((tm // SUB, 2 * GLA_DK), lambda i: (i, 0)), pq_spec)
                  + tuple(slab(w) for w in cast) + gate_out_spec,
        out_shape=(jax.ShapeDtypeStruct((GLA_HEADS, m, HEAD_DV), BF16),
                   jax.ShapeDtypeStruct((m, GLA_DV), BF16),
                   wide, wide, jax.ShapeDtypeStruct((m // SUB, 2 * GLA_DK), F32), pq_shape)
                  + tuple(jax.ShapeDtypeStruct(w.shape, BF16) for w in cast) + gate_out_shape,
        scratch_shapes=[pltpu.VMEM((tm, D_MODEL), F32)] if lead else [],
        compiler_params=pltpu.CompilerParams(
            dimension_semantics=("arbitrary",), vmem_limit_bytes=V7X_VMEM_LIMIT),
        name="mixin",
    )(x2d, nw, w_in_bf, w_in_bf, *consts, *cast, *([gate_t] * len(gate_specs)))


def _transpose_cast_kernel(x_ref, o_ref):
    o_ref[...] = x_ref[...].T.astype(BF16)


def _transpose_cast_call(w_t, n_rows, block):
    d_model = w_t.shape[1]
    assert n_rows % block == 0 and n_rows <= w_t.shape[0]
    return pl.pallas_call(
        _transpose_cast_kernel,
        grid=(n_rows // block,),
        in_specs=[pl.BlockSpec((block, d_model), lambda i: (i, 0))],
        out_specs=pl.BlockSpec((d_model, block), lambda i: (0, i)),
        out_shape=jax.ShapeDtypeStruct((d_model, n_rows), BF16),
        compiler_params=pltpu.CompilerParams(
            dimension_semantics=("arbitrary",), vmem_limit_bytes=V7X_VMEM_LIMIT),
        name="w_in_prep",
    )(w_t)


SCAN_G = 1024
SCAN_RB = 4096


def _scan_group(dirs):
    s, n = SUB, SCAN_G // SUB
    ahead = 2
    dec, st, inter = [], [], [{}, {}]
    for qe_ref, ke_ref, v_ref, bend_ref, st_ref, row0, chunk0, backward in dirs:
        dec_rows = jnp.tile(jnp.exp2(bend_ref[chunk0:chunk0 + n, :]), (HEAD_DK // n, 1))
        dec.append(dec_rows.T)
        st.append(st_ref[...])

    def order(step, backward):
        return n - 1 - step if backward else step

    def increment(d, step):
        _, ke_ref, v_ref, _, _, row0, _, backward = dirs[d]
        i = order(step, backward)
        sub = slice(row0 + i * s, row0 + (i + 1) * s)
        return lax.dot_general(ke_ref[sub, :], v_ref[sub, :], _TN,
                               preferred_element_type=F32)

    inc = {(d, step): increment(d, step) for step in range(ahead) for d in range(2)}
    for step in range(n):
        for d, (qe_ref, _, _, _, _, row0, _, backward) in enumerate(dirs):
            i = order(step, backward)
            sub = slice(row0 + i * s, row0 + (i + 1) * s)
            inter[d][i] = jnp.dot(qe_ref[sub, :], st[d].astype(BF16),
                                  preferred_element_type=F32)
            if step + ahead < n:
                inc[d, step + ahead] = increment(d, step + ahead)
            decay = jnp.broadcast_to(dec[d][:, i:i + 1], (HEAD_DK, HEAD_DV))
            st[d] = st[d] * decay + inc.pop((d, step))
    for d, spec in enumerate(dirs):
        spec[4][...] = st[d]
    return [jnp.concatenate([inter[d][i] for i in range(n)], axis=0) for d in range(2)]


def _gla_scan_kernel(qef_ref, kef_ref, vf_ref, bendf_ref,
                     qeb_ref, keb_ref, vb_ref, bendb_ref, kem_ref, vm_ref,
                     of_ref, ob_ref, stf_ref, stb_ref):
    rb = qef_ref.shape[0]
    ng = rb // SCAN_G

    @pl.when(pl.program_id(2) == 0)
    def _():
        stf_ref[...] = lax.dot_general(kem_ref[...], vm_ref[...], _TN,
                                       preferred_element_type=F32)
        stb_ref[...] = jnp.zeros_like(stb_ref)

    for j in range(ng):
        jb = ng - 1 - j
        cpg = SCAN_G // SUB
        in_f, in_b = _scan_group([
            (qef_ref, kef_ref, vf_ref, bendf_ref, stf_ref, j * SCAN_G, j * cpg, False),
            (qeb_ref, keb_ref, vb_ref, bendb_ref, stb_ref, jb * SCAN_G, jb * cpg, True)])
        of_ref[j * SCAN_G:(j + 1) * SCAN_G, :] = in_f.astype(BF16)
        ob_ref[jb * SCAN_G:(jb + 1) * SCAN_G, :] = in_b.astype(BF16)


def _gla_scan_call(qe, ke, v, bend, kem, vm, seq, batch):
    h, rb = GLA_HEADS, SCAN_RB
    nt = seq // rb
    fwd = lambda b, hh, t: b * nt + t
    bwd = lambda b, hh, t: b * nt + nt - 1 - t
    cb = rb // SUB

    def dk_blk(rowfn, dircol, rows):
        return pl.BlockSpec((rows, HEAD_DK), lambda b, hh, t: (rowfn(b, hh, t), dircol + hh))

    def dv_blk(rowfn):
        return pl.BlockSpec((rb, HEAD_DV), lambda b, hh, t: (rowfn(b, hh, t), hh))

    def head_blk(rowfn, head0, rows, width):
        return pl.BlockSpec((None, rows, width), lambda b, hh, t: (head0 + hh, rowfn(b, hh, t), 0))

    first = lambda b, hh, t: 0
    in_specs = [
        head_blk(fwd, 0, rb, HEAD_DK), head_blk(fwd, 0, rb, HEAD_DK), head_blk(fwd, 0, rb, HEAD_DV),
        dk_blk(fwd, 0, cb),
        head_blk(bwd, h, rb, HEAD_DK), head_blk(bwd, h, rb, HEAD_DK), head_blk(bwd, 0, rb, HEAD_DV),
        dk_blk(bwd, h, cb),
        head_blk(first, 0, SUB, HEAD_DK), head_blk(first, 0, SUB, HEAD_DV),
    ]
    out = jax.ShapeDtypeStruct((batch * seq, GLA_DV), BF16)
    return pl.pallas_call(
        _gla_scan_kernel,
        grid=(batch, h, nt),
        in_specs=in_specs,
        out_specs=(dv_blk(fwd), dv_blk(bwd)),
        out_shape=(out, out),
        scratch_shapes=[pltpu.VMEM((HEAD_DK, HEAD_DV), F32), pltpu.VMEM((HEAD_DK, HEAD_DV), F32)],
        compiler_params=pltpu.CompilerParams(
            dimension_semantics=("arbitrary", "arbitrary", "arbitrary"),
            vmem_limit_bytes=V7X_VMEM_LIMIT),
        name="gla_scan",
    )(qe, ke, v, bend, qe, ke, v, bend, kem, vm)


DFT_TILE = 256


def _fold_kernel(x_ref, z_ref, j_ref, o_ref, *, m_last):
    t = DFT_TILE
    n_real = x_ref.shape[1] // t
    sign = jnp.where(pl.program_id(0) == 0, 1.0, -1.0)
    real = lambda i: x_ref[0, i * t:(i + 1) * t, :]
    lead = jnp.concatenate([jnp.zeros((t - z_ref.shape[1], z_ref.shape[2]), BF16), z_ref[0]],
                           axis=0)
    for i in range(o_ref.shape[1] // t):
        direct = (lead if i == 0 else real(i - 1)).astype(F32)
        mirror = (jnp.dot(j_ref[i, :, :t], real(min(n_real - i, n_real - 1)),
                          preferred_element_type=F32)
                  + jnp.dot(j_ref[i, :, t:], real(n_real - 1 - i), preferred_element_type=F32))
        m = i * t + lax.broadcasted_iota(jnp.int32, (t, 1), 0)
        folded = jnp.where(m <= m_last, direct + sign * mirror, 0.0)
        o_ref[0, i * t:(i + 1) * t, :] = folded.astype(BF16)


def _dft_kernel(peqo_ref, dc_ref, ds_ref, rowc_ref, rows_ref, u_ref, w_ref, *, scale):
    t = DFT_TILE
    acc_a = acc_b = None
    for c in range(peqo_ref.shape[1] // t):
        sl = slice(c * t, (c + 1) * t)
        rc, rs = rowc_ref[0, 0:1, sl], rows_ref[0, 0:1, sl]
        dc, ds = dc_ref[:, sl], ds_ref[:, sl]
        tc = (dc * rc - ds * rs).astype(BF16)
        ts = (ds * rc + dc * rs).astype(BF16)
        a = jnp.dot(tc, peqo_ref[0, sl, :], preferred_element_type=F32)
        b = jnp.dot(ts, peqo_ref[1, sl, :], preferred_element_type=F32)
        acc_a = a if acc_a is None else acc_a + a
        acc_b = b if acc_b is None else acc_b + b
    u_ref[...] = ((acc_a - acc_b) * scale).astype(BF16)
    w_ref[...] = ((acc_a + acc_b) * scale).astype(BF16)


def _unfold_kernel(u_ref, w_ref, j_ref, y_ref, *, r_last, i_mixed):
    t = DFT_TILE
    tile = lambda ref, i: ref[i * t:(i + 1) * t, :]
    n_real = y_ref.shape[0] // t
    for i in range(n_real):
        if i < i_mixed:
            y_ref[i * t:(i + 1) * t, :] = tile(u_ref, i + 1)
            continue
        j = 0 if i == i_mixed else 1
        mirror = jnp.dot(j_ref[j, :, t:], tile(w_ref, n_real - 1 - i), preferred_element_type=F32)
        if (n_real - i + 1) * t <= w_ref.shape[0]:
            mirror += jnp.dot(j_ref[j, :, :t], tile(w_ref, n_real - i),
                              preferred_element_type=F32)
        mirror = mirror.astype(BF16)
        if i == i_mixed:
            r = i * t + lax.broadcasted_iota(jnp.int32, (t, 1), 0)
            mirror = jnp.where(r <= r_last, tile(u_ref, i + 1), mirror)
        y_ref[i * t:(i + 1) * t, :] = mirror


@functools.lru_cache(maxsize=None)
def _dft_constants(seq, n_meta):
    t = DFT_TILE
    off = t - n_meta
    big_l = seq + n_meta
    half = big_l // 2
    n_real = seq // t
    nf = -(-(half + 1 + off) // t)
    hp = nf * t
    jf = np.zeros((nf, t, 2 * t), np.float32)
    for i in range(nf):
        a_idx, b_idx = min(n_real - i, n_real - 1), n_real - 1 - i
        for j in range(t):
            pos = i * t + j - off
            if 1 <= pos <= half - 1:
                tau, rho = divmod(big_l - pos - n_meta, t)
                assert tau in (a_idx, b_idx)
                jf[i, j, (t if tau == b_idx else 0) + rho] = 1.0
    r_last = half - n_meta
    i_mixed = r_last // t
    ju = np.zeros((3, t, 2 * t), np.float32)
    for i in range(i_mixed, n_real):
        a_idx, b_idx = n_real - i, n_real - 1 - i
        blk = np.zeros((t, 2 * t), np.float32)
        for j in range(t):
            r = i * t + j
            if r > r_last:
                tau, rho = divmod(big_l - (r + n_meta) + off, t)
                assert tau in (a_idx, b_idx)
                blk[j, (t if tau == b_idx else 0) + rho] = 1.0
        var = 1 if i == i_mixed else 2
        assert var == 1 or not ju[2].any() or (ju[2] == blk).all()
        ju[var] = blk
    ang = lambda prod: 2.0 * np.pi * (prod % big_l).astype(np.float64) / big_l
    pos = np.arange(hp, dtype=np.int64) - off
    d = ang(np.arange(t, dtype=np.int64)[:, None] * pos[None, :])
    k0 = t * np.arange(nf, dtype=np.int64) - off
    row = np.broadcast_to(ang(k0[:, None] * pos[None, :])[:, None, :], (nf, 8, hp))
    f = lambda a: np.ascontiguousarray(a, dtype=np.float32)
    return dict(jf=jf, ju=ju, dc=f(np.cos(d)), ds=f(np.sin(d)), rowc=f(np.cos(row)),
                rows=f(np.sin(row)), nf=nf, hp=hp, off=off, n_real=n_real,
                m_last=half + off, r_last=r_last, i_mixed=i_mixed)


def _seq_dft(pq, pqm, seq, n_meta):
    c = _dft_constants(seq, n_meta)
    t, nf, hp, ncb = DFT_TILE, c["nf"], c["hp"], pq.shape[1]
    ncol = ncb * FNET_WIDTH
    assert pq.shape[2:] == (seq, FNET_WIDTH)
    assert c["off"] == t - n_meta and pqm.shape[2] == FNET_WIDTH and pqm.shape[1] <= t
    col_blk = lambda rows: pl.BlockSpec((1, rows, FNET_WIDTH), lambda g, b: (g, 0, b))
    slab_blk = pl.BlockSpec((1, None, seq, FNET_WIDTH), lambda g, b: (g, b, 0, 0))
    meta_blk = pl.BlockSpec((1, pqm.shape[1], FNET_WIDTH), lambda g, b: (g, 0, 0))
    peqo = pl.pallas_call(
        functools.partial(_fold_kernel, m_last=c["m_last"]),
        grid=(2, ncb),
        in_specs=[slab_blk, meta_blk, _resident(c["jf"].shape)],
        out_specs=col_blk(hp),
        out_shape=jax.ShapeDtypeStruct((2, hp, ncol), BF16),
        compiler_params=pltpu.CompilerParams(
            dimension_semantics=("arbitrary", "arbitrary"), vmem_limit_bytes=V7X_VMEM_LIMIT),
        name="dft_fold",
    )(pq, pqm, jnp.asarray(c["jf"]).astype(BF16))

    scale = float((seq + n_meta) * FNET_GROUP_DIM) ** -0.5
    tile = pl.BlockSpec((t, ncol), lambda i: (i, 0))
    rowspec = pl.BlockSpec((1, 8, hp), lambda i: (i, 0, 0))
    u, w = pl.pallas_call(
        functools.partial(_dft_kernel, scale=scale),
        grid=(nf,),
        in_specs=[_resident((2, hp, ncol)), _resident((t, hp)), _resident((t, hp)),
                  rowspec, rowspec],
        out_specs=(tile, tile),
        out_shape=(jax.ShapeDtypeStruct((hp, ncol), BF16),) * 2,
        compiler_params=pltpu.CompilerParams(
            dimension_semantics=("arbitrary",), vmem_limit_bytes=V7X_VMEM_LIMIT),
        name="seq_dft",
    )(peqo, jnp.asarray(c["dc"]), jnp.asarray(c["ds"]), jnp.asarray(c["rowc"]),
      jnp.asarray(c["rows"]))

    col = lambda rows: pl.BlockSpec((rows, FNET_WIDTH), lambda b: (0, b))
    return pl.pallas_call(
        functools.partial(_unfold_kernel, r_last=c["r_last"], i_mixed=c["i_mixed"]),
        grid=(ncb,),
        in_specs=[col(hp), col(hp), _resident((2, t, 2 * t))],
        out_specs=col(seq),
        out_shape=jax.ShapeDtypeStruct((seq, ncol), BF16),
        compiler_params=pltpu.CompilerParams(
            dimension_semantics=("arbitrary",), vmem_limit_bytes=V7X_VMEM_LIMIT),
        name="dft_unfold",
    )(u, w, jnp.asarray(c["ju"][1:]).astype(BF16))


FFN_CHUNK = 1024
TAIL_SPLIT = 2
TAIL_TM = 512


def _tail_kernel(x_ref, ol_ref, of_ref, ob_ref, y_ref, nmix_ref, gn_ref, nffn_ref, nfin_ref,
                 wg_ref, wm_ref, wo_ref, wf_ref, wout_ref, w1_ref, w2_ref, o_ref):
    tm = x_ref.shape[0]
    sub = tm // TAIL_SPLIT
    tiles = [(dict(), slice(i * sub, (i + 1) * sub)) for i in range(TAIL_SPLIT)]
    dot = functools.partial(jnp.dot, preferred_element_type=F32)
    nchunk = D_FF // FFN_CHUNK

    def norm_in(s, r):
        s["x"] = x_ref[r, :]
        s["xn"] = _rms(s["x"], nmix_ref[...]).astype(BF16)

    def gates(s, r):
        s["g"] = [dot(s["xn"], wg_ref[:, h * HEAD_DV:(h + 1) * HEAD_DV])
                  for h in range(GLA_HEADS)]
        s["ga"] = dot(s["xn"], wm_ref[:, 0:D_MODEL])
        s["gf"] = dot(s["xn"], wm_ref[:, D_MODEL:2 * D_MODEL])
        s["y_fnet"] = dot(y_ref[r, :], wf_ref[...])

    def gla_out(s, r):
        y_gla = None
        for h in range(GLA_HEADS):
            hv = slice(h * HEAD_DV, (h + 1) * HEAD_DV)
            o = (ol_ref[r, hv].astype(F32) + of_ref[r, hv].astype(F32)
                 + ob_ref[r, hv].astype(F32))
            g = s["g"][h]
            og = (_rms(o, gn_ref[...]) * (g * _sigmoid(g))).astype(BF16)
            part = dot(og, wo_ref[hv, :])
            y_gla = part if y_gla is None else y_gla + part
        s["y_gla"] = y_gla

    def merge(s, r):
        merged = _sigmoid(s["ga"]) * s["y_gla"] + _sigmoid(s["gf"]) * s["y_fnet"]
        s["h1"] = s["x"] + dot(merged.astype(BF16), wout_ref[...])
        s["u"] = _rms(s["h1"], nffn_ref[...]).astype(BF16)
        s["acc"] = s["h1"]

    def ffn_up(c):
        def stage(s, r):
            a = jnp.maximum(dot(s["u"], w1_ref[:, c * FFN_CHUNK:(c + 1) * FFN_CHUNK]), 0.0)
            s["a", c] = (a * a).astype(BF16)
        return stage

    def ffn_down(c):
        def stage(s, r):
            s["acc"] = s["acc"] + dot(s.pop(("a", c)),
                                      w2_ref[c * FFN_CHUNK:(c + 1) * FFN_CHUNK, :])
        return stage

    def norm_out(s, r):
        o_ref[r, :] = _rms(s["acc"], nfin_ref[...])

    stages = [norm_in, gates, gla_out, merge, ffn_up(0)]
    for c in range(nchunk):
        if c + 1 < nchunk:
            stages.append(ffn_up(c + 1))
        stages.append(ffn_down(c))
    stages.append(norm_out)
    _run_interleaved(stages, tiles)


def _tail_call(x2d, ol, o_f, o_b, y, nmix, gn, nffn, nfin, w_in_bf, w_merge, wo, wf, wout, w1,
               w2, seq, tm):
    m = x2d.shape[0]
    nt = seq // tm
    row = lambda n: pl.BlockSpec((tm, n), lambda i: (i, 0))
    norms = [nmix, gn, nffn, nfin]
    consts = [w_merge, wo, wf, wout, w1, w2]
    return pl.pallas_call(
        _tail_kernel,
        grid=(m // tm,),
        in_specs=[row(D_MODEL), row(GLA_DV), row(GLA_DV), row(GLA_DV),
                  pl.BlockSpec((tm, FNET_WIDTH), lambda i: (i % nt, i // nt))]
                 + [_resident(c.shape) for c in norms]
                 + [_window((w_in_bf.shape[0], W_WINDOW), (0, 2))]
                 + [_resident(c.shape) for c in consts],
        out_specs=row(D_MODEL),
        out_shape=jax.ShapeDtypeStruct((m, D_MODEL), F32),
        compiler_params=pltpu.CompilerParams(
            dimension_semantics=("arbitrary",), vmem_limit_bytes=V7X_VMEM_LIMIT),
        name="tail",
    )(x2d, ol, o_f, o_b, y, *norms, w_in_bf, *consts)


@functools.lru_cache(maxsize=None)
def _gla_constants(rows):
    idx = np.arange(rows)
    same = (idx[:, None] // SUB) == (idx[None, :] // SUB)
    tri_f = (same & (idx[None, :] <= idx[:, None])).astype(np.float32)
    return tri_f, np.ascontiguousarray(tri_f.T)


@functools.lru_cache(maxsize=None)
def _channel_dft():
    d = np.arange(FNET_GROUP_DIM, dtype=np.int64)
    ang = 2.0 * np.pi * ((d[:, None] * d[None, :]) % FNET_GROUP_DIM) / FNET_GROUP_DIM
    cs = np.zeros((2 * FNET_GROUP_DIM, 2 * FNET_GROUP_DIM), np.float32)
    cs[F_LEAD:F_LEAD + FNET_GROUP_DIM] = np.concatenate([np.cos(ang), np.sin(ang)], axis=1)
    return cs


def kernel(x, meta_tokens, norm_mix, w_in, w_decay_fwd, b_decay_fwd, w_decay_bwd, b_decay_bwd,
           gla_norm, w_o_gla, w_fnet, w_out, norm_ffn, w_ff1, w_ff2, norm_final):
    batch, seq, d = x.shape
    assert d == D_MODEL and meta_tokens.shape == (N_META, D_MODEL)
    assert w_in.shape[0] == 1 and seq % (2 * SCAN_RB) == 0 and seq % DFT_TILE == 0
    gate0 = 2 * GLA_DK + 2 * GLA_DV + 2 * DECAY_RANK + FNET_WIDTH
    assert w_in.shape[2] == gate0 + 2 * D_MODEL and gate0 <= 4 * W_WINDOW
    w_in_t = jnp.swapaxes(w_in[0], 0, 1)
    wi = _transpose_cast_call(w_in_t, 4 * W_WINDOW, W_WINDOW)
    wdec = (w_decay_fwd[0], w_decay_bwd[0])
    bdec = (b_decay_fwd, b_decay_bwd)
    cs = jnp.asarray(_channel_dft())
    nw_mix = norm_mix[0][None, :]

    x2d = x.reshape(batch * seq, D_MODEL)
    tail_weights = (w_o_gla[0], w_fnet[0], w_out[0], w_ff1[0], w_ff2[0])
    v, ol, qe, ke, bend, pq, *tail_weights, w_merge = _mixin_call(
        x2d, nw_mix, wi, wdec, bdec, cs, *_gla_constants(GLA_C), seq, MIX_TM, GLA_C,
        cast=tail_weights, gate_t=w_in_t, gate_row0=gate0)
    vm, _, _, kem, _, pqm = _mixin_call(meta_tokens, nw_mix, wi, wdec, bdec, cs,
                                        *_gla_constants(SUB), SUB, SUB, SUB, lead=SUB - N_META)
    o_f, o_b = _gla_scan_call(qe, ke, v, bend, kem, vm, seq, batch)

    y = _seq_dft(pq, pqm[:, 0], seq, N_META)

    out = _tail_call(x2d, ol, o_f, o_b, y, nw_mix, gla_norm[0][None, :], norm_ffn[0][None, :],
                     norm_final[None, :], wi, w_merge, *tail_weights, seq, TAIL_TM)
    return out.reshape(batch, seq, D_MODEL)
```

```python
import functools

import numpy as np
import jax
import jax.numpy as jnp
from jax import lax
from jax.experimental import pallas as pl
from jax.experimental.pallas import tpu as pltpu

F32 = jnp.float32
BF16 = jnp.bfloat16

D_MODEL = 1024
N_META = 16
GLA_HEADS = 4
GLA_DK = 512
GLA_DV = 1024
HEAD_DK = GLA_DK // GLA_HEADS
HEAD_DV = GLA_DV // GLA_HEADS
DECAY_RANK = 16
GATE_TEMP = 16.0
SUB = 64
FNET_GROUPS = 4
FNET_WIDTH = 512
FNET_GROUP_DIM = FNET_WIDTH // FNET_GROUPS
D_FF = 4 * D_MODEL
EPS = 1e-6
LOG2E = 1.4426950408889634
LANE = 128

_OFF_Q, _OFF_K, _OFF_V, _OFF_QKV_END = 0, 512, 1024, 2048
W_WINDOW = 1024
R_PAD = 128
F_LEAD = 2 * DECAY_RANK
F_EXT = F_LEAD + FNET_WIDTH + (LANE - F_LEAD)

GLA_C = 4 * SUB
MIX_TM = 1024
GATE_SUB = 32

V7X_VMEM_LIMIT = 58 * 1024 * 1024


def _rms(x, w):
    ms = jnp.mean(x * x, axis=-1, keepdims=True)
    return x * lax.rsqrt(ms + EPS) * w


def _sigmoid(x):
    return 1.0 / (1.0 + jnp.exp(-x))


def _run_interleaved(stages, tiles):
    for stage in stages:
        for state, rows in tiles:
            stage(state, rows)


def _window(block, index):
    return pl.BlockSpec(block, lambda *_: index, pipeline_mode=pl.Buffered(1))


def _resident(shape):
    return _window(shape, (0,) * len(shape))


_NT = (((1,), (1,)), ((), ()))
_TN = (((0,), (0,)), ((), ()))


def _mixin_kernel(x_ref, nw_ref, wqkv_ref, wrf_ref, wdf_ref, wdb_ref, bdf_ref, bdb_ref, cs_ref,
                  trif_ref, trib_ref, maskf_ref, maskb_ref, *refs, chunk, n_cast, n_gate, lead):
    if lead:
        *refs, tile_ref = refs
        tile_ref[...] = jnp.concatenate([jnp.zeros((lead, x_ref.shape[1]), F32), x_ref[...]],
                                        axis=0)
        x_ref = tile_ref
    n_in = n_cast + n_gate
    cast_in, gate_in = refs[:n_cast], refs[n_cast:n_in]
    v_ref, ol_ref, qe_ref, ke_ref, bend_ref, pq_ref = refs[n_in:n_in + 6]
    cast_out = refs[n_in + 6:n_in + 6 + n_cast]
    _mixin_body(x_ref, nw_ref, wqkv_ref, wrf_ref, (wdf_ref, wdb_ref), (bdf_ref, bdb_ref), cs_ref,
                trif_ref, trib_ref,
                maskf_ref, maskb_ref, v_ref, ol_ref, qe_ref, ke_ref, bend_ref, pq_ref, chunk=chunk)
    for src, dst in zip(cast_in, cast_out):
        dst[...] = src[...].astype(BF16)
    if n_gate:
        feat_major = jnp.concatenate([r[...] for r in gate_in], axis=0)
        refs[-1][...] = feat_major.T.astype(BF16)


def _mixin_body(x_ref, nw_ref, wqkv_ref, wrf_ref, wdec_refs, bdec_refs, cs_ref,
                trif_ref, trib_ref, maskf_ref, maskb_ref,
                v_ref, ol_ref, qe_ref, ke_ref, bend_ref, pq_ref, *, chunk):
    tm = x_ref.shape[0]
    n, s = chunk // SUB, SUB
    tiles = [(dict(), slice(i * chunk, (i + 1) * chunk)) for i in range(tm // chunk)]
    dot = functools.partial(jnp.dot, preferred_element_type=F32)
    cs = cs_ref[...].astype(BF16)
    tris = (trif_ref[...], trib_ref[...])
    masks = (maskf_ref[0:s, 0:s] != 0.0, maskb_ref[0:s, 0:s] != 0.0)
    half = GLA_DK // 2
    zeros = lambda rows: jnp.zeros((rows, GLA_DK), F32)
    wdec = jnp.concatenate([
        jnp.concatenate([wdec_refs[0][...], zeros(R_PAD - DECAY_RANK)], axis=0),
        jnp.concatenate([zeros(DECAY_RANK), wdec_refs[1][...], zeros(R_PAD - 2 * DECAY_RANK)],
                        axis=0)], axis=1).astype(BF16)
    bdec = jnp.concatenate([bdec_refs[0][...], bdec_refs[1][...]], axis=1)

    def by_head(ref, head0, r, val, width):
        for i in range(val.shape[1] // width):
            ref[head0 + i, r, :] = val[:, i * width:(i + 1) * width]

    def norm_in(st, r):
        st["xn"] = _rms(x_ref[r, :], nw_ref[...]).astype(BF16)

    def decay_rank(st, r):
        st["f"] = dot(st["xn"], wrf_ref[:, 0:F_EXT]).astype(BF16)
        st["r"] = st["f"][:, 0:R_PAD]

    def log_decay(st, j):
        cols = slice(j * half, (j + 1) * half)
        x = dot(st["r"], wdec[:, cols]) + bdec[:, cols]
        la = (jnp.minimum(x, 0.0) - jnp.log(1.0 + jnp.exp(-jnp.abs(x)))) * (LOG2E / GATE_TEMP)
        hi = la.astype(BF16)
        st["la", j] = (hi, (la - hi.astype(F32)).astype(BF16))

    def project(st, r):
        mm = lambda lo, hi: dot(st["xn"], wqkv_ref[:, lo:hi])
        vmid = (_OFF_V + _OFF_QKV_END) // 2
        log_decay(st, 0)
        st["q"] = mm(_OFF_Q, _OFF_K) * (HEAD_DK ** -0.5)
        log_decay(st, 1)
        st["k"] = mm(_OFF_K, _OFF_V)
        log_decay(st, 2)
        by_head(v_ref, 0, r, mm(_OFF_V, vmid).astype(BF16), HEAD_DV)
        log_decay(st, 3)
        by_head(v_ref, GLA_HEADS // 2, r, mm(vmid, _OFF_QKV_END).astype(BF16), HEAD_DV)

    def decay_factors(st, r):
        c0 = r.start // s
        for j in range(4):
            d, qcols = j // 2, slice((j % 2) * half, (j % 2 + 1) * half)
            cols = slice(j * half, (j + 1) * half)
            hi, lo = st.pop(("la", j))
            b = (dot(tris[d], hi) + dot(tris[d], lo)).reshape(n, s, half)
            if d == 0:
                b_mid, b_end = b[:, s // 2 - 1:s // 2, :], b[:, s - 1:s, :]
            else:
                b_mid, b_end = b[:, s // 2:s // 2 + 1, :], b[:, 0:1, :]
            qd = st["q"][:, qcols].reshape(n, s, half) * jnp.exp2(b - b_mid)
            kd = st["k"][:, qcols].reshape(n, s, half) * jnp.exp2(b_mid - b)
            qe = qd * jnp.exp2(b_mid)
            ke = kd * jnp.exp2(b_end - b_mid)
            st["qd", d, j % 2] = qd.reshape(chunk, half).astype(BF16)
            st["kd", d, j % 2] = kd.reshape(chunk, half).astype(BF16)
            by_head(qe_ref, cols.start // HEAD_DK, r, qe.reshape(chunk, half).astype(BF16), HEAD_DK)
            by_head(ke_ref, cols.start // HEAD_DK, r, ke.reshape(chunk, half).astype(BF16), HEAD_DK)
            bend_ref[c0:c0 + n, cols] = b_end.reshape(n, half)

    def intra_chunk(st, r):
        for h in range(GLA_HEADS):
            hk = slice((h % 2) * HEAD_DK, (h % 2 + 1) * HEAD_DK)
            hv = slice(h * HEAD_DV, (h + 1) * HEAD_DV)
            sc = [[lax.dot_general(st["qd", d, h // 2][c * s:(c + 1) * s, hk],
                                   st["kd", d, h // 2][c * s:(c + 1) * s, hk], _NT,
                                   preferred_element_type=F32) for d in range(2)]
                  for c in range(n)]
            for c, (sf, sb) in enumerate(sc):
                rows = slice(r.start + c * s, r.start + (c + 1) * s)
                both = (jnp.where(masks[0], sf, 0.0) + jnp.where(masks[1], sb, 0.0)).astype(BF16)
                ol_ref[rows, hv] = dot(both, v_ref[h, rows, :]).astype(BF16)

    def channel_dft(st, r):
        for grp in range(FNET_GROUPS):
            sl = slice(grp * FNET_GROUP_DIM, (grp + 1) * FNET_GROUP_DIM)
            pq = dot(st["f"][:, grp * FNET_GROUP_DIM:(grp + 2) * FNET_GROUP_DIM], cs)
            pq_ref[0, r, sl] = pq[:, :FNET_GROUP_DIM].astype(BF16)
            pq_ref[1, r, sl] = pq[:, FNET_GROUP_DIM:].astype(BF16)

    _run_interleaved((norm_in, decay_rank, project, decay_factors, intra_chunk, channel_dft),
                     tiles)


def _mixin_call(x2d, nw, w_in_bf, wdec, bdec, cs, tri_f, tri_b, rows_per_batch, tm, chunk,
                cast=(), gate_t=None, gate_row0=0, lead=0):
    m = x2d.shape[0] + lead
    d_model = w_in_bf.shape[0]
    nt = rows_per_batch // tm
    nb = m // rows_per_batch
    steps = m // tm
    assert not lead or (steps == 1 and lead % 8 == 0)
    slab = lambda w: pl.BlockSpec((w.shape[0] // steps, w.shape[1]), lambda i: (i, 0))
    assert all(w.shape[0] % (16 * steps) == 0 for w in cast)
    gate_specs, gate_out_spec, gate_out_shape = [], (), ()
    if gate_t is not None:
        feat = 2 * D_MODEL // steps
        assert feat % LANE == 0 and feat % GATE_SUB == 0 and gate_row0 % GATE_SUB == 0
        assert gate_t.shape == (gate_row0 + 2 * D_MODEL, d_model)
        nsub, blk0 = feat // GATE_SUB, gate_row0 // GATE_SUB
        gate_specs = [pl.BlockSpec((GATE_SUB, d_model), lambda i, j=j: (blk0 + i * nsub + j, 0))
                      for j in range(nsub)]
        gate_out_spec = (pl.BlockSpec((d_model, feat), lambda i: (0, i)),)
        gate_out_shape = (jax.ShapeDtypeStruct((d_model, 2 * D_MODEL), BF16),)
    row = lambda n: pl.BlockSpec((tm, n), lambda i: (i, 0))
    pq_spec = pl.BlockSpec((2, None, tm, FNET_WIDTH), lambda i: (0, i // nt, i % nt, 0))
    wide = jax.ShapeDtypeStruct((2 * GLA_HEADS, m, HEAD_DK), BF16)
    heads = lambda nh, width: pl.BlockSpec((nh, tm, width), lambda i: (0, i, 0))
    pq_shape = jax.ShapeDtypeStruct((2, nb, rows_per_batch, FNET_WIDTH), BF16)
    consts = [*wdec, *bdec, cs,
              jnp.asarray(tri_f).astype(BF16), jnp.asarray(tri_b).astype(BF16),
              jnp.asarray(tri_f), jnp.asarray(tri_b)]
    return pl.pallas_call(
        functools.partial(_mixin_kernel, chunk=chunk, n_cast=len(cast), n_gate=len(gate_specs),
                          lead=lead),
        grid=(steps,),
        in_specs=[_resident(x2d.shape) if lead else row(D_MODEL), _resident(nw.shape),
                  _window((d_model, _OFF_QKV_END), (0, 0)),
                  _window((d_model, W_WINDOW), (0, 3))]
                 + [_resident(c.shape) for c in consts] + [slab(w) for w in cast] + gate_specs,
        out_specs=(heads(GLA_HEADS, HEAD_DV), row(GLA_DV),
                   heads(2 * GLA_HEADS, HEAD_DK), heads(2 * GLA_HEADS, HEAD_DK),
                   pl.BlockSpec((tm // SUB, 2 * GLA_DK), lambda i: (i, 0)), pq_spec)
                  + tuple(slab(w) for w in cast) + gate_out_spec,
        out_shape=(jax.ShapeDtypeStruct((GLA_HEADS, m, HEAD_DV), BF16),
                   jax.ShapeDtypeStruct((m, GLA_DV), BF16),
                   wide, wide, jax.ShapeDtypeStruct((m // SUB, 2 * GLA_DK), F32), pq_shape)
                  + tuple(jax.ShapeDtypeStruct(w.shape, BF16) for w in cast) + gate_out_shape,
        scratch_shapes=[pltpu.VMEM((tm, D_MODEL), F32)] if lead else [],
        compiler_params=pltpu.CompilerParams(
            dimension_semantics=("arbitrary",), vmem_limit_bytes=V7X_VMEM_LIMIT),
        name="mixin",
    )(x2d, nw, w_in_bf, w_in_bf, *consts, *cast, *([gate_t] * len(gate_specs)))


def _transpose_cast_kernel(x_ref, o_ref):
    o_ref[...] = x_ref[...].T.astype(BF16)


def _transpose_cast_call(w_t, n_rows, block):
    d_model = w_t.shape[1]
    assert n_rows % block == 0 and n_rows <= w_t.shape[0]
    return pl.pallas_call(
        _transpose_cast_kernel,
        grid=(n_rows // block,),
        in_specs=[pl.BlockSpec((block, d_model), lambda i: (i, 0))],
        out_specs=pl.BlockSpec((d_model, block), lambda i: (0, i)),
        out_shape=jax.ShapeDtypeStruct((d_model, n_rows), BF16),
        compiler_params=pltpu.CompilerParams(
            dimension_semantics=("arbitrary",), vmem_limit_bytes=V7X_VMEM_LIMIT),
        name="w_in_prep",
    )(w_t)


SCAN_G = 1024
SCAN_RB = 4096


def _scan_group(dirs):
    s, n = SUB, SCAN_G // SUB
    ahead = 2
    dec, st, inter = [], [], [{}, {}]
    for qe_ref, ke_ref, v_ref, bend_ref, st_ref, row0, chunk0, backward in dirs:
        dec_rows = jnp.tile(jnp.exp2(bend_ref[chunk0:chunk0 + n, :]), (HEAD_DK // n, 1))
        dec.append(dec_rows.T)
        st.append(st_ref[...])

    def order(step, backward):
        return n - 1 - step if backward else step

    def increment(d, step):
        _, ke_ref, v_ref, _, _, row0, _, backward = dirs[d]
        i = order(step, backward)
        sub = slice(row0 + i * s, row0 + (i + 1) * s)
        return lax.dot_general(ke_ref[sub, :], v_ref[sub, :], _TN,
                               preferred_element_type=F32)

    inc = {(d, step): increment(d, step) for step in range(ahead) for d in range(2)}
    for step in range(n):
        for d, (qe_ref, _, _, _, _, row0, _, backward) in enumerate(dirs):
            i = order(step, backward)
            sub = slice(row0 + i * s, row0 + (i + 1) * s)
            inter[d][i] = jnp.dot(qe_ref[sub, :], st[d].astype(BF16),
                                  preferred_element_type=F32)
            if step + ahead < n:
                inc[d, step + ahead] = increment(d, step + ahead)
            decay = jnp.broadcast_to(dec[d][:, i:i + 1], (HEAD_DK, HEAD_DV))
            st[d] = st[d] * decay + inc.pop((d, step))
    for d, spec in enumerate(dirs):
        spec[4][...] = st[d]
    return [jnp.concatenate([inter[d][i] for i in range(n)], axis=0) for d in range(2)]


def _gla_scan_kernel(qef_ref, kef_ref, vf_ref, bendf_ref,
                     qeb_ref, keb_ref, vb_ref, bendb_ref, kem_ref, vm_ref,
                     of_ref, ob_ref, stf_ref, stb_ref):
    rb = qef_ref.shape[0]
    ng = rb // SCAN_G

    @pl.when(pl.program_id(2) == 0)
    def _():
        stf_ref[...] = lax.dot_general(kem_ref[...], vm_ref[...], _TN,
                                       preferred_element_type=F32)
        stb_ref[...] = jnp.zeros_like(stb_ref)

    for j in range(ng):
        jb = ng - 1 - j
        cpg = SCAN_G // SUB
        in_f, in_b = _scan_group([
            (qef_ref, kef_ref, vf_ref, bendf_ref, stf_ref, j * SCAN_G, j * cpg, False),
            (qeb_ref, keb_ref, vb_ref, bendb_ref, stb_ref, jb * SCAN_G, jb * cpg, True)])
        of_ref[j * SCAN_G:(j + 1) * SCAN_G, :] = in_f.astype(BF16)
        ob_ref[jb * SCAN_G:(jb + 1) * SCAN_G, :] = in_b.astype(BF16)


def _gla_scan_call(qe, ke, v, bend, kem, vm, seq, batch):
    h, rb = GLA_HEADS, SCAN_RB
    nt = seq // rb
    fwd = lambda b, hh, t: b * nt + t
    bwd = lambda b, hh, t: b * nt + nt - 1 - t
    cb = rb // SUB

    def dk_blk(rowfn, dircol, rows):
        return pl.BlockSpec((rows, HEAD_DK), lambda b, hh, t: (rowfn(b, hh, t), dircol + hh))

    def dv_blk(rowfn):
        return pl.BlockSpec((rb, HEAD_DV), lambda b, hh, t: (rowfn(b, hh, t), hh))

    def head_blk(rowfn, head0, rows, width):
        return pl.BlockSpec((None, rows, width), lambda b, hh, t: (head0 + hh, rowfn(b, hh, t), 0))

    first = lambda b, hh, t: 0
    in_specs = [
        head_blk(fwd, 0, rb, HEAD_DK), head_blk(fwd, 0, rb, HEAD_DK), head_blk(fwd, 0, rb, HEAD_DV),
        dk_blk(fwd, 0, cb),
        head_blk(bwd, h, rb, HEAD_DK), head_blk(bwd, h, rb, HEAD_DK), head_blk(bwd, 0, rb, HEAD_DV),
        dk_blk(bwd, h, cb),
        head_blk(first, 0, SUB, HEAD_DK), head_blk(first, 0, SUB, HEAD_DV),
    ]
    out = jax.ShapeDtypeStruct((batch * seq, GLA_DV), BF16)
    return pl.pallas_call(
        _gla_scan_kernel,
        grid=(batch, h, nt),
        in_specs=in_specs,
        out_specs=(dv_blk(fwd), dv_blk(bwd)),
        out_shape=(out, out),
        scratch_shapes=[pltpu.VMEM((HEAD_DK, HEAD_DV), F32), pltpu.VMEM((HEAD_DK, HEAD_DV), F32)],
        compiler_params=pltpu.CompilerParams(
            dimension_semantics=("arbitrary", "arbitrary", "arbitrary"),
            vmem_limit_bytes=V7X_VMEM_LIMIT),
        name="gla_scan",
    )(qe, ke, v, bend, qe, ke, v, bend, kem, vm)


DFT_TILE = 256


def _fold_kernel(x_ref, z_ref, j_ref, o_ref, *, m_last):
    t = DFT_TILE
    n_real = x_ref.shape[1] // t
    sign = jnp.where(pl.program_id(0) == 0, 1.0, -1.0)
    real = lambda i: x_ref[0, i * t:(i + 1) * t, :]
    lead = jnp.concatenate([jnp.zeros((t - z_ref.shape[1], z_ref.shape[2]), BF16), z_ref[0]],
                           axis=0)
    for i in range(o_ref.shape[1] // t):
        direct = (lead if i == 0 else real(i - 1)).astype(F32)
        mirror = (jnp.dot(j_ref[i, :, :t], real(min(n_real - i, n_real - 1)),
                          preferred_element_type=F32)
                  + jnp.dot(j_ref[i, :, t:], real(n_real - 1 - i), preferred_element_type=F32))
        m = i * t + lax.broadcasted_iota(jnp.int32, (t, 1), 0)
        folded = jnp.where(m <= m_last, direct + sign * mirror, 0.0)
        o_ref[0, i * t:(i + 1) * t, :] = folded.astype(BF16)


def _dft_kernel(peqo_hbm, dc_ref, ds_ref, rowc_ref, rows_ref, u_ref, w_ref, peqo_ref, sem, *,
                scale):
    t = DFT_TILE
    nc = peqo_ref.shape[1] // t
    chunk = lambda c: pltpu.make_async_copy(peqo_hbm.at[:, c * t:(c + 1) * t, :],
                                            peqo_ref.at[:, c * t:(c + 1) * t, :], sem.at[c])

    def tile(loading):
        acc_a = acc_b = None
        for c in range(nc):
            sl = slice(c * t, (c + 1) * t)
            rc, rs = rowc_ref[0, 0:1, sl], rows_ref[0, 0:1, sl]
            dc, ds = dc_ref[:, sl], ds_ref[:, sl]
            tc = (dc * rc - ds * rs).astype(BF16)
            ts = (ds * rc + dc * rs).astype(BF16)
            if loading:
                chunk(c).wait()
            a = jnp.dot(tc, peqo_ref[0, sl, :], preferred_element_type=F32)
            b = jnp.dot(ts, peqo_ref[1, sl, :], preferred_element_type=F32)
            acc_a = a if acc_a is None else acc_a + a
            acc_b = b if acc_b is None else acc_b + b
        u_ref[...] = ((acc_a - acc_b) * scale).astype(BF16)
        w_ref[...] = ((acc_a + acc_b) * scale).astype(BF16)

    first = pl.program_id(0) == 0

    @pl.when(first)
    def _():
        for c in range(nc):
            chunk(c).start()
        tile(True)

    @pl.when(jnp.logical_not(first))
    def _():
        tile(False)


def _unfold_kernel(u_ref, w_ref, j_ref, y_ref, *, r_last, i_mixed):
    t = DFT_TILE
    tile = lambda ref, i: ref[i * t:(i + 1) * t, :]
    n_real = y_ref.shape[0] // t
    for i in range(n_real):
        if i < i_mixed:
            y_ref[i * t:(i + 1) * t, :] = tile(u_ref, i + 1)
            continue
        j = 0 if i == i_mixed else 1
        mirror = jnp.dot(j_ref[j, :, t:], tile(w_ref, n_real - 1 - i), preferred_element_type=F32)
        if (n_real - i + 1) * t <= w_ref.shape[0]:
            mirror += jnp.dot(j_ref[j, :, :t], tile(w_ref, n_real - i),
                              preferred_element_type=F32)
        mirror = mirror.astype(BF16)
        if i == i_mixed:
            r = i * t + lax.broadcasted_iota(jnp.int32, (t, 1), 0)
            mirror = jnp.where(r <= r_last, tile(u_ref, i + 1), mirror)
        y_ref[i * t:(i + 1) * t, :] = mirror


@functools.lru_cache(maxsize=None)
def _dft_constants(seq, n_meta):
    t = DFT_TILE
    off = t - n_meta
    big_l = seq + n_meta
    half = big_l // 2
    n_real = seq // t
    nf = -(-(half + 1 + off) // t)
    hp = nf * t
    jf = np.zeros((nf, t, 2 * t), np.float32)
    for i in range(nf):
        a_idx, b_idx = min(n_real - i, n_real - 1), n_real - 1 - i
        for j in range(t):
            pos = i * t + j - off
            if 1 <= pos <= half - 1:
                tau, rho = divmod(big_l - pos - n_meta, t)
                assert tau in (a_idx, b_idx)
                jf[i, j, (t if tau == b_idx else 0) + rho] = 1.0
    r_last = half - n_meta
    i_mixed = r_last // t
    ju = np.zeros((3, t, 2 * t), np.float32)
    for i in range(i_mixed, n_real):
        a_idx, b_idx = n_real - i, n_real - 1 - i
        blk = np.zeros((t, 2 * t), np.float32)
        for j in range(t):
            r = i * t + j
            if r > r_last:
                tau, rho = divmod(big_l - (r + n_meta) + off, t)
                assert tau in (a_idx, b_idx)
                blk[j, (t if tau == b_idx else 0) + rho] = 1.0
        var = 1 if i == i_mixed else 2
        assert var == 1 or not ju[2].any() or (ju[2] == blk).all()
        ju[var] = blk
    ang = lambda prod: 2.0 * np.pi * (prod % big_l).astype(np.float64) / big_l
    pos = np.arange(hp, dtype=np.int64) - off
    d = ang(np.arange(t, dtype=np.int64)[:, None] * pos[None, :])
    k0 = t * np.arange(nf, dtype=np.int64) - off
    row = np.broadcast_to(ang(k0[:, None] * pos[None, :])[:, None, :], (nf, 8, hp))
    f = lambda a: np.ascontiguousarray(a, dtype=np.float32)
    return dict(jf=jf, ju=ju, dc=f(np.cos(d)), ds=f(np.sin(d)), rowc=f(np.cos(row)),
                rows=f(np.sin(row)), nf=nf, hp=hp, off=off, n_real=n_real,
                m_last=half + off, r_last=r_last, i_mixed=i_mixed)


def _seq_dft(pq, pqm, seq, n_meta):
    c = _dft_constants(seq, n_meta)
    t, nf, hp, ncb = DFT_TILE, c["nf"], c["hp"], pq.shape[1]
    ncol = ncb * FNET_WIDTH
    assert pq.shape[2:] == (seq, FNET_WIDTH)
    assert c["off"] == t - n_meta and pqm.shape[2] == FNET_WIDTH and pqm.shape[1] <= t
    col_blk = lambda rows: pl.BlockSpec((1, rows, FNET_WIDTH), lambda g, b: (g, 0, b))
    slab_blk = pl.BlockSpec((1, None, seq, FNET_WIDTH), lambda g, b: (g, b, 0, 0))
    meta_blk = pl.BlockSpec((1, pqm.shape[1], FNET_WIDTH), lambda g, b: (g, 0, 0))
    peqo = pl.pallas_call(
        functools.partial(_fold_kernel, m_last=c["m_last"]),
        grid=(2, ncb),
        in_specs=[slab_blk, meta_blk, _resident(c["jf"].shape)],
        out_specs=col_blk(hp),
        out_shape=jax.ShapeDtypeStruct((2, hp, ncol), BF16),
        compiler_params=pltpu.CompilerParams(
            dimension_semantics=("arbitrary", "arbitrary"), vmem_limit_bytes=V7X_VMEM_LIMIT),
        name="dft_fold",
    )(pq, pqm, jnp.asarray(c["jf"]).astype(BF16))

    scale = float((seq + n_meta) * FNET_GROUP_DIM) ** -0.5
    tile = pl.BlockSpec((t, ncol), lambda i: (i, 0))
    rowspec = pl.BlockSpec((1, 8, hp), lambda i: (i, 0, 0))
    u, w = pl.pallas_call(
        functools.partial(_dft_kernel, scale=scale),
        grid=(nf,),
        in_specs=[pl.BlockSpec(memory_space=pl.ANY), _resident((t, hp)), _resident((t, hp)),
                  rowspec, rowspec],
        out_specs=(tile, tile),
        out_shape=(jax.ShapeDtypeStruct((hp, ncol), BF16),) * 2,
        scratch_shapes=[pltpu.VMEM((2, hp, ncol), BF16), pltpu.SemaphoreType.DMA((hp // t,))],
        compiler_params=pltpu.CompilerParams(
            dimension_semantics=("arbitrary",), vmem_limit_bytes=V7X_VMEM_LIMIT),
        name="seq_dft",
    )(peqo, jnp.asarray(c["dc"]), jnp.asarray(c["ds"]), jnp.asarray(c["rowc"]),
      jnp.asarray(c["rows"]))

    col = lambda rows: pl.BlockSpec((rows, FNET_WIDTH), lambda b: (0, b))
    return pl.pallas_call(
        functools.partial(_unfold_kernel, r_last=c["r_last"], i_mixed=c["i_mixed"]),
        grid=(ncb,),
        in_specs=[col(hp), col(hp), _resident((2, t, 2 * t))],
        out_specs=col(seq),
        out_shape=jax.ShapeDtypeStruct((seq, ncol), BF16),
        compiler_params=pltpu.CompilerParams(
            dimension_semantics=("arbitrary",), vmem_limit_bytes=V7X_VMEM_LIMIT),
        name="dft_unfold",
    )(u, w, jnp.asarray(c["ju"][1:]).astype(BF16))


FFN_CHUNK = 1024
TAIL_SPLIT = 2
TAIL_TM = 512


def _tail_kernel(x_ref, ol_ref, of_ref, ob_ref, y_ref, nmix_ref, gn_ref, nffn_ref, nfin_ref,
                 wg_ref, wm_ref, wo_ref, wf_ref, wout_ref, w1_ref, w2_ref, o_ref):
    tm = x_ref.shape[0]
    sub = tm // TAIL_SPLIT
    tiles = [(dict(), slice(i * sub, (i + 1) * sub)) for i in range(TAIL_SPLIT)]
    dot = functools.partial(jnp.dot, preferred_element_type=F32)
    nchunk = D_FF // FFN_CHUNK

    def norm_in(s, r):
        s["x"] = x_ref[r, :]
        s["xn"] = _rms(s["x"], nmix_ref[...]).astype(BF16)

    def gates(s, r):
        s["g"] = [dot(s["xn"], wg_ref[:, h * HEAD_DV:(h + 1) * HEAD_DV])
                  for h in range(GLA_HEADS)]
        s["ga"] = dot(s["xn"], wm_ref[:, 0:D_MODEL])
        s["gf"] = dot(s["xn"], wm_ref[:, D_MODEL:2 * D_MODEL])
        s["y_fnet"] = dot(y_ref[r, :], wf_ref[...])

    def gla_out(s, r):
        y_gla = None
        for h in range(GLA_HEADS):
            hv = slice(h * HEAD_DV, (h + 1) * HEAD_DV)
            o = (ol_ref[r, hv].astype(F32) + of_ref[r, hv].astype(F32)
                 + ob_ref[r, hv].astype(F32))
            g = s["g"][h]
            og = (_rms(o, gn_ref[...]) * (g * _sigmoid(g))).astype(BF16)
            part = dot(og, wo_ref[hv, :])
            y_gla = part if y_gla is None else y_gla + part
        s["y_gla"] = y_gla

    def merge(s, r):
        merged = _sigmoid(s["ga"]) * s["y_gla"] + _sigmoid(s["gf"]) * s["y_fnet"]
        s["h1"] = s["x"] + dot(merged.astype(BF16), wout_ref[...])
        s["u"] = _rms(s["h1"], nffn_ref[...]).astype(BF16)
        s["acc"] = s["h1"]

    def ffn_up(c):
        def stage(s, r):
            a = jnp.maximum(dot(s["u"], w1_ref[:, c * FFN_CHUNK:(c + 1) * FFN_CHUNK]), 0.0)
            s["a", c] = (a * a).astype(BF16)
        return stage

    def ffn_down(c):
        def stage(s, r):
            s["acc"] = s["acc"] + dot(s.pop(("a", c)),
                                      w2_ref[c * FFN_CHUNK:(c + 1) * FFN_CHUNK, :])
        return stage

    def norm_out(s, r):
        o_ref[r, :] = _rms(s["acc"], nfin_ref[...])

    stages = [norm_in, gates, gla_out, merge, ffn_up(0)]
    for c in range(nchunk):
        if c + 1 < nchunk:
            stages.append(ffn_up(c + 1))
        stages.append(ffn_down(c))
    stages.append(norm_out)
    _run_interleaved(stages, tiles)


def _tail_call(x2d, ol, o_f, o_b, y, nmix, gn, nffn, nfin, w_in_bf, w_merge, wo, wf, wout, w1,
               w2, seq, tm):
    m = x2d.shape[0]
    nt = seq // tm
    row = lambda n: pl.BlockSpec((tm, n), lambda i: (i, 0))
    norms = [nmix, gn, nffn, nfin]
    consts = [w_merge, wo, wf, wout, w1, w2]
    return pl.pallas_call(
        _tail_kernel,
        grid=(m // tm,),
        in_specs=[row(D_MODEL), row(GLA_DV), row(GLA_DV), row(GLA_DV),
                  pl.BlockSpec((tm, FNET_WIDTH), lambda i: (i % nt, i // nt))]
                 + [_resident(c.shape) for c in norms]
                 + [_window((w_in_bf.shape[0], W_WINDOW), (0, 2))]
                 + [_resident(c.shape) for c in consts],
        out_specs=row(D_MODEL),
        out_shape=jax.ShapeDtypeStruct((m, D_MODEL), F32),
        compiler_params=pltpu.CompilerParams(
            dimension_semantics=("arbitrary",), vmem_limit_bytes=V7X_VMEM_LIMIT),
        name="tail",
    )(x2d, ol, o_f, o_b, y, *norms, w_in_bf, *consts)


@functools.lru_cache(maxsize=None)
def _gla_constants(rows):
    idx = np.arange(rows)
    same = (idx[:, None] // SUB) == (idx[None, :] // SUB)
    tri_f = (same & (idx[None, :] <= idx[:, None])).astype(np.float32)
    return tri_f, np.ascontiguousarray(tri_f.T)


@functools.lru_cache(maxsize=None)
def _channel_dft():
    d = np.arange(FNET_GROUP_DIM, dtype=np.int64)
    ang = 2.0 * np.pi * ((d[:, None] * d[None, :]) % FNET_GROUP_DIM) / FNET_GROUP_DIM
    cs = np.zeros((2 * FNET_GROUP_DIM, 2 * FNET_GROUP_DIM), np.float32)
    cs[F_LEAD:F_LEAD + FNET_GROUP_DIM] = np.concatenate([np.cos(ang), np.sin(ang)], axis=1)
    return cs


def kernel(x, meta_tokens, norm_mix, w_in, w_decay_fwd, b_decay_fwd, w_decay_bwd, b_decay_bwd,
           gla_norm, w_o_gla, w_fnet, w_out, norm_ffn, w_ff1, w_ff2, norm_final):
    batch, seq, d = x.shape
    assert d == D_MODEL and meta_tokens.shape == (N_META, D_MODEL)
    assert w_in.shape[0] == 1 and seq % (2 * SCAN_RB) == 0 and seq % DFT_TILE == 0
    gate0 = 2 * GLA_DK + 2 * GLA_DV + 2 * DECAY_RANK + FNET_WIDTH
    assert w_in.shape[2] == gate0 + 2 * D_MODEL and gate0 <= 4 * W_WINDOW
    w_in_t = jnp.swapaxes(w_in[0], 0, 1)
    wi = _transpose_cast_call(w_in_t, 4 * W_WINDOW, W_WINDOW)
    wdec = (w_decay_fwd[0], w_decay_bwd[0])
    bdec = (b_decay_fwd, b_decay_bwd)
    cs = jnp.asarray(_channel_dft())
    nw_mix = norm_mix[0][None, :]

    x2d = x.reshape(batch * seq, D_MODEL)
    tail_weights = (w_o_gla[0], w_fnet[0], w_out[0], w_ff1[0], w_ff2[0])
    v, ol, qe, ke, bend, pq, *tail_weights, w_merge = _mixin_call(
        x2d, nw_mix, wi, wdec, bdec, cs, *_gla_constants(GLA_C), seq, MIX_TM, GLA_C,
        cast=tail_weights, gate_t=w_in_t, gate_row0=gate0)
    vm, _, _, kem, _, pqm = _mixin_call(meta_tokens, nw_mix, wi, wdec, bdec, cs,
                                        *_gla_constants(SUB), SUB, SUB, SUB, lead=SUB - N_META)
    o_f, o_b = _gla_scan_call(qe, ke, v, bend, kem, vm, seq, batch)

    y = _seq_dft(pq, pqm[:, 0], seq, N_META)

    out = _tail_call(x2d, ol, o_f, o_b, y, nw_mix, gla_norm[0][None, :], norm_ffn[0][None, :],
                     norm_final[None, :], wi, w_merge, *tail_weights, seq, TAIL_TM)
    return out.reshape(batch, seq, D_MODEL)
```

```python
import functools

import numpy as np
import jax
import jax.numpy as jnp
from jax import lax
from jax.experimental import pallas as pl
from jax.experimental.pallas import tpu as pltpu

F32 = jnp.float32
BF16 = jnp.bfloat16

D_MODEL = 1024
N_META = 16
GLA_HEADS = 4
GLA_DK = 512
GLA_DV = 1024
HEAD_DK = GLA_DK // GLA_HEADS
HEAD_DV = GLA_DV // GLA_HEADS
DECAY_RANK = 16
GATE_TEMP = 16.0
SUB = 64
FNET_GROUPS = 4
FNET_WIDTH = 512
FNET_GROUP_DIM = FNET_WIDTH // FNET_GROUPS
D_FF = 4 * D_MODEL
EPS = 1e-6
LOG2E = 1.4426950408889634
LANE = 128

_OFF_Q, _OFF_K, _OFF_V, _OFF_QKV_END = 0, 512, 1024, 2048
W_WINDOW = 1024
R_PAD = 128
F_LEAD = 2 * DECAY_RANK
F_EXT = F_LEAD + FNET_WIDTH + (LANE - F_LEAD)

GLA_C = 4 * SUB
MIX_TM = 1024
GATE_SUB = 32

V7X_VMEM_LIMIT = 58 * 1024 * 1024


def _rms(x, w):
    ms = jnp.mean(x * x, axis=-1, keepdims=True)
    return x * lax.rsqrt(ms + EPS) * w


def _sigmoid(x):
    return 1.0 / (1.0 + jnp.exp(-x))


def _run_interleaved(stages, tiles):
    for stage in stages:
        for state, rows in tiles:
            stage(state, rows)


def _window(block, index):
    return pl.BlockSpec(block, lambda *_: index, pipeline_mode=pl.Buffered(1))


def _resident(shape):
    return _window(shape, (0,) * len(shape))


_NT = (((1,), (1,)), ((), ()))
_TN = (((0,), (0,)), ((), ()))


def _mixin_kernel(x_ref, nw_ref, wqkv_ref, wrf_ref, wdf_ref, wdb_ref, bdf_ref, bdb_ref, cs_ref,
                  trif_ref, trib_ref, maskf_ref, maskb_ref, *refs, chunk, n_cast, n_gate, lead):
    if lead:
        *refs, tile_ref = refs
        tile_ref[...] = jnp.concatenate([jnp.zeros((lead, x_ref.shape[1]), F32), x_ref[...]],
                                        axis=0)
        x_ref = tile_ref
    n_in = n_cast + n_gate
    cast_in, gate_in = refs[:n_cast], refs[n_cast:n_in]
    v_ref, ol_ref, qe_ref, ke_ref, bend_ref, pq_ref = refs[n_in:n_in + 6]
    cast_out = refs[n_in + 6:n_in + 6 + n_cast]
    _mixin_body(x_ref, nw_ref, wqkv_ref, wrf_ref, (wdf_ref, wdb_ref), (bdf_ref, bdb_ref), cs_ref,
                trif_ref, trib_ref,
                maskf_ref, maskb_ref, v_ref, ol_ref, qe_ref, ke_ref, bend_ref, pq_ref, chunk=chunk)
    for src, dst in zip(cast_in, cast_out):
        dst[...] = src[...].astype(BF16)
    if n_gate:
        feat_major = jnp.concatenate([r[...] for r in gate_in], axis=0)
        refs[-1][...] = feat_major.T.astype(BF16)


def _mixin_body(x_ref, nw_ref, wqkv_ref, wrf_ref, wdec_refs, bdec_refs, cs_ref,
                trif_ref, trib_ref, maskf_ref, maskb_ref,
                v_ref, ol_ref, qe_ref, ke_ref, bend_ref, pq_ref, *, chunk):
    tm = x_ref.shape[0]
    n, s = chunk // SUB, SUB
    tiles = [(dict(), slice(i * chunk, (i + 1) * chunk)) for i in range(tm // chunk)]
    dot = functools.partial(jnp.dot, preferred_element_type=F32)
    cs = cs_ref[...].astype(BF16)
    tris = (trif_ref[...], trib_ref[...])
    masks = (maskf_ref[0:s, 0:s] != 0.0, maskb_ref[0:s, 0:s] != 0.0)
    half = GLA_DK // 2
    zeros = lambda rows: jnp.zeros((rows, GLA_DK), F32)
    wdec = jnp.concatenate([
        jnp.concatenate([wdec_refs[0][...], zeros(R_PAD - DECAY_RANK)], axis=0),
        jnp.concatenate([zeros(DECAY_RANK), wdec_refs[1][...], zeros(R_PAD - 2 * DECAY_RANK)],
                        axis=0)], axis=1).astype(BF16)
    bdec = jnp.concatenate([bdec_refs[0][...], bdec_refs[1][...]], axis=1)

    def by_head(ref, head0, r, val, width):
        for i in range(val.shape[1] // width):
            ref[head0 + i, r, :] = val[:, i * width:(i + 1) * width]

    def norm_in(st, r):
        st["xn"] = _rms(x_ref[r, :], nw_ref[...]).astype(BF16)

    def decay_rank(st, r):
        st["f"] = dot(st["xn"], wrf_ref[:, 0:F_EXT]).astype(BF16)
        st["r"] = st["f"][:, 0:R_PAD]

    def log_decay(st, j):
        cols = slice(j * half, (j + 1) * half)
        x = dot(st["r"], wdec[:, cols]) + bdec[:, cols]
        la = (jnp.minimum(x, 0.0) - jnp.log(1.0 + jnp.exp(-jnp.abs(x)))) * (LOG2E / GATE_TEMP)
        hi = la.astype(BF16)
        st["la", j] = (hi, (la - hi.astype(F32)).astype(BF16))

    def project(st, r):
        mm = lambda lo, hi: dot(st["xn"], wqkv_ref[:, lo:hi])
        vmid = (_OFF_V + _OFF_QKV_END) // 2
        log_decay(st, 0)
        st["q"] = mm(_OFF_Q, _OFF_K) * (HEAD_DK ** -0.5)
        log_decay(st, 1)
        st["k"] = mm(_OFF_K, _OFF_V)
        log_decay(st, 2)
        by_head(v_ref, 0, r, mm(_OFF_V, vmid).astype(BF16), HEAD_DV)
        log_decay(st, 3)
        by_head(v_ref, GLA_HEADS // 2, r, mm(vmid, _OFF_QKV_END).astype(BF16), HEAD_DV)

    def decay_factors(st, r):
        c0 = r.start // s
        for j in range(4):
            d, qcols = j // 2, slice((j % 2) * half, (j % 2 + 1) * half)
            cols = slice(j * half, (j + 1) * half)
            hi, lo = st.pop(("la", j))
            b = (dot(tris[d], hi) + dot(tris[d], lo)).reshape(n, s, half)
            if d == 0:
                b_mid, b_end = b[:, s // 2 - 1:s // 2, :], b[:, s - 1:s, :]
            else:
                b_mid, b_end = b[:, s // 2:s // 2 + 1, :], b[:, 0:1, :]
            qd = st["q"][:, qcols].reshape(n, s, half) * jnp.exp2(b - b_mid)
            kd = st["k"][:, qcols].reshape(n, s, half) * jnp.exp2(b_mid - b)
            qe = qd * jnp.exp2(b_mid)
            ke = kd * jnp.exp2(b_end - b_mid)
            st["qd", d, j % 2] = qd.reshape(chunk, half).astype(BF16)
            st["kd", d, j % 2] = kd.reshape(chunk, half).astype(BF16)
            by_head(qe_ref, cols.start // HEAD_DK, r, qe.reshape(chunk, half).astype(BF16), HEAD_DK)
            by_head(ke_ref, cols.start // HEAD_DK, r, ke.reshape(chunk, half).astype(BF16), HEAD_DK)
            bend_ref[c0:c0 + n, cols] = b_end.reshape(n, half)

    def intra_chunk(st, r):
        for h in range(GLA_HEADS):
            hk = slice((h % 2) * HEAD_DK, (h % 2 + 1) * HEAD_DK)
            hv = slice(h * HEAD_DV, (h + 1) * HEAD_DV)
            sc = [[lax.dot_general(st["qd", d, h // 2][c * s:(c + 1) * s, hk],
                                   st["kd", d, h // 2][c * s:(c + 1) * s, hk], _NT,
                                   preferred_element_type=F32) for d in range(2)]
                  for c in range(n)]
            for c, (sf, sb) in enumerate(sc):
                rows = slice(r.start + c * s, r.start + (c + 1) * s)
                both = (jnp.where(masks[0], sf, 0.0) + jnp.where(masks[1], sb, 0.0)).astype(BF16)
                ol_ref[rows, hv] = dot(both, v_ref[h, rows, :]).astype(BF16)

    def channel_dft(st, r):
        for grp in range(FNET_GROUPS):
            sl = slice(grp * FNET_GROUP_DIM, (grp + 1) * FNET_GROUP_DIM)
            pq = dot(st["f"][:, grp * FNET_GROUP_DIM:(grp + 2) * FNET_GROUP_DIM], cs)
            pq_ref[0, r, sl] = pq[:, :FNET_GROUP_DIM].astype(BF16)
            pq_ref[1, r, sl] = pq[:, FNET_GROUP_DIM:].astype(BF16)

    _run_interleaved((norm_in, decay_rank, project, decay_factors, intra_chunk, channel_dft),
                     tiles)


def _mixin_call(x2d, nw, w_in_bf, wdec, bdec, cs, tri_f, tri_b, rows_per_batch, tm, chunk,
                cast=(), gate_t=None, gate_row0=0, lead=0):
    m = x2d.shape[0] + lead
    d_model = w_in_bf.shape[0]
    nt = rows_per_batch // tm
    nb = m // rows_per_batch
    steps = m // tm
    assert not lead or (steps == 1 and lead % 8 == 0)
    slab = lambda w: pl.BlockSpec((w.shape[0] // steps, w.shape[1]), lambda i: (i, 0))
    assert all(w.shape[0] % (16 * steps) == 0 for w in cast)
    gate_specs, gate_out_spec, gate_out_shape = [], (), ()
    if gate_t is not None:
        feat = 2 * D_MODEL // steps
        assert feat % LANE == 0 and feat % GATE_SUB == 0 and gate_row0 % GATE_SUB == 0
        assert gate_t.shape == (gate_row0 + 2 * D_MODEL, d_model)
        nsub, blk0 = feat // GATE_SUB, gate_row0 // GATE_SUB
        gate_specs = [pl.BlockSpec((GATE_SUB, d_model), lambda i, j=j: (blk0 + i * nsub + j, 0))
                      for j in range(nsub)]
        gate_out_spec = (pl.BlockSpec((d_model, feat), lambda i: (0, i)),)
        gate_out_shape = (jax.ShapeDtypeStruct((d_model, 2 * D_MODEL), BF16),)
    row = lambda n: pl.BlockSpec((tm, n), lambda i: (i, 0))
    pq_spec = pl.BlockSpec((2, None, tm, FNET_WIDTH), lambda i: (0, i // nt, i % nt, 0))
    wide = jax.ShapeDtypeStruct((2 * GLA_HEADS, m, HEAD_DK), BF16)
    heads = lambda nh, width: pl.BlockSpec((nh, tm, width), lambda i: (0, i, 0))
    pq_shape = jax.ShapeDtypeStruct((2, nb, rows_per_batch, FNET_WIDTH), BF16)
    consts = [*wdec, *bdec, cs,
              jnp.asarray(tri_f).astype(BF16), jnp.asarray(tri_b).astype(BF16),
              jnp.asarray(tri_f), jnp.asarray(tri_b)]
    return pl.pallas_call(
        functools.partial(_mixin_kernel, chunk=chunk, n_cast=len(cast), n_gate=len(gate_specs),
                          lead=lead),
        grid=(steps,),
        in_specs=[_resident(x2d.shape) if lead else row(D_MODEL), _resident(nw.shape),
                  _window((d_model, _OFF_QKV_END), (0, 0)),
                  _window((d_model, W_WINDOW), (0, 3))]
                 + [_resident(c.shape) for c in consts] + [slab(w) for w in cast] + gate_specs,
        out_specs=(heads(GLA_HEADS, HEAD_DV), row(GLA_DV),
                   heads(2 * GLA_HEADS, HEAD_DK), heads(2 * GLA_HEADS, HEAD_DK),
                   pl.BlockSpec((tm // SUB, 2 * GLA_DK), lambda i: (i, 0)), pq_spec)
                  + tuple(slab(w) for w in cast) + gate_out_spec,
        out_shape=(jax.ShapeDtypeStruct((GLA_HEADS, m, HEAD_DV), BF16),
                   jax.ShapeDtypeStruct((m, GLA_DV), BF16),
                   wide, wide, jax.ShapeDtypeStruct((m // SUB, 2 * GLA_DK), F32), pq_shape)
                  + tuple(jax.ShapeDtypeStruct(w.shape, BF16) for w in cast) + gate_out_shape,
        scratch_shapes=[pltpu.VMEM((tm, D_MODEL), F32)] if lead else [],
        compiler_params=pltpu.CompilerParams(
            dimension_semantics=("arbitrary",), vmem_limit_bytes=V7X_VMEM_LIMIT),
        name="mixin",
    )(x2d, nw, w_in_bf, w_in_bf, *consts, *cast, *([gate_t] * len(gate_specs)))


def _transpose_cast_kernel(x_ref, o_ref):
    o_ref[...] = x_ref[...].T.astype(BF16)


def _transpose_cast_call(w_t, n_rows, block):
    d_model = w_t.shape[1]
    assert n_rows % block == 0 and n_rows <= w_t.shape[0]
    return pl.pallas_call(
        _transpose_cast_kernel,
        grid=(n_rows // block,),
        in_specs=[pl.BlockSpec((block, d_model), lambda i: (i, 0))],
        out_specs=pl.BlockSpec((d_model, block), lambda i: (0, i)),
        out_shape=jax.ShapeDtypeStruct((d_model, n_rows), BF16),
        compiler_params=pltpu.CompilerParams(
            dimension_semantics=("arbitrary",), vmem_limit_bytes=V7X_VMEM_LIMIT),
        name="w_in_prep",
    )(w_t)


SCAN_G = 2048
SCAN_RB = 4096


def _scan_group(dirs):
    s, n = SUB, SCAN_G // SUB
    ahead = 2
    dec, st, inter = [], [], [{}, {}]
    for qe_ref, ke_ref, v_ref, bend_ref, st_ref, row0, chunk0, backward in dirs:
        dec_rows = jnp.tile(jnp.exp2(bend_ref[chunk0:chunk0 + n, :]), (HEAD_DK // n, 1))
        dec.append(dec_rows.T)
        st.append(st_ref[...])

    def order(step, backward):
        return n - 1 - step if backward else step

    def increment(d, step):
        _, ke_ref, v_ref, _, _, row0, _, backward = dirs[d]
        i = order(step, backward)
        sub = slice(row0 + i * s, row0 + (i + 1) * s)
        return lax.dot_general(ke_ref[sub, :], v_ref[sub, :], _TN,
                               preferred_element_type=F32)

    inc = {(d, step): increment(d, step) for step in range(ahead) for d in range(2)}
    for step in range(n):
        for d, (qe_ref, _, _, _, _, row0, _, backward) in enumerate(dirs):
            i = order(step, backward)
            sub = slice(row0 + i * s, row0 + (i + 1) * s)
            inter[d][i] = jnp.dot(qe_ref[sub, :], st[d].astype(BF16),
                                  preferred_element_type=F32)
            if step + ahead < n:
                inc[d, step + ahead] = increment(d, step + ahead)
            decay = jnp.broadcast_to(dec[d][:, i:i + 1], (HEAD_DK, HEAD_DV))
            st[d] = st[d] * decay + inc.pop((d, step))
    for d, spec in enumerate(dirs):
        spec[4][...] = st[d]
    return [jnp.concatenate([inter[d][i] for i in range(n)], axis=0) for d in range(2)]


def _gla_scan_kernel(qef_ref, kef_ref, vf_ref, bendf_ref,
                     qeb_ref, keb_ref, vb_ref, bendb_ref, kem_ref, vm_ref,
                     of_ref, ob_ref, stf_ref, stb_ref):
    rb = qef_ref.shape[0]
    ng = rb // SCAN_G

    @pl.when(pl.program_id(2) == 0)
    def _():
        stf_ref[...] = lax.dot_general(kem_ref[...], vm_ref[...], _TN,
                                       preferred_element_type=F32)
        stb_ref[...] = jnp.zeros_like(stb_ref)

    for j in range(ng):
        jb = ng - 1 - j
        cpg = SCAN_G // SUB
        in_f, in_b = _scan_group([
            (qef_ref, kef_ref, vf_ref, bendf_ref, stf_ref, j * SCAN_G, j * cpg, False),
            (qeb_ref, keb_ref, vb_ref, bendb_ref, stb_ref, jb * SCAN_G, jb * cpg, True)])
        of_ref[j * SCAN_G:(j + 1) * SCAN_G, :] = in_f.astype(BF16)
        ob_ref[jb * SCAN_G:(jb + 1) * SCAN_G, :] = in_b.astype(BF16)


def _gla_scan_call(qe, ke, v, bend, kem, vm, seq, batch):
    h, rb = GLA_HEADS, SCAN_RB
    nt = seq // rb
    fwd = lambda b, hh, t: b * nt + t
    bwd = lambda b, hh, t: b * nt + nt - 1 - t
    cb = rb // SUB

    def dk_blk(rowfn, dircol, rows):
        return pl.BlockSpec((rows, HEAD_DK), lambda b, hh, t: (rowfn(b, hh, t), dircol + hh))

    def dv_blk(rowfn):
        return pl.BlockSpec((rb, HEAD_DV), lambda b, hh, t: (rowfn(b, hh, t), hh))

    def head_blk(rowfn, head0, rows, width):
        return pl.BlockSpec((None, rows, width), lambda b, hh, t: (head0 + hh, rowfn(b, hh, t), 0))

    first = lambda b, hh, t: 0
    in_specs = [
        head_blk(fwd, 0, rb, HEAD_DK), head_blk(fwd, 0, rb, HEAD_DK), head_blk(fwd, 0, rb, HEAD_DV),
        dk_blk(fwd, 0, cb),
        head_blk(bwd, h, rb, HEAD_DK), head_blk(bwd, h, rb, HEAD_DK), head_blk(bwd, 0, rb, HEAD_DV),
        dk_blk(bwd, h, cb),
        head_blk(first, 0, SUB, HEAD_DK), head_blk(first, 0, SUB, HEAD_DV),
    ]
    out = jax.ShapeDtypeStruct((batch * seq, GLA_DV), BF16)
    return pl.pallas_call(
        _gla_scan_kernel,
        grid=(batch, h, nt),
        in_specs=in_specs,
        out_specs=(dv_blk(fwd), dv_blk(bwd)),
        out_shape=(out, out),
        scratch_shapes=[pltpu.VMEM((HEAD_DK, HEAD_DV), F32), pltpu.VMEM((HEAD_DK, HEAD_DV), F32)],
        compiler_params=pltpu.CompilerParams(
            dimension_semantics=("arbitrary", "arbitrary", "arbitrary"),
            vmem_limit_bytes=V7X_VMEM_LIMIT),
        name="gla_scan",
    )(qe, ke, v, bend, qe, ke, v, bend, kem, vm)


DFT_TILE = 256


def _fold_kernel(x_ref, z_ref, j_ref, o_ref, *, m_last):
    t = DFT_TILE
    n_real = x_ref.shape[1] // t
    sign = jnp.where(pl.program_id(0) == 0, 1.0, -1.0)
    real = lambda i: x_ref[0, i * t:(i + 1) * t, :]
    lead = jnp.concatenate([jnp.zeros((t - z_ref.shape[1], z_ref.shape[2]), BF16), z_ref[0]],
                           axis=0)
    for i in range(o_ref.shape[1] // t):
        direct = (lead if i == 0 else real(i - 1)).astype(F32)
        mirror = (jnp.dot(j_ref[i, :, :t], real(min(n_real - i, n_real - 1)),
                          preferred_element_type=F32)
                  + jnp.dot(j_ref[i, :, t:], real(n_real - 1 - i), preferred_element_type=F32))
        m = i * t + lax.broadcasted_iota(jnp.int32, (t, 1), 0)
        folded = jnp.where(m <= m_last, direct + sign * mirror, 0.0)
        o_ref[0, i * t:(i + 1) * t, :] = folded.astype(BF16)


def _dft_kernel(peqo_ref, dc_ref, ds_ref, rowc_ref, rows_ref, u_ref, w_ref, *, scale):
    t = DFT_TILE
    acc_a = acc_b = None
    for c in range(peqo_ref.shape[1] // t):
        sl = slice(c * t, (c + 1) * t)
        rc, rs = rowc_ref[0, 0:1, sl], rows_ref[0, 0:1, sl]
        dc, ds = dc_ref[:, sl], ds_ref[:, sl]
        tc = (dc * rc - ds * rs).astype(BF16)
        ts = (ds * rc + dc * rs).astype(BF16)
        a = jnp.dot(tc, peqo_ref[0, sl, :], preferred_element_type=F32)
        b = jnp.dot(ts, peqo_ref[1, sl, :], preferred_element_type=F32)
        acc_a = a if acc_a is None else acc_a + a
        acc_b = b if acc_b is None else acc_b + b
    u_ref[...] = ((acc_a - acc_b) * scale).astype(BF16)
    w_ref[...] = ((acc_a + acc_b) * scale).astype(BF16)


def _unfold_kernel(u_ref, w_ref, j_ref, y_ref, *, r_last, i_mixed):
    t = DFT_TILE
    tile = lambda ref, i: ref[i * t:(i + 1) * t, :]
    n_real = y_ref.shape[0] // t
    for i in range(n_real):
        if i < i_mixed:
            y_ref[i * t:(i + 1) * t, :] = tile(u_ref, i + 1)
            continue
        j = 0 if i == i_mixed else 1
        mirror = jnp.dot(j_ref[j, :, t:], tile(w_ref, n_real - 1 - i), preferred_element_type=F32)
        if (n_real - i + 1) * t <= w_ref.shape[0]:
            mirror += jnp.dot(j_ref[j, :, :t], tile(w_ref, n_real - i),
                              preferred_element_type=F32)
        mirror = mirror.astype(BF16)
        if i == i_mixed:
            r = i * t + lax.broadcasted_iota(jnp.int32, (t, 1), 0)
            mirror = jnp.where(r <= r_last, tile(u_ref, i + 1), mirror)
        y_ref[i * t:(i + 1) * t, :] = mirror


@functools.lru_cache(maxsize=None)
def _dft_constants(seq, n_meta):
    t = DFT_TILE
    off = t - n_meta
    big_l = seq + n_meta
    half = big_l // 2
    n_real = seq // t
    nf = -(-(half + 1 + off) // t)
    hp = nf * t
    jf = np.zeros((nf, t, 2 * t), np.float32)
    for i in range(nf):
        a_idx, b_idx = min(n_real - i, n_real - 1), n_real - 1 - i
        for j in range(t):
            pos = i * t + j - off
            if 1 <= pos <= half - 1:
                tau, rho = divmod(big_l - pos - n_meta, t)
                assert tau in (a_idx, b_idx)
                jf[i, j, (t if tau == b_idx else 0) + rho] = 1.0
    r_last = half - n_meta
    i_mixed = r_last // t
    ju = np.zeros((3, t, 2 * t), np.float32)
    for i in range(i_mixed, n_real):
        a_idx, b_idx = n_real - i, n_real - 1 - i
        blk = np.zeros((t, 2 * t), np.float32)
        for j in range(t):
            r = i * t + j
            if r > r_last:
                tau, rho = divmod(big_l - (r + n_meta) + off, t)
                assert tau in (a_idx, b_idx)
                blk[j, (t if tau == b_idx else 0) + rho] = 1.0
        var = 1 if i == i_mixed else 2
        assert var == 1 or not ju[2].any() or (ju[2] == blk).all()
        ju[var] = blk
    ang = lambda prod: 2.0 * np.pi * (prod % big_l).astype(np.float64) / big_l
    pos = np.arange(hp, dtype=np.int64) - off
    d = ang(np.arange(t, dtype=np.int64)[:, None] * pos[None, :])
    k0 = t * np.arange(nf, dtype=np.int64) - off
    row = np.broadcast_to(ang(k0[:, None] * pos[None, :])[:, None, :], (nf, 8, hp))
    f = lambda a: np.ascontiguousarray(a, dtype=np.float32)
    return dict(jf=jf, ju=ju, dc=f(np.cos(d)), ds=f(np.sin(d)), rowc=f(np.cos(row)),
                rows=f(np.sin(row)), nf=nf, hp=hp, off=off, n_real=n_real,
                m_last=half + off, r_last=r_last, i_mixed=i_mixed)


def _seq_dft(pq, pqm, seq, n_meta):
    c = _dft_constants(seq, n_meta)
    t, nf, hp, ncb = DFT_TILE, c["nf"], c["hp"], pq.shape[1]
    ncol = ncb * FNET_WIDTH
    assert pq.shape[2:] == (seq, FNET_WIDTH)
    assert c["off"] == t - n_meta and pqm.shape[2] == FNET_WIDTH and pqm.shape[1] <= t
    col_blk = lambda rows: pl.BlockSpec((1, rows, FNET_WIDTH), lambda g, b: (g, 0, b))
    slab_blk = pl.BlockSpec((1, None, seq, FNET_WIDTH), lambda g, b: (g, b, 0, 0))
    meta_blk = pl.BlockSpec((1, pqm.shape[1], FNET_WIDTH), lambda g, b: (g, 0, 0))
    peqo = pl.pallas_call(
        functools.partial(_fold_kernel, m_last=c["m_last"]),
        grid=(2, ncb),
        in_specs=[slab_blk, meta_blk, _resident(c["jf"].shape)],
        out_specs=col_blk(hp),
        out_shape=jax.ShapeDtypeStruct((2, hp, ncol), BF16),
        compiler_params=pltpu.CompilerParams(
            dimension_semantics=("arbitrary", "arbitrary"), vmem_limit_bytes=V7X_VMEM_LIMIT),
        name="dft_fold",
    )(pq, pqm, jnp.asarray(c["jf"]).astype(BF16))

    scale = float((seq + n_meta) * FNET_GROUP_DIM) ** -0.5
    tile = pl.BlockSpec((t, ncol), lambda i: (i, 0))
    rowspec = pl.BlockSpec((1, 8, hp), lambda i: (i, 0, 0))
    u, w = pl.pallas_call(
        functools.partial(_dft_kernel, scale=scale),
        grid=(nf,),
        in_specs=[_resident((2, hp, ncol)), _resident((t, hp)), _resident((t, hp)),
                  rowspec, rowspec],
        out_specs=(tile, tile),
        out_shape=(jax.ShapeDtypeStruct((hp, ncol), BF16),) * 2,
        compiler_params=pltpu.CompilerParams(
            dimension_semantics=("arbitrary",), vmem_limit_bytes=V7X_VMEM_LIMIT),
        name="seq_dft",
    )(peqo, jnp.asarray(c["dc"]), jnp.asarray(c["ds"]), jnp.asarray(c["rowc"]),
      jnp.asarray(c["rows"]))

    col = lambda rows: pl.BlockSpec((rows, FNET_WIDTH), lambda b: (0, b))
    return pl.pallas_call(
        functools.partial(_unfold_kernel, r_last=c["r_last"], i_mixed=c["i_mixed"]),
        grid=(ncb,),
        in_specs=[col(hp), col(hp), _resident((2, t, 2 * t))],
        out_specs=col(seq),
        out_shape=jax.ShapeDtypeStruct((seq, ncol), BF16),
        compiler_params=pltpu.CompilerParams(
            dimension_semantics=("arbitrary",), vmem_limit_bytes=V7X_VMEM_LIMIT),
        name="dft_unfold",
    )(u, w, jnp.asarray(c["ju"][1:]).astype(BF16))


FFN_CHUNK = 1024
TAIL_SPLIT = 2
TAIL_TM = 512


def _tail_kernel(x_ref, ol_ref, of_ref, ob_ref, y_ref, nmix_ref, gn_ref, nffn_ref, nfin_ref,
                 wg_ref, wm_ref, wo_ref, wf_ref, wout_ref, w1_ref, w2_ref, o_ref):
    tm = x_ref.shape[0]
    sub = tm // TAIL_SPLIT
    tiles = [(dict(), slice(i * sub, (i + 1) * sub)) for i in range(TAIL_SPLIT)]
    dot = functools.partial(jnp.dot, preferred_element_type=F32)
    nchunk = D_FF // FFN_CHUNK

    def norm_in(s, r):
        s["x"] = x_ref[r, :]
        s["xn"] = _rms(s["x"], nmix_ref[...]).astype(BF16)

    def gates(s, r):
        s["g"] = [dot(s["xn"], wg_ref[:, h * HEAD_DV:(h + 1) * HEAD_DV])
                  for h in range(GLA_HEADS)]
        s["ga"] = dot(s["xn"], wm_ref[:, 0:D_MODEL])
        s["gf"] = dot(s["xn"], wm_ref[:, D_MODEL:2 * D_MODEL])
        s["y_fnet"] = dot(y_ref[r, :], wf_ref[...])

    def gla_out(s, r):
        y_gla = None
        for h in range(GLA_HEADS):
            hv = slice(h * HEAD_DV, (h + 1) * HEAD_DV)
            o = (ol_ref[r, hv].astype(F32) + of_ref[r, hv].astype(F32)
                 + ob_ref[r, hv].astype(F32))
            g = s["g"][h]
            og = (_rms(o, gn_ref[...]) * (g * _sigmoid(g))).astype(BF16)
            part = dot(og, wo_ref[hv, :])
            y_gla = part if y_gla is None else y_gla + part
        s["y_gla"] = y_gla

    def merge(s, r):
        merged = _sigmoid(s["ga"]) * s["y_gla"] + _sigmoid(s["gf"]) * s["y_fnet"]
        s["h1"] = s["x"] + dot(merged.astype(BF16), wout_ref[...])
        s["u"] = _rms(s["h1"], nffn_ref[...]).astype(BF16)
        s["acc"] = s["h1"]

    def ffn_up(c):
        def stage(s, r):
            a = jnp.maximum(dot(s["u"], w1_ref[:, c * FFN_CHUNK:(c + 1) * FFN_CHUNK]), 0.0)
            s["a", c] = (a * a).astype(BF16)
        return stage

    def ffn_down(c):
        def stage(s, r):
            s["acc"] = s["acc"] + dot(s.pop(("a", c)),
                                      w2_ref[c * FFN_CHUNK:(c + 1) * FFN_CHUNK, :])
        return stage

    def norm_out(s, r):
        o_ref[r, :] = _rms(s["acc"], nfin_ref[...])

    stages = [norm_in, gates, gla_out, merge, ffn_up(0)]
    for c in range(nchunk):
        if c + 1 < nchunk:
            stages.append(ffn_up(c + 1))
        stages.append(ffn_down(c))
    stages.append(norm_out)
    _run_interleaved(stages, tiles)


def _tail_call(x2d, ol, o_f, o_b, y, nmix, gn, nffn, nfin, w_in_bf, w_merge, wo, wf, wout, w1,
               w2, seq, tm):
    m = x2d.shape[0]
    nt = seq // tm
    row = lambda n: pl.BlockSpec((tm, n), lambda i: (i, 0))
    norms = [nmix, gn, nffn, nfin]
    consts = [w_merge, wo, wf, wout, w1, w2]
    return pl.pallas_call(
        _tail_kernel,
        grid=(m // tm,),
        in_specs=[row(D_MODEL), row(GLA_DV), row(GLA_DV), row(GLA_DV),
                  pl.BlockSpec((tm, FNET_WIDTH), lambda i: (i % nt, i // nt))]
                 + [_resident(c.shape) for c in norms]
                 + [_window((w_in_bf.shape[0], W_WINDOW), (0, 2))]
                 + [_resident(c.shape) for c in consts],
        out_specs=row(D_MODEL),
        out_shape=jax.ShapeDtypeStruct((m, D_MODEL), F32),
        compiler_params=pltpu.CompilerParams(
            dimension_semantics=("arbitrary",), vmem_limit_bytes=V7X_VMEM_LIMIT),
        name="tail",
    )(x2d, ol, o_f, o_b, y, *norms, w_in_bf, *consts)


@functools.lru_cache(maxsize=None)
def _gla_constants(rows):
    idx = np.arange(rows)
    same = (idx[:, None] // SUB) == (idx[None, :] // SUB)
    tri_f = (same & (idx[None, :] <= idx[:, None])).astype(np.float32)
    return tri_f, np.ascontiguousarray(tri_f.T)


@functools.lru_cache(maxsize=None)
def _channel_dft():
    d = np.arange(FNET_GROUP_DIM, dtype=np.int64)
    ang = 2.0 * np.pi * ((d[:, None] * d[None, :]) % FNET_GROUP_DIM) / FNET_GROUP_DIM
    cs = np.zeros((2 * FNET_GROUP_DIM, 2 * FNET_GROUP_DIM), np.float32)
    cs[F_LEAD:F_LEAD + FNET_GROUP_DIM] = np.concatenate([np.cos(ang), np.sin(ang)], axis=1)
    return cs


def kernel(x, meta_tokens, norm_mix, w_in, w_decay_fwd, b_decay_fwd, w_decay_bwd, b_decay_bwd,
           gla_norm, w_o_gla, w_fnet, w_out, norm_ffn, w_ff1, w_ff2, norm_final):
    batch, seq, d = x.shape
    assert d == D_MODEL and meta_tokens.shape == (N_META, D_MODEL)
    assert w_in.shape[0] == 1 and seq % (2 * SCAN_RB) == 0 and seq % DFT_TILE == 0
    gate0 = 2 * GLA_DK + 2 * GLA_DV + 2 * DECAY_RANK + FNET_WIDTH
    assert w_in.shape[2] == gate0 + 2 * D_MODEL and gate0 <= 4 * W_WINDOW
    w_in_t = jnp.swapaxes(w_in[0], 0, 1)
    wi = _transpose_cast_call(w_in_t, 4 * W_WINDOW, W_WINDOW)
    wdec = (w_decay_fwd[0], w_decay_bwd[0])
    bdec = (b_decay_fwd, b_decay_bwd)
    cs = jnp.asarray(_channel_dft())
    nw_mix = norm_mix[0][None, :]

    x2d = x.reshape(batch * seq, D_MODEL)
    tail_weights = (w_o_gla[0], w_fnet[0], w_out[0], w_ff1[0], w_ff2[0])
    v, ol, qe, ke, bend, pq, *tail_weights, w_merge = _mixin_call(
        x2d, nw_mix, wi, wdec, bdec, cs, *_gla_constants(GLA_C), seq, MIX_TM, GLA_C,
        cast=tail_weights, gate_t=w_in_t, gate_row0=gate0)
    vm, _, _, kem, _, pqm = _mixin_call(meta_tokens, nw_mix, wi, wdec, bdec, cs,
                                        *_gla_constants(SUB), SUB, SUB, SUB, lead=SUB - N_META)
    o_f, o_b = _gla_scan_call(qe, ke, v, bend, kem, vm, seq, batch)

    y = _seq_dft(pq, pqm[:, 0], seq, N_META)

    out = _tail_call(x2d, ol, o_f, o_b, y, nw_mix, gla_norm[0][None, :], norm_ffn[0][None, :],
                     norm_final[None, :], wi, w_merge, *tail_weights, seq, TAIL_TM)
    return out.reshape(batch, seq, D_MODEL)
```
